```python
import math
import jax
import jax.numpy as jnp
from jax import lax
import numpy as np

D_MODEL = 1024
BATCH = 4
SEQ = 8192
DEPTH = 2

CTX_LEN = 256
GRID_W = 64
QBLOCK = 128
ROPE_THETA = 10000.0
EPS = 1e-6
F32 = jnp.float32

DIFF_HEADS = 4
DIFF_HEAD_DIM = 64
LRU_WIDTH = 512
LRU_BLOCKS = 8
LRU_BW = LRU_WIDTH // LRU_BLOCKS
CONV_W = 4
LRU_C = 8.0
GQA_HEADS = 8
GQA_KV_HEADS = 2
GQA_HEAD_DIM = 64
MLA_HEADS = 8
MLA_Q_RANK = 256
MLA_KV_RANK = 128
MLA_NOPE = 32
MLA_ROPE = 16
MLA_V = 64
N_EXPERTS = 64
N_GROUPS = 8
TOPK_GROUPS = 4
TOP_K = 8
EXPERT_FF = 256
SHARED_FF = 256
ROUTED_SCALE = 2.5
MOE_BLOCK = 128

AB_SPLITS = (DIFF_HEADS * 2 * DIFF_HEAD_DIM, DIFF_HEADS * 2 * DIFF_HEAD_DIM,
             DIFF_HEADS * 2 * DIFF_HEAD_DIM, LRU_WIDTH, LRU_WIDTH)
AB_IN = sum(AB_SPLITS)
AB_OUT = DIFF_HEADS * 2 * DIFF_HEAD_DIM + LRU_WIDTH
CD_SPLITS = (GQA_HEADS * GQA_HEAD_DIM, GQA_KV_HEADS * GQA_HEAD_DIM, GQA_KV_HEADS * GQA_HEAD_DIM,
             MLA_Q_RANK, MLA_KV_RANK, MLA_ROPE)
CD_IN = sum(CD_SPLITS)
CD_OUT = GQA_HEADS * GQA_HEAD_DIM + MLA_HEADS * MLA_V

kernel_name = 'hybrid_diffattn_rglru_gqa_mla_moe'


def split_cols(p, sizes):
    return jnp.split(p, np.cumsum(sizes)[:-1].tolist(), axis=-1)


def rms_norm(x):
    xf = x.astype(F32)
    return (xf * lax.rsqrt(jnp.mean(xf * xf, axis=-1, keepdims=True) + EPS)).astype(x.dtype)


def modulate(x, shift, scale):
    return rms_norm(x) * (1.0 + scale) + shift


def grid_positions(length):
    rows = length // GRID_W
    row = jnp.repeat(jnp.arange(rows, dtype=F32), GRID_W)
    col = jnp.tile(jnp.arange(GRID_W, dtype=F32), rows)
    return row, col


def _rotate(x, pos):
    m = x.shape[-1]
    inv = ROPE_THETA ** (-jnp.arange(0, m, 2, dtype=F32) / m)
    ang = pos[:, None] * inv[None, :]
    cos, sin = jnp.cos(ang), jnp.sin(ang)
    x1 = x[..., : m // 2].astype(F32)
    x2 = x[..., m // 2:].astype(F32)
    return jnp.concatenate([x1 * cos - x2 * sin, x1 * sin + x2 * cos], axis=-1)


def axial_rope(x, row, col):
    half = x.shape[-1] // 2
    out = jnp.concatenate([_rotate(x[..., :half], row), _rotate(x[..., half:], col)], axis=-1)
    return out.astype(x.dtype)


def sweep_query_blocks(fn, q):
    L = q.shape[-2]
    nb = L // QBLOCK
    qb = jnp.moveaxis(q.reshape(q.shape[:-2] + (nb, QBLOCK, q.shape[-1])), -3, 0)
    out = jnp.moveaxis(lax.map(fn, qb), 0, -3)
    return out.reshape(out.shape[:-3] + (L, out.shape[-1]))


def gqa_attention(q, k, v):
    scale = q.shape[-1] ** -0.5

    def block(qb):
        s = jnp.einsum('bkgqd,bkmd->bkgqm', qb, k).astype(F32) * scale
        p = jax.nn.softmax(s, axis=-1)
        return jnp.einsum('bkgqm,bkmd->bkgqd', p.astype(v.dtype), v)

    return sweep_query_blocks(block, q)


def diff_attention(q, k, v, lam):
    scale = q.shape[-1] ** -0.5

    def block(qb):
        s = jnp.einsum('bhjqd,bhjmd->bhjqm', qb, k).astype(F32) * scale
        p = jax.nn.softmax(s, axis=-1)
        w = p[:, :, 0] - lam * p[:, :, 1]
        return jnp.einsum('bhqm,bhmd->bhqd', w.astype(v.dtype), v)

    return sweep_query_blocks(block, q)


def short_conv(u, w, b):
    y = lax.conv_general_dilated(
        u, w[:, None, :].astype(u.dtype), window_strides=(1,),
        padding=[(CONV_W // 2, CONV_W - 1 - CONV_W // 2)],
        dimension_numbers=('NWC', 'WIO', 'NWC'), feature_group_count=u.shape[-1])
    return y + b


def rglru_coeffs(u, w_a, b_a, w_x, b_x, lam):
    bsz, L, C = u.shape
    ub = u.reshape(bsz, L, LRU_BLOCKS, LRU_BW)
    r = jax.nn.sigmoid((jnp.einsum('blni,nij->blnj', ub, w_a) + b_a).astype(F32)).reshape(bsz, L, C)
    i = jax.nn.sigmoid((jnp.einsum('blni,nij->blnj', ub, w_x) + b_x).astype(F32)).reshape(bsz, L, C)
    log_a = -LRU_C * r * jax.nn.softplus(-lam.astype(F32)).reshape(C)
    a = jnp.exp(log_a)
    b = jnp.sqrt(-jnp.expm1(2.0 * log_a)) * i * u.astype(F32)
    return a, b


def _combine(e1, e2):
    a1, b1 = e1
    a2, b2 = e2
    return a1 * a2, a2 * b1 + b2


def linear_scan(a, b, h0, reverse):
    if reverse:
        a, b = jnp.flip(a, 1), jnp.flip(b, 1)
    b = b.at[:, 0].add(a[:, 0] * h0)
    _, h = lax.associative_scan(_combine, (a, b), axis=1)
    return jnp.flip(h, 1) if reverse else h


def rglru_bidirectional(uc, ux, w_a, b_a, w_x, b_x, lam):
    hc_sum, hx_sum = 0.0, 0.0
    for d, reverse in enumerate((False, True)):
        ac, bc = rglru_coeffs(uc, w_a[d], b_a[d], w_x[d], b_x[d], lam[d])
        hc = linear_scan(ac, bc, jnp.zeros_like(ac[:, 0]), reverse)
        h0 = hc[:, 0] if reverse else hc[:, -1]
        ax, bx = rglru_coeffs(ux, w_a[d], b_a[d], w_x[d], b_x[d], lam[d])
        hx = linear_scan(ax, bx, h0, reverse)
        hc_sum = hc_sum + hc
        hx_sum = hx_sum + hx
    return hc_sum, hx_sum


def mixer_ab(hc, hx, w_in, w_out, lam_q, lam_k, subln, conv_w, conv_b, w_a, b_a, w_x, b_x,
             lru_lam, lambda_init, row, col, need_ctx):
    lam = (jnp.exp(jnp.sum(lam_q[0] * lam_k[0]).astype(F32))
           - jnp.exp(jnp.sum(lam_q[1] * lam_k[1]).astype(F32)) + lambda_init)

    def project(h):
        bsz, L, _ = h.shape
        q, k, v, u, g = split_cols(h @ w_in, AB_SPLITS)
        q = q.reshape(bsz, L, DIFF_HEADS, 2, DIFF_HEAD_DIM).transpose(0, 2, 3, 1, 4)
        k = k.reshape(bsz, L, DIFF_HEADS, 2, DIFF_HEAD_DIM).transpose(0, 2, 3, 1, 4)
        v = v.reshape(bsz, L, DIFF_HEADS, 2 * DIFF_HEAD_DIM).transpose(0, 2, 1, 3)
        return q, k, v, short_conv(u, conv_w, conv_b), g

    qc, kc, vc, uc, gc = project(hc)
    qx, kx, vx, ux, gx = project(hx)
    qx, kx = axial_rope(qx, row, col), axial_rope(kx, row, col)
    lc, lx = rglru_bidirectional(uc, ux, w_a, b_a, w_x, b_x, lru_lam)

    def finish(o_att, l, g):
        bsz, H, L, dv = o_att.shape
        o_att = (rms_norm(o_att) * subln * (1.0 - lambda_init)).transpose(0, 2, 1, 3).reshape(bsz, L, H * dv)
        o_rec = (l * jax.nn.gelu(g.astype(F32))).astype(o_att.dtype)
        return jnp.concatenate([o_att, o_rec], axis=-1) @ w_out

    ox = finish(diff_attention(qx, jnp.concatenate([kc, kx], axis=-2),
                               jnp.concatenate([vc, vx], axis=-2), lam), lx, gx)
    oc = finish(diff_attention(qc, kc, vc, lam), lc, gc) if need_ctx else None
    return oc, ox


def mixer_cd(hc, hx, w_in, w_out, q_norm, k_norm, cq_norm, ckv_norm, w_uq, w_ukv, row, col, need_ctx):
    groups = GQA_HEADS // GQA_KV_HEADS

    def project(h):
        bsz, L, _ = h.shape
        q, k, v, cq, ckv, kr = split_cols(h @ w_in, CD_SPLITS)
        q = (rms_norm(q.reshape(bsz, L, GQA_KV_HEADS, groups, GQA_HEAD_DIM)) * q_norm).transpose(0, 2, 3, 1, 4)
        k = (rms_norm(k.reshape(bsz, L, GQA_KV_HEADS, GQA_HEAD_DIM)) * k_norm).transpose(0, 2, 1, 3)
        v = v.reshape(bsz, L, GQA_KV_HEADS, GQA_HEAD_DIM).transpose(0, 2, 1, 3)
        mq = ((rms_norm(cq) * cq_norm) @ w_uq).reshape(bsz, L, MLA_HEADS, MLA_NOPE + MLA_ROPE).transpose(0, 2, 1, 3)
        mkv = ((rms_norm(ckv) * ckv_norm) @ w_ukv).reshape(bsz, L, MLA_HEADS, MLA_NOPE + MLA_V).transpose(0, 2, 1, 3)
        return q, k, v, mq, mkv[..., :MLA_NOPE], mkv[..., MLA_NOPE:], kr[:, None]

    def mla_keys(k_nope, kr):
        return jnp.concatenate([k_nope, jnp.broadcast_to(kr, k_nope.shape[:-1] + (MLA_ROPE,))], axis=-1)

    qc, kc, vc, mqc, knc, mvc, krc = project(hc)
    qx, kx, vx, mqx, knx, mvx, krx = project(hx)
    qx, kx = axial_rope(qx, row, col), axial_rope(kx, row, col)
    mqx = jnp.concatenate([mqx[..., :MLA_NOPE], axial_rope(mqx[..., MLA_NOPE:], row, col)], axis=-1)
    krx = axial_rope(krx, row, col)
    mkc, mkx = mla_keys(knc, krc), mla_keys(knx, krx)

    def finish(o_gqa, o_mla):
        bsz, L = o_gqa.shape[0], o_gqa.shape[3]
        o_gqa = o_gqa.transpose(0, 3, 1, 2, 4).reshape(bsz, L, GQA_HEADS * GQA_HEAD_DIM)
        o_mla = o_mla[:, :, 0].transpose(0, 2, 1, 3).reshape(bsz, L, MLA_HEADS * MLA_V)
        return jnp.concatenate([o_gqa, o_mla], axis=-1) @ w_out

    ox = finish(
        gqa_attention(qx, jnp.concatenate([kc, kx], axis=-2), jnp.concatenate([vc, vx], axis=-2)),
        gqa_attention(mqx[:, :, None], jnp.concatenate([mkc, mkx], axis=-2), jnp.concatenate([mvc, mvx], axis=-2)))
    oc = finish(gqa_attention(qc, kc, vc), gqa_attention(mqc[:, :, None], mkc, mvc)) if need_ctx else None
    return oc, ox


def moe_ffn(t, router_w, router_bias, w_gate, w_up, w_down, sh_gate, sh_up, sh_down):
    n_tok, d = t.shape
    per_group = N_EXPERTS // N_GROUPS

    def block(tb):
        scores = jax.nn.sigmoid((tb @ router_w).astype(F32))
        choice = scores + router_bias.astype(F32)
        grp_score = lax.top_k(choice.reshape(-1, N_GROUPS, per_group), 2)[0].sum(-1)
        _, g_idx = lax.top_k(grp_score, TOPK_GROUPS)
        g_mask = jax.nn.one_hot(g_idx, N_GROUPS, dtype=F32).sum(1)
        e_mask = jnp.repeat(g_mask, per_group, axis=1) > 0
        _, e_idx = lax.top_k(jnp.where(e_mask, choice, -jnp.inf), TOP_K)
        w = jnp.take_along_axis(scores, e_idx, axis=-1)
        w = w / jnp.sum(w, axis=-1, keepdims=True) * ROUTED_SCALE
        gates = jnp.sum(jax.nn.one_hot(e_idx, N_EXPERTS, dtype=F32) * w[..., None], axis=1)
        h = jax.nn.silu(jnp.einsum('nd,edf->nef', tb, w_gate)) * jnp.einsum('nd,edf->nef', tb, w_up)
        routed = jnp.einsum('nef,efd->nd', h * gates[..., None].astype(h.dtype), w_down)
        shared = (jax.nn.silu(tb @ sh_gate) * (tb @ sh_up)) @ sh_down
        return routed + shared

    y = lax.map(block, t.reshape(n_tok // MOE_BLOCK, MOE_BLOCK, d))
    return y.reshape(n_tok, d)


def setup_inputs(seed: int = 0) -> dict:
    key = jax.random.key(seed)
    keys = iter(jax.random.split(key, 48))

    def nrm(shape, scale):
        return jax.random.normal(next(keys), shape, F32) * scale

    d = D_MODEL
    n_even = (DEPTH + 1) // 2
    n_odd = DEPTH // 2
    x = nrm((BATCH, SEQ, d), 1.0)
    c = nrm((BATCH, d), 1.0)
    ctx = nrm((BATCH, CTX_LEN, d), 1.0)
    c_ctx = nrm((d,), 1.0)
    mod_w = nrm((DEPTH, d, 6 * d), 0.5 * d ** -0.5)
    mod_b = nrm((DEPTH, 6 * d), 0.02)
    ab_w_in = nrm((n_even, d, AB_IN), d ** -0.5)
    ab_w_out = nrm((n_even, AB_OUT, d), AB_OUT ** -0.5)
    diff_lambda_q = nrm((n_even, 2, DIFF_HEAD_DIM), 0.1)
    diff_lambda_k = nrm((n_even, 2, DIFF_HEAD_DIM), 0.1)
    diff_subln = 1.0 + nrm((n_even, 2 * DIFF_HEAD_DIM), 0.02)
    lru_conv_w = nrm((n_even, CONV_W, LRU_WIDTH), CONV_W ** -0.5)
    lru_conv_b = nrm((n_even, LRU_WIDTH), 0.02)
    lru_w_a = nrm((n_even, 2, LRU_BLOCKS, LRU_BW, LRU_BW), LRU_BW ** -0.5)
    lru_b_a = nrm((n_even, 2, LRU_BLOCKS, LRU_BW), 0.02)
    lru_w_x = nrm((n_even, 2, LRU_BLOCKS, LRU_BW, LRU_BW), LRU_BW ** -0.5)
    lru_b_x = nrm((n_even, 2, LRU_BLOCKS, LRU_BW), 0.02)
    a0 = jax.random.uniform(next(keys), (n_even, 2, LRU_BLOCKS, LRU_BW), F32, 0.9, 0.999)
    s = a0 ** (1.0 / LRU_C)
    lru_lambda = jnp.log(s) - jnp.log1p(-s)
    cd_w_in = nrm((n_odd, d, CD_IN), d ** -0.5)
    cd_w_out = nrm((n_odd, CD_OUT, d), CD_OUT ** -0.5)
    gqa_q_norm = 1.0 + nrm((n_odd, GQA_HEAD_DIM), 0.02)
    gqa_k_norm = 1.0 + nrm((n_odd, GQA_HEAD_DIM), 0.02)
    mla_q_norm = 1.0 + nrm((n_odd, MLA_Q_RANK), 0.02)
    mla_kv_norm = 1.0 + nrm((n_odd, MLA_KV_RANK), 0.02)
    mla_w_uq = nrm((n_odd, MLA_Q_RANK, MLA_HEADS * (MLA_NOPE + MLA_ROPE)), MLA_Q_RANK ** -0.5)
    mla_w_ukv = nrm((n_odd, MLA_KV_RANK, MLA_HEADS * (MLA_NOPE + MLA_V)), MLA_KV_RANK ** -0.5)
    router_w = nrm((DEPTH, d, N_EXPERTS), d ** -0.5)
    router_bias = nrm((DEPTH, N_EXPERTS), 0.01)
    exp_w_gate = nrm((DEPTH, N_EXPERTS, d, EXPERT_FF), d ** -0.5)
    exp_w_up = nrm((DEPTH, N_EXPERTS, d, EXPERT_FF), d ** -0.5)
    exp_w_down = nrm((DEPTH, N_EXPERTS, EXPERT_FF, d), EXPERT_FF ** -0.5)
    sh_w_gate = nrm((DEPTH, d, SHARED_FF), d ** -0.5)
    sh_w_up = nrm((DEPTH, d, SHARED_FF), d ** -0.5)
    sh_w_down = nrm((DEPTH, SHARED_FF, d), SHARED_FF ** -0.5)
    final_norm = 1.0 + nrm((d,), 0.02)
    return {'x': x, 'c': c, 'ctx': ctx, 'c_ctx': c_ctx, 'mod_w': mod_w, 'mod_b': mod_b,
            'ab_w_in': ab_w_in, 'ab_w_out': ab_w_out, 'diff_lambda_q': diff_lambda_q,
            'diff_lambda_k': diff_lambda_k, 'diff_subln': diff_subln, 'lru_conv_w': lru_conv_w,
            'lru_conv_b': lru_conv_b, 'lru_w_a': lru_w_a, 'lru_b_a': lru_b_a, 'lru_w_x': lru_w_x,
            'lru_b_x': lru_b_x, 'lru_lambda': lru_lambda, 'cd_w_in': cd_w_in, 'cd_w_out': cd_w_out,
            'gqa_q_norm': gqa_q_norm, 'gqa_k_norm': gqa_k_norm, 'mla_q_norm': mla_q_norm,
            'mla_kv_norm': mla_kv_norm, 'mla_w_uq': mla_w_uq, 'mla_w_ukv': mla_w_ukv,
            'router_w': router_w, 'router_bias': router_bias, 'exp_w_gate': exp_w_gate,
            'exp_w_up': exp_w_up, 'exp_w_down': exp_w_down, 'sh_w_gate': sh_w_gate,
            'sh_w_up': sh_w_up, 'sh_w_down': sh_w_down, 'final_norm': final_norm}


def reference(x, c, ctx, c_ctx, mod_w, mod_b, ab_w_in, ab_w_out, diff_lambda_q, diff_lambda_k,
              diff_subln, lru_conv_w, lru_conv_b, lru_w_a, lru_b_a, lru_w_x, lru_b_x, lru_lambda,
              cd_w_in, cd_w_out, gqa_q_norm, gqa_k_norm, mla_q_norm, mla_kv_norm, mla_w_uq, mla_w_ukv,
              router_w, router_bias, exp_w_gate, exp_w_up, exp_w_down, sh_w_gate, sh_w_up, sh_w_down,
              final_norm):
    d = x.shape[-1]
    row, col = grid_positions(x.shape[1])
    xc = ctx
    for i in range(DEPTH):
        last = i == DEPTH - 1
        mod_x = (jax.nn.silu(c) @ mod_w[i] + mod_b[i])[:, None, :]
        mod_c = (jax.nn.silu(c_ctx) @ mod_w[i] + mod_b[i])[None, None, :]
        sh1x, sc1x, g1x, sh2x, sc2x, g2x = jnp.split(mod_x, 6, axis=-1)
        sh1c, sc1c, g1c, sh2c, sc2c, g2c = jnp.split(mod_c, 6, axis=-1)
        hx = modulate(x, sh1x, sc1x)
        hc = modulate(xc, sh1c, sc1c)
        j = i // 2
        if i % 2 == 0:
            oc, ox = mixer_ab(hc, hx, ab_w_in[j], ab_w_out[j], diff_lambda_q[j], diff_lambda_k[j],
                              diff_subln[j], lru_conv_w[j], lru_conv_b[j], lru_w_a[j], lru_b_a[j],
                              lru_w_x[j], lru_b_x[j], lru_lambda[j],
                              0.8 - 0.6 * math.exp(-0.3 * i), row, col, not last)
        else:
            oc, ox = mixer_cd(hc, hx, cd_w_in[j], cd_w_out[j], gqa_q_norm[j], gqa_k_norm[j],
                              mla_q_norm[j], mla_kv_norm[j], mla_w_uq[j], mla_w_ukv[j],
                              row, col, not last)
        x = x + g1x * ox
        hx2 = modulate(x, sh2x, sc2x)
        if last:
            y = moe_ffn(hx2.reshape(-1, d), router_w[i], router_bias[i], exp_w_gate[i], exp_w_up[i],
                        exp_w_down[i], sh_w_gate[i], sh_w_up[i], sh_w_down[i])
            x = x + g2x * y.reshape(x.shape)
        else:
            xc = xc + g1c * oc
            hc2 = modulate(xc, sh2c, sc2c)
            n_ctx = hc2.shape[0] * hc2.shape[1]
            y = moe_ffn(jnp.concatenate([hc2.reshape(-1, d), hx2.reshape(-1, d)], axis=0),
                        router_w[i], router_bias[i], exp_w_gate[i], exp_w_up[i], exp_w_down[i],
                        sh_w_gate[i], sh_w_up[i], sh_w_down[i])
            xc = xc + g2c * y[:n_ctx].reshape(xc.shape)
            x = x + g2x * y[n_ctx:].reshape(x.shape)
    return rms_norm(x) * final_norm
```

```python
import functools
import math

import jax
import jax.numpy as jnp
from jax import lax
from jax.experimental import pallas as pl
from jax.experimental.pallas import tpu as pltpu

F32 = jnp.float32
BF16 = jnp.bfloat16

GRID_W = 64
ROPE_THETA = 10000.0
EPS = 1e-6
DIFF_HEADS = 4
DIFF_HEAD_DIM = 64
LRU_WIDTH = 512
LRU_BLOCKS = 8
LRU_BW = LRU_WIDTH // LRU_BLOCKS
CONV_W = 4
LRU_C = 8.0
GQA_HEADS = 8
GQA_KV_HEADS = 2
GQA_HEAD_DIM = 64
MLA_HEADS = 8
MLA_Q_RANK = 256
MLA_KV_RANK = 128
MLA_NOPE = 32
MLA_ROPE = 16
MLA_V = 64
MLA_QK = MLA_NOPE + MLA_ROPE
N_EXPERTS = 64
N_GROUPS = 8
PER_GROUP = N_EXPERTS // N_GROUPS
TOPK_GROUPS = 4
TOP_K = 8
EXPERT_FF = 256
ROUTED_SCALE = 2.5

ROW_TILE = 256
ATTN_Q_TILE = 128
LRU_CHUNK = 128
LRU_LANES = 256
LANE = 128
SUBLANE = 8
MOD_ROWS = 8
VMEM_LIMIT = 56 * 1024 * 1024


def _params(sem, vmem=VMEM_LIMIT):
    return pltpu.CompilerParams(dimension_semantics=sem, vmem_limit_bytes=vmem)


def _silu(x):
    return x * jax.nn.sigmoid(x)


def _rms(x):
    return x * lax.rsqrt(jnp.mean(x * x, axis=-1, keepdims=True) + EPS)


def _mod_rows(mod_ref, ctx_row, b, k, d, row0, rows, n_lat):
    lat = mod_ref[pl.ds(b, 1), k * d:(k + 1) * d]
    ctx = mod_ref[ctx_row:ctx_row + 1, k * d:(k + 1) * d]
    if n_lat % rows == 0:
        return jnp.where(row0 >= n_lat, ctx, lat)
    r = row0 + lax.broadcasted_iota(jnp.int32, (rows, 1), 0)
    return jnp.where(r >= n_lat, ctx, lat)


def _rope(x, cos, sin_lo, sin_hi, half):
    n = x.shape[-1]
    return x * cos + pltpu.roll(x, n - half, 1) * sin_lo + pltpu.roll(x, half, 1) * sin_hi


def _mod_kernel(c_ref, w_ref, b_ref, o_ref):
    s = _silu(c_ref[...])
    o_ref[0] = jnp.dot(s, w_ref[0], preferred_element_type=F32, precision=lax.Precision.HIGHEST) + b_ref[0]


def _modulation(c_all, mod_w, mod_b):
    depth, d, n = mod_w.shape
    tn = n // 4
    return pl.pallas_call(
        _mod_kernel,
        grid=(depth, n // tn),
        in_specs=[pl.BlockSpec((MOD_ROWS, d), lambda i, j: (0, 0)),
                  pl.BlockSpec((1, d, tn), lambda i, j: (i, 0, j)),
                  pl.BlockSpec((1, 1, tn), lambda i, j: (i, 0, j))],
        out_specs=pl.BlockSpec((1, MOD_ROWS, tn), lambda i, j: (i, 0, j)),
        out_shape=jax.ShapeDtypeStruct((depth, MOD_ROWS, n), F32),
        compiler_params=_params(("arbitrary", "arbitrary")),
        name="modulation",
    )(c_all, mod_w, mod_b.reshape(depth, 1, n))


def _proj_ab_kernel(x_ref, mod_ref, w_ref, cos_ref, slo_ref, shi_ref,
                    q_ref, kt_ref, v_ref, u_ref, g_ref, *, n_lat, ctx_row):
    b, t = pl.program_id(0), pl.program_id(1)
    d = x_ref.shape[-1]
    row0 = t * ROW_TILE
    shift = _mod_rows(mod_ref, ctx_row, b, 0, d, row0, ROW_TILE, n_lat)
    scale = _mod_rows(mod_ref, ctx_row, b, 1, d, row0, ROW_TILE, n_lat)
    h = (_rms(x_ref[0]) * (1.0 + scale) + shift).astype(BF16)
    acc = jnp.dot(h, w_ref[...], preferred_element_type=F32)
    w = DIFF_HEADS * 2 * DIFF_HEAD_DIM
    cos, slo, shi = cos_ref[...], slo_ref[...], shi_ref[...]
    q = _rope(acc[:, 0:w], cos, slo, shi, DIFF_HEAD_DIM // 4) * (DIFF_HEAD_DIM ** -0.5)
    k = _rope(acc[:, w:2 * w], cos, slo, shi, DIFF_HEAD_DIM // 4)
    q_ref[0] = q.astype(BF16)
    kt_ref[0, 0] = k.T.astype(BF16)
    v_ref[0] = acc[:, 2 * w:3 * w].astype(BF16)
    u_ref[0] = acc[:, 3 * w:3 * w + LRU_WIDTH]
    g_ref[0] = jax.nn.gelu(acc[:, 3 * w + LRU_WIDTH:3 * w + 2 * LRU_WIDTH]).astype(BF16)


def _proj_ab(xs, mod, w_in, tabs, n_lat, ctx_row):
    bsz, m, d = xs.shape
    nt = m // ROW_TILE
    w = DIFF_HEADS * 2 * DIFF_HEAD_DIM
    n_in = w_in.shape[1]
    tile = lambda n: pl.BlockSpec((1, ROW_TILE, n), lambda b, t: (b, t, 0))
    tab = pl.BlockSpec((ROW_TILE, w), lambda b, t: (t, 0))
    return pl.pallas_call(
        functools.partial(_proj_ab_kernel, n_lat=n_lat, ctx_row=ctx_row),
        grid=(bsz, nt),
        in_specs=[tile(d),
                  pl.BlockSpec(mod.shape, lambda b, t: (0, 0)),
                  pl.BlockSpec((d, n_in), lambda b, t: (0, 0)),
                  tab, tab, tab],
        out_specs=[tile(w),
                   pl.BlockSpec((1, 1, w, ROW_TILE), lambda b, t: (b, t, 0, 0)),
                   tile(w), tile(LRU_WIDTH), tile(LRU_WIDTH)],
        out_shape=[jax.ShapeDtypeStruct((bsz, m, w), BF16),
                   jax.ShapeDtypeStruct((bsz, nt, w, ROW_TILE), BF16),
                   jax.ShapeDtypeStruct((bsz, m, w), BF16),
                   jax.ShapeDtypeStruct((bsz, m, LRU_WIDTH), F32),
                   jax.ShapeDtypeStruct((bsz, m, LRU_WIDTH), BF16)],
        compiler_params=_params(("parallel", "parallel")),
        name="proj_ab",
    )(xs, mod, w_in, *tabs)


def _group_rms(x, ones_bd):
    x2 = x * x
    hi = x2.astype(BF16)
    lo = (x2 - hi.astype(F32)).astype(BF16)
    ss = jnp.dot(hi, ones_bd, preferred_element_type=F32) + jnp.dot(lo, ones_bd, preferred_element_type=F32)
    return x * lax.rsqrt(ss * (1.0 / GQA_HEAD_DIM) + EPS)


def _proj_cd_kernel(x_ref, mod_ref, w_ref, bd_ref, qn_ref, kn_ref, cqn_ref, ckvn_ref, wuq_ref, wkn_ref, wv_ref,
                    cos_ref, slo_ref, shi_ref, cosm_ref, slom_ref, shim_ref,
                    qg_ref, kgt_ref, vg_ref, qm_ref, kmt_ref, vm_ref, *, n_lat, ctx_row):
    b, t = pl.program_id(0), pl.program_id(1)
    d = x_ref.shape[-1]
    row0 = t * ROW_TILE
    shift = _mod_rows(mod_ref, ctx_row, b, 0, d, row0, ROW_TILE, n_lat)
    scale = _mod_rows(mod_ref, ctx_row, b, 1, d, row0, ROW_TILE, n_lat)
    h = (_rms(x_ref[0]) * (1.0 + scale) + shift).astype(BF16)
    acc = jnp.dot(h, w_ref[...], preferred_element_type=F32)
    wq = GQA_HEADS * GQA_HEAD_DIM
    wk = GQA_KV_HEADS * GQA_HEAD_DIM
    o = 0
    q = acc[:, o:o + wq]; o += wq
    k = acc[:, o:o + wk]; o += wk
    v = acc[:, o:o + wk]; o += wk
    cq = acc[:, o:o + MLA_Q_RANK]; o += MLA_Q_RANK
    ckv = acc[:, o:o + MLA_KV_RANK]; o += MLA_KV_RANK
    kr = acc[:, o:o + MLA_HEADS * MLA_QK]
    cos, slo, shi = cos_ref[...], slo_ref[...], shi_ref[...]
    bd = bd_ref[...]
    q = _rope(_group_rms(q, bd) * qn_ref[...], cos, slo, shi, GQA_HEAD_DIM // 4) * (GQA_HEAD_DIM ** -0.5)
    k = _rope(_group_rms(k, bd[:wk, :wk]) * kn_ref[...], cos[:, :wk], slo[:, :wk], shi[:, :wk], GQA_HEAD_DIM // 4)
    qg_ref[0] = q.astype(BF16)
    kgt_ref[0, 0] = k.T.astype(BF16)
    vg_ref[0] = v.astype(BF16)
    cosm, slom, shim = cosm_ref[...], slom_ref[...], shim_ref[...]
    cqn = (_rms(cq) * cqn_ref[...]).astype(BF16)
    mq = jnp.dot(cqn, wuq_ref[...], preferred_element_type=F32)
    qm_ref[0] = (_rope(mq, cosm, slom, shim, MLA_ROPE // 4) * (MLA_QK ** -0.5)).astype(BF16)
    ckvn = (_rms(ckv) * ckvn_ref[...]).astype(BF16)
    km = jnp.dot(ckvn, wkn_ref[...], preferred_element_type=F32) + _rope(kr, cosm, slom, shim, MLA_ROPE // 4)
    kmt_ref[0, 0] = km.T.astype(BF16)
    vm_ref[0] = jnp.dot(ckvn, wv_ref[...], preferred_element_type=F32).astype(BF16)


def _proj_cd(xs, mod, w_in, bd, qn, kn, cqn, ckvn, wuq, wkn, wv, tabs, tabs_m, n_lat, ctx_row):
    bsz, m, d = xs.shape
    nt = m // ROW_TILE
    wq = GQA_HEADS * GQA_HEAD_DIM
    wk = GQA_KV_HEADS * GQA_HEAD_DIM
    wm = MLA_HEADS * MLA_QK
    wmv = MLA_HEADS * MLA_V
    tile = lambda n: pl.BlockSpec((1, ROW_TILE, n), lambda b, t: (b, t, 0))
    ttile = lambda n: pl.BlockSpec((1, 1, n, ROW_TILE), lambda b, t: (b, t, 0, 0))
    whole = lambda a: pl.BlockSpec(a.shape, lambda b, t: (0,) * a.ndim)
    tab = pl.BlockSpec((ROW_TILE, wq), lambda b, t: (t, 0))
    tabm = pl.BlockSpec((ROW_TILE, wm), lambda b, t: (t, 0))
    return pl.pallas_call(
        functools.partial(_proj_cd_kernel, n_lat=n_lat, ctx_row=ctx_row),
        grid=(bsz, nt),
        in_specs=[tile(d), whole(mod), whole(w_in), whole(bd), whole(qn), whole(kn), whole(cqn), whole(ckvn),
                  whole(wuq), whole(wkn), whole(wv), tab, tab, tab, tabm, tabm, tabm],
        out_specs=[tile(wq), ttile(wk), tile(wk), tile(wm), ttile(wm), tile(wmv)],
        out_shape=[jax.ShapeDtypeStruct((bsz, m, wq), BF16),
                   jax.ShapeDtypeStruct((bsz, nt, wk, ROW_TILE), BF16),
                   jax.ShapeDtypeStruct((bsz, m, wk), BF16),
                   jax.ShapeDtypeStruct((bsz, m, wm), BF16),
                   jax.ShapeDtypeStruct((bsz, nt, wm, ROW_TILE), BF16),
                   jax.ShapeDtypeStruct((bsz, m, wmv), BF16)],
        compiler_params=_params(("parallel", "parallel")),
        name="proj_cd",
    )(xs, mod, w_in, bd, qn, kn, cqn, ckvn, wuq, wkn, wv, *tabs, *tabs_m)


def _score_pass(q, kt_ref, krow, dk, s_ref, lo, hi):
    rows = q.shape[0]

    def body(c, mp):
        s = jnp.dot(q, kt_ref[0, c, krow:krow + dk, :], preferred_element_type=F32)
        s_ref[c] = s
        for j in range(ROW_TILE // LANE):
            mp = jnp.maximum(mp, s[:, j * LANE:(j + 1) * LANE])
        return mp

    mp = lax.fori_loop(lo, hi, body, jnp.full((rows, LANE), -jnp.inf, F32))
    return jnp.max(mp, axis=-1, keepdims=True)


def _value_pass(s_ref, m, v_ref, vcol, dv, lo, hi):
    rows = m.shape[0]

    def body(c, carry):
        lp, acc = carry
        p = jnp.exp(s_ref[c] - m)
        for j in range(ROW_TILE // LANE):
            lp = lp + p[:, j * LANE:(j + 1) * LANE]
        k0 = pl.multiple_of(c * ROW_TILE, ROW_TILE)
        base = vcol // LANE * LANE if dv < LANE else vcol
        v = v_ref[0, pl.ds(k0, ROW_TILE), base:base + max(dv, LANE)]
        acc = acc + jnp.dot(p.astype(BF16), v[:, vcol - base:vcol - base + dv], preferred_element_type=F32)
        return lp, acc

    lp, acc = lax.fori_loop(lo, hi, body, (jnp.zeros((rows, LANE), F32), jnp.zeros((rows, dv), F32)))
    return acc, jnp.sum(lp, axis=-1, keepdims=True)


def _key_range(n_lat, n_chunks):
    t = pl.program_id(1)
    is_ctx = t * ATTN_Q_TILE >= n_lat
    return jnp.where(is_ctx, n_lat // ROW_TILE, 0), n_chunks


def _diff_attn_kernel(q_ref, kt_ref, v_ref, lq_ref, lk_ref, sub_ref, o_ref, s0_ref, s1_ref, *, n_lat, lambda_init):
    n_chunks = kt_ref.shape[1]
    lo, hi = _key_range(n_lat, n_chunks)
    lq, lk = lq_ref[...], lk_ref[...]
    lam = (jnp.exp(jnp.sum(lq[0:1] * lk[0:1], axis=-1, keepdims=True))
           - jnp.exp(jnp.sum(lq[1:2] * lk[1:2], axis=-1, keepdims=True)) + lambda_init)
    dk, dv = DIFF_HEAD_DIM, 2 * DIFF_HEAD_DIM
    q = q_ref[0]
    outs = []
    for h in range(DIFF_HEADS):
        m0 = _score_pass(q[:, (2 * h) * dk:(2 * h + 1) * dk], kt_ref, (2 * h) * dk, dk, s0_ref, lo, hi)
        m1 = _score_pass(q[:, (2 * h + 1) * dk:(2 * h + 2) * dk], kt_ref, (2 * h + 1) * dk, dk, s1_ref, lo, hi)
        a0, l0 = _value_pass(s0_ref, m0, v_ref, h * dv, dv, lo, hi)
        a1, l1 = _value_pass(s1_ref, m1, v_ref, h * dv, dv, lo, hi)
        o = a0 / l0 - lam * (a1 / l1)
        outs.append((_rms(o) * sub_ref[...] * (1.0 - lambda_init)).astype(BF16))
    o_ref[0] = jnp.concatenate(outs, axis=-1)


def _diff_attention(q, kt, v, lam_q, lam_k, subln, n_lat, lambda_init):
    bsz, m, w = q.shape
    nc = kt.shape[1]
    whole = lambda a: pl.BlockSpec(a.shape, lambda b, t: (0,) * a.ndim)
    return pl.pallas_call(
        functools.partial(_diff_attn_kernel, n_lat=n_lat, lambda_init=lambda_init),
        grid=(bsz, m // ATTN_Q_TILE),
        in_specs=[pl.BlockSpec((1, ATTN_Q_TILE, w), lambda b, t: (b, t, 0)),
                  pl.BlockSpec((1,) + kt.shape[1:], lambda b, t: (b, 0, 0, 0)),
                  pl.BlockSpec((1, m, w), lambda b, t: (b, 0, 0)),
                  whole(lam_q), whole(lam_k), whole(subln)],
        out_specs=pl.BlockSpec((1, ATTN_Q_TILE, w), lambda b, t: (b, t, 0)),
        out_shape=jax.ShapeDtypeStruct((bsz, m, w), BF16),
        scratch_shapes=[pltpu.VMEM((nc, ATTN_Q_TILE, ROW_TILE), F32),
                        pltpu.VMEM((nc, ATTN_Q_TILE, ROW_TILE), F32)],
        compiler_params=_params(("parallel", "arbitrary")),
        name="diff_attention",
    )(q, kt, v, lam_q, lam_k, subln)


def _softmax_attn_kernel(q_ref, kt_ref, v_ref, o_ref, s_ref, *, n_lat, heads, kv_of_head, dk, dv):
    n_chunks = kt_ref.shape[1]
    lo, hi = _key_range(n_lat, n_chunks)
    q = q_ref[0]
    outs = []
    for h in range(heads):
        kv = kv_of_head(h)
        m = _score_pass(q[:, h * dk:(h + 1) * dk], kt_ref, kv * dk, dk, s_ref, lo, hi)
        a, l = _value_pass(s_ref, m, v_ref, kv * dv, dv, lo, hi)
        outs.append((a / l).astype(BF16))
    o_ref[0] = jnp.concatenate(outs, axis=-1)


def _softmax_attention(q, kt, v, n_lat, heads, kv_of_head, dk, dv, name):
    bsz, m, wq = q.shape
    nc = kt.shape[1]
    return pl.pallas_call(
        functools.partial(_softmax_attn_kernel, n_lat=n_lat, heads=heads, kv_of_head=kv_of_head, dk=dk, dv=dv),
        grid=(bsz, m // ATTN_Q_TILE),
        in_specs=[pl.BlockSpec((1, ATTN_Q_TILE, wq), lambda b, t: (b, t, 0)),
                  pl.BlockSpec((1,) + kt.shape[1:], lambda b, t: (b, 0, 0, 0)),
                  pl.BlockSpec((1, m, v.shape[-1]), lambda b, t: (b, 0, 0))],
        out_specs=pl.BlockSpec((1, ATTN_Q_TILE, heads * dv), lambda b, t: (b, t, 0)),
        out_shape=jax.ShapeDtypeStruct((bsz, m, heads * dv), BF16),
        scratch_shapes=[pltpu.VMEM((nc, ATTN_Q_TILE, ROW_TILE), F32)],
        compiler_params=_params(("parallel", "arbitrary")),
        name=name,
    )(q, kt, v)


def _scan_chunk(a, b, carry, reverse):
    n = a.shape[0]
    rows = lax.broadcasted_iota(jnp.int32, (n, 1), 0)
    d = 1
    while d < n:
        sh = n - d if reverse else d
        valid = rows < n - d if reverse else rows >= d
        b = jnp.where(valid, a * pltpu.roll(b, sh, 0) + b, b)
        a = jnp.where(valid, a * pltpu.roll(a, sh, 0), a)
        d *= 2
    h = a * carry + b
    return h, (h[0:1] if reverse else h[n - 1:n])


def _lru_kernel(u_ref, g_ref, cw_ref, cb_ref, wa_ref, ba_ref, wx_ref, bx_ref, lam_ref, o_ref, up_ref, hf_ref, *, n_lat):
    m = u_ref.shape[1]
    lanes = u_ref.shape[2]
    n_chunks = m // LRU_CHUNK
    n_ctx_chunks = (m - n_lat) // LRU_CHUNK
    pad = SUBLANE

    up_ref[0:pad, :] = jnp.zeros((pad, lanes), F32)
    up_ref[pad + m:pad + m + pad, :] = jnp.zeros((pad, lanes), F32)

    def copy(c, _):
        r0 = pl.multiple_of(c * LRU_CHUNK, LRU_CHUNK)
        up_ref[pl.ds(pad + r0, LRU_CHUNK), :] = u_ref[0, pl.ds(r0, LRU_CHUNK), :]
        return 0

    lax.fori_loop(0, n_chunks, copy, 0)

    cw = cw_ref[...]
    cb = cb_ref[...]
    win_rows = LRU_CHUNK + 2 * pad

    def coeffs(c, d):
        r0 = pl.multiple_of(c * LRU_CHUNK, LRU_CHUNK)
        win = up_ref[pl.ds(r0, win_rows), :]
        r = r0 + lax.broadcasted_iota(jnp.int32, (LRU_CHUNK, 1), 0)
        at = lambda k: pltpu.roll(win, (win_rows - k) % win_rows, 0)[pad:pad + LRU_CHUNK]
        y = (cw[0:1] * jnp.where((r == n_lat) | (r == n_lat + 1), 0.0, at(-2))
             + cw[1:2] * jnp.where(r == n_lat, 0.0, at(-1))
             + cw[2:3] * win[pad:pad + LRU_CHUNK]
             + cw[3:4] * jnp.where(r == n_lat - 1, 0.0, at(1))
             + cb)
        yb = y.astype(BF16)
        ra = jax.nn.sigmoid(jnp.dot(yb, wa_ref[d, 0], preferred_element_type=F32) + ba_ref[d])
        ix = jax.nn.sigmoid(jnp.dot(yb, wx_ref[d, 0], preferred_element_type=F32) + bx_ref[d])
        z = -lam_ref[d]
        softplus = jnp.maximum(z, 0.0) + jnp.log(1.0 + jnp.exp(-jnp.abs(z)))
        a = jnp.exp(-LRU_C * ra * softplus)
        return a, jnp.sqrt(1.0 - a * a) * ix * y

    def fwd(s, carry):
        c = lax.rem(s + (n_chunks - n_ctx_chunks), n_chunks)
        a, b = coeffs(c, 0)
        h, carry = _scan_chunk(a, b, carry, False)
        r0 = pl.multiple_of(c * LRU_CHUNK, LRU_CHUNK)
        hf_ref[pl.ds(r0, LRU_CHUNK), :] = h
        return carry

    lax.fori_loop(0, n_chunks, fwd, jnp.zeros((1, lanes), F32))

    def bwd(s, carry):
        c = n_chunks - 1 - s
        a, b = coeffs(c, 1)
        h, carry = _scan_chunk(a, b, carry, True)
        r0 = pl.multiple_of(c * LRU_CHUNK, LRU_CHUNK)
        tot = hf_ref[pl.ds(r0, LRU_CHUNK), :] + h
        o_ref[0, pl.ds(r0, LRU_CHUNK), :] = (tot * g_ref[0, pl.ds(r0, LRU_CHUNK), :].astype(F32)).astype(BF16)
        return carry

    lax.fori_loop(0, n_chunks, bwd, jnp.zeros((1, lanes), F32))


def _lru(u, g, conv_w, conv_b, wa_bd, ba, wx_bd, bx, lam, n_lat):
    bsz, m, c = u.shape
    nl = c // LRU_LANES
    seq = lambda: pl.BlockSpec((1, m, LRU_LANES), lambda b, j: (b, 0, j))
    vec = lambda a: pl.BlockSpec(a.shape[:-1] + (LRU_LANES,), lambda b, j: (0,) * (a.ndim - 1) + (j,))
    mat = pl.BlockSpec((2, 1, LRU_LANES, LRU_LANES), lambda b, j: (0, j, 0, 0))
    return pl.pallas_call(
        functools.partial(_lru_kernel, n_lat=n_lat),
        grid=(bsz, nl),
        in_specs=[seq(), seq(), vec(conv_w), vec(conv_b), mat, vec(ba), mat, vec(bx), vec(lam)],
        out_specs=seq(),
        out_shape=jax.ShapeDtypeStruct((bsz, m, c), BF16),
        scratch_shapes=[pltpu.VMEM((m + 2 * SUBLANE, LRU_LANES), F32), pltpu.VMEM((m, LRU_LANES), F32)],
        compiler_params=_params(("parallel", "parallel")),
        name="rglru",
    )(u, g, conv_w, conv_b, wa_bd, ba, wx_bd, bx, lam)


def _route(logits_t, bias):
    n = logits_t.shape[-1]
    scores = jax.nn.sigmoid(logits_t)
    choice = scores + bias
    sub = lax.broadcasted_iota(jnp.int32, (PER_GROUP, n), 0)
    neg = -jnp.inf
    groups, gs = [], []
    for g in range(N_GROUPS):
        cg = choice[g * PER_GROUP:(g + 1) * PER_GROUP]
        m1 = jnp.max(cg, axis=0, keepdims=True)
        i1 = jnp.min(jnp.where(cg == m1, sub, PER_GROUP), axis=0, keepdims=True)
        m2 = jnp.max(jnp.where(sub == i1, neg, cg), axis=0, keepdims=True)
        groups.append(cg)
        gs.append(m1 + m2)
    masked = []
    for g in range(N_GROUPS):
        rank = jnp.zeros((1, n), jnp.int32)
        for o in range(N_GROUPS):
            if o != g:
                ahead = (gs[o] >= gs[g]) if o < g else (gs[o] > gs[g])
                rank = rank + jnp.where(ahead, 1, 0)
        masked.append(jnp.where(rank < TOPK_GROUPS, groups[g], neg))
    masked = jnp.concatenate(masked, axis=0)
    eidx = lax.broadcasted_iota(jnp.int32, (N_EXPERTS, n), 0)
    picked = jnp.zeros((N_EXPERTS, n), F32)
    for _ in range(TOP_K):
        mx = jnp.max(masked, axis=0, keepdims=True)
        first = jnp.min(jnp.where(masked == mx, eidx, N_EXPERTS), axis=0, keepdims=True)
        hit = eidx == first
        picked = jnp.where(hit, 1.0, picked)
        masked = jnp.where(hit, neg, masked)
    w = picked * scores
    return w / jnp.sum(w, axis=0, keepdims=True) * ROUTED_SCALE


def _out_proj_kernel(oa_ref, ob_ref, x_ref, mod_ref, wa_ref, wb_ref, rw_ref, rb_ref,
                     xo_ref, h_ref, gates_ref, *, n_lat, ctx_row):
    b, t = pl.program_id(0), pl.program_id(1)
    d = x_ref.shape[-1]
    row0 = t * ROW_TILE
    mv = lambda k: _mod_rows(mod_ref, ctx_row, b, k, d, row0, ROW_TILE, n_lat)
    y = (jnp.dot(oa_ref[0], wa_ref[...], preferred_element_type=F32)
         + jnp.dot(ob_ref[0], wb_ref[...], preferred_element_type=F32))
    x = x_ref[0] + mv(2) * y
    xo_ref[0] = x
    h = _rms(x) * (1.0 + mv(4)) + mv(3)
    h_ref[0] = h.astype(BF16)
    logits_t = lax.dot_general(rw_ref[...], h, (((1,), (1,)), ((), ())),
                               preferred_element_type=F32, precision=lax.Precision.HIGHEST)
    gates_t = _route(logits_t, rb_ref[...])
    gates_t = jnp.concatenate([gates_t, jnp.zeros((LANE - N_EXPERTS, ROW_TILE), F32)], axis=0)
    gates_ref[0] = gates_t.T


def _out_proj(oa, ob, xs, mod, w_out, router_wt, router_b, n_lat, ctx_row):
    bsz, m, d = xs.shape
    na, nb = oa.shape[-1], ob.shape[-1]
    wa, wb = w_out[:na], w_out[na:]
    tile = lambda n: pl.BlockSpec((1, ROW_TILE, n), lambda b, t: (b, t, 0))
    whole = lambda a: pl.BlockSpec(a.shape, lambda b, t: (0,) * a.ndim)
    return pl.pallas_call(
        functools.partial(_out_proj_kernel, n_lat=n_lat, ctx_row=ctx_row),
        grid=(bsz, m // ROW_TILE),
        in_specs=[tile(na), tile(nb), tile(d), whole(mod), whole(wa), whole(wb), whole(router_wt), whole(router_b)],
        out_specs=[tile(d), tile(d), tile(LANE)],
        out_shape=[jax.ShapeDtypeStruct((bsz, m, d), F32),
                   jax.ShapeDtypeStruct((bsz, m, d), BF16),
                   jax.ShapeDtypeStruct((bsz, m, LANE), F32)],
        compiler_params=_params(("parallel", "parallel")),
        name="out_proj_router",
    )(oa, ob, xs, mod, wa, wb, router_wt, router_b)


def _moe_kernel(h_ref, gates_ref, wgu_ref, wd_ref, sgu_ref, sd_ref, x_ref, mod_ref, fn_ref, o_ref, acc_ref,
                *, n_lat, ctx_row, rows, final):
    b, t, e = pl.program_id(0), pl.program_id(1), pl.program_id(2)
    d = x_ref.shape[-1]
    h = h_ref[0]

    def ffn(wgu, wd, gate):
        gu = jnp.dot(h, wgu, preferred_element_type=F32)
        a = _silu(gu[:, :EXPERT_FF]) * gu[:, EXPERT_FF:]
        if gate is not None:
            a = a * gate
        return jnp.dot(a.astype(BF16), wd, preferred_element_type=F32)

    @pl.when(e == 0)
    def _():
        acc_ref[...] = ffn(sgu_ref[...], sd_ref[...], None)

    lane = lax.broadcasted_iota(jnp.int32, (rows, LANE), 1)
    gate = jnp.sum(jnp.where(lane == e, gates_ref[0], 0.0), axis=1, keepdims=True)
    acc_ref[...] += ffn(wgu_ref[0], wd_ref[0], gate)

    @pl.when(e == N_EXPERTS - 1)
    def _():
        g2 = _mod_rows(mod_ref, ctx_row, b, 5, d, t * rows, rows, n_lat)
        y = x_ref[0] + g2 * acc_ref[...]
        if final:
            y = _rms(y) * fn_ref[...]
        o_ref[0] = y


def _moe(h, gates, xs, mod, wgu, wd, sgu, sd, final_norm, n_lat, ctx_row, rows, n_rows, final):
    bsz, m, d = xs.shape
    tile = lambda n: pl.BlockSpec((1, rows, n), lambda b, t, e: (b, t, 0))
    whole = lambda a: pl.BlockSpec(a.shape, lambda b, t, e: (0,) * a.ndim)
    return pl.pallas_call(
        functools.partial(_moe_kernel, n_lat=n_lat, ctx_row=ctx_row, rows=rows, final=final),
        grid=(bsz, n_rows // rows, N_EXPERTS),
        in_specs=[tile(d), tile(LANE),
                  pl.BlockSpec((1,) + wgu.shape[1:], lambda b, t, e: (e, 0, 0)),
                  pl.BlockSpec((1,) + wd.shape[1:], lambda b, t, e: (e, 0, 0)),
                  whole(sgu), whole(sd), tile(d), whole(mod), whole(final_norm)],
        out_specs=tile(d),
        out_shape=jax.ShapeDtypeStruct((bsz, n_rows, d), F32),
        scratch_shapes=[pltpu.VMEM((rows, d), F32)],
        compiler_params=_params(("parallel", "parallel", "arbitrary")),
        name="moe_final" if final else "moe",
    )(h, gates, wgu, wd, sgu, sd, xs, mod, final_norm)


def _rope_tables(n_lat, m, head_dim, width):
    pos = jnp.arange(n_lat, dtype=jnp.int32)
    row = (pos // GRID_W).astype(F32)
    col = (pos % GRID_W).astype(F32)
    half = head_dim // 2
    quarter = half // 2
    lane = jnp.arange(head_dim)
    freq = (lane % quarter).astype(F32)
    inv = ROPE_THETA ** (-(2.0 * freq) / half)
    ang = jnp.where(lane[None, :] < half, row[:, None], col[:, None]) * inv[None, :]
    low = (lane % half) < quarter
    cos = jnp.cos(ang)
    sin = jnp.sin(ang)
    slo = jnp.where(low[None, :], -sin, 0.0)
    shi = jnp.where(low[None, :], 0.0, sin)
    reps = width // head_dim
    pad = lambda a, fill: jnp.concatenate(
        [jnp.tile(a, (1, reps)), jnp.full((m - n_lat, width), fill, F32)], axis=0)
    return pad(cos, 1.0), pad(slo, 0.0), pad(shi, 0.0)


def _mla_tables(tabs16):
    outs = []
    for a, fill in zip(tabs16, (1.0, 0.0, 0.0)):
        m = a.shape[0]
        slot = jnp.concatenate([jnp.full((m, MLA_NOPE), fill, F32), a], axis=1)
        outs.append(jnp.tile(slot, (1, MLA_HEADS)))
    return tuple(outs)


def _block_diag(w):
    dirs = w.shape[0]
    eye = jnp.eye(LRU_BLOCKS, dtype=w.dtype)
    full = jnp.einsum('dnij,nm->dnimj', w, eye).reshape(dirs, LRU_WIDTH, LRU_WIDTH)
    nl = LRU_WIDTH // LRU_LANES
    return jnp.stack([full[:, j * LRU_LANES:(j + 1) * LRU_LANES, j * LRU_LANES:(j + 1) * LRU_LANES]
                      for j in range(nl)], axis=1)


def _moe_rows(n):
    for rows in (1024, 768, 512, 256):
        if n % rows == 0:
            return rows
    raise ValueError(f"token count {n} is not a multiple of {ROW_TILE}")


def kernel(x, c, ctx, c_ctx, mod_w, mod_b, ab_w_in, ab_w_out, diff_lambda_q, diff_lambda_k, diff_subln, lru_conv_w, lru_conv_b, lru_w_a, lru_b_a, lru_w_x, lru_b_x, lru_lambda, cd_w_in, cd_w_out, gqa_q_norm, gqa_k_norm, mla_q_norm, mla_kv_norm, mla_w_uq, mla_w_ukv, router_w, router_bias, exp_w_gate, exp_w_up, exp_w_down, sh_w_gate, sh_w_up, sh_w_down, final_norm):
    bsz, n_lat, d = x.shape
    n_ctx = ctx.shape[1]
    m = n_lat + n_ctx
    depth = mod_w.shape[0]
    assert depth == 2 and bsz < MOD_ROWS
    assert n_lat % ROW_TILE == 0 and n_ctx % ROW_TILE == 0 and n_lat % GRID_W == 0
    ctx_row = bsz

    xs = jnp.concatenate([x, ctx], axis=1)
    c_all = jnp.concatenate([c, c_ctx[None, :], jnp.zeros((MOD_ROWS - bsz - 1, d), F32)], axis=0)
    mods = _modulation(c_all, mod_w, mod_b)

    tabs64 = _rope_tables(n_lat, m, DIFF_HEAD_DIM, DIFF_HEADS * 2 * DIFF_HEAD_DIM)
    tabs_m = _mla_tables(_rope_tables(n_lat, m, MLA_ROPE, MLA_ROPE))

    def moe_weights(i):
        wgu = jnp.concatenate([exp_w_gate[i], exp_w_up[i]], axis=-1).astype(BF16)
        sgu = jnp.concatenate([sh_w_gate[i], sh_w_up[i]], axis=-1).astype(BF16)
        return wgu, exp_w_down[i].astype(BF16), sgu, sh_w_down[i].astype(BF16)

    fn = final_norm.reshape(1, d)

    mod = mods[0]
    q, kt, v, u, g = _proj_ab(xs, mod, ab_w_in[0].astype(BF16), tabs64, n_lat, ctx_row)
    lambda_init = 0.8 - 0.6 * math.exp(-0.3 * 0)
    o_att = _diff_attention(q, kt, v, diff_lambda_q[0], diff_lambda_k[0],
                            diff_subln[0].reshape(1, -1), n_lat, lambda_init)
    o_rec = _lru(u, g, lru_conv_w[0], lru_conv_b[0].reshape(1, -1),
                 _block_diag(lru_w_a[0]).astype(BF16), lru_b_a[0].reshape(2, 1, LRU_WIDTH),
                 _block_diag(lru_w_x[0]).astype(BF16), lru_b_x[0].reshape(2, 1, LRU_WIDTH),
                 lru_lambda[0].reshape(2, 1, LRU_WIDTH), n_lat)
    xs, h2, gates = _out_proj(o_att, o_rec, xs, mod, ab_w_out[0].astype(BF16),
                              router_w[0].T, router_bias[0].reshape(-1, 1), n_lat, ctx_row)
    xs = _moe(h2, gates, xs, mod, *moe_weights(0), fn, n_lat, ctx_row, _moe_rows(m), m, False)

    mod = mods[1]
    w_in = cd_w_in[0]
    wq, wk = GQA_HEADS * GQA_HEAD_DIM, GQA_KV_HEADS * GQA_HEAD_DIM
    o_kr = wq + 2 * wk + MLA_Q_RANK + MLA_KV_RANK
    kr_slots = jnp.concatenate([jnp.zeros((d, MLA_NOPE), F32), w_in[:, o_kr:o_kr + MLA_ROPE]], axis=1)
    w_in = jnp.concatenate([w_in[:, :o_kr], jnp.tile(kr_slots, (1, MLA_HEADS))], axis=1).astype(BF16)
    ukv = mla_w_ukv[0].reshape(MLA_KV_RANK, MLA_HEADS, MLA_NOPE + MLA_V)
    w_kn = jnp.concatenate([ukv[:, :, :MLA_NOPE], jnp.zeros((MLA_KV_RANK, MLA_HEADS, MLA_ROPE), F32)],
                           axis=-1).reshape(MLA_KV_RANK, MLA_HEADS * MLA_QK).astype(BF16)
    w_v = ukv[:, :, MLA_NOPE:].reshape(MLA_KV_RANK, MLA_HEADS * MLA_V).astype(BF16)
    lane = jnp.arange(wq)
    ones_bd = (lane[:, None] // GQA_HEAD_DIM == lane[None, :] // GQA_HEAD_DIM).astype(BF16)
    qg, kgt, vg, qm, kmt, vm = _proj_cd(
        xs, mod, w_in, ones_bd, jnp.tile(gqa_q_norm[0], GQA_HEADS).reshape(1, -1),
        jnp.tile(gqa_k_norm[0], GQA_KV_HEADS).reshape(1, -1), mla_q_norm[0].reshape(1, -1),
        mla_kv_norm[0].reshape(1, -1), mla_w_uq[0].astype(BF16), w_kn, w_v, tabs64, tabs_m, n_lat, ctx_row)
    groups = GQA_HEADS // GQA_KV_HEADS
    o_gqa = _softmax_attention(qg, kgt, vg, n_lat, GQA_HEADS, lambda h: h // groups,
                               GQA_HEAD_DIM, GQA_HEAD_DIM, "gqa_attention")
    o_mla = _softmax_attention(qm, kmt, vm, n_lat, MLA_HEADS, lambda h: h, MLA_QK, MLA_V, "mla_attention")
    xs, h2, gates = _out_proj(o_gqa, o_mla, xs, mod, cd_w_out[0].astype(BF16),
                              router_w[1].T, router_bias[1].reshape(-1, 1), n_lat, ctx_row)
    return _moe(h2, gates, xs, mod, *moe_weights(1), fn, n_lat, ctx_row, _moe_rows(n_lat), n_lat, True)
```

```python
import functools
import math
from typing import NamedTuple

import jax
import jax.numpy as jnp
from jax import lax
from jax.experimental import pallas as pl
from jax.experimental.pallas import tpu as pltpu

F32 = jnp.float32
BF16 = jnp.bfloat16

GRID_W = 64
ROPE_THETA = 10000.0
EPS = 1e-6
DIFF_HEADS = 4
DIFF_HEAD_DIM = 64
LRU_WIDTH = 512
LRU_BLOCKS = 8
LRU_BW = LRU_WIDTH // LRU_BLOCKS
CONV_W = 4
LRU_C = 8.0
GQA_HEADS = 8
GQA_KV_HEADS = 2
GQA_HEAD_DIM = 64
MLA_HEADS = 8
MLA_Q_RANK = 256
MLA_KV_RANK = 128
MLA_NOPE = 32
MLA_ROPE = 16
MLA_V = 64
MLA_QK = MLA_NOPE + MLA_ROPE
N_EXPERTS = 64
N_GROUPS = 8
PER_GROUP = N_EXPERTS // N_GROUPS
TOPK_GROUPS = 4
TOP_K = 8
EXPERT_FF = 256
ROUTED_SCALE = 2.5

QK_SLOT = 64
LOG2E = math.log2(math.e)

ROW_TILE = 256
LRU_CHUNK = 128
LRU_LANES = 256
LANE = 128
SUBLANE = 8
MOD_ROWS = 8
VMEM_LIMIT = 56 * 1024 * 1024


def _params(sem, vmem=VMEM_LIMIT):
    return pltpu.CompilerParams(dimension_semantics=sem, vmem_limit_bytes=vmem)


def _silu(x):
    return x * jax.nn.sigmoid(x)


def _rms(x):
    return x * lax.rsqrt(jnp.mean(x * x, axis=-1, keepdims=True) + EPS)


def _mod_rows(mod_ref, ctx_row, b, k, d, row0, rows, n_lat):
    lat = mod_ref[pl.ds(b, 1), k * d:(k + 1) * d]
    ctx = mod_ref[ctx_row:ctx_row + 1, k * d:(k + 1) * d]
    if n_lat % rows == 0:
        return jnp.where(row0 >= n_lat, ctx, lat)
    r = row0 + lax.broadcasted_iota(jnp.int32, (rows, 1), 0)
    return jnp.where(r >= n_lat, ctx, lat)


def _rope(x, cos, sin_lo, sin_hi, half):
    n = x.shape[-1]
    return x * cos + pltpu.roll(x, n - half, 1) * sin_lo + pltpu.roll(x, half, 1) * sin_hi


def _mod_kernel(c_ref, w_ref, b_ref, o_ref):
    s = _silu(c_ref[...])
    o_ref[0] = jnp.dot(s, w_ref[0], preferred_element_type=F32, precision=lax.Precision.HIGHEST) + b_ref[0]


def _modulation(c_all, mod_w, mod_b):
    depth, d, n = mod_w.shape
    tn = n // 4
    return pl.pallas_call(
        _mod_kernel,
        grid=(depth, n // tn),
        in_specs=[pl.BlockSpec((MOD_ROWS, d), lambda i, j: (0, 0)),
                  pl.BlockSpec((1, d, tn), lambda i, j: (i, 0, j)),
                  pl.BlockSpec((1, 1, tn), lambda i, j: (i, 0, j))],
        out_specs=pl.BlockSpec((1, MOD_ROWS, tn), lambda i, j: (i, 0, j)),
        out_shape=jax.ShapeDtypeStruct((depth, MOD_ROWS, n), F32),
        compiler_params=_params(("arbitrary", "arbitrary")),
        name="modulation",
    )(c_all, mod_w, mod_b.reshape(depth, 1, n))


def _proj_ab_kernel(x_ref, mod_ref, w_ref, cos_ref, slo_ref, shi_ref,
                    qt_ref, k_ref, vt_ref, u_ref, g_ref, *, n_lat, ctx_row):
    b, t = pl.program_id(0), pl.program_id(1)
    d = x_ref.shape[-1]
    row0 = t * ROW_TILE
    shift = _mod_rows(mod_ref, ctx_row, b, 0, d, row0, ROW_TILE, n_lat)
    scale = _mod_rows(mod_ref, ctx_row, b, 1, d, row0, ROW_TILE, n_lat)
    h = (_rms(x_ref[0]) * (1.0 + scale) + shift).astype(BF16)
    acc = jnp.dot(h, w_ref[...], preferred_element_type=F32)
    w = DIFF_HEADS * 2 * DIFF_HEAD_DIM
    cos, slo, shi = cos_ref[...], slo_ref[...], shi_ref[...]
    q = _rope(acc[:, 0:w], cos, slo, shi, DIFF_HEAD_DIM // 4) * (DIFF_HEAD_DIM ** -0.5 * LOG2E)
    k = _rope(acc[:, w:2 * w], cos, slo, shi, DIFF_HEAD_DIM // 4)
    qt_ref[0, 0] = q.T.astype(BF16)
    k_ref[0] = k.astype(BF16)
    vt_ref[0, 0] = acc[:, 2 * w:3 * w].T.astype(BF16)
    u_ref[0] = acc[:, 3 * w:3 * w + LRU_WIDTH]
    g_ref[0] = jax.nn.gelu(acc[:, 3 * w + LRU_WIDTH:3 * w + 2 * LRU_WIDTH]).astype(BF16)


def _proj_ab(xs, mod, w_in, tabs, n_lat, ctx_row):
    bsz, m, d = xs.shape
    nt = m // ROW_TILE
    w = DIFF_HEADS * 2 * DIFF_HEAD_DIM
    n_in = w_in.shape[1]
    tile = lambda n: pl.BlockSpec((1, ROW_TILE, n), lambda b, t: (b, t, 0))
    ttile = pl.BlockSpec((1, 1, w, ROW_TILE), lambda b, t: (b, t, 0, 0))
    tab = pl.BlockSpec((ROW_TILE, w), lambda b, t: (t, 0))
    return pl.pallas_call(
        functools.partial(_proj_ab_kernel, n_lat=n_lat, ctx_row=ctx_row),
        grid=(bsz, nt),
        in_specs=[tile(d),
                  pl.BlockSpec(mod.shape, lambda b, t: (0, 0)),
                  pl.BlockSpec((d, n_in), lambda b, t: (0, 0)),
                  tab, tab, tab],
        out_specs=[ttile, tile(w), ttile, tile(LRU_WIDTH), tile(LRU_WIDTH)],
        out_shape=[jax.ShapeDtypeStruct((bsz, nt, w, ROW_TILE), BF16),
                   jax.ShapeDtypeStruct((bsz, m, w), BF16),
                   jax.ShapeDtypeStruct((bsz, nt, w, ROW_TILE), BF16),
                   jax.ShapeDtypeStruct((bsz, m, LRU_WIDTH), F32),
                   jax.ShapeDtypeStruct((bsz, m, LRU_WIDTH), BF16)],
        compiler_params=_params(("parallel", "parallel")),
        name="proj_ab",
    )(xs, mod, w_in, *tabs)


def _group_rms(x, ones_bd):
    x2 = x * x
    hi = x2.astype(BF16)
    lo = (x2 - hi.astype(F32)).astype(BF16)
    ss = jnp.dot(hi, ones_bd, preferred_element_type=F32) + jnp.dot(lo, ones_bd, preferred_element_type=F32)
    return x * lax.rsqrt(ss * (1.0 / GQA_HEAD_DIM) + EPS)


def _proj_cd_kernel(x_ref, mod_ref, w_ref, bd_ref, qn_ref, kn_ref, cqn_ref, ckvn_ref, wuq_ref, wkn_ref, wv_ref,
                    cos_ref, slo_ref, shi_ref, cosm_ref, slom_ref, shim_ref,
                    qgt_ref, kg_ref, vgt_ref, qmt_ref, km_ref, vmt_ref, *, n_lat, ctx_row):
    b, t = pl.program_id(0), pl.program_id(1)
    d = x_ref.shape[-1]
    row0 = t * ROW_TILE
    shift = _mod_rows(mod_ref, ctx_row, b, 0, d, row0, ROW_TILE, n_lat)
    scale = _mod_rows(mod_ref, ctx_row, b, 1, d, row0, ROW_TILE, n_lat)
    h = (_rms(x_ref[0]) * (1.0 + scale) + shift).astype(BF16)
    acc = jnp.dot(h, w_ref[...], preferred_element_type=F32)
    wq = GQA_HEADS * GQA_HEAD_DIM
    wk = GQA_KV_HEADS * GQA_HEAD_DIM
    o = 0
    q = acc[:, o:o + wq]; o += wq
    k = acc[:, o:o + wk]; o += wk
    v = acc[:, o:o + wk]; o += wk
    cq = acc[:, o:o + MLA_Q_RANK]; o += MLA_Q_RANK
    ckv = acc[:, o:o + MLA_KV_RANK]; o += MLA_KV_RANK
    kr = acc[:, o:o + MLA_HEADS * QK_SLOT]
    cos, slo, shi = cos_ref[...], slo_ref[...], shi_ref[...]
    bd = bd_ref[...]
    q = (_rope(_group_rms(q, bd) * qn_ref[...], cos, slo, shi, GQA_HEAD_DIM // 4)
         * (GQA_HEAD_DIM ** -0.5 * LOG2E))
    k = _rope(_group_rms(k, bd[:wk, :wk]) * kn_ref[...], cos[:, :wk], slo[:, :wk], shi[:, :wk], GQA_HEAD_DIM // 4)
    qgt_ref[0, 0] = q.T.astype(BF16)
    kg_ref[0] = k.astype(BF16)
    vgt_ref[0, 0] = v.T.astype(BF16)
    cosm, slom, shim = cosm_ref[...], slom_ref[...], shim_ref[...]
    cqn = (_rms(cq) * cqn_ref[...]).astype(BF16)
    mq = jnp.dot(cqn, wuq_ref[...], preferred_element_type=F32)
    qmt_ref[0, 0] = (_rope(mq, cosm, slom, shim, MLA_ROPE // 4) * (MLA_QK ** -0.5 * LOG2E)).T.astype(BF16)
    ckvn = (_rms(ckv) * ckvn_ref[...]).astype(BF16)
    km = jnp.dot(ckvn, wkn_ref[...], preferred_element_type=F32) + _rope(kr, cosm, slom, shim, MLA_ROPE // 4)
    km_ref[0] = km.astype(BF16)
    vmt_ref[0, 0] = jnp.dot(ckvn, wv_ref[...], preferred_element_type=F32).T.astype(BF16)


def _proj_cd(xs, mod, w_in, bd, qn, kn, cqn, ckvn, wuq, wkn, wv, tabs, tabs_m, n_lat, ctx_row):
    bsz, m, d = xs.shape
    nt = m // ROW_TILE
    wq = GQA_HEADS * GQA_HEAD_DIM
    wk = GQA_KV_HEADS * GQA_HEAD_DIM
    wm = MLA_HEADS * QK_SLOT
    wmv = MLA_HEADS * MLA_V
    tile = lambda n: pl.BlockSpec((1, ROW_TILE, n), lambda b, t: (b, t, 0))
    ttile = lambda n: pl.BlockSpec((1, 1, n, ROW_TILE), lambda b, t: (b, t, 0, 0))
    whole = lambda a: pl.BlockSpec(a.shape, lambda b, t: (0,) * a.ndim)
    tab = pl.BlockSpec((ROW_TILE, wq), lambda b, t: (t, 0))
    tabm = pl.BlockSpec((ROW_TILE, wm), lambda b, t: (t, 0))
    return pl.pallas_call(
        functools.partial(_proj_cd_kernel, n_lat=n_lat, ctx_row=ctx_row),
        grid=(bsz, nt),
        in_specs=[tile(d), whole(mod), whole(w_in), whole(bd), whole(qn), whole(kn), whole(cqn), whole(ckvn),
                  whole(wuq), whole(wkn), whole(wv), tab, tab, tab, tabm, tabm, tabm],
        out_specs=[ttile(wq), tile(wk), ttile(wk), ttile(wm), tile(wm), ttile(wmv)],
        out_shape=[jax.ShapeDtypeStruct((bsz, nt, wq, ROW_TILE), BF16),
                   jax.ShapeDtypeStruct((bsz, m, wk), BF16),
                   jax.ShapeDtypeStruct((bsz, nt, wk, ROW_TILE), BF16),
                   jax.ShapeDtypeStruct((bsz, nt, wm, ROW_TILE), BF16),
                   jax.ShapeDtypeStruct((bsz, m, wm), BF16),
                   jax.ShapeDtypeStruct((bsz, nt, wmv, ROW_TILE), BF16)],
        compiler_params=_params(("parallel", "parallel")),
        name="proj_cd",
    )(xs, mod, w_in, bd, qn, kn, cqn, ckvn, wuq, wkn, wv, *tabs, *tabs_m)


class _Job(NamedTuple):
    q_row: int
    k_group: int
    k_half: int
    v_row: int
    dv: int


def _attend(qt_ref, k_ref, vt_ref, s_ref, jobs, lo, hi):
    tq = qt_ref.shape[-1]
    n = hi - lo
    group = next(g for g in (11, 3, 2, 1) if n % g == 0)
    steps = n // group
    row_groups = ROW_TILE // SUBLANE

    def weights(job):
        qh = qt_ref[0, 0, job.q_row:job.q_row + QK_SLOT, :]
        z = jnp.zeros_like(qh)
        return jnp.concatenate([z, qh] if job.k_half else [qh, z], axis=0)

    def score(c, buf, qw, job, mp):
        r0 = pl.multiple_of(c * ROW_TILE, ROW_TILE)
        kc = k_ref[0, pl.ds(r0, ROW_TILE), job.k_group * LANE:(job.k_group + 1) * LANE]
        s = jnp.dot(kc, qw, preferred_element_type=F32)
        s_ref[buf, c] = s
        return jnp.maximum(mp, jnp.max(s.reshape(row_groups, SUBLANE, tq), axis=0))

    def value(c, buf, m, job, lp, acc):
        p = jnp.exp2(s_ref[buf, c] - m)
        lp = lp + jnp.sum(p.reshape(row_groups, SUBLANE, tq), axis=0)
        acc = acc + jnp.dot(vt_ref[0, c, job.v_row:job.v_row + job.dv, :], p.astype(BF16),
                            preferred_element_type=F32)
        return lp, acc

    def loop(body, init):
        return body(0, init) if steps == 1 else lax.fori_loop(0, steps, body, init)

    neg = jnp.full((SUBLANE, tq), -jnp.inf, F32)
    qw = weights(jobs[0])

    def first(i, mp):
        for g in range(group):
            mp = score(lo + i * group + g, 0, qw, jobs[0], mp)
        return mp

    mp = loop(first, neg)
    results = []
    for j, job in enumerate(jobs):
        buf = j % 2
        m = jnp.max(mp, axis=0, keepdims=True)
        nxt = jobs[j + 1] if j + 1 < len(jobs) else None
        if nxt is not None:
            qw = weights(nxt)

        def body(i, carry):
            mp, lp, acc = carry
            for g in range(group):
                c = lo + i * group + g
                if nxt is not None:
                    mp = score(c, 1 - buf, qw, nxt, mp)
                lp, acc = value(c, buf, m, job, lp, acc)
            return mp, lp, acc

        mp, lp, acc = loop(body, (neg, jnp.zeros((SUBLANE, tq), F32), jnp.zeros((job.dv, tq), F32)))
        results.append((acc, jnp.sum(lp, axis=0, keepdims=True)))
    return results


def _attn_kernel(*refs, jobs, n_lat, lambda_init):
    if lambda_init is None:
        qt_ref, k_ref, vt_ref, o_ref, s_ref = refs
    else:
        qt_ref, k_ref, vt_ref, lq_ref, lk_ref, sub_ref, o_ref, s_ref = refs
    n_chunks = vt_ref.shape[1]
    first_ctx = n_lat // ROW_TILE
    t = pl.program_id(1)

    def run(lo, hi):
        res = _attend(qt_ref, k_ref, vt_ref, s_ref, jobs, lo, hi)
        if lambda_init is None:
            outs = [acc / l for acc, l in res]
        else:
            lq, lk = lq_ref[...], lk_ref[...]
            lam = (jnp.exp(jnp.sum(lq[0:1] * lk[0:1], axis=-1, keepdims=True))
                   - jnp.exp(jnp.sum(lq[1:2] * lk[1:2], axis=-1, keepdims=True)) + lambda_init)
            outs = []
            for h in range(len(res) // 2):
                (a0, l0), (a1, l1) = res[2 * h], res[2 * h + 1]
                o = a0 / l0 - lam * (a1 / l1)
                o = o * lax.rsqrt(jnp.mean(o * o, axis=0, keepdims=True) + EPS)
                outs.append(o * sub_ref[...] * (1.0 - lambda_init))
        o_ref[0] = jnp.concatenate(outs, axis=0).T.astype(BF16)

    @pl.when(t < first_ctx)
    def _():
        run(0, n_chunks)

    @pl.when(t >= first_ctx)
    def _():
        run(first_ctx, n_chunks)


def _attention(qt, k, vt, extra, jobs, n_lat, lambda_init, name):
    bsz, nt, wq, tq = qt.shape
    m = k.shape[1]
    wo = sum(j.dv for j in jobs) if lambda_init is None else sum(j.dv for j in jobs) // 2
    once = pl.Buffered(1)
    whole = lambda a: pl.BlockSpec(a.shape, lambda b, t: (0,) * a.ndim)
    return pl.pallas_call(
        functools.partial(_attn_kernel, jobs=jobs, n_lat=n_lat, lambda_init=lambda_init),
        grid=(bsz, nt),
        in_specs=[pl.BlockSpec((1, 1, wq, tq), lambda b, t: (b, t, 0, 0)),
                  pl.BlockSpec((1,) + k.shape[1:], lambda b, t: (b, 0, 0), pipeline_mode=once),
                  pl.BlockSpec((1,) + vt.shape[1:], lambda b, t: (b, 0, 0, 0), pipeline_mode=once)]
                 + [whole(a) for a in extra],
        out_specs=pl.BlockSpec((1, tq, wo), lambda b, t: (b, t, 0)),
        out_shape=jax.ShapeDtypeStruct((bsz, m, wo), BF16),
        scratch_shapes=[pltpu.VMEM((2, nt, ROW_TILE, tq), F32)],
        compiler_params=_params(("parallel", "arbitrary")),
        name=name,
    )(qt, k, vt, *extra)


def _scan_chunk(a, b, carry, reverse):
    n = a.shape[0]
    rows = lax.broadcasted_iota(jnp.int32, (n, 1), 0)
    d = 1
    while d < n:
        sh = n - d if reverse else d
        valid = rows < n - d if reverse else rows >= d
        b = jnp.where(valid, a * pltpu.roll(b, sh, 0) + b, b)
        a = jnp.where(valid, a * pltpu.roll(a, sh, 0), a)
        d *= 2
    h = a * carry + b
    return h, (h[0:1] if reverse else h[n - 1:n])


def _lru_kernel(u_ref, g_ref, cw_ref, cb_ref, wa_ref, ba_ref, wx_ref, bx_ref, lam_ref, o_ref, up_ref, hf_ref, *, n_lat):
    m = u_ref.shape[1]
    lanes = u_ref.shape[2]
    n_chunks = m // LRU_CHUNK
    n_ctx_chunks = (m - n_lat) // LRU_CHUNK
    pad = SUBLANE

    up_ref[0:pad, :] = jnp.zeros((pad, lanes), F32)
    up_ref[pad + m:pad + m + pad, :] = jnp.zeros((pad, lanes), F32)

    def copy(c, _):
        r0 = pl.multiple_of(c * LRU_CHUNK, LRU_CHUNK)
        up_ref[pl.ds(pad + r0, LRU_CHUNK), :] = u_ref[0, pl.ds(r0, LRU_CHUNK), :]
        return 0

    lax.fori_loop(0, n_chunks, copy, 0)

    cw = cw_ref[...]
    cb = cb_ref[...]
    win_rows = LRU_CHUNK + 2 * pad

    def coeffs(c, d):
        r0 = pl.multiple_of(c * LRU_CHUNK, LRU_CHUNK)
        win = up_ref[pl.ds(r0, win_rows), :]
        r = r0 + lax.broadcasted_iota(jnp.int32, (LRU_CHUNK, 1), 0)
        at = lambda k: pltpu.roll(win, (win_rows - k) % win_rows, 0)[pad:pad + LRU_CHUNK]
        y = (cw[0:1] * jnp.where((r == n_lat) | (r == n_lat + 1), 0.0, at(-2))
             + cw[1:2] * jnp.where(r == n_lat, 0.0, at(-1))
             + cw[2:3] * win[pad:pad + LRU_CHUNK]
             + cw[3:4] * jnp.where(r == n_lat - 1, 0.0, at(1))
             + cb)
        yb = y.astype(BF16)
        ra = jax.nn.sigmoid(jnp.dot(yb, wa_ref[d, 0], preferred_element_type=F32) + ba_ref[d])
        ix = jax.nn.sigmoid(jnp.dot(yb, wx_ref[d, 0], preferred_element_type=F32) + bx_ref[d])
        z = -lam_ref[d]
        softplus = jnp.maximum(z, 0.0) + jnp.log(1.0 + jnp.exp(-jnp.abs(z)))
        a = jnp.exp(-LRU_C * ra * softplus)
        return a, jnp.sqrt(1.0 - a * a) * ix * y

    def fwd(s, carry):
        c = lax.rem(s + (n_chunks - n_ctx_chunks), n_chunks)
        a, b = coeffs(c, 0)
        h, carry = _scan_chunk(a, b, carry, False)
        r0 = pl.multiple_of(c * LRU_CHUNK, LRU_CHUNK)
        hf_ref[pl.ds(r0, LRU_CHUNK), :] = h
        return carry

    lax.fori_loop(0, n_chunks, fwd, jnp.zeros((1, lanes), F32))

    def bwd(s, carry):
        c = n_chunks - 1 - s
        a, b = coeffs(c, 1)
        h, carry = _scan_chunk(a, b, carry, True)
        r0 = pl.multiple_of(c * LRU_CHUNK, LRU_CHUNK)
        tot = hf_ref[pl.ds(r0, LRU_CHUNK), :] + h
        o_ref[0, pl.ds(r0, LRU_CHUNK), :] = (tot * g_ref[0, pl.ds(r0, LRU_CHUNK), :].astype(F32)).astype(BF16)
        return carry

    lax.fori_loop(0, n_chunks, bwd, jnp.zeros((1, lanes), F32))


def _lru(u, g, conv_w, conv_b, wa_bd, ba, wx_bd, bx, lam, n_lat):
    bsz, m, c = u.shape
    nl = c // LRU_LANES
    seq = lambda: pl.BlockSpec((1, m, LRU_LANES), lambda b, j: (b, 0, j))
    vec = lambda a: pl.BlockSpec(a.shape[:-1] + (LRU_LANES,), lambda b, j: (0,) * (a.ndim - 1) + (j,))
    mat = pl.BlockSpec((2, 1, LRU_LANES, LRU_LANES), lambda b, j: (0, j, 0, 0))
    return pl.pallas_call(
        functools.partial(_lru_kernel, n_lat=n_lat),
        grid=(bsz, nl),
        in_specs=[seq(), seq(), vec(conv_w), vec(conv_b), mat, vec(ba), mat, vec(bx), vec(lam)],
        out_specs=seq(),
        out_shape=jax.ShapeDtypeStruct((bsz, m, c), BF16),
        scratch_shapes=[pltpu.VMEM((m + 2 * SUBLANE, LRU_LANES), F32), pltpu.VMEM((m, LRU_LANES), F32)],
        compiler_params=_params(("parallel", "parallel")),
        name="rglru",
    )(u, g, conv_w, conv_b, wa_bd, ba, wx_bd, bx, lam)


def _route(logits_t, bias):
    n = logits_t.shape[-1]
    scores = jax.nn.sigmoid(logits_t)
    choice = scores + bias
    sub = lax.broadcasted_iota(jnp.int32, (PER_GROUP, n), 0)
    neg = -jnp.inf
    groups, gs = [], []
    for g in range(N_GROUPS):
        cg = choice[g * PER_GROUP:(g + 1) * PER_GROUP]
        m1 = jnp.max(cg, axis=0, keepdims=True)
        i1 = jnp.min(jnp.where(cg == m1, sub, PER_GROUP), axis=0, keepdims=True)
        m2 = jnp.max(jnp.where(sub == i1, neg, cg), axis=0, keepdims=True)
        groups.append(cg)
        gs.append(m1 + m2)
    masked = []
    for g in range(N_GROUPS):
        rank = jnp.zeros((1, n), jnp.int32)
        for o in range(N_GROUPS):
            if o != g:
                ahead = (gs[o] >= gs[g]) if o < g else (gs[o] > gs[g])
                rank = rank + jnp.where(ahead, 1, 0)
        masked.append(jnp.where(rank < TOPK_GROUPS, groups[g], neg))
    masked = jnp.concatenate(masked, axis=0)
    eidx = lax.broadcasted_iota(jnp.int32, (N_EXPERTS, n), 0)
    picked = jnp.zeros((N_EXPERTS, n), F32)
    for _ in range(TOP_K):
        mx = jnp.max(masked, axis=0, keepdims=True)
        first = jnp.min(jnp.where(masked == mx, eidx, N_EXPERTS), axis=0, keepdims=True)
        hit = eidx == first
        picked = jnp.where(hit, 1.0, picked)
        masked = jnp.where(hit, neg, masked)
    w = picked * scores
    return w / jnp.sum(w, axis=0, keepdims=True) * ROUTED_SCALE


def _out_proj_kernel(oa_ref, ob_ref, x_ref, mod_ref, wa_ref, wb_ref, rw_ref, rb_ref,
                     xo_ref, h_ref, gates_ref, *, n_lat, ctx_row):
    b, t = pl.program_id(0), pl.program_id(1)
    d = x_ref.shape[-1]
    row0 = t * ROW_TILE
    mv = lambda k: _mod_rows(mod_ref, ctx_row, b, k, d, row0, ROW_TILE, n_lat)
    y = (jnp.dot(oa_ref[0], wa_ref[...], preferred_element_type=F32)
         + jnp.dot(ob_ref[0], wb_ref[...], preferred_element_type=F32))
    x = x_ref[0] + mv(2) * y
    xo_ref[0] = x
    h = _rms(x) * (1.0 + mv(4)) + mv(3)
    h_ref[0] = h.astype(BF16)
    logits_t = lax.dot_general(rw_ref[...], h, (((1,), (1,)), ((), ())),
                               preferred_element_type=F32, precision=lax.Precision.HIGHEST)
    gates_t = _route(logits_t, rb_ref[...])
    gates_t = jnp.concatenate([gates_t, jnp.zeros((LANE - N_EXPERTS, ROW_TILE), F32)], axis=0)
    gates_ref[0] = gates_t.T


def _out_proj(oa, ob, xs, mod, w_out, router_wt, router_b, n_lat, ctx_row):
    bsz, m, d = xs.shape
    na, nb = oa.shape[-1], ob.shape[-1]
    wa, wb = w_out[:na], w_out[na:]
    tile = lambda n: pl.BlockSpec((1, ROW_TILE, n), lambda b, t: (b, t, 0))
    whole = lambda a: pl.BlockSpec(a.shape, lambda b, t: (0,) * a.ndim)
    return pl.pallas_call(
        functools.partial(_out_proj_kernel, n_lat=n_lat, ctx_row=ctx_row),
        grid=(bsz, m // ROW_TILE),
        in_specs=[tile(na), tile(nb), tile(d), whole(mod), whole(wa), whole(wb), whole(router_wt), whole(router_b)],
        out_specs=[tile(d), tile(d), tile(LANE)],
        out_shape=[jax.ShapeDtypeStruct((bsz, m, d), F32),
                   jax.ShapeDtypeStruct((bsz, m, d), BF16),
                   jax.ShapeDtypeStruct((bsz, m, LANE), F32)],
        compiler_params=_params(("parallel", "parallel")),
        name="out_proj_router",
    )(oa, ob, xs, mod, wa, wb, router_wt, router_b)


def _moe_kernel(h_ref, gates_ref, wgu_ref, wd_ref, sgu_ref, sd_ref, x_ref, mod_ref, fn_ref, o_ref, acc_ref,
                *, n_lat, ctx_row, rows, final):
    b, t, e = pl.program_id(0), pl.program_id(1), pl.program_id(2)
    d = x_ref.shape[-1]
    h = h_ref[0]

    def ffn(wgu, wd, gate):
        gu = jnp.dot(h, wgu, preferred_element_type=F32)
        a = _silu(gu[:, :EXPERT_FF]) * gu[:, EXPERT_FF:]
        if gate is not None:
            a = a * gate
        return jnp.dot(a.astype(BF16), wd, preferred_element_type=F32)

    @pl.when(e == 0)
    def _():
        acc_ref[...] = ffn(sgu_ref[...], sd_ref[...], None)

    lane = lax.broadcasted_iota(jnp.int32, (rows, LANE), 1)
    gate = jnp.sum(jnp.where(lane == e, gates_ref[0], 0.0), axis=1, keepdims=True)
    acc_ref[...] += ffn(wgu_ref[0], wd_ref[0], gate)

    @pl.when(e == N_EXPERTS - 1)
    def _():
        g2 = _mod_rows(mod_ref, ctx_row, b, 5, d, t * rows, rows, n_lat)
        y = x_ref[0] + g2 * acc_ref[...]
        if final:
            y = _rms(y) * fn_ref[...]
        o_ref[0] = y


def _moe(h, gates, xs, mod, wgu, wd, sgu, sd, final_norm, n_lat, ctx_row, rows, n_rows, final):
    bsz, m, d = xs.shape
    tile = lambda n: pl.BlockSpec((1, rows, n), lambda b, t, e: (b, t, 0))
    whole = lambda a: pl.BlockSpec(a.shape, lambda b, t, e: (0,) * a.ndim)
    return pl.pallas_call(
        functools.partial(_moe_kernel, n_lat=n_lat, ctx_row=ctx_row, rows=rows, final=final),
        grid=(bsz, n_rows // rows, N_EXPERTS),
        in_specs=[tile(d), tile(LANE),
                  pl.BlockSpec((1,) + wgu.shape[1:], lambda b, t, e: (e, 0, 0)),
                  pl.BlockSpec((1,) + wd.shape[1:], lambda b, t, e: (e, 0, 0)),
                  whole(sgu), whole(sd), tile(d), whole(mod), whole(final_norm)],
        out_specs=tile(d),
        out_shape=jax.ShapeDtypeStruct((bsz, n_rows, d), F32),
        scratch_shapes=[pltpu.VMEM((rows, d), F32)],
        compiler_params=_params(("parallel", "parallel", "arbitrary")),
        name="moe_final" if final else "moe",
    )(h, gates, wgu, wd, sgu, sd, xs, mod, final_norm)


def _rope_tables(n_lat, m, head_dim, width):
    pos = jnp.arange(n_lat, dtype=jnp.int32)
    row = (pos // GRID_W).astype(F32)
    col = (pos % GRID_W).astype(F32)
    half = head_dim // 2
    quarter = half // 2
    lane = jnp.arange(head_dim)
    freq = (lane % quarter).astype(F32)
    inv = ROPE_THETA ** (-(2.0 * freq) / half)
    ang = jnp.where(lane[None, :] < half, row[:, None], col[:, None]) * inv[None, :]
    low = (lane % half) < quarter
    cos = jnp.cos(ang)
    sin = jnp.sin(ang)
    slo = jnp.where(low[None, :], -sin, 0.0)
    shi = jnp.where(low[None, :], 0.0, sin)
    reps = width // head_dim
    pad = lambda a, fill: jnp.concatenate(
        [jnp.tile(a, (1, reps)), jnp.full((m - n_lat, width), fill, F32)], axis=0)
    return pad(cos, 1.0), pad(slo, 0.0), pad(shi, 0.0)


def _mla_tables(tabs16):
    outs = []
    for a, fill in zip(tabs16, (1.0, 0.0, 0.0)):
        m = a.shape[0]
        slot = jnp.concatenate([jnp.full((m, MLA_NOPE), fill, F32), a,
                                jnp.full((m, QK_SLOT - MLA_QK), fill, F32)], axis=1)
        outs.append(jnp.tile(slot, (1, MLA_HEADS)))
    return tuple(outs)


def _slots(w, heads, width):
    w = w.reshape(w.shape[0], heads, width)
    pad = jnp.zeros((w.shape[0], heads, QK_SLOT - width), w.dtype)
    return jnp.concatenate([w, pad], axis=-1).reshape(w.shape[0], heads * QK_SLOT)


def _block_diag(w):
    dirs = w.shape[0]
    eye = jnp.eye(LRU_BLOCKS, dtype=w.dtype)
    full = jnp.einsum('dnij,nm->dnimj', w, eye).reshape(dirs, LRU_WIDTH, LRU_WIDTH)
    nl = LRU_WIDTH // LRU_LANES
    return jnp.stack([full[:, j * LRU_LANES:(j + 1) * LRU_LANES, j * LRU_LANES:(j + 1) * LRU_LANES]
                      for j in range(nl)], axis=1)


def _moe_rows(n):
    for rows in (1024, 768, 512, 256):
        if n % rows == 0:
            return rows
    raise ValueError(f"token count {n} is not a multiple of {ROW_TILE}")


def kernel(x, c, ctx, c_ctx, mod_w, mod_b, ab_w_in, ab_w_out, diff_lambda_q, diff_lambda_k, diff_subln, lru_conv_w, lru_conv_b, lru_w_a, lru_b_a, lru_w_x, lru_b_x, lru_lambda, cd_w_in, cd_w_out, gqa_q_norm, gqa_k_norm, mla_q_norm, mla_kv_norm, mla_w_uq, mla_w_ukv, router_w, router_bias, exp_w_gate, exp_w_up, exp_w_down, sh_w_gate, sh_w_up, sh_w_down, final_norm):
    bsz, n_lat, d = x.shape
    n_ctx = ctx.shape[1]
    m = n_lat + n_ctx
    depth = mod_w.shape[0]
    assert depth == 2 and bsz < MOD_ROWS
    assert n_lat % ROW_TILE == 0 and n_ctx % ROW_TILE == 0 and n_lat % GRID_W == 0
    ctx_row = bsz

    xs = jnp.concatenate([x, ctx], axis=1)
    c_all = jnp.concatenate([c, c_ctx[None, :], jnp.zeros((MOD_ROWS - bsz - 1, d), F32)], axis=0)
    mods = _modulation(c_all, mod_w, mod_b)

    tabs64 = _rope_tables(n_lat, m, DIFF_HEAD_DIM, DIFF_HEADS * 2 * DIFF_HEAD_DIM)
    tabs_m = _mla_tables(_rope_tables(n_lat, m, MLA_ROPE, MLA_ROPE))

    def moe_weights(i):
        wgu = jnp.concatenate([exp_w_gate[i], exp_w_up[i]], axis=-1).astype(BF16)
        sgu = jnp.concatenate([sh_w_gate[i], sh_w_up[i]], axis=-1).astype(BF16)
        return wgu, exp_w_down[i].astype(BF16), sgu, sh_w_down[i].astype(BF16)

    fn = final_norm.reshape(1, d)

    mod = mods[0]
    qt, k, vt, u, g = _proj_ab(xs, mod, ab_w_in[0].astype(BF16), tabs64, n_lat, ctx_row)
    lambda_init = 0.8 - 0.6 * math.exp(-0.3 * 0)
    dv = 2 * DIFF_HEAD_DIM
    jobs = tuple(_Job((2 * h + j) * QK_SLOT, h, j, h * dv, dv) for h in range(DIFF_HEADS) for j in range(2))
    o_att = _attention(qt, k, vt, (diff_lambda_q[0], diff_lambda_k[0], diff_subln[0].reshape(-1, 1)),
                       jobs, n_lat, lambda_init, "diff_attention")
    o_rec = _lru(u, g, lru_conv_w[0], lru_conv_b[0].reshape(1, -1),
                 _block_diag(lru_w_a[0]).astype(BF16), lru_b_a[0].reshape(2, 1, LRU_WIDTH),
                 _block_diag(lru_w_x[0]).astype(BF16), lru_b_x[0].reshape(2, 1, LRU_WIDTH),
                 lru_lambda[0].reshape(2, 1, LRU_WIDTH), n_lat)
    xs, h2, gates = _out_proj(o_att, o_rec, xs, mod, ab_w_out[0].astype(BF16),
                              router_w[0].T, router_bias[0].reshape(-1, 1), n_lat, ctx_row)
    xs = _moe(h2, gates, xs, mod, *moe_weights(0), fn, n_lat, ctx_row, _moe_rows(m), m, False)

    mod = mods[1]
    w_in = cd_w_in[0]
    wq, wk = GQA_HEADS * GQA_HEAD_DIM, GQA_KV_HEADS * GQA_HEAD_DIM
    o_kr = wq + 2 * wk + MLA_Q_RANK + MLA_KV_RANK
    kr_slots = jnp.concatenate([jnp.zeros((d, MLA_NOPE), F32), w_in[:, o_kr:o_kr + MLA_ROPE],
                                jnp.zeros((d, QK_SLOT - MLA_QK), F32)], axis=1)
    w_in = jnp.concatenate([w_in[:, :o_kr], jnp.tile(kr_slots, (1, MLA_HEADS))], axis=1).astype(BF16)
    ukv = mla_w_ukv[0].reshape(MLA_KV_RANK, MLA_HEADS, MLA_NOPE + MLA_V)
    w_kn = _slots(ukv[:, :, :MLA_NOPE].reshape(MLA_KV_RANK, -1), MLA_HEADS, MLA_NOPE).astype(BF16)
    w_v = ukv[:, :, MLA_NOPE:].reshape(MLA_KV_RANK, MLA_HEADS * MLA_V).astype(BF16)
    w_uq = _slots(mla_w_uq[0], MLA_HEADS, MLA_QK).astype(BF16)
    lane = jnp.arange(wq)
    ones_bd = (lane[:, None] // GQA_HEAD_DIM == lane[None, :] // GQA_HEAD_DIM).astype(BF16)
    qgt, kg, vgt, qmt, km, vmt = _proj_cd(
        xs, mod, w_in, ones_bd, jnp.tile(gqa_q_norm[0], GQA_HEADS).reshape(1, -1),
        jnp.tile(gqa_k_norm[0], GQA_KV_HEADS).reshape(1, -1), mla_q_norm[0].reshape(1, -1),
        mla_kv_norm[0].reshape(1, -1), w_uq, w_kn, w_v, tabs64, tabs_m, n_lat, ctx_row)
    groups = GQA_HEADS // GQA_KV_HEADS
    jobs = tuple(_Job(h * QK_SLOT, 0, h // groups, (h // groups) * GQA_HEAD_DIM, GQA_HEAD_DIM)
                 for h in range(GQA_HEADS))
    o_gqa = _attention(qgt, kg, vgt, (), jobs, n_lat, None, "gqa_attention")
    jobs = tuple(_Job(h * QK_SLOT, h // 2, h % 2, h * MLA_V, MLA_V) for h in range(MLA_HEADS))
    o_mla = _attention(qmt, km, vmt, (), jobs, n_lat, None, "mla_attention")
    xs, h2, gates = _out_proj(o_gqa, o_mla, xs, mod, cd_w_out[0].astype(BF16),
                              router_w[1].T, router_bias[1].reshape(-1, 1), n_lat, ctx_row)
    return _moe(h2, gates, xs, mod, *moe_weights(1), fn, n_lat, ctx_row, _moe_rows(n_lat), n_lat, True)
```

```python
import functools
import math
from typing import NamedTuple

import jax
import jax.numpy as jnp
from jax import lax
from jax.experimental import pallas as pl
from jax.experimental.pallas import tpu as pltpu

F32 = jnp.float32
BF16 = jnp.bfloat16

GRID_W = 64
ROPE_THETA = 10000.0
EPS = 1e-6
DIFF_HEADS = 4
DIFF_HEAD_DIM = 64
LRU_WIDTH = 512
LRU_BLOCKS = 8
LRU_BW = LRU_WIDTH // LRU_BLOCKS
CONV_W = 4
LRU_C = 8.0
GQA_HEADS = 8
GQA_KV_HEADS = 2
GQA_HEAD_DIM = 64
MLA_HEADS = 8
MLA_Q_RANK = 256
MLA_KV_RANK = 128
MLA_NOPE = 32
MLA_ROPE = 16
MLA_V = 64
MLA_QK = MLA_NOPE + MLA_ROPE
N_EXPERTS = 64
N_GROUPS = 8
PER_GROUP = N_EXPERTS // N_GROUPS
TOPK_GROUPS = 4
TOP_K = 8
EXPERT_FF = 256
ROUTED_SCALE = 2.5

QK_SLOT = 64
LOG2E = math.log2(math.e)

ROW_TILE = 256
LRU_CHUNK = 128
LRU_LANES = 256
EXPERTS_PER_STEP = 2
LANE = 128
SUBLANE = 8
MOD_ROWS = 8
VMEM_LIMIT = 56 * 1024 * 1024


def _params(sem, vmem=VMEM_LIMIT):
    return pltpu.CompilerParams(dimension_semantics=sem, vmem_limit_bytes=vmem)


def _silu(x):
    return x * jax.nn.sigmoid(x)


def _rms(x):
    return x * lax.rsqrt(jnp.mean(x * x, axis=-1, keepdims=True) + EPS)


def _mod_rows(mod_ref, ctx_row, b, k, d, row0, rows, n_lat):
    lat = mod_ref[pl.ds(b, 1), k * d:(k + 1) * d]
    ctx = mod_ref[ctx_row:ctx_row + 1, k * d:(k + 1) * d]
    if n_lat % rows == 0:
        return jnp.where(row0 >= n_lat, ctx, lat)
    r = row0 + lax.broadcasted_iota(jnp.int32, (rows, 1), 0)
    return jnp.where(r >= n_lat, ctx, lat)


def _rope(x, cos, sin_lo, sin_hi, half):
    n = x.shape[-1]
    return x * cos + pltpu.roll(x, n - half, 1) * sin_lo + pltpu.roll(x, half, 1) * sin_hi


def _mod_kernel(c_ref, w_ref, b_ref, o_ref):
    s = _silu(c_ref[...])
    o_ref[0] = jnp.dot(s, w_ref[0], preferred_element_type=F32, precision=lax.Precision.HIGHEST) + b_ref[0]


def _modulation(c_all, mod_w, mod_b):
    depth, d, n = mod_w.shape
    tn = n // 4
    return pl.pallas_call(
        _mod_kernel,
        grid=(depth, n // tn),
        in_specs=[pl.BlockSpec((MOD_ROWS, d), lambda i, j: (0, 0)),
                  pl.BlockSpec((1, d, tn), lambda i, j: (i, 0, j)),
                  pl.BlockSpec((1, 1, tn), lambda i, j: (i, 0, j))],
        out_specs=pl.BlockSpec((1, MOD_ROWS, tn), lambda i, j: (i, 0, j)),
        out_shape=jax.ShapeDtypeStruct((depth, MOD_ROWS, n), F32),
        compiler_params=_params(("arbitrary", "arbitrary")),
        name="modulation",
    )(c_all, mod_w, mod_b.reshape(depth, 1, n))


def _proj_ab_kernel(x_ref, mod_ref, w_ref, cos_ref, slo_ref, shi_ref,
                    qt_ref, k_ref, vt_ref, u_ref, g_ref, *, n_lat, ctx_row):
    b, t = pl.program_id(0), pl.program_id(1)
    d = x_ref.shape[-1]
    row0 = t * ROW_TILE
    shift = _mod_rows(mod_ref, ctx_row, b, 0, d, row0, ROW_TILE, n_lat)
    scale = _mod_rows(mod_ref, ctx_row, b, 1, d, row0, ROW_TILE, n_lat)
    h = (_rms(x_ref[0]) * (1.0 + scale) + shift).astype(BF16)
    acc = jnp.dot(h, w_ref[...], preferred_element_type=F32)
    w = DIFF_HEADS * 2 * DIFF_HEAD_DIM
    cos, slo, shi = cos_ref[...], slo_ref[...], shi_ref[...]
    q = _rope(acc[:, 0:w], cos, slo, shi, DIFF_HEAD_DIM // 4) * (DIFF_HEAD_DIM ** -0.5 * LOG2E)
    k = _rope(acc[:, w:2 * w], cos, slo, shi, DIFF_HEAD_DIM // 4)
    qt_ref[0, 0] = q.T.astype(BF16)
    k_ref[0] = k.astype(BF16)
    vt_ref[0, 0] = acc[:, 2 * w:3 * w].T.astype(BF16)
    u_ref[0] = acc[:, 3 * w:3 * w + LRU_WIDTH]
    g_ref[0] = jax.nn.gelu(acc[:, 3 * w + LRU_WIDTH:3 * w + 2 * LRU_WIDTH]).astype(BF16)


def _proj_ab(xs, mod, w_in, tabs, n_lat, ctx_row):
    bsz, m, d = xs.shape
    nt = m // ROW_TILE
    w = DIFF_HEADS * 2 * DIFF_HEAD_DIM
    n_in = w_in.shape[1]
    tile = lambda n: pl.BlockSpec((1, ROW_TILE, n), lambda b, t: (b, t, 0))
    ttile = pl.BlockSpec((1, 1, w, ROW_TILE), lambda b, t: (b, t, 0, 0))
    tab = pl.BlockSpec((ROW_TILE, w), lambda b, t: (t, 0))
    return pl.pallas_call(
        functools.partial(_proj_ab_kernel, n_lat=n_lat, ctx_row=ctx_row),
        grid=(bsz, nt),
        in_specs=[tile(d),
                  pl.BlockSpec(mod.shape, lambda b, t: (0, 0)),
                  pl.BlockSpec((d, n_in), lambda b, t: (0, 0)),
                  tab, tab, tab],
        out_specs=[ttile, tile(w), ttile, tile(LRU_WIDTH), tile(LRU_WIDTH)],
        out_shape=[jax.ShapeDtypeStruct((bsz, nt, w, ROW_TILE), BF16),
                   jax.ShapeDtypeStruct((bsz, m, w), BF16),
                   jax.ShapeDtypeStruct((bsz, nt, w, ROW_TILE), BF16),
                   jax.ShapeDtypeStruct((bsz, m, LRU_WIDTH), F32),
                   jax.ShapeDtypeStruct((bsz, m, LRU_WIDTH), BF16)],
        compiler_params=_params(("parallel", "parallel")),
        name="proj_ab",
    )(xs, mod, w_in, *tabs)


def _group_rms(x, ones_bd):
    x2 = x * x
    hi = x2.astype(BF16)
    lo = (x2 - hi.astype(F32)).astype(BF16)
    ss = jnp.dot(hi, ones_bd, preferred_element_type=F32) + jnp.dot(lo, ones_bd, preferred_element_type=F32)
    return x * lax.rsqrt(ss * (1.0 / GQA_HEAD_DIM) + EPS)


def _proj_cd_kernel(x_ref, mod_ref, w_ref, bd_ref, qn_ref, kn_ref, cqn_ref, ckvn_ref, wuq_ref, wkn_ref, wv_ref,
                    cos_ref, slo_ref, shi_ref, cosm_ref, slom_ref, shim_ref,
                    qgt_ref, kg_ref, vgt_ref, qmt_ref, km_ref, vmt_ref, *, n_lat, ctx_row):
    b, t = pl.program_id(0), pl.program_id(1)
    d = x_ref.shape[-1]
    row0 = t * ROW_TILE
    shift = _mod_rows(mod_ref, ctx_row, b, 0, d, row0, ROW_TILE, n_lat)
    scale = _mod_rows(mod_ref, ctx_row, b, 1, d, row0, ROW_TILE, n_lat)
    h = (_rms(x_ref[0]) * (1.0 + scale) + shift).astype(BF16)
    acc = jnp.dot(h, w_ref[...], preferred_element_type=F32)
    wq = GQA_HEADS * GQA_HEAD_DIM
    wk = GQA_KV_HEADS * GQA_HEAD_DIM
    o = 0
    q = acc[:, o:o + wq]; o += wq
    k = acc[:, o:o + wk]; o += wk
    v = acc[:, o:o + wk]; o += wk
    cq = acc[:, o:o + MLA_Q_RANK]; o += MLA_Q_RANK
    ckv = acc[:, o:o + MLA_KV_RANK]; o += MLA_KV_RANK
    kr = acc[:, o:o + MLA_HEADS * QK_SLOT]
    cos, slo, shi = cos_ref[...], slo_ref[...], shi_ref[...]
    bd = bd_ref[...]
    q = (_rope(_group_rms(q, bd) * qn_ref[...], cos, slo, shi, GQA_HEAD_DIM // 4)
         * (GQA_HEAD_DIM ** -0.5 * LOG2E))
    k = _rope(_group_rms(k, bd[:wk, :wk]) * kn_ref[...], cos[:, :wk], slo[:, :wk], shi[:, :wk], GQA_HEAD_DIM // 4)
    qgt_ref[0, 0] = q.T.astype(BF16)
    kg_ref[0] = k.astype(BF16)
    vgt_ref[0, 0] = v.T.astype(BF16)
    cosm, slom, shim = cosm_ref[...], slom_ref[...], shim_ref[...]
    cqn = (_rms(cq) * cqn_ref[...]).astype(BF16)
    mq = jnp.dot(cqn, wuq_ref[...], preferred_element_type=F32)
    qmt_ref[0, 0] = (_rope(mq, cosm, slom, shim, MLA_ROPE // 4) * (MLA_QK ** -0.5 * LOG2E)).T.astype(BF16)
    ckvn = (_rms(ckv) * ckvn_ref[...]).astype(BF16)
    km = jnp.dot(ckvn, wkn_ref[...], preferred_element_type=F32) + _rope(kr, cosm, slom, shim, MLA_ROPE // 4)
    km_ref[0] = km.astype(BF16)
    vmt_ref[0, 0] = jnp.dot(ckvn, wv_ref[...], preferred_element_type=F32).T.astype(BF16)


def _proj_cd(xs, mod, w_in, bd, qn, kn, cqn, ckvn, wuq, wkn, wv, tabs, tabs_m, n_lat, ctx_row):
    bsz, m, d = xs.shape
    nt = m // ROW_TILE
    wq = GQA_HEADS * GQA_HEAD_DIM
    wk = GQA_KV_HEADS * GQA_HEAD_DIM
    wm = MLA_HEADS * QK_SLOT
    wmv = MLA_HEADS * MLA_V
    tile = lambda n: pl.BlockSpec((1, ROW_TILE, n), lambda b, t: (b, t, 0))
    ttile = lambda n: pl.BlockSpec((1, 1, n, ROW_TILE), lambda b, t: (b, t, 0, 0))
    whole = lambda a: pl.BlockSpec(a.shape, lambda b, t: (0,) * a.ndim)
    tab = pl.BlockSpec((ROW_TILE, wq), lambda b, t: (t, 0))
    tabm = pl.BlockSpec((ROW_TILE, wm), lambda b, t: (t, 0))
    return pl.pallas_call(
        functools.partial(_proj_cd_kernel, n_lat=n_lat, ctx_row=ctx_row),
        grid=(bsz, nt),
        in_specs=[tile(d), whole(mod), whole(w_in), whole(bd), whole(qn), whole(kn), whole(cqn), whole(ckvn),
                  whole(wuq), whole(wkn), whole(wv), tab, tab, tab, tabm, tabm, tabm],
        out_specs=[ttile(wq), tile(wk), ttile(wk), ttile(wm), tile(wm), ttile(wmv)],
        out_shape=[jax.ShapeDtypeStruct((bsz, nt, wq, ROW_TILE), BF16),
                   jax.ShapeDtypeStruct((bsz, m, wk), BF16),
                   jax.ShapeDtypeStruct((bsz, nt, wk, ROW_TILE), BF16),
                   jax.ShapeDtypeStruct((bsz, nt, wm, ROW_TILE), BF16),
                   jax.ShapeDtypeStruct((bsz, m, wm), BF16),
                   jax.ShapeDtypeStruct((bsz, nt, wmv, ROW_TILE), BF16)],
        compiler_params=_params(("parallel", "parallel")),
        name="proj_cd",
    )(xs, mod, w_in, bd, qn, kn, cqn, ckvn, wuq, wkn, wv, *tabs, *tabs_m)


class _Job(NamedTuple):
    q_row: int
    k_group: int
    k_half: int
    v_row: int
    dv: int


def _attend(qt_ref, qnext_ref, k_ref, vt_ref, s_ref, mp_ref, jobs, lo, hi, chained):
    assert len(jobs) % 2 == 0
    tq = qt_ref.shape[-1]
    n = hi - lo
    group = next(g for g in (11, 3, 2, 1) if n % g == 0)
    steps = n // group
    row_groups = ROW_TILE // SUBLANE

    def weights(job, ref=qt_ref):
        qh = ref[0, 0, job.q_row:job.q_row + QK_SLOT, :]
        z = jnp.zeros_like(qh)
        return jnp.concatenate([z, qh] if job.k_half else [qh, z], axis=0)

    def score(c, buf, qw, job, mp):
        r0 = pl.multiple_of(c * ROW_TILE, ROW_TILE)
        kc = k_ref[0, pl.ds(r0, ROW_TILE), job.k_group * LANE:(job.k_group + 1) * LANE]
        s = jnp.dot(kc, qw, preferred_element_type=F32)
        s_ref[buf, c] = s
        return jnp.maximum(mp, jnp.max(s.reshape(row_groups, SUBLANE, tq), axis=0))

    def value(c, buf, m, job, lp, acc):
        p = jnp.exp2(s_ref[buf, c] - m)
        lp = lp + jnp.sum(p.reshape(row_groups, SUBLANE, tq), axis=0)
        acc = acc + jnp.dot(vt_ref[0, c, job.v_row:job.v_row + job.dv, :], p.astype(BF16),
                            preferred_element_type=F32)
        return lp, acc

    def loop(body, init):
        return body(0, init) if steps == 1 else lax.fori_loop(0, steps, body, init)

    neg = jnp.full((SUBLANE, tq), -jnp.inf, F32)

    def prologue():
        qw = weights(jobs[0])

        def first(i, mp):
            for g in range(group):
                mp = score(lo + i * group + g, 0, qw, jobs[0], mp)
            return mp

        mp_ref[...] = loop(first, neg)

    if chained:
        pl.when(pl.program_id(1) == 0)(prologue)
    else:
        prologue()
    mp = mp_ref[...]
    results = []
    for j, job in enumerate(jobs):
        buf = j % 2
        m = jnp.max(mp, axis=0, keepdims=True)
        if j + 1 < len(jobs):
            nxt, qw = jobs[j + 1], weights(jobs[j + 1])
        elif chained:
            nxt, qw = jobs[0], weights(jobs[0], qnext_ref)
        else:
            nxt = None

        def body(i, carry):
            mp, lp, acc = carry
            for g in range(group):
                c = lo + i * group + g
                if nxt is not None:
                    mp = score(c, 1 - buf, qw, nxt, mp)
                lp, acc = value(c, buf, m, job, lp, acc)
            return mp, lp, acc

        mp, lp, acc = loop(body, (neg, jnp.zeros((SUBLANE, tq), F32), jnp.zeros((job.dv, tq), F32)))
        results.append((acc, jnp.sum(lp, axis=0, keepdims=True)))
    if chained:
        mp_ref[...] = mp
    return results


def _attn_kernel(*refs, jobs, n_lat, lambda_init):
    if lambda_init is None:
        qt_ref, qnext_ref, k_ref, vt_ref, o_ref, s_ref, mp_ref = refs
    else:
        qt_ref, qnext_ref, k_ref, vt_ref, lq_ref, lk_ref, sub_ref, o_ref, s_ref, mp_ref = refs
    n_chunks = vt_ref.shape[1]
    first_ctx = n_lat // ROW_TILE
    t = pl.program_id(1)

    def run(lo, hi, chained):
        res = _attend(qt_ref, qnext_ref, k_ref, vt_ref, s_ref, mp_ref, jobs, lo, hi, chained)
        if lambda_init is None:
            outs = [acc / l for acc, l in res]
        else:
            lq, lk = lq_ref[...], lk_ref[...]
            lam = (jnp.exp(jnp.sum(lq[0:1] * lk[0:1], axis=-1, keepdims=True))
                   - jnp.exp(jnp.sum(lq[1:2] * lk[1:2], axis=-1, keepdims=True)) + lambda_init)
            outs = []
            for h in range(len(res) // 2):
                (a0, l0), (a1, l1) = res[2 * h], res[2 * h + 1]
                o = a0 / l0 - lam * (a1 / l1)
                o = o * lax.rsqrt(jnp.mean(o * o, axis=0, keepdims=True) + EPS)
                outs.append(o * sub_ref[...] * (1.0 - lambda_init))
        o_ref[0] = jnp.concatenate(outs, axis=0).T.astype(BF16)

    @pl.when(t < first_ctx)
    def _():
        run(0, n_chunks, True)

    @pl.when(t >= first_ctx)
    def _():
        run(first_ctx, n_chunks, False)


def _attention(qt, k, vt, extra, jobs, n_lat, lambda_init, name):
    bsz, nt, wq, tq = qt.shape
    m = k.shape[1]
    wo = sum(j.dv for j in jobs) if lambda_init is None else sum(j.dv for j in jobs) // 2
    once = pl.Buffered(1)
    whole = lambda a: pl.BlockSpec(a.shape, lambda b, t: (0,) * a.ndim)
    return pl.pallas_call(
        functools.partial(_attn_kernel, jobs=jobs, n_lat=n_lat, lambda_init=lambda_init),
        grid=(bsz, nt),
        in_specs=[pl.BlockSpec((1, 1, wq, tq), lambda b, t: (b, t, 0, 0)),
                  pl.BlockSpec((1, 1, wq, tq), lambda b, t: (b, jnp.minimum(t + 1, nt - 1), 0, 0)),
                  pl.BlockSpec((1,) + k.shape[1:], lambda b, t: (b, 0, 0), pipeline_mode=once),
                  pl.BlockSpec((1,) + vt.shape[1:], lambda b, t: (b, 0, 0, 0), pipeline_mode=once)]
                 + [whole(a) for a in extra],
        out_specs=pl.BlockSpec((1, tq, wo), lambda b, t: (b, t, 0)),
        out_shape=jax.ShapeDtypeStruct((bsz, m, wo), BF16),
        scratch_shapes=[pltpu.VMEM((2, nt, ROW_TILE, tq), F32), pltpu.VMEM((SUBLANE, tq), F32)],
        compiler_params=_params(("arbitrary", "arbitrary")),
        name=name,
    )(qt, qt, k, vt, *extra)


def _scan_chunk(a, b, carry, reverse):
    n = a.shape[0]
    rows = lax.broadcasted_iota(jnp.int32, (n, 1), 0)
    d = 1
    while d < n:
        sh = n - d if reverse else d
        valid = rows < n - d if reverse else rows >= d
        b = jnp.where(valid, a * pltpu.roll(b, sh, 0) + b, b)
        a = jnp.where(valid, a * pltpu.roll(a, sh, 0), a)
        d *= 2
    h = a * carry + b
    return h, (h[0:1] if reverse else h[n - 1:n])


def _lru_kernel(u_ref, g_ref, cw_ref, cb_ref, wa_ref, ba_ref, wx_ref, bx_ref, lam_ref, o_ref, up_ref, hf_ref, *, n_lat):
    m = u_ref.shape[1]
    lanes = u_ref.shape[2]
    n_chunks = m // LRU_CHUNK
    n_ctx_chunks = (m - n_lat) // LRU_CHUNK
    pad = SUBLANE

    up_ref[0:pad, :] = jnp.zeros((pad, lanes), F32)
    up_ref[pad + m:pad + m + pad, :] = jnp.zeros((pad, lanes), F32)

    def copy(c, _):
        r0 = pl.multiple_of(c * LRU_CHUNK, LRU_CHUNK)
        up_ref[pl.ds(pad + r0, LRU_CHUNK), :] = u_ref[0, pl.ds(r0, LRU_CHUNK), :]
        return 0

    lax.fori_loop(0, n_chunks, copy, 0)

    cw = cw_ref[...]
    cb = cb_ref[...]
    win_rows = LRU_CHUNK + 2 * pad

    def coeffs(c, d):
        r0 = pl.multiple_of(c * LRU_CHUNK, LRU_CHUNK)
        win = up_ref[pl.ds(r0, win_rows), :]
        r = r0 + lax.broadcasted_iota(jnp.int32, (LRU_CHUNK, 1), 0)
        at = lambda k: pltpu.roll(win, (win_rows - k) % win_rows, 0)[pad:pad + LRU_CHUNK]
        y = (cw[0:1] * jnp.where((r == n_lat) | (r == n_lat + 1), 0.0, at(-2))
             + cw[1:2] * jnp.where(r == n_lat, 0.0, at(-1))
             + cw[2:3] * win[pad:pad + LRU_CHUNK]
             + cw[3:4] * jnp.where(r == n_lat - 1, 0.0, at(1))
             + cb)
        yb = y.astype(BF16)
        ra = jax.nn.sigmoid(jnp.dot(yb, wa_ref[d, 0], preferred_element_type=F32) + ba_ref[d])
        ix = jax.nn.sigmoid(jnp.dot(yb, wx_ref[d, 0], preferred_element_type=F32) + bx_ref[d])
        z = -lam_ref[d]
        softplus = jnp.maximum(z, 0.0) + jnp.log(1.0 + jnp.exp(-jnp.abs(z)))
        a = jnp.exp(-LRU_C * ra * softplus)
        return a, jnp.sqrt(1.0 - a * a) * ix * y

    def fwd(s, carry):
        c = lax.rem(s + (n_chunks - n_ctx_chunks), n_chunks)
        a, b = coeffs(c, 0)
        h, carry = _scan_chunk(a, b, carry, False)
        r0 = pl.multiple_of(c * LRU_CHUNK, LRU_CHUNK)
        hf_ref[pl.ds(r0, LRU_CHUNK), :] = h
        return carry

    lax.fori_loop(0, n_chunks, fwd, jnp.zeros((1, lanes), F32))

    def bwd(s, carry):
        c = n_chunks - 1 - s
        a, b = coeffs(c, 1)
        h, carry = _scan_chunk(a, b, carry, True)
        r0 = pl.multiple_of(c * LRU_CHUNK, LRU_CHUNK)
        tot = hf_ref[pl.ds(r0, LRU_CHUNK), :] + h
        o_ref[0, pl.ds(r0, LRU_CHUNK), :] = (tot * g_ref[0, pl.ds(r0, LRU_CHUNK), :].astype(F32)).astype(BF16)
        return carry

    lax.fori_loop(0, n_chunks, bwd, jnp.zeros((1, lanes), F32))


def _lru(u, g, conv_w, conv_b, wa_bd, ba, wx_bd, bx, lam, n_lat):
    bsz, m, c = u.shape
    nl = c // LRU_LANES
    seq = lambda: pl.BlockSpec((1, m, LRU_LANES), lambda b, j: (b, 0, j))
    vec = lambda a: pl.BlockSpec(a.shape[:-1] + (LRU_LANES,), lambda b, j: (0,) * (a.ndim - 1) + (j,))
    mat = pl.BlockSpec((2, 1, LRU_LANES, LRU_LANES), lambda b, j: (0, j, 0, 0))
    return pl.pallas_call(
        functools.partial(_lru_kernel, n_lat=n_lat),
        grid=(bsz, nl),
        in_specs=[seq(), seq(), vec(conv_w), vec(conv_b), mat, vec(ba), mat, vec(bx), vec(lam)],
        out_specs=seq(),
        out_shape=jax.ShapeDtypeStruct((bsz, m, c), BF16),
        scratch_shapes=[pltpu.VMEM((m + 2 * SUBLANE, LRU_LANES), F32), pltpu.VMEM((m, LRU_LANES), F32)],
        compiler_params=_params(("parallel", "parallel")),
        name="rglru",
    )(u, g, conv_w, conv_b, wa_bd, ba, wx_bd, bx, lam)


def _route(logits_t, bias):
    n = logits_t.shape[-1]
    scores = jax.nn.sigmoid(logits_t)
    choice = scores + bias
    sub = lax.broadcasted_iota(jnp.int32, (PER_GROUP, n), 0)
    neg = -jnp.inf
    groups, gs = [], []
    for g in range(N_GROUPS):
        cg = choice[g * PER_GROUP:(g + 1) * PER_GROUP]
        m1 = jnp.max(cg, axis=0, keepdims=True)
        i1 = jnp.min(jnp.where(cg == m1, sub, PER_GROUP), axis=0, keepdims=True)
        m2 = jnp.max(jnp.where(sub == i1, neg, cg), axis=0, keepdims=True)
        groups.append(cg)
        gs.append(m1 + m2)
    masked = []
    for g in range(N_GROUPS):
        rank = jnp.zeros((1, n), jnp.int32)
        for o in range(N_GROUPS):
            if o != g:
                ahead = (gs[o] >= gs[g]) if o < g else (gs[o] > gs[g])
                rank = rank + jnp.where(ahead, 1, 0)
        masked.append(jnp.where(rank < TOPK_GROUPS, groups[g], neg))
    masked = jnp.concatenate(masked, axis=0)
    eidx = lax.broadcasted_iota(jnp.int32, (N_EXPERTS, n), 0)
    picked = jnp.zeros((N_EXPERTS, n), F32)
    for _ in range(TOP_K):
        mx = jnp.max(masked, axis=0, keepdims=True)
        first = jnp.min(jnp.where(masked == mx, eidx, N_EXPERTS), axis=0, keepdims=True)
        hit = eidx == first
        picked = jnp.where(hit, 1.0, picked)
        masked = jnp.where(hit, neg, masked)
    w = picked * scores
    return w / jnp.sum(w, axis=0, keepdims=True) * ROUTED_SCALE


def _out_proj_kernel(oa_ref, ob_ref, x_ref, mod_ref, wa_ref, wb_ref, rw_ref, rb_ref,
                     xo_ref, h_ref, gates_ref, *, n_lat, ctx_row):
    b, t = pl.program_id(0), pl.program_id(1)
    d = x_ref.shape[-1]
    row0 = t * ROW_TILE
    mv = lambda k: _mod_rows(mod_ref, ctx_row, b, k, d, row0, ROW_TILE, n_lat)
    y = (jnp.dot(oa_ref[0], wa_ref[...], preferred_element_type=F32)
         + jnp.dot(ob_ref[0], wb_ref[...], preferred_element_type=F32))
    x = x_ref[0] + mv(2) * y
    xo_ref[0] = x
    h = _rms(x) * (1.0 + mv(4)) + mv(3)
    h_ref[0] = h.astype(BF16)
    logits_t = lax.dot_general(rw_ref[...], h, (((1,), (1,)), ((), ())),
                               preferred_element_type=F32, precision=lax.Precision.HIGHEST)
    gates_t = _route(logits_t, rb_ref[...])
    gates_t = jnp.concatenate([gates_t, jnp.zeros((LANE - N_EXPERTS, ROW_TILE), F32)], axis=0)
    gates_ref[0] = gates_t.T


def _out_proj(oa, ob, xs, mod, w_out, router_wt, router_b, n_lat, ctx_row):
    bsz, m, d = xs.shape
    na, nb = oa.shape[-1], ob.shape[-1]
    wa, wb = w_out[:na], w_out[na:]
    tile = lambda n: pl.BlockSpec((1, ROW_TILE, n), lambda b, t: (b, t, 0))
    whole = lambda a: pl.BlockSpec(a.shape, lambda b, t: (0,) * a.ndim)
    return pl.pallas_call(
        functools.partial(_out_proj_kernel, n_lat=n_lat, ctx_row=ctx_row),
        grid=(bsz, m // ROW_TILE),
        in_specs=[tile(na), tile(nb), tile(d), whole(mod), whole(wa), whole(wb), whole(router_wt), whole(router_b)],
        out_specs=[tile(d), tile(d), tile(LANE)],
        out_shape=[jax.ShapeDtypeStruct((bsz, m, d), F32),
                   jax.ShapeDtypeStruct((bsz, m, d), BF16),
                   jax.ShapeDtypeStruct((bsz, m, LANE), F32)],
        compiler_params=_params(("parallel", "parallel")),
        name="out_proj_router",
    )(oa, ob, xs, mod, wa, wb, router_wt, router_b)


def _moe_kernel(h_ref, gates_ref, wg_ref, wu_ref, wd_ref, sg_ref, su_ref, sd_ref, x_ref, mod_ref, fn_ref,
                o_ref, acc_ref, *, n_lat, ctx_row, rows, final):
    b, t, s = pl.program_id(0), pl.program_id(1), pl.program_id(2)
    d = x_ref.shape[-1]
    h = h_ref[0]

    def ffn(wg, wu, wd, gate):
        a = (_silu(jnp.dot(h, wg.astype(BF16), preferred_element_type=F32))
             * jnp.dot(h, wu.astype(BF16), preferred_element_type=F32))
        if gate is not None:
            a = a * gate
        return jnp.dot(a.astype(BF16), wd.astype(BF16), preferred_element_type=F32)

    @pl.when(s == 0)
    def _():
        acc_ref[...] = ffn(sg_ref[...], su_ref[...], sd_ref[...], None)

    lane = lax.broadcasted_iota(jnp.int32, (rows, LANE), 1)
    gates = gates_ref[0]
    for j in range(EXPERTS_PER_STEP):
        e = s * EXPERTS_PER_STEP + j
        gate = jnp.sum(jnp.where(lane == e, gates, 0.0), axis=1, keepdims=True)
        acc_ref[...] += ffn(wg_ref[j], wu_ref[j], wd_ref[j], gate)

    @pl.when(s == N_EXPERTS // EXPERTS_PER_STEP - 1)
    def _():
        g2 = _mod_rows(mod_ref, ctx_row, b, 5, d, t * rows, rows, n_lat)
        y = x_ref[0] + g2 * acc_ref[...]
        if final:
            y = _rms(y) * fn_ref[...]
        o_ref[0] = y


def _moe(h, gates, xs, mod, wg, wu, wd, sg, su, sd, final_norm, layer, n_lat, ctx_row, rows, n_rows, final):
    bsz, m, d = xs.shape
    tile = lambda n: pl.BlockSpec((1, rows, n), lambda b, t, s: (b, t, 0))
    whole = lambda a: pl.BlockSpec(a.shape, lambda b, t, s: (0,) * a.ndim, pipeline_mode=pl.Buffered(1))
    experts = lambda a: pl.BlockSpec((None, EXPERTS_PER_STEP) + a.shape[2:], lambda b, t, s: (layer, s, 0, 0))
    return pl.pallas_call(
        functools.partial(_moe_kernel, n_lat=n_lat, ctx_row=ctx_row, rows=rows, final=final),
        grid=(bsz, n_rows // rows, N_EXPERTS // EXPERTS_PER_STEP),
        in_specs=[tile(d), tile(LANE), experts(wg), experts(wu), experts(wd),
                  whole(sg), whole(su), whole(sd), tile(d), whole(mod), whole(final_norm)],
        out_specs=tile(d),
        out_shape=jax.ShapeDtypeStruct((bsz, n_rows, d), F32),
        scratch_shapes=[pltpu.VMEM((rows, d), F32)],
        compiler_params=_params(("parallel", "parallel", "arbitrary")),
        name="moe_final" if final else "moe",
    )(h, gates, wg, wu, wd, sg, su, sd, xs, mod, final_norm)


def _rope_tables(n_lat, m, head_dim, width):
    pos = jnp.arange(n_lat, dtype=jnp.int32)
    row = (pos // GRID_W).astype(F32)
    col = (pos % GRID_W).astype(F32)
    half = head_dim // 2
    quarter = half // 2
    lane = jnp.arange(head_dim)
    freq = (lane % quarter).astype(F32)
    inv = ROPE_THETA ** (-(2.0 * freq) / half)
    ang = jnp.where(lane[None, :] < half, row[:, None], col[:, None]) * inv[None, :]
    low = (lane % half) < quarter
    cos = jnp.cos(ang)
    sin = jnp.sin(ang)
    slo = jnp.where(low[None, :], -sin, 0.0)
    shi = jnp.where(low[None, :], 0.0, sin)
    reps = width // head_dim
    pad = lambda a, fill: jnp.concatenate(
        [jnp.tile(a, (1, reps)), jnp.full((m - n_lat, width), fill, F32)], axis=0)
    return pad(cos, 1.0), pad(slo, 0.0), pad(shi, 0.0)


def _mla_tables(tabs16):
    outs = []
    for a, fill in zip(tabs16, (1.0, 0.0, 0.0)):
        m = a.shape[0]
        slot = jnp.concatenate([jnp.full((m, MLA_NOPE), fill, F32), a,
                                jnp.full((m, QK_SLOT - MLA_QK), fill, F32)], axis=1)
        outs.append(jnp.tile(slot, (1, MLA_HEADS)))
    return tuple(outs)


def _slots(w, heads, width):
    w = w.reshape(w.shape[0], heads, width)
    pad = jnp.zeros((w.shape[0], heads, QK_SLOT - width), w.dtype)
    return jnp.concatenate([w, pad], axis=-1).reshape(w.shape[0], heads * QK_SLOT)


def _block_diag(w):
    dirs = w.shape[0]
    eye = jnp.eye(LRU_BLOCKS, dtype=w.dtype)
    full = jnp.einsum('dnij,nm->dnimj', w, eye).reshape(dirs, LRU_WIDTH, LRU_WIDTH)
    nl = LRU_WIDTH // LRU_LANES
    return jnp.stack([full[:, j * LRU_LANES:(j + 1) * LRU_LANES, j * LRU_LANES:(j + 1) * LRU_LANES]
                      for j in range(nl)], axis=1)


def _moe_rows(n):
    for rows in (1024, 768, 512, 256):
        if n % rows == 0:
            return rows
    raise ValueError(f"token count {n} is not a multiple of {ROW_TILE}")


def kernel(x, c, ctx, c_ctx, mod_w, mod_b, ab_w_in, ab_w_out, diff_lambda_q, diff_lambda_k, diff_subln, lru_conv_w, lru_conv_b, lru_w_a, lru_b_a, lru_w_x, lru_b_x, lru_lambda, cd_w_in, cd_w_out, gqa_q_norm, gqa_k_norm, mla_q_norm, mla_kv_norm, mla_w_uq, mla_w_ukv, router_w, router_bias, exp_w_gate, exp_w_up, exp_w_down, sh_w_gate, sh_w_up, sh_w_down, final_norm):
    bsz, n_lat, d = x.shape
    n_ctx = ctx.shape[1]
    m = n_lat + n_ctx
    depth = mod_w.shape[0]
    assert depth == 2 and bsz < MOD_ROWS
    assert n_lat % ROW_TILE == 0 and n_ctx % ROW_TILE == 0 and n_lat % GRID_W == 0
    ctx_row = bsz

    xs = jnp.concatenate([x, ctx], axis=1)
    c_all = jnp.concatenate([c, c_ctx[None, :], jnp.zeros((MOD_ROWS - bsz - 1, d), F32)], axis=0)
    mods = _modulation(c_all, mod_w, mod_b)

    tabs64 = _rope_tables(n_lat, m, DIFF_HEAD_DIM, DIFF_HEADS * 2 * DIFF_HEAD_DIM)
    tabs_m = _mla_tables(_rope_tables(n_lat, m, MLA_ROPE, MLA_ROPE))

    def moe_weights(i):
        return exp_w_gate, exp_w_up, exp_w_down, sh_w_gate[i], sh_w_up[i], sh_w_down[i]

    fn = final_norm.reshape(1, d)

    mod = mods[0]
    qt, k, vt, u, g = _proj_ab(xs, mod, ab_w_in[0].astype(BF16), tabs64, n_lat, ctx_row)
    lambda_init = 0.8 - 0.6 * math.exp(-0.3 * 0)
    dv = 2 * DIFF_HEAD_DIM
    jobs = tuple(_Job((2 * h + j) * QK_SLOT, h, j, h * dv, dv) for h in range(DIFF_HEADS) for j in range(2))
    o_att = _attention(qt, k, vt, (diff_lambda_q[0], diff_lambda_k[0], diff_subln[0].reshape(-1, 1)),
                       jobs, n_lat, lambda_init, "diff_attention")
    o_rec = _lru(u, g, lru_conv_w[0], lru_conv_b[0].reshape(1, -1),
                 _block_diag(lru_w_a[0]).astype(BF16), lru_b_a[0].reshape(2, 1, LRU_WIDTH),
                 _block_diag(lru_w_x[0]).astype(BF16), lru_b_x[0].reshape(2, 1, LRU_WIDTH),
                 lru_lambda[0].reshape(2, 1, LRU_WIDTH), n_lat)
    xs, h2, gates = _out_proj(o_att, o_rec, xs, mod, ab_w_out[0].astype(BF16),
                              router_w[0].T, router_bias[0].reshape(-1, 1), n_lat, ctx_row)
    xs = _moe(h2, gates, xs, mod, *moe_weights(0), fn, 0, n_lat, ctx_row, _moe_rows(m), m, False)

    mod = mods[1]
    w_in = cd_w_in[0]
    wq, wk = GQA_HEADS * GQA_HEAD_DIM, GQA_KV_HEADS * GQA_HEAD_DIM
    o_kr = wq + 2 * wk + MLA_Q_RANK + MLA_KV_RANK
    kr_slots = jnp.concatenate([jnp.zeros((d, MLA_NOPE), F32), w_in[:, o_kr:o_kr + MLA_ROPE],
                                jnp.zeros((d, QK_SLOT - MLA_QK), F32)], axis=1)
    w_in = jnp.concatenate([w_in[:, :o_kr], jnp.tile(kr_slots, (1, MLA_HEADS))], axis=1).astype(BF16)
    ukv = mla_w_ukv[0].reshape(MLA_KV_RANK, MLA_HEADS, MLA_NOPE + MLA_V)
    w_kn = _slots(ukv[:, :, :MLA_NOPE].reshape(MLA_KV_RANK, -1), MLA_HEADS, MLA_NOPE).astype(BF16)
    w_v = ukv[:, :, MLA_NOPE:].reshape(MLA_KV_RANK, MLA_HEADS * MLA_V).astype(BF16)
    w_uq = _slots(mla_w_uq[0], MLA_HEADS, MLA_QK).astype(BF16)
    lane = jnp.arange(wq)
    ones_bd = (lane[:, None] // GQA_HEAD_DIM == lane[None, :] // GQA_HEAD_DIM).astype(BF16)
    qgt, kg, vgt, qmt, km, vmt = _proj_cd(
        xs, mod, w_in, ones_bd, jnp.tile(gqa_q_norm[0], GQA_HEADS).reshape(1, -1),
        jnp.tile(gqa_k_norm[0], GQA_KV_HEADS).reshape(1, -1), mla_q_norm[0].reshape(1, -1),
        mla_kv_norm[0].reshape(1, -1), w_uq, w_kn, w_v, tabs64, tabs_m, n_lat, ctx_row)
    groups = GQA_HEADS // GQA_KV_HEADS
    jobs = tuple(_Job(h * QK_SLOT, 0, h // groups, (h // groups) * GQA_HEAD_DIM, GQA_HEAD_DIM)
                 for h in range(GQA_HEADS))
    o_gqa = _attention(qgt, kg, vgt, (), jobs, n_lat, None, "gqa_attention")
    jobs = tuple(_Job(h * QK_SLOT, h // 2, h % 2, h * MLA_V, MLA_V) for h in range(MLA_HEADS))
    o_mla = _attention(qmt, km, vmt, (), jobs, n_lat, None, "mla_attention")
    xs, h2, gates = _out_proj(o_gqa, o_mla, xs, mod, cd_w_out[0].astype(BF16),
                              router_w[1].T, router_bias[1].reshape(-1, 1), n_lat, ctx_row)
    return _moe(h2, gates, xs, mod, *moe_weights(1), fn, 1, n_lat, ctx_row, _moe_rows(n_lat), n_lat, True)
```

```python
import functools
import math
from typing import NamedTuple

import jax
import jax.numpy as jnp
from jax import lax
from jax.experimental import pallas as pl
from jax.experimental.pallas import tpu as pltpu

F32 = jnp.float32
BF16 = jnp.bfloat16

GRID_W = 64
ROPE_THETA = 10000.0
EPS = 1e-6
DIFF_HEADS = 4
DIFF_HEAD_DIM = 64
LRU_WIDTH = 512
LRU_BLOCKS = 8
LRU_BW = LRU_WIDTH // LRU_BLOCKS
CONV_W = 4
LRU_C = 8.0
GQA_HEADS = 8
GQA_KV_HEADS = 2
GQA_HEAD_DIM = 64
MLA_HEADS = 8
MLA_Q_RANK = 256
MLA_KV_RANK = 128
MLA_NOPE = 32
MLA_ROPE = 16
MLA_V = 64
MLA_QK = MLA_NOPE + MLA_ROPE
N_EXPERTS = 64
N_GROUPS = 8
PER_GROUP = N_EXPERTS // N_GROUPS
TOPK_GROUPS = 4
TOP_K = 8
EXPERT_FF = 256
ROUTED_SCALE = 2.5

QK_SLOT = 64
LOG2E = math.log2(math.e)

ROW_TILE = 256
LRU_CHUNK = 128
LRU_LANES = 256
EXPERTS_PER_STEP = 2
LANE = 128
SUBLANE = 8
MOD_ROWS = 8
VMEM_LIMIT = 56 * 1024 * 1024
MOE_VMEM_LIMIT = 62 * 1024 * 1024
MOE_TILE = 1024
MOE_CAP = 160
SEG_ALIGN = 16
MOE_SLOTS = MOE_TILE * TOP_K + N_EXPERTS * SEG_ALIGN
COMBINE_BLOCK = 1024


def _params(sem, vmem=VMEM_LIMIT):
    return pltpu.CompilerParams(dimension_semantics=sem, vmem_limit_bytes=vmem)


def _silu(x):
    return x * jax.nn.sigmoid(x)


def _rms(x):
    return x * lax.rsqrt(jnp.mean(x * x, axis=-1, keepdims=True) + EPS)


def _mod_rows(mod_ref, ctx_row, b, k, d, row0, rows, n_lat):
    lat = mod_ref[pl.ds(b, 1), k * d:(k + 1) * d]
    ctx = mod_ref[ctx_row:ctx_row + 1, k * d:(k + 1) * d]
    if n_lat % rows == 0:
        return jnp.where(row0 >= n_lat, ctx, lat)
    r = row0 + lax.broadcasted_iota(jnp.int32, (rows, 1), 0)
    return jnp.where(r >= n_lat, ctx, lat)


def _rope(x, cos, sin_lo, sin_hi, half):
    n = x.shape[-1]
    return x * cos + pltpu.roll(x, n - half, 1) * sin_lo + pltpu.roll(x, half, 1) * sin_hi


def _mod_kernel(c_ref, w_ref, b_ref, o_ref):
    s = _silu(c_ref[...])
    o_ref[0] = jnp.dot(s, w_ref[0], preferred_element_type=F32, precision=lax.Precision.HIGHEST) + b_ref[0]


def _modulation(c_all, mod_w, mod_b):
    depth, d, n = mod_w.shape
    tn = n // 4
    return pl.pallas_call(
        _mod_kernel,
        grid=(depth, n // tn),
        in_specs=[pl.BlockSpec((MOD_ROWS, d), lambda i, j: (0, 0)),
                  pl.BlockSpec((1, d, tn), lambda i, j: (i, 0, j)),
                  pl.BlockSpec((1, 1, tn), lambda i, j: (i, 0, j))],
        out_specs=pl.BlockSpec((1, MOD_ROWS, tn), lambda i, j: (i, 0, j)),
        out_shape=jax.ShapeDtypeStruct((depth, MOD_ROWS, n), F32),
        compiler_params=_params(("arbitrary", "arbitrary")),
        name="modulation",
    )(c_all, mod_w, mod_b.reshape(depth, 1, n))


def _proj_ab_kernel(x_ref, mod_ref, w_ref, cos_ref, slo_ref, shi_ref,
                    qt_ref, k_ref, vt_ref, u_ref, g_ref, *, n_lat, ctx_row):
    b, t = pl.program_id(0), pl.program_id(1)
    d = x_ref.shape[-1]
    row0 = t * ROW_TILE
    shift = _mod_rows(mod_ref, ctx_row, b, 0, d, row0, ROW_TILE, n_lat)
    scale = _mod_rows(mod_ref, ctx_row, b, 1, d, row0, ROW_TILE, n_lat)
    h = (_rms(x_ref[0]) * (1.0 + scale) + shift).astype(BF16)
    acc = jnp.dot(h, w_ref[...], preferred_element_type=F32)
    w = DIFF_HEADS * 2 * DIFF_HEAD_DIM
    cos, slo, shi = cos_ref[...], slo_ref[...], shi_ref[...]
    q = _rope(acc[:, 0:w], cos, slo, shi, DIFF_HEAD_DIM // 4) * (DIFF_HEAD_DIM ** -0.5 * LOG2E)
    k = _rope(acc[:, w:2 * w], cos, slo, shi, DIFF_HEAD_DIM // 4)
    qt_ref[0, 0] = q.T.astype(BF16)
    k_ref[0] = k.astype(BF16)
    vt_ref[0, 0] = acc[:, 2 * w:3 * w].T.astype(BF16)
    u_ref[0] = acc[:, 3 * w:3 * w + LRU_WIDTH]
    g_ref[0] = jax.nn.gelu(acc[:, 3 * w + LRU_WIDTH:3 * w + 2 * LRU_WIDTH]).astype(BF16)


def _proj_ab(xs, mod, w_in, tabs, n_lat, ctx_row):
    bsz, m, d = xs.shape
    nt = m // ROW_TILE
    w = DIFF_HEADS * 2 * DIFF_HEAD_DIM
    n_in = w_in.shape[1]
    tile = lambda n: pl.BlockSpec((1, ROW_TILE, n), lambda b, t: (b, t, 0))
    ttile = pl.BlockSpec((1, 1, w, ROW_TILE), lambda b, t: (b, t, 0, 0))
    tab = pl.BlockSpec((ROW_TILE, w), lambda b, t: (t, 0))
    return pl.pallas_call(
        functools.partial(_proj_ab_kernel, n_lat=n_lat, ctx_row=ctx_row),
        grid=(bsz, nt),
        in_specs=[tile(d),
                  pl.BlockSpec(mod.shape, lambda b, t: (0, 0)),
                  pl.BlockSpec((d, n_in), lambda b, t: (0, 0)),
                  tab, tab, tab],
        out_specs=[ttile, tile(w), ttile, tile(LRU_WIDTH), tile(LRU_WIDTH)],
        out_shape=[jax.ShapeDtypeStruct((bsz, nt, w, ROW_TILE), BF16),
                   jax.ShapeDtypeStruct((bsz, m, w), BF16),
                   jax.ShapeDtypeStruct((bsz, nt, w, ROW_TILE), BF16),
                   jax.ShapeDtypeStruct((bsz, m, LRU_WIDTH), F32),
                   jax.ShapeDtypeStruct((bsz, m, LRU_WIDTH), BF16)],
        compiler_params=_params(("parallel", "parallel")),
        name="proj_ab",
    )(xs, mod, w_in, *tabs)


def _group_rms(x, ones_bd):
    x2 = x * x
    hi = x2.astype(BF16)
    lo = (x2 - hi.astype(F32)).astype(BF16)
    ss = jnp.dot(hi, ones_bd, preferred_element_type=F32) + jnp.dot(lo, ones_bd, preferred_element_type=F32)
    return x * lax.rsqrt(ss * (1.0 / GQA_HEAD_DIM) + EPS)


def _proj_cd_kernel(x_ref, mod_ref, w_ref, bd_ref, qn_ref, kn_ref, cqn_ref, ckvn_ref, wuq_ref, wkn_ref, wv_ref,
                    cos_ref, slo_ref, shi_ref, cosm_ref, slom_ref, shim_ref,
                    qgt_ref, kg_ref, vgt_ref, qmt_ref, km_ref, vmt_ref, *, n_lat, ctx_row):
    b, t = pl.program_id(0), pl.program_id(1)
    d = x_ref.shape[-1]
    row0 = t * ROW_TILE
    shift = _mod_rows(mod_ref, ctx_row, b, 0, d, row0, ROW_TILE, n_lat)
    scale = _mod_rows(mod_ref, ctx_row, b, 1, d, row0, ROW_TILE, n_lat)
    h = (_rms(x_ref[0]) * (1.0 + scale) + shift).astype(BF16)
    acc = jnp.dot(h, w_ref[...], preferred_element_type=F32)
    wq = GQA_HEADS * GQA_HEAD_DIM
    wk = GQA_KV_HEADS * GQA_HEAD_DIM
    o = 0
    q = acc[:, o:o + wq]; o += wq
    k = acc[:, o:o + wk]; o += wk
    v = acc[:, o:o + wk]; o += wk
    cq = acc[:, o:o + MLA_Q_RANK]; o += MLA_Q_RANK
    ckv = acc[:, o:o + MLA_KV_RANK]; o += MLA_KV_RANK
    kr = acc[:, o:o + MLA_HEADS * QK_SLOT]
    cos, slo, shi = cos_ref[...], slo_ref[...], shi_ref[...]
    bd = bd_ref[...]
    q = (_rope(_group_rms(q, bd) * qn_ref[...], cos, slo, shi, GQA_HEAD_DIM // 4)
         * (GQA_HEAD_DIM ** -0.5 * LOG2E))
    k = _rope(_group_rms(k, bd[:wk, :wk]) * kn_ref[...], cos[:, :wk], slo[:, :wk], shi[:, :wk], GQA_HEAD_DIM // 4)
    qgt_ref[0, 0] = q.T.astype(BF16)
    kg_ref[0] = k.astype(BF16)
    vgt_ref[0, 0] = v.T.astype(BF16)
    cosm, slom, shim = cosm_ref[...], slom_ref[...], shim_ref[...]
    cqn = (_rms(cq) * cqn_ref[...]).astype(BF16)
    mq = jnp.dot(cqn, wuq_ref[...], preferred_element_type=F32)
    qmt_ref[0, 0] = (_rope(mq, cosm, slom, shim, MLA_ROPE // 4) * (MLA_QK ** -0.5 * LOG2E)).T.astype(BF16)
    ckvn = (_rms(ckv) * ckvn_ref[...]).astype(BF16)
    km = jnp.dot(ckvn, wkn_ref[...], preferred_element_type=F32) + _rope(kr, cosm, slom, shim, MLA_ROPE // 4)
    km_ref[0] = km.astype(BF16)
    vmt_ref[0, 0] = jnp.dot(ckvn, wv_ref[...], preferred_element_type=F32).T.astype(BF16)


def _proj_cd(xs, mod, w_in, bd, qn, kn, cqn, ckvn, wuq, wkn, wv, tabs, tabs_m, n_lat, ctx_row):
    bsz, m, d = xs.shape
    nt = m // ROW_TILE
    wq = GQA_HEADS * GQA_HEAD_DIM
    wk = GQA_KV_HEADS * GQA_HEAD_DIM
    wm = MLA_HEADS * QK_SLOT
    wmv = MLA_HEADS * MLA_V
    tile = lambda n: pl.BlockSpec((1, ROW_TILE, n), lambda b, t: (b, t, 0))
    ttile = lambda n: pl.BlockSpec((1, 1, n, ROW_TILE), lambda b, t: (b, t, 0, 0))
    whole = lambda a: pl.BlockSpec(a.shape, lambda b, t: (0,) * a.ndim)
    tab = pl.BlockSpec((ROW_TILE, wq), lambda b, t: (t, 0))
    tabm = pl.BlockSpec((ROW_TILE, wm), lambda b, t: (t, 0))
    return pl.pallas_call(
        functools.partial(_proj_cd_kernel, n_lat=n_lat, ctx_row=ctx_row),
        grid=(bsz, nt),
        in_specs=[tile(d), whole(mod), whole(w_in), whole(bd), whole(qn), whole(kn), whole(cqn), whole(ckvn),
                  whole(wuq), whole(wkn), whole(wv), tab, tab, tab, tabm, tabm, tabm],
        out_specs=[ttile(wq), tile(wk), ttile(wk), ttile(wm), tile(wm), ttile(wmv)],
        out_shape=[jax.ShapeDtypeStruct((bsz, nt, wq, ROW_TILE), BF16),
                   jax.ShapeDtypeStruct((bsz, m, wk), BF16),
                   jax.ShapeDtypeStruct((bsz, nt, wk, ROW_TILE), BF16),
                   jax.ShapeDtypeStruct((bsz, nt, wm, ROW_TILE), BF16),
                   jax.ShapeDtypeStruct((bsz, m, wm), BF16),
                   jax.ShapeDtypeStruct((bsz, nt, wmv, ROW_TILE), BF16)],
        compiler_params=_params(("parallel", "parallel")),
        name="proj_cd",
    )(xs, mod, w_in, bd, qn, kn, cqn, ckvn, wuq, wkn, wv, *tabs, *tabs_m)


class _Job(NamedTuple):
    q_row: int
    k_group: int
    k_half: int
    v_row: int
    dv: int


def _attend(qt_ref, qnext_ref, k_ref, vt_ref, s_ref, mp_ref, jobs, lo, hi, chained):
    assert len(jobs) % 2 == 0
    tq = qt_ref.shape[-1]
    n = hi - lo
    group = next(g for g in (11, 3, 2, 1) if n % g == 0)
    steps = n // group
    row_groups = ROW_TILE // SUBLANE

    def weights(job, ref=qt_ref):
        qh = ref[0, 0, job.q_row:job.q_row + QK_SLOT, :]
        z = jnp.zeros_like(qh)
        return jnp.concatenate([z, qh] if job.k_half else [qh, z], axis=0)

    def score(c, buf, qw, job, mp):
        r0 = pl.multiple_of(c * ROW_TILE, ROW_TILE)
        kc = k_ref[0, pl.ds(r0, ROW_TILE), job.k_group * LANE:(job.k_group + 1) * LANE]
        s = jnp.dot(kc, qw, preferred_element_type=F32)
        s_ref[buf, c] = s
        return jnp.maximum(mp, jnp.max(s.reshape(row_groups, SUBLANE, tq), axis=0))

    def value(c, buf, m, job, lp, acc):
        p = jnp.exp2(s_ref[buf, c] - m)
        lp = lp + jnp.sum(p.reshape(row_groups, SUBLANE, tq), axis=0)
        acc = acc + jnp.dot(vt_ref[0, c, job.v_row:job.v_row + job.dv, :], p.astype(BF16),
                            preferred_element_type=F32)
        return lp, acc

    def loop(body, init):
        return body(0, init) if steps == 1 else lax.fori_loop(0, steps, body, init)

    neg = jnp.full((SUBLANE, tq), -jnp.inf, F32)

    def prologue():
        qw = weights(jobs[0])

        def first(i, mp):
            for g in range(group):
                mp = score(lo + i * group + g, 0, qw, jobs[0], mp)
            return mp

        mp_ref[...] = loop(first, neg)

    if chained:
        pl.when(pl.program_id(1) == 0)(prologue)
    else:
        prologue()
    mp = mp_ref[...]
    results = []
    for j, job in enumerate(jobs):
        buf = j % 2
        m = jnp.max(mp, axis=0, keepdims=True)
        if j + 1 < len(jobs):
            nxt, qw = jobs[j + 1], weights(jobs[j + 1])
        elif chained:
            nxt, qw = jobs[0], weights(jobs[0], qnext_ref)
        else:
            nxt = None

        def body(i, carry):
            mp, lp, acc = carry
            for g in range(group):
                c = lo + i * group + g
                if nxt is not None:
                    mp = score(c, 1 - buf, qw, nxt, mp)
                lp, acc = value(c, buf, m, job, lp, acc)
            return mp, lp, acc

        mp, lp, acc = loop(body, (neg, jnp.zeros((SUBLANE, tq), F32), jnp.zeros((job.dv, tq), F32)))
        results.append((acc, jnp.sum(lp, axis=0, keepdims=True)))
    if chained:
        mp_ref[...] = mp
    return results


def _attn_kernel(*refs, jobs, n_lat, lambda_init):
    if lambda_init is None:
        qt_ref, qnext_ref, k_ref, vt_ref, o_ref, s_ref, mp_ref = refs
    else:
        qt_ref, qnext_ref, k_ref, vt_ref, lq_ref, lk_ref, sub_ref, o_ref, s_ref, mp_ref = refs
    n_chunks = vt_ref.shape[1]
    first_ctx = n_lat // ROW_TILE
    t = pl.program_id(1)

    def run(lo, hi, chained):
        res = _attend(qt_ref, qnext_ref, k_ref, vt_ref, s_ref, mp_ref, jobs, lo, hi, chained)
        if lambda_init is None:
            outs = [acc / l for acc, l in res]
        else:
            lq, lk = lq_ref[...], lk_ref[...]
            lam = (jnp.exp(jnp.sum(lq[0:1] * lk[0:1], axis=-1, keepdims=True))
                   - jnp.exp(jnp.sum(lq[1:2] * lk[1:2], axis=-1, keepdims=True)) + lambda_init)
            outs = []
            for h in range(len(res) // 2):
                (a0, l0), (a1, l1) = res[2 * h], res[2 * h + 1]
                o = a0 / l0 - lam * (a1 / l1)
                o = o * lax.rsqrt(jnp.mean(o * o, axis=0, keepdims=True) + EPS)
                outs.append(o * sub_ref[...] * (1.0 - lambda_init))
        o_ref[0] = jnp.concatenate(outs, axis=0).T.astype(BF16)

    @pl.when(t < first_ctx)
    def _():
        run(0, n_chunks, True)

    @pl.when(t >= first_ctx)
    def _():
        run(first_ctx, n_chunks, False)


def _attention(qt, k, vt, extra, jobs, n_lat, lambda_init, name):
    bsz, nt, wq, tq = qt.shape
    m = k.shape[1]
    wo = sum(j.dv for j in jobs) if lambda_init is None else sum(j.dv for j in jobs) // 2
    once = pl.Buffered(1)
    whole = lambda a: pl.BlockSpec(a.shape, lambda b, t: (0,) * a.ndim)
    return pl.pallas_call(
        functools.partial(_attn_kernel, jobs=jobs, n_lat=n_lat, lambda_init=lambda_init),
        grid=(bsz, nt),
        in_specs=[pl.BlockSpec((1, 1, wq, tq), lambda b, t: (b, t, 0, 0)),
                  pl.BlockSpec((1, 1, wq, tq), lambda b, t: (b, jnp.minimum(t + 1, nt - 1), 0, 0)),
                  pl.BlockSpec((1,) + k.shape[1:], lambda b, t: (b, 0, 0), pipeline_mode=once),
                  pl.BlockSpec((1,) + vt.shape[1:], lambda b, t: (b, 0, 0, 0), pipeline_mode=once)]
                 + [whole(a) for a in extra],
        out_specs=pl.BlockSpec((1, tq, wo), lambda b, t: (b, t, 0)),
        out_shape=jax.ShapeDtypeStruct((bsz, m, wo), BF16),
        scratch_shapes=[pltpu.VMEM((2, nt, ROW_TILE, tq), F32), pltpu.VMEM((SUBLANE, tq), F32)],
        compiler_params=_params(("arbitrary", "arbitrary")),
        name=name,
    )(qt, qt, k, vt, *extra)


def _scan_chunk(a, b, carry, reverse):
    n = a.shape[0]
    rows = lax.broadcasted_iota(jnp.int32, (n, 1), 0)
    d = 1
    while d < n:
        sh = n - d if reverse else d
        valid = rows < n - d if reverse else rows >= d
        b = jnp.where(valid, a * pltpu.roll(b, sh, 0) + b, b)
        a = jnp.where(valid, a * pltpu.roll(a, sh, 0), a)
        d *= 2
    h = a * carry + b
    return h, (h[0:1] if reverse else h[n - 1:n])


def _lru_kernel(u_ref, g_ref, cw_ref, cb_ref, wa_ref, ba_ref, wx_ref, bx_ref, lam_ref, o_ref, up_ref, hf_ref, *, n_lat):
    m = u_ref.shape[1]
    lanes = u_ref.shape[2]
    n_chunks = m // LRU_CHUNK
    n_ctx_chunks = (m - n_lat) // LRU_CHUNK
    pad = SUBLANE

    up_ref[0:pad, :] = jnp.zeros((pad, lanes), F32)
    up_ref[pad + m:pad + m + pad, :] = jnp.zeros((pad, lanes), F32)

    def copy(c, _):
        r0 = pl.multiple_of(c * LRU_CHUNK, LRU_CHUNK)
        up_ref[pl.ds(pad + r0, LRU_CHUNK), :] = u_ref[0, pl.ds(r0, LRU_CHUNK), :]
        return 0

    lax.fori_loop(0, n_chunks, copy, 0)

    cw = cw_ref[...]
    cb = cb_ref[...]
    win_rows = LRU_CHUNK + 2 * pad

    def coeffs(c, d):
        r0 = pl.multiple_of(c * LRU_CHUNK, LRU_CHUNK)
        win = up_ref[pl.ds(r0, win_rows), :]
        r = r0 + lax.broadcasted_iota(jnp.int32, (LRU_CHUNK, 1), 0)
        at = lambda k: pltpu.roll(win, (win_rows - k) % win_rows, 0)[pad:pad + LRU_CHUNK]
        y = (cw[0:1] * jnp.where((r == n_lat) | (r == n_lat + 1), 0.0, at(-2))
             + cw[1:2] * jnp.where(r == n_lat, 0.0, at(-1))
             + cw[2:3] * win[pad:pad + LRU_CHUNK]
             + cw[3:4] * jnp.where(r == n_lat - 1, 0.0, at(1))
             + cb)
        yb = y.astype(BF16)
        ra = jax.nn.sigmoid(jnp.dot(yb, wa_ref[d, 0], preferred_element_type=F32) + ba_ref[d])
        ix = jax.nn.sigmoid(jnp.dot(yb, wx_ref[d, 0], preferred_element_type=F32) + bx_ref[d])
        z = -lam_ref[d]
        softplus = jnp.maximum(z, 0.0) + jnp.log(1.0 + jnp.exp(-jnp.abs(z)))
        a = jnp.exp(-LRU_C * ra * softplus)
        return a, jnp.sqrt(1.0 - a * a) * ix * y

    def fwd(s, carry):
        c = lax.rem(s + (n_chunks - n_ctx_chunks), n_chunks)
        a, b = coeffs(c, 0)
        h, carry = _scan_chunk(a, b, carry, False)
        r0 = pl.multiple_of(c * LRU_CHUNK, LRU_CHUNK)
        hf_ref[pl.ds(r0, LRU_CHUNK), :] = h
        return carry

    lax.fori_loop(0, n_chunks, fwd, jnp.zeros((1, lanes), F32))

    def bwd(s, carry):
        c = n_chunks - 1 - s
        a, b = coeffs(c, 1)
        h, carry = _scan_chunk(a, b, carry, True)
        r0 = pl.multiple_of(c * LRU_CHUNK, LRU_CHUNK)
        tot = hf_ref[pl.ds(r0, LRU_CHUNK), :] + h
        o_ref[0, pl.ds(r0, LRU_CHUNK), :] = (tot * g_ref[0, pl.ds(r0, LRU_CHUNK), :].astype(F32)).astype(BF16)
        return carry

    lax.fori_loop(0, n_chunks, bwd, jnp.zeros((1, lanes), F32))


def _lru(u, g, conv_w, conv_b, wa_bd, ba, wx_bd, bx, lam, n_lat):
    bsz, m, c = u.shape
    nl = c // LRU_LANES
    seq = lambda: pl.BlockSpec((1, m, LRU_LANES), lambda b, j: (b, 0, j))
    vec = lambda a: pl.BlockSpec(a.shape[:-1] + (LRU_LANES,), lambda b, j: (0,) * (a.ndim - 1) + (j,))
    mat = pl.BlockSpec((2, 1, LRU_LANES, LRU_LANES), lambda b, j: (0, j, 0, 0))
    return pl.pallas_call(
        functools.partial(_lru_kernel, n_lat=n_lat),
        grid=(bsz, nl),
        in_specs=[seq(), seq(), vec(conv_w), vec(conv_b), mat, vec(ba), mat, vec(bx), vec(lam)],
        out_specs=seq(),
        out_shape=jax.ShapeDtypeStruct((bsz, m, c), BF16),
        scratch_shapes=[pltpu.VMEM((m + 2 * SUBLANE, LRU_LANES), F32), pltpu.VMEM((m, LRU_LANES), F32)],
        compiler_params=_params(("parallel", "parallel")),
        name="rglru",
    )(u, g, conv_w, conv_b, wa_bd, ba, wx_bd, bx, lam)


def _route(logits_t, bias):
    n = logits_t.shape[-1]
    scores = jax.nn.sigmoid(logits_t)
    choice = scores + bias
    sub = lax.broadcasted_iota(jnp.int32, (PER_GROUP, n), 0)
    neg = -jnp.inf
    groups, gs = [], []
    for g in range(N_GROUPS):
        cg = choice[g * PER_GROUP:(g + 1) * PER_GROUP]
        m1 = jnp.max(cg, axis=0, keepdims=True)
        i1 = jnp.min(jnp.where(cg == m1, sub, PER_GROUP), axis=0, keepdims=True)
        m2 = jnp.max(jnp.where(sub == i1, neg, cg), axis=0, keepdims=True)
        groups.append(cg)
        gs.append(m1 + m2)
    masked = []
    for g in range(N_GROUPS):
        rank = jnp.zeros((1, n), jnp.int32)
        for o in range(N_GROUPS):
            if o != g:
                ahead = (gs[o] >= gs[g]) if o < g else (gs[o] > gs[g])
                rank = rank + jnp.where(ahead, 1, 0)
        masked.append(jnp.where(rank < TOPK_GROUPS, groups[g], neg))
    masked = jnp.concatenate(masked, axis=0)
    eidx = lax.broadcasted_iota(jnp.int32, (N_EXPERTS, n), 0)
    picked = jnp.zeros((N_EXPERTS, n), F32)
    for _ in range(TOP_K):
        mx = jnp.max(masked, axis=0, keepdims=True)
        first = jnp.min(jnp.where(masked == mx, eidx, N_EXPERTS), axis=0, keepdims=True)
        hit = eidx == first
        picked = jnp.where(hit, 1.0, picked)
        masked = jnp.where(hit, neg, masked)
    w = picked * scores
    return w / jnp.sum(w, axis=0, keepdims=True) * ROUTED_SCALE


def _out_proj_kernel(oa_ref, ob_ref, x_ref, mod_ref, wa_ref, wb_ref, rw_ref, rb_ref,
                     xo_ref, h_ref, gates_ref, *, n_lat, ctx_row):
    b, t = pl.program_id(0), pl.program_id(1)
    d = x_ref.shape[-1]
    row0 = t * ROW_TILE
    mv = lambda k: _mod_rows(mod_ref, ctx_row, b, k, d, row0, ROW_TILE, n_lat)
    y = (jnp.dot(oa_ref[0], wa_ref[...], preferred_element_type=F32)
         + jnp.dot(ob_ref[0], wb_ref[...], preferred_element_type=F32))
    x = x_ref[0] + mv(2) * y
    xo_ref[0] = x
    h = _rms(x) * (1.0 + mv(4)) + mv(3)
    h_ref[0] = h.astype(BF16)
    logits_t = lax.dot_general(rw_ref[...], h, (((1,), (1,)), ((), ())),
                               preferred_element_type=F32, precision=lax.Precision.HIGHEST)
    gates_t = _route(logits_t, rb_ref[...])
    gates_t = jnp.concatenate([gates_t, jnp.zeros((LANE - N_EXPERTS, ROW_TILE), F32)], axis=0)
    gates_ref[0] = gates_t.T


def _out_proj(oa, ob, xs, mod, w_out, router_wt, router_b, n_lat, ctx_row):
    bsz, m, d = xs.shape
    na, nb = oa.shape[-1], ob.shape[-1]
    wa, wb = w_out[:na], w_out[na:]
    tile = lambda n: pl.BlockSpec((1, ROW_TILE, n), lambda b, t: (b, t, 0))
    whole = lambda a: pl.BlockSpec(a.shape, lambda b, t: (0,) * a.ndim)
    return pl.pallas_call(
        functools.partial(_out_proj_kernel, n_lat=n_lat, ctx_row=ctx_row),
        grid=(bsz, m // ROW_TILE),
        in_specs=[tile(na), tile(nb), tile(d), whole(mod), whole(wa), whole(wb), whole(router_wt), whole(router_b)],
        out_specs=[tile(d), tile(d), tile(LANE)],
        out_shape=[jax.ShapeDtypeStruct((bsz, m, d), F32),
                   jax.ShapeDtypeStruct((bsz, m, d), BF16),
                   jax.ShapeDtypeStruct((bsz, m, LANE), F32)],
        compiler_params=_params(("parallel", "parallel")),
        name="out_proj_router",
    )(oa, ob, xs, mod, wa, wb, router_wt, router_b)


def _plan_kernel(gates_ref, tri_ref, upper_ref, pos_ref, gem_ref, cnt_ref, off_ref, own_ref):
    g = gates_ref[0]
    sel = g != 0.0
    one = jnp.where(sel, 1.0, 0.0)
    pos = jnp.dot(tri_ref[...], one.astype(BF16), preferred_element_type=F32)
    cnt = jnp.sum(one, axis=0, keepdims=True)
    cpad = jnp.floor((cnt + (SEG_ALIGN - 1)) * (1.0 / SEG_ALIGN)) * SEG_ALIGN
    off = jnp.dot(jnp.broadcast_to(cpad, (SUBLANE, LANE)), upper_ref[...], preferred_element_type=F32,
                  precision=lax.Precision.HIGHEST)[0:1]
    pos_ref[0] = jnp.where(sel, pos, -1.0).T
    gem_ref[0] = g.T
    cnt_ref[0] = jnp.broadcast_to(cnt, (SUBLANE, LANE)).astype(jnp.int32)
    off_ref[0] = jnp.broadcast_to(off, (SUBLANE, LANE)).astype(jnp.int32)
    groups = own_ref.shape[-1]
    ends = jnp.broadcast_to(off + cpad, (LANE, LANE)).T
    ends = jnp.concatenate([ends] * (groups // LANE), axis=1)
    start = lax.broadcasted_iota(jnp.int32, (LANE, groups), 1).astype(F32) * SEG_ALIGN
    real = lax.broadcasted_iota(jnp.int32, (LANE, groups), 0) < N_EXPERTS
    before = jnp.where(real, jnp.where(ends <= start, 1.0, 0.0), 0.0)
    owner = jnp.minimum(jnp.sum(before, axis=0, keepdims=True), N_EXPERTS - 1.0)
    own_ref[0] = jnp.broadcast_to(owner, (SUBLANE, groups)).astype(jnp.int32)


def _moe_kernel(cnt_s, off_s, own_s, h_ref, pos_ref, gem_ref, wg_ref, wu_ref, wd_ref,
                sg_ref, su_ref, sd_ref, x_ref, mod_ref, fn_ref, o_ref, y_ref, p_ref,
                *, bsz, rows_per_batch, n_lat, tiles_per_batch, final):
    i, s = pl.program_id(0), pl.program_id(1)
    rows, d = h_ref.shape[1], h_ref.shape[2]
    h = h_ref[0]

    def ffn(x, wg, wu, wd, gate):
        a = _silu(jnp.dot(x, wg, preferred_element_type=F32)) * jnp.dot(x, wu, preferred_element_type=F32)
        if gate is not None:
            a = a * gate
        return jnp.dot(a.astype(BF16), wd, preferred_element_type=F32)

    @pl.when(s == 0)
    def _():
        y_ref[...] = jnp.zeros(y_ref.shape, BF16)
        o_ref[0] = ffn(h, sg_ref[...].astype(BF16), su_ref[...].astype(BF16), sd_ref[...].astype(BF16), None)

    cap_rows = lax.broadcasted_iota(jnp.int32, (MOE_CAP, 1), 0)
    for j in range(EXPERTS_PER_STEP):
        e = s * EXPERTS_PER_STEP + j
        cnt, off = cnt_s[i, e], off_s[i, e]
        pos_row = pos_ref[0, pl.ds(e, 1), :]
        gate_row = gem_ref[0, pl.ds(e, 1), :]
        wg, wu, wd = wg_ref[j].astype(BF16), wu_ref[j].astype(BF16), wd_ref[j].astype(BF16)

        def block(q, carry):
            ranks = cap_rows + q * MOE_CAP
            hit = pos_row == ranks.astype(F32)
            x = jnp.dot(jnp.where(hit, 1.0, 0.0).astype(BF16), h, preferred_element_type=F32).astype(BF16)
            gate = jnp.sum(jnp.where(hit, gate_row, 0.0), axis=1, keepdims=True)
            y = ffn(x, wg, wu, wd, gate)
            dst = pl.multiple_of(off + q * MOE_CAP, SEG_ALIGN)
            y_ref[pl.ds(dst, MOE_CAP), :] = jnp.where(ranks < cnt, y, 0.0).astype(BF16)
            return carry

        lax.fori_loop(0, (cnt + MOE_CAP - 1) // MOE_CAP, block, 0)

    @pl.when(s == N_EXPERTS // EXPERTS_PER_STEP - 1)
    def _():
        group_rows = lax.broadcasted_iota(jnp.int32, (SEG_ALIGN, 1), 0)
        groups_per_block = COMBINE_BLOCK // SEG_ALIGN

        def combine(r, carry):
            base = pl.multiple_of(r * COMBINE_BLOCK, COMBINE_BLOCK)

            def onehot_group(gi, carry):
                g0 = pl.multiple_of(gi * SEG_ALIGN, SEG_ALIGN)
                e = own_s[i, r * groups_per_block + gi]
                rank = (group_rows + (base + g0 - off_s[i, e])).astype(F32)
                hit = pos_ref[0, pl.ds(e, 1), :] == rank
                p_ref[pl.ds(g0, SEG_ALIGN), :] = jnp.where(hit, 1.0, 0.0).astype(BF16)
                return carry

            lax.fori_loop(0, groups_per_block, onehot_group, 0)
            for n in range(d // ROW_TILE):
                cols = slice(n * ROW_TILE, (n + 1) * ROW_TILE)
                o_ref[0, :, cols] += lax.dot_general(p_ref[...], y_ref[pl.ds(base, COMBINE_BLOCK), cols],
                                                     (((0,), (0,)), ((), ())), preferred_element_type=F32)
            return carry

        lax.fori_loop(0, MOE_SLOTS // COMBINE_BLOCK, combine, 0)

        ctx = mod_ref[bsz:bsz + 1, 5 * d:6 * d]
        if tiles_per_batch is None:
            r = i * rows + lax.broadcasted_iota(jnp.int32, (rows, 1), 0)
            g2 = ctx
            for b in range(bsz):
                lo = b * rows_per_batch
                inside = jnp.where(r >= lo, jnp.where(r < lo + n_lat, 1, 0), 0)
                g2 = jnp.where(inside == 1, mod_ref[b:b + 1, 5 * d:6 * d], g2)
        else:
            g2 = mod_ref[pl.ds(i // tiles_per_batch, 1), 5 * d:6 * d]
        y = x_ref[0] + g2 * o_ref[0]
        if final:
            y = _rms(y) * fn_ref[...]
        o_ref[0] = y


def _moe(h, gates, xs, mod, wg, wu, wd, sg, su, sd, final_norm, layer, n_lat, final):
    bsz, m, d = xs.shape
    t = MOE_TILE
    if final:
        assert n_lat % t == 0
        tiles_per_batch = n_lat // t
        n_tiles = bsz * tiles_per_batch
        where = lambda i: (i // tiles_per_batch, i % tiles_per_batch, 0)
        out_shape = (bsz, n_lat, d)
    else:
        assert (bsz * m) % t == 0
        tiles_per_batch = None
        n_tiles = bsz * m // t
        where = lambda i: (0, i, 0)
        out_shape = (1, bsz * m, d)
        h, gates, xs = (a.reshape(1, bsz * m, a.shape[-1]) for a in (h, gates, xs))
    once = pl.Buffered(1)

    tok = jnp.arange(t)
    tri = (tok[None, :] < tok[:, None]).astype(BF16)
    lane = jnp.arange(LANE)
    upper = (lane[:, None] < lane[None, :]).astype(F32)
    groups = -(-MOE_SLOTS // SEG_ALIGN // LANE) * LANE
    em = jax.ShapeDtypeStruct((n_tiles, LANE, t), F32)
    ints = lambda n: jax.ShapeDtypeStruct((n_tiles, SUBLANE, n), jnp.int32)
    per_tile = lambda a, b: pl.BlockSpec((1, a, b), lambda i: (i, 0, 0))
    pos, gem, cnt, off, own = pl.pallas_call(
        _plan_kernel,
        grid=(n_tiles,),
        in_specs=[pl.BlockSpec((1, t, LANE), lambda i: where(i)),
                  pl.BlockSpec((t, t), lambda i: (0, 0)), pl.BlockSpec((LANE, LANE), lambda i: (0, 0))],
        out_specs=[per_tile(LANE, t), per_tile(LANE, t), per_tile(SUBLANE, LANE), per_tile(SUBLANE, LANE),
                   per_tile(SUBLANE, groups)],
        out_shape=[em, em, ints(LANE), ints(LANE), ints(groups)],
        compiler_params=_params(("parallel",)),
        name="moe_plan",
    )(gates, tri, upper)

    tile = lambda n: pl.BlockSpec((1, t, n), lambda i, s, *_: where(i), pipeline_mode=once)
    planned = lambda a, b: pl.BlockSpec((1, a, b), lambda i, s, *_: (i, 0, 0), pipeline_mode=once)
    whole = lambda a: pl.BlockSpec(a.shape, lambda i, s, *_: (0,) * a.ndim, pipeline_mode=once)
    experts = lambda a: pl.BlockSpec((None, EXPERTS_PER_STEP) + a.shape[2:], lambda i, s, *_: (layer, s, 0, 0))
    out = pl.pallas_call(
        functools.partial(_moe_kernel, bsz=bsz, rows_per_batch=m, n_lat=n_lat,
                          tiles_per_batch=tiles_per_batch, final=final),
        grid_spec=pltpu.PrefetchScalarGridSpec(
            num_scalar_prefetch=3,
            grid=(n_tiles, N_EXPERTS // EXPERTS_PER_STEP),
            in_specs=[tile(d), planned(LANE, t), planned(LANE, t),
                      experts(wg), experts(wu), experts(wd), whole(sg), whole(su), whole(sd),
                      tile(d), whole(mod), whole(final_norm)],
            out_specs=pl.BlockSpec((1, t, d), lambda i, s, *_: where(i)),
            scratch_shapes=[pltpu.VMEM((MOE_SLOTS + MOE_CAP, d), BF16), pltpu.VMEM((COMBINE_BLOCK, t), BF16)]),
        out_shape=jax.ShapeDtypeStruct(out_shape, F32),
        compiler_params=_params(("parallel", "arbitrary"), MOE_VMEM_LIMIT),
        name="moe_final" if final else "moe",
    )(cnt[:, 0], off[:, 0], own[:, 0], h, pos, gem, wg, wu, wd, sg, su, sd, xs, mod, final_norm)
    return out if final else out.reshape(bsz, m, d)


def _rope_tables(n_lat, m, head_dim, width):
    pos = jnp.arange(n_lat, dtype=jnp.int32)
    row = (pos // GRID_W).astype(F32)
    col = (pos % GRID_W).astype(F32)
    half = head_dim // 2
    quarter = half // 2
    lane = jnp.arange(head_dim)
    freq = (lane % quarter).astype(F32)
    inv = ROPE_THETA ** (-(2.0 * freq) / half)
    ang = jnp.where(lane[None, :] < half, row[:, None], col[:, None]) * inv[None, :]
    low = (lane % half) < quarter
    cos = jnp.cos(ang)
    sin = jnp.sin(ang)
    slo = jnp.where(low[None, :], -sin, 0.0)
    shi = jnp.where(low[None, :], 0.0, sin)
    reps = width // head_dim
    pad = lambda a, fill: jnp.concatenate(
        [jnp.tile(a, (1, reps)), jnp.full((m - n_lat, width), fill, F32)], axis=0)
    return pad(cos, 1.0), pad(slo, 0.0), pad(shi, 0.0)


def _mla_tables(tabs16):
    outs = []
    for a, fill in zip(tabs16, (1.0, 0.0, 0.0)):
        m = a.shape[0]
        slot = jnp.concatenate([jnp.full((m, MLA_NOPE), fill, F32), a,
                                jnp.full((m, QK_SLOT - MLA_QK), fill, F32)], axis=1)
        outs.append(jnp.tile(slot, (1, MLA_HEADS)))
    return tuple(outs)


def _slots(w, heads, width):
    w = w.reshape(w.shape[0], heads, width)
    pad = jnp.zeros((w.shape[0], heads, QK_SLOT - width), w.dtype)
    return jnp.concatenate([w, pad], axis=-1).reshape(w.shape[0], heads * QK_SLOT)


def _block_diag(w):
    dirs = w.shape[0]
    eye = jnp.eye(LRU_BLOCKS, dtype=w.dtype)
    full = jnp.einsum('dnij,nm->dnimj', w, eye).reshape(dirs, LRU_WIDTH, LRU_WIDTH)
    nl = LRU_WIDTH // LRU_LANES
    return jnp.stack([full[:, j * LRU_LANES:(j + 1) * LRU_LANES, j * LRU_LANES:(j + 1) * LRU_LANES]
                      for j in range(nl)], axis=1)


def kernel(x, c, ctx, c_ctx, mod_w, mod_b, ab_w_in, ab_w_out, diff_lambda_q, diff_lambda_k, diff_subln, lru_conv_w, lru_conv_b, lru_w_a, lru_b_a, lru_w_x, lru_b_x, lru_lambda, cd_w_in, cd_w_out, gqa_q_norm, gqa_k_norm, mla_q_norm, mla_kv_norm, mla_w_uq, mla_w_ukv, router_w, router_bias, exp_w_gate, exp_w_up, exp_w_down, sh_w_gate, sh_w_up, sh_w_down, final_norm):
    bsz, n_lat, d = x.shape
    n_ctx = ctx.shape[1]
    m = n_lat + n_ctx
    depth = mod_w.shape[0]
    assert depth == 2 and bsz < MOD_ROWS
    assert n_lat % ROW_TILE == 0 and n_ctx % ROW_TILE == 0 and n_lat % GRID_W == 0
    ctx_row = bsz

    xs = jnp.concatenate([x, ctx], axis=1)
    c_all = jnp.concatenate([c, c_ctx[None, :], jnp.zeros((MOD_ROWS - bsz - 1, d), F32)], axis=0)
    mods = _modulation(c_all, mod_w, mod_b)

    tabs64 = _rope_tables(n_lat, m, DIFF_HEAD_DIM, DIFF_HEADS * 2 * DIFF_HEAD_DIM)
    tabs_m = _mla_tables(_rope_tables(n_lat, m, MLA_ROPE, MLA_ROPE))

    def moe_weights(i):
        return exp_w_gate, exp_w_up, exp_w_down, sh_w_gate[i], sh_w_up[i], sh_w_down[i]

    fn = final_norm.reshape(1, d)

    mod = mods[0]
    qt, k, vt, u, g = _proj_ab(xs, mod, ab_w_in[0].astype(BF16), tabs64, n_lat, ctx_row)
    lambda_init = 0.8 - 0.6 * math.exp(-0.3 * 0)
    dv = 2 * DIFF_HEAD_DIM
    jobs = tuple(_Job((2 * h + j) * QK_SLOT, h, j, h * dv, dv) for h in range(DIFF_HEADS) for j in range(2))
    o_att = _attention(qt, k, vt, (diff_lambda_q[0], diff_lambda_k[0], diff_subln[0].reshape(-1, 1)),
                       jobs, n_lat, lambda_init, "diff_attention")
    o_rec = _lru(u, g, lru_conv_w[0], lru_conv_b[0].reshape(1, -1),
                 _block_diag(lru_w_a[0]).astype(BF16), lru_b_a[0].reshape(2, 1, LRU_WIDTH),
                 _block_diag(lru_w_x[0]).astype(BF16), lru_b_x[0].reshape(2, 1, LRU_WIDTH),
                 lru_lambda[0].reshape(2, 1, LRU_WIDTH), n_lat)
    xs, h2, gates = _out_proj(o_att, o_rec, xs, mod, ab_w_out[0].astype(BF16),
                              router_w[0].T, router_bias[0].reshape(-1, 1), n_lat, ctx_row)
    xs = _moe(h2, gates, xs, mod, *moe_weights(0), fn, 0, n_lat, False)

    mod = mods[1]
    w_in = cd_w_in[0]
    wq, wk = GQA_HEADS * GQA_HEAD_DIM, GQA_KV_HEADS * GQA_HEAD_DIM
    o_kr = wq + 2 * wk + MLA_Q_RANK + MLA_KV_RANK
    kr_slots = jnp.concatenate([jnp.zeros((d, MLA_NOPE), F32), w_in[:, o_kr:o_kr + MLA_ROPE],
                                jnp.zeros((d, QK_SLOT - MLA_QK), F32)], axis=1)
    w_in = jnp.concatenate([w_in[:, :o_kr], jnp.tile(kr_slots, (1, MLA_HEADS))], axis=1).astype(BF16)
    ukv = mla_w_ukv[0].reshape(MLA_KV_RANK, MLA_HEADS, MLA_NOPE + MLA_V)
    w_kn = _slots(ukv[:, :, :MLA_NOPE].reshape(MLA_KV_RANK, -1), MLA_HEADS, MLA_NOPE).astype(BF16)
    w_v = ukv[:, :, MLA_NOPE:].reshape(MLA_KV_RANK, MLA_HEADS * MLA_V).astype(BF16)
    w_uq = _slots(mla_w_uq[0], MLA_HEADS, MLA_QK).astype(BF16)
    lane = jnp.arange(wq)
    ones_bd = (lane[:, None] // GQA_HEAD_DIM == lane[None, :] // GQA_HEAD_DIM).astype(BF16)
    qgt, kg, vgt, qmt, km, vmt = _proj_cd(
        xs, mod, w_in, ones_bd, jnp.tile(gqa_q_norm[0], GQA_HEADS).reshape(1, -1),
        jnp.tile(gqa_k_norm[0], GQA_KV_HEADS).reshape(1, -1), mla_q_norm[0].reshape(1, -1),
        mla_kv_norm[0].reshape(1, -1), w_uq, w_kn, w_v, tabs64, tabs_m, n_lat, ctx_row)
    groups = GQA_HEADS // GQA_KV_HEADS
    jobs = tuple(_Job(h * QK_SLOT, 0, h // groups, (h // groups) * GQA_HEAD_DIM, GQA_HEAD_DIM)
                 for h in range(GQA_HEADS))
    o_gqa = _attention(qgt, kg, vgt, (), jobs, n_lat, None, "gqa_attention")
    jobs = tuple(_Job(h * QK_SLOT, h // 2, h % 2, h * MLA_V, MLA_V) for h in range(MLA_HEADS))
    o_mla = _attention(qmt, km, vmt, (), jobs, n_lat, None, "mla_attention")
    xs, h2, gates = _out_proj(o_gqa, o_mla, xs, mod, cd_w_out[0].astype(BF16),
                              router_w[1].T, router_bias[1].reshape(-1, 1), n_lat, ctx_row)
    return _moe(h2, gates, xs, mod, *moe_weights(1), fn, 1, n_lat, True)
```

```python
import functools
import math
from typing import NamedTuple

import jax
import jax.numpy as jnp
from jax import lax
from jax.experimental import pallas as pl
from jax.experimental.pallas import tpu as pltpu

F32 = jnp.float32
BF16 = jnp.bfloat16

GRID_W = 64
ROPE_THETA = 10000.0
EPS = 1e-6
DIFF_HEADS = 4
DIFF_HEAD_DIM = 64
LRU_WIDTH = 512
LRU_BLOCKS = 8
LRU_BW = LRU_WIDTH // LRU_BLOCKS
CONV_W = 4
LRU_C = 8.0
GQA_HEADS = 8
GQA_KV_HEADS = 2
GQA_HEAD_DIM = 64
MLA_HEADS = 8
MLA_Q_RANK = 256
MLA_KV_RANK = 128
MLA_NOPE = 32
MLA_ROPE = 16
MLA_V = 64
MLA_QK = MLA_NOPE + MLA_ROPE
N_EXPERTS = 64
N_GROUPS = 8
PER_GROUP = N_EXPERTS // N_GROUPS
TOPK_GROUPS = 4
TOP_K = 8
EXPERT_FF = 256
ROUTED_SCALE = 2.5

QK_SLOT = 64
LOG2E = math.log2(math.e)

ROW_TILE = 256
LRU_CHUNK = 128
LRU_LANES = 256
EXPERTS_PER_STEP = 4
LANE = 128
SUBLANE = 8
MOD_ROWS = 8
VMEM_LIMIT = 56 * 1024 * 1024
MOE_VMEM_LIMIT = 62 * 1024 * 1024
MOE_TILE = 1024
MOE_CAP = 160
SEG_ALIGN = 16
MOE_SLOTS = MOE_TILE * TOP_K + N_EXPERTS * SEG_ALIGN
COMBINE_BLOCK = 1024
GROUP_UNROLL = 4


def _params(sem, vmem=VMEM_LIMIT):
    return pltpu.CompilerParams(dimension_semantics=sem, vmem_limit_bytes=vmem)


def _silu(x):
    return x * jax.nn.sigmoid(x)


def _rms(x):
    return x * lax.rsqrt(jnp.mean(x * x, axis=-1, keepdims=True) + EPS)


def _mod_rows(mod_ref, ctx_row, b, k, d, row0, rows, n_lat):
    lat = mod_ref[pl.ds(b, 1), k * d:(k + 1) * d]
    ctx = mod_ref[ctx_row:ctx_row + 1, k * d:(k + 1) * d]
    if n_lat % rows == 0:
        return jnp.where(row0 >= n_lat, ctx, lat)
    r = row0 + lax.broadcasted_iota(jnp.int32, (rows, 1), 0)
    return jnp.where(r >= n_lat, ctx, lat)


def _rope(x, cos, sin_lo, sin_hi, half):
    n = x.shape[-1]
    return x * cos + pltpu.roll(x, n - half, 1) * sin_lo + pltpu.roll(x, half, 1) * sin_hi


def _mod_kernel(c_ref, w_ref, b_ref, o_ref):
    s = _silu(c_ref[...])
    o_ref[0] = jnp.dot(s, w_ref[0], preferred_element_type=F32, precision=lax.Precision.HIGHEST) + b_ref[0]


def _modulation(c_all, mod_w, mod_b):
    depth, d, n = mod_w.shape
    tn = n // 4
    return pl.pallas_call(
        _mod_kernel,
        grid=(depth, n // tn),
        in_specs=[pl.BlockSpec((MOD_ROWS, d), lambda i, j: (0, 0)),
                  pl.BlockSpec((1, d, tn), lambda i, j: (i, 0, j)),
                  pl.BlockSpec((1, 1, tn), lambda i, j: (i, 0, j))],
        out_specs=pl.BlockSpec((1, MOD_ROWS, tn), lambda i, j: (i, 0, j)),
        out_shape=jax.ShapeDtypeStruct((depth, MOD_ROWS, n), F32),
        compiler_params=_params(("arbitrary", "arbitrary")),
        name="modulation",
    )(c_all, mod_w, mod_b.reshape(depth, 1, n))


def _proj_ab_kernel(x_ref, mod_ref, w_ref, cos_ref, slo_ref, shi_ref,
                    qt_ref, k_ref, vt_ref, u_ref, g_ref, *, n_lat, ctx_row):
    b, t = pl.program_id(0), pl.program_id(1)
    d = x_ref.shape[-1]
    row0 = t * ROW_TILE
    shift = _mod_rows(mod_ref, ctx_row, b, 0, d, row0, ROW_TILE, n_lat)
    scale = _mod_rows(mod_ref, ctx_row, b, 1, d, row0, ROW_TILE, n_lat)
    h = (_rms(x_ref[0]) * (1.0 + scale) + shift).astype(BF16)
    acc = jnp.dot(h, w_ref[...], preferred_element_type=F32)
    w = DIFF_HEADS * 2 * DIFF_HEAD_DIM
    cos, slo, shi = cos_ref[...], slo_ref[...], shi_ref[...]
    q = _rope(acc[:, 0:w], cos, slo, shi, DIFF_HEAD_DIM // 4) * (DIFF_HEAD_DIM ** -0.5 * LOG2E)
    k = _rope(acc[:, w:2 * w], cos, slo, shi, DIFF_HEAD_DIM // 4)
    qt_ref[0, 0] = q.T.astype(BF16)
    k_ref[0] = k.astype(BF16)
    vt_ref[0, 0] = acc[:, 2 * w:3 * w].T.astype(BF16)
    u_ref[0] = acc[:, 3 * w:3 * w + LRU_WIDTH]
    g_ref[0] = jax.nn.gelu(acc[:, 3 * w + LRU_WIDTH:3 * w + 2 * LRU_WIDTH]).astype(BF16)


def _proj_ab(xs, mod, w_in, tabs, n_lat, ctx_row):
    bsz, m, d = xs.shape
    nt = m // ROW_TILE
    w = DIFF_HEADS * 2 * DIFF_HEAD_DIM
    n_in = w_in.shape[1]
    tile = lambda n: pl.BlockSpec((1, ROW_TILE, n), lambda b, t: (b, t, 0))
    ttile = pl.BlockSpec((1, 1, w, ROW_TILE), lambda b, t: (b, t, 0, 0))
    tab = pl.BlockSpec((ROW_TILE, w), lambda b, t: (t, 0))
    return pl.pallas_call(
        functools.partial(_proj_ab_kernel, n_lat=n_lat, ctx_row=ctx_row),
        grid=(bsz, nt),
        in_specs=[tile(d),
                  pl.BlockSpec(mod.shape, lambda b, t: (0, 0)),
                  pl.BlockSpec((d, n_in), lambda b, t: (0, 0)),
                  tab, tab, tab],
        out_specs=[ttile, tile(w), ttile, tile(LRU_WIDTH), tile(LRU_WIDTH)],
        out_shape=[jax.ShapeDtypeStruct((bsz, nt, w, ROW_TILE), BF16),
                   jax.ShapeDtypeStruct((bsz, m, w), BF16),
                   jax.ShapeDtypeStruct((bsz, nt, w, ROW_TILE), BF16),
                   jax.ShapeDtypeStruct((bsz, m, LRU_WIDTH), F32),
                   jax.ShapeDtypeStruct((bsz, m, LRU_WIDTH), BF16)],
        compiler_params=_params(("parallel", "parallel")),
        name="proj_ab",
    )(xs, mod, w_in, *tabs)


def _group_rms(x, ones_bd):
    x2 = x * x
    hi = x2.astype(BF16)
    lo = (x2 - hi.astype(F32)).astype(BF16)
    ss = jnp.dot(hi, ones_bd, preferred_element_type=F32) + jnp.dot(lo, ones_bd, preferred_element_type=F32)
    return x * lax.rsqrt(ss * (1.0 / GQA_HEAD_DIM) + EPS)


def _proj_cd_kernel(x_ref, mod_ref, w_ref, bd_ref, qn_ref, kn_ref, cqn_ref, ckvn_ref, wuq_ref, wkn_ref, wv_ref,
                    cos_ref, slo_ref, shi_ref, cosm_ref, slom_ref, shim_ref,
                    qgt_ref, kg_ref, vgt_ref, qmt_ref, km_ref, vmt_ref, *, n_lat, ctx_row):
    b, t = pl.program_id(0), pl.program_id(1)
    d = x_ref.shape[-1]
    row0 = t * ROW_TILE
    shift = _mod_rows(mod_ref, ctx_row, b, 0, d, row0, ROW_TILE, n_lat)
    scale = _mod_rows(mod_ref, ctx_row, b, 1, d, row0, ROW_TILE, n_lat)
    h = (_rms(x_ref[0]) * (1.0 + scale) + shift).astype(BF16)
    acc = jnp.dot(h, w_ref[...], preferred_element_type=F32)
    wq = GQA_HEADS * GQA_HEAD_DIM
    wk = GQA_KV_HEADS * GQA_HEAD_DIM
    o = 0
    q = acc[:, o:o + wq]; o += wq
    k = acc[:, o:o + wk]; o += wk
    v = acc[:, o:o + wk]; o += wk
    cq = acc[:, o:o + MLA_Q_RANK]; o += MLA_Q_RANK
    ckv = acc[:, o:o + MLA_KV_RANK]; o += MLA_KV_RANK
    kr = acc[:, o:o + MLA_HEADS * QK_SLOT]
    cos, slo, shi = cos_ref[...], slo_ref[...], shi_ref[...]
    bd = bd_ref[...]
    q = (_rope(_group_rms(q, bd) * qn_ref[...], cos, slo, shi, GQA_HEAD_DIM // 4)
         * (GQA_HEAD_DIM ** -0.5 * LOG2E))
    k = _rope(_group_rms(k, bd[:wk, :wk]) * kn_ref[...], cos[:, :wk], slo[:, :wk], shi[:, :wk], GQA_HEAD_DIM // 4)
    qgt_ref[0, 0] = q.T.astype(BF16)
    kg_ref[0] = k.astype(BF16)
    vgt_ref[0, 0] = v.T.astype(BF16)
    cosm, slom, shim = cosm_ref[...], slom_ref[...], shim_ref[...]
    cqn = (_rms(cq) * cqn_ref[...]).astype(BF16)
    mq = jnp.dot(cqn, wuq_ref[...], preferred_element_type=F32)
    qmt_ref[0, 0] = (_rope(mq, cosm, slom, shim, MLA_ROPE // 4) * (MLA_QK ** -0.5 * LOG2E)).T.astype(BF16)
    ckvn = (_rms(ckv) * ckvn_ref[...]).astype(BF16)
    km = jnp.dot(ckvn, wkn_ref[...], preferred_element_type=F32) + _rope(kr, cosm, slom, shim, MLA_ROPE // 4)
    km_ref[0] = km.astype(BF16)
    vmt_ref[0, 0] = jnp.dot(ckvn, wv_ref[...], preferred_element_type=F32).T.astype(BF16)


def _proj_cd(xs, mod, w_in, bd, qn, kn, cqn, ckvn, wuq, wkn, wv, tabs, tabs_m, n_lat, ctx_row):
    bsz, m, d = xs.shape
    nt = m // ROW_TILE
    wq = GQA_HEADS * GQA_HEAD_DIM
    wk = GQA_KV_HEADS * GQA_HEAD_DIM
    wm = MLA_HEADS * QK_SLOT
    wmv = MLA_HEADS * MLA_V
    tile = lambda n: pl.BlockSpec((1, ROW_TILE, n), lambda b, t: (b, t, 0))
    ttile = lambda n: pl.BlockSpec((1, 1, n, ROW_TILE), lambda b, t: (b, t, 0, 0))
    whole = lambda a: pl.BlockSpec(a.shape, lambda b, t: (0,) * a.ndim)
    tab = pl.BlockSpec((ROW_TILE, wq), lambda b, t: (t, 0))
    tabm = pl.BlockSpec((ROW_TILE, wm), lambda b, t: (t, 0))
    return pl.pallas_call(
        functools.partial(_proj_cd_kernel, n_lat=n_lat, ctx_row=ctx_row),
        grid=(bsz, nt),
        in_specs=[tile(d), whole(mod), whole(w_in), whole(bd), whole(qn), whole(kn), whole(cqn), whole(ckvn),
                  whole(wuq), whole(wkn), whole(wv), tab, tab, tab, tabm, tabm, tabm],
        out_specs=[ttile(wq), tile(wk), ttile(wk), ttile(wm), tile(wm), ttile(wmv)],
        out_shape=[jax.ShapeDtypeStruct((bsz, nt, wq, ROW_TILE), BF16),
                   jax.ShapeDtypeStruct((bsz, m, wk), BF16),
                   jax.ShapeDtypeStruct((bsz, nt, wk, ROW_TILE), BF16),
                   jax.ShapeDtypeStruct((bsz, nt, wm, ROW_TILE), BF16),
                   jax.ShapeDtypeStruct((bsz, m, wm), BF16),
                   jax.ShapeDtypeStruct((bsz, nt, wmv, ROW_TILE), BF16)],
        compiler_params=_params(("parallel", "parallel")),
        name="proj_cd",
    )(xs, mod, w_in, bd, qn, kn, cqn, ckvn, wuq, wkn, wv, *tabs, *tabs_m)


class _Job(NamedTuple):
    q_row: int
    k_group: int
    k_half: int
    v_row: int
    dv: int


def _attend(qt_ref, qnext_ref, k_ref, vt_ref, s_ref, mp_ref, jobs, lo, hi, chained):
    assert len(jobs) % 2 == 0
    tq = qt_ref.shape[-1]
    n = hi - lo
    group = next(g for g in (11, 3, 2, 1) if n % g == 0)
    steps = n // group
    row_groups = ROW_TILE // SUBLANE

    def weights(job, ref=qt_ref):
        qh = ref[0, 0, job.q_row:job.q_row + QK_SLOT, :]
        z = jnp.zeros_like(qh)
        return jnp.concatenate([z, qh] if job.k_half else [qh, z], axis=0)

    def score(c, buf, qw, job, mp):
        r0 = pl.multiple_of(c * ROW_TILE, ROW_TILE)
        kc = k_ref[0, pl.ds(r0, ROW_TILE), job.k_group * LANE:(job.k_group + 1) * LANE]
        s = jnp.dot(kc, qw, preferred_element_type=F32)
        s_ref[buf, c] = s
        return jnp.maximum(mp, jnp.max(s.reshape(row_groups, SUBLANE, tq), axis=0))

    def value(c, buf, m, job, lp, acc):
        p = jnp.exp2(s_ref[buf, c] - m)
        lp = lp + jnp.sum(p.reshape(row_groups, SUBLANE, tq), axis=0)
        acc = acc + jnp.dot(vt_ref[0, c, job.v_row:job.v_row + job.dv, :], p.astype(BF16),
                            preferred_element_type=F32)
        return lp, acc

    def loop(body, init):
        return body(0, init) if steps == 1 else lax.fori_loop(0, steps, body, init)

    neg = jnp.full((SUBLANE, tq), -jnp.inf, F32)

    def prologue():
        qw = weights(jobs[0])

        def first(i, mp):
            for g in range(group):
                mp = score(lo + i * group + g, 0, qw, jobs[0], mp)
            return mp

        mp_ref[...] = loop(first, neg)

    if chained:
        pl.when(pl.program_id(1) == 0)(prologue)
    else:
        prologue()
    mp = mp_ref[...]
    results = []
    for j, job in enumerate(jobs):
        buf = j % 2
        m = jnp.max(mp, axis=0, keepdims=True)
        if j + 1 < len(jobs):
            nxt, qw = jobs[j + 1], weights(jobs[j + 1])
        elif chained:
            nxt, qw = jobs[0], weights(jobs[0], qnext_ref)
        else:
            nxt = None

        def body(i, carry):
            mp, lp, acc = carry
            for g in range(group):
                c = lo + i * group + g
                if nxt is not None:
                    mp = score(c, 1 - buf, qw, nxt, mp)
                lp, acc = value(c, buf, m, job, lp, acc)
            return mp, lp, acc

        mp, lp, acc = loop(body, (neg, jnp.zeros((SUBLANE, tq), F32), jnp.zeros((job.dv, tq), F32)))
        results.append((acc, jnp.sum(lp, axis=0, keepdims=True)))
    if chained:
        mp_ref[...] = mp
    return results


def _attn_kernel(*refs, jobs, n_lat, lambda_init):
    if lambda_init is None:
        qt_ref, qnext_ref, k_ref, vt_ref, o_ref, s_ref, mp_ref = refs
    else:
        qt_ref, qnext_ref, k_ref, vt_ref, lq_ref, lk_ref, sub_ref, o_ref, s_ref, mp_ref = refs
    n_chunks = vt_ref.shape[1]
    first_ctx = n_lat // ROW_TILE
    t = pl.program_id(1)

    def run(lo, hi, chained):
        res = _attend(qt_ref, qnext_ref, k_ref, vt_ref, s_ref, mp_ref, jobs, lo, hi, chained)
        if lambda_init is None:
            outs = [acc / l for acc, l in res]
        else:
            lq, lk = lq_ref[...], lk_ref[...]
            lam = (jnp.exp(jnp.sum(lq[0:1] * lk[0:1], axis=-1, keepdims=True))
                   - jnp.exp(jnp.sum(lq[1:2] * lk[1:2], axis=-1, keepdims=True)) + lambda_init)
            outs = []
            for h in range(len(res) // 2):
                (a0, l0), (a1, l1) = res[2 * h], res[2 * h + 1]
                o = a0 / l0 - lam * (a1 / l1)
                o = o * lax.rsqrt(jnp.mean(o * o, axis=0, keepdims=True) + EPS)
                outs.append(o * sub_ref[...] * (1.0 - lambda_init))
        o_ref[0] = jnp.concatenate(outs, axis=0).T.astype(BF16)

    @pl.when(t < first_ctx)
    def _():
        run(0, n_chunks, True)

    @pl.when(t >= first_ctx)
    def _():
        run(first_ctx, n_chunks, False)


def _attention(qt, k, vt, extra, jobs, n_lat, lambda_init, name):
    bsz, nt, wq, tq = qt.shape
    m = k.shape[1]
    wo = sum(j.dv for j in jobs) if lambda_init is None else sum(j.dv for j in jobs) // 2
    once = pl.Buffered(1)
    whole = lambda a: pl.BlockSpec(a.shape, lambda b, t: (0,) * a.ndim)
    return pl.pallas_call(
        functools.partial(_attn_kernel, jobs=jobs, n_lat=n_lat, lambda_init=lambda_init),
        grid=(bsz, nt),
        in_specs=[pl.BlockSpec((1, 1, wq, tq), lambda b, t: (b, t, 0, 0)),
                  pl.BlockSpec((1, 1, wq, tq), lambda b, t: (b, jnp.minimum(t + 1, nt - 1), 0, 0)),
                  pl.BlockSpec((1,) + k.shape[1:], lambda b, t: (b, 0, 0), pipeline_mode=once),
                  pl.BlockSpec((1,) + vt.shape[1:], lambda b, t: (b, 0, 0, 0), pipeline_mode=once)]
                 + [whole(a) for a in extra],
        out_specs=pl.BlockSpec((1, tq, wo), lambda b, t: (b, t, 0)),
        out_shape=jax.ShapeDtypeStruct((bsz, m, wo), BF16),
        scratch_shapes=[pltpu.VMEM((2, nt, ROW_TILE, tq), F32), pltpu.VMEM((SUBLANE, tq), F32)],
        compiler_params=_params(("arbitrary", "arbitrary")),
        name=name,
    )(qt, qt, k, vt, *extra)


def _scan_chunk(a, b, carry, reverse):
    n = a.shape[0]
    rows = lax.broadcasted_iota(jnp.int32, (n, 1), 0)
    d = 1
    while d < n:
        sh = n - d if reverse else d
        valid = rows < n - d if reverse else rows >= d
        b = jnp.where(valid, a * pltpu.roll(b, sh, 0) + b, b)
        a = jnp.where(valid, a * pltpu.roll(a, sh, 0), a)
        d *= 2
    h = a * carry + b
    return h, (h[0:1] if reverse else h[n - 1:n])


def _lru_kernel(u_ref, g_ref, cw_ref, cb_ref, wa_ref, ba_ref, wx_ref, bx_ref, lam_ref, o_ref, up_ref, hf_ref, *, n_lat):
    m = u_ref.shape[1]
    lanes = u_ref.shape[2]
    n_chunks = m // LRU_CHUNK
    n_ctx_chunks = (m - n_lat) // LRU_CHUNK
    pad = SUBLANE

    up_ref[0:pad, :] = jnp.zeros((pad, lanes), F32)
    up_ref[pad + m:pad + m + pad, :] = jnp.zeros((pad, lanes), F32)

    def copy(c, _):
        r0 = pl.multiple_of(c * LRU_CHUNK, LRU_CHUNK)
        up_ref[pl.ds(pad + r0, LRU_CHUNK), :] = u_ref[0, pl.ds(r0, LRU_CHUNK), :]
        return 0

    lax.fori_loop(0, n_chunks, copy, 0)

    cw = cw_ref[...]
    cb = cb_ref[...]
    win_rows = LRU_CHUNK + 2 * pad

    def coeffs(c, d):
        r0 = pl.multiple_of(c * LRU_CHUNK, LRU_CHUNK)
        win = up_ref[pl.ds(r0, win_rows), :]
        r = r0 + lax.broadcasted_iota(jnp.int32, (LRU_CHUNK, 1), 0)
        at = lambda k: pltpu.roll(win, (win_rows - k) % win_rows, 0)[pad:pad + LRU_CHUNK]
        y = (cw[0:1] * jnp.where((r == n_lat) | (r == n_lat + 1), 0.0, at(-2))
             + cw[1:2] * jnp.where(r == n_lat, 0.0, at(-1))
             + cw[2:3] * win[pad:pad + LRU_CHUNK]
             + cw[3:4] * jnp.where(r == n_lat - 1, 0.0, at(1))
             + cb)
        yb = y.astype(BF16)
        ra = jax.nn.sigmoid(jnp.dot(yb, wa_ref[d, 0], preferred_element_type=F32) + ba_ref[d])
        ix = jax.nn.sigmoid(jnp.dot(yb, wx_ref[d, 0], preferred_element_type=F32) + bx_ref[d])
        z = -lam_ref[d]
        softplus = jnp.maximum(z, 0.0) + jnp.log(1.0 + jnp.exp(-jnp.abs(z)))
        a = jnp.exp(-LRU_C * ra * softplus)
        return a, jnp.sqrt(1.0 - a * a) * ix * y

    def fwd(s, carry):
        c = lax.rem(s + (n_chunks - n_ctx_chunks), n_chunks)
        a, b = coeffs(c, 0)
        h, carry = _scan_chunk(a, b, carry, False)
        r0 = pl.multiple_of(c * LRU_CHUNK, LRU_CHUNK)
        hf_ref[pl.ds(r0, LRU_CHUNK), :] = h
        return carry

    lax.fori_loop(0, n_chunks, fwd, jnp.zeros((1, lanes), F32))

    def bwd(s, carry):
        c = n_chunks - 1 - s
        a, b = coeffs(c, 1)
        h, carry = _scan_chunk(a, b, carry, True)
        r0 = pl.multiple_of(c * LRU_CHUNK, LRU_CHUNK)
        tot = hf_ref[pl.ds(r0, LRU_CHUNK), :] + h
        o_ref[0, pl.ds(r0, LRU_CHUNK), :] = (tot * g_ref[0, pl.ds(r0, LRU_CHUNK), :].astype(F32)).astype(BF16)
        return carry

    lax.fori_loop(0, n_chunks, bwd, jnp.zeros((1, lanes), F32))


def _lru(u, g, conv_w, conv_b, wa_bd, ba, wx_bd, bx, lam, n_lat):
    bsz, m, c = u.shape
    nl = c // LRU_LANES
    seq = lambda: pl.BlockSpec((1, m, LRU_LANES), lambda b, j: (b, 0, j))
    vec = lambda a: pl.BlockSpec(a.shape[:-1] + (LRU_LANES,), lambda b, j: (0,) * (a.ndim - 1) + (j,))
    mat = pl.BlockSpec((2, 1, LRU_LANES, LRU_LANES), lambda b, j: (0, j, 0, 0))
    return pl.pallas_call(
        functools.partial(_lru_kernel, n_lat=n_lat),
        grid=(bsz, nl),
        in_specs=[seq(), seq(), vec(conv_w), vec(conv_b), mat, vec(ba), mat, vec(bx), vec(lam)],
        out_specs=seq(),
        out_shape=jax.ShapeDtypeStruct((bsz, m, c), BF16),
        scratch_shapes=[pltpu.VMEM((m + 2 * SUBLANE, LRU_LANES), F32), pltpu.VMEM((m, LRU_LANES), F32)],
        compiler_params=_params(("parallel", "parallel")),
        name="rglru",
    )(u, g, conv_w, conv_b, wa_bd, ba, wx_bd, bx, lam)


def _route(logits_t, bias):
    n = logits_t.shape[-1]
    scores = jax.nn.sigmoid(logits_t)
    choice = scores + bias
    sub = lax.broadcasted_iota(jnp.int32, (PER_GROUP, n), 0)
    neg = -jnp.inf
    groups, gs = [], []
    for g in range(N_GROUPS):
        cg = choice[g * PER_GROUP:(g + 1) * PER_GROUP]
        m1 = jnp.max(cg, axis=0, keepdims=True)
        i1 = jnp.min(jnp.where(cg == m1, sub, PER_GROUP), axis=0, keepdims=True)
        m2 = jnp.max(jnp.where(sub == i1, neg, cg), axis=0, keepdims=True)
        groups.append(cg)
        gs.append(m1 + m2)
    masked = []
    for g in range(N_GROUPS):
        rank = jnp.zeros((1, n), jnp.int32)
        for o in range(N_GROUPS):
            if o != g:
                ahead = (gs[o] >= gs[g]) if o < g else (gs[o] > gs[g])
                rank = rank + jnp.where(ahead, 1, 0)
        masked.append(jnp.where(rank < TOPK_GROUPS, groups[g], neg))
    masked = jnp.concatenate(masked, axis=0)
    eidx = lax.broadcasted_iota(jnp.int32, (N_EXPERTS, n), 0)
    picked = jnp.zeros((N_EXPERTS, n), F32)
    for _ in range(TOP_K):
        mx = jnp.max(masked, axis=0, keepdims=True)
        first = jnp.min(jnp.where(masked == mx, eidx, N_EXPERTS), axis=0, keepdims=True)
        hit = eidx == first
        picked = jnp.where(hit, 1.0, picked)
        masked = jnp.where(hit, neg, masked)
    w = picked * scores
    return w / jnp.sum(w, axis=0, keepdims=True) * ROUTED_SCALE


def _out_proj_kernel(oa_ref, ob_ref, x_ref, mod_ref, wa_ref, wb_ref, rw_ref, rb_ref,
                     xo_ref, h_ref, gates_ref, *, n_lat, ctx_row):
    b, t = pl.program_id(0), pl.program_id(1)
    d = x_ref.shape[-1]
    row0 = t * ROW_TILE
    mv = lambda k: _mod_rows(mod_ref, ctx_row, b, k, d, row0, ROW_TILE, n_lat)
    y = (jnp.dot(oa_ref[0], wa_ref[...], preferred_element_type=F32)
         + jnp.dot(ob_ref[0], wb_ref[...], preferred_element_type=F32))
    x = x_ref[0] + mv(2) * y
    xo_ref[0] = x
    h = _rms(x) * (1.0 + mv(4)) + mv(3)
    h_ref[0] = h.astype(BF16)
    logits_t = lax.dot_general(rw_ref[...], h, (((1,), (1,)), ((), ())),
                               preferred_element_type=F32, precision=lax.Precision.HIGHEST)
    gates_t = _route(logits_t, rb_ref[...])
    gates_t = jnp.concatenate([gates_t, jnp.zeros((LANE - N_EXPERTS, ROW_TILE), F32)], axis=0)
    gates_ref[0] = gates_t.T


def _out_proj(oa, ob, xs, mod, w_out, router_wt, router_b, n_lat, ctx_row):
    bsz, m, d = xs.shape
    na, nb = oa.shape[-1], ob.shape[-1]
    wa, wb = w_out[:na], w_out[na:]
    tile = lambda n: pl.BlockSpec((1, ROW_TILE, n), lambda b, t: (b, t, 0))
    whole = lambda a: pl.BlockSpec(a.shape, lambda b, t: (0,) * a.ndim)
    return pl.pallas_call(
        functools.partial(_out_proj_kernel, n_lat=n_lat, ctx_row=ctx_row),
        grid=(bsz, m // ROW_TILE),
        in_specs=[tile(na), tile(nb), tile(d), whole(mod), whole(wa), whole(wb), whole(router_wt), whole(router_b)],
        out_specs=[tile(d), tile(d), tile(LANE)],
        out_shape=[jax.ShapeDtypeStruct((bsz, m, d), F32),
                   jax.ShapeDtypeStruct((bsz, m, d), BF16),
                   jax.ShapeDtypeStruct((bsz, m, LANE), F32)],
        compiler_params=_params(("parallel", "parallel")),
        name="out_proj_router",
    )(oa, ob, xs, mod, wa, wb, router_wt, router_b)


def _plan_kernel(gates_ref, tri_ref, upper_ref, pos_ref, gem_ref, cnt_ref, off_ref, own_ref):
    g = gates_ref[0]
    sel = g != 0.0
    one = jnp.where(sel, 1.0, 0.0)
    pos = jnp.dot(tri_ref[...], one.astype(BF16), preferred_element_type=F32)
    cnt = jnp.sum(one, axis=0, keepdims=True)
    cpad = jnp.floor((cnt + (SEG_ALIGN - 1)) * (1.0 / SEG_ALIGN)) * SEG_ALIGN
    off = jnp.dot(jnp.broadcast_to(cpad, (SUBLANE, LANE)), upper_ref[...], preferred_element_type=F32,
                  precision=lax.Precision.HIGHEST)[0:1]
    pos_ref[0] = jnp.where(sel, pos, -1.0).T
    gem_ref[0] = g.T
    cnt_ref[0] = jnp.broadcast_to(cnt, (SUBLANE, LANE)).astype(jnp.int32)
    off_ref[0] = jnp.broadcast_to(off, (SUBLANE, LANE)).astype(jnp.int32)
    groups = own_ref.shape[-1]
    ends = jnp.broadcast_to(off + cpad, (LANE, LANE)).T
    ends = jnp.concatenate([ends] * (groups // LANE), axis=1)
    start = lax.broadcasted_iota(jnp.int32, (LANE, groups), 1).astype(F32) * SEG_ALIGN
    real = lax.broadcasted_iota(jnp.int32, (LANE, groups), 0) < N_EXPERTS
    before = jnp.where(real, jnp.where(ends <= start, 1.0, 0.0), 0.0)
    owner = jnp.minimum(jnp.sum(before, axis=0, keepdims=True), N_EXPERTS - 1.0)
    own_ref[0] = jnp.broadcast_to(owner, (SUBLANE, groups)).astype(jnp.int32)


def _moe_kernel(cnt_s, off_s, own_s, h_ref, pos_ref, gem_ref, wg_ref, wu_ref, wd_ref,
                sg_ref, su_ref, sd_ref, x_ref, mod_ref, fn_ref, o_ref, y_ref, p_ref,
                *, bsz, rows_per_batch, n_lat, tiles_per_batch, final):
    i, s = pl.program_id(0), pl.program_id(1)
    rows, d = h_ref.shape[1], h_ref.shape[2]
    h = h_ref[0]

    def ffn(x, wg, wu, wd, gate):
        a = _silu(jnp.dot(x, wg, preferred_element_type=F32)) * jnp.dot(x, wu, preferred_element_type=F32)
        if gate is not None:
            a = a * gate
        return jnp.dot(a.astype(BF16), wd, preferred_element_type=F32)

    @pl.when(s == 0)
    def _():
        y_ref[...] = jnp.zeros(y_ref.shape, BF16)
        o_ref[0] = ffn(h, sg_ref[...].astype(BF16), su_ref[...].astype(BF16), sd_ref[...].astype(BF16), None)

    cap_rows = lax.broadcasted_iota(jnp.int32, (MOE_CAP, 1), 0)
    for j in range(EXPERTS_PER_STEP):
        e = s * EXPERTS_PER_STEP + j
        cnt, off = cnt_s[i, e], off_s[i, e]
        pos_row = pos_ref[0, pl.ds(e, 1), :]
        gate_row = gem_ref[0, pl.ds(e, 1), :]
        wg, wu, wd = wg_ref[j].astype(BF16), wu_ref[j].astype(BF16), wd_ref[j].astype(BF16)

        def block(q, carry):
            ranks = cap_rows + q * MOE_CAP
            hit = pos_row == ranks.astype(F32)
            x = jnp.dot(jnp.where(hit, 1.0, 0.0).astype(BF16), h, preferred_element_type=F32).astype(BF16)
            gate = jnp.sum(jnp.where(hit, gate_row, 0.0), axis=1, keepdims=True)
            y = ffn(x, wg, wu, wd, gate)
            dst = pl.multiple_of(off + q * MOE_CAP, SEG_ALIGN)
            y_ref[pl.ds(dst, MOE_CAP), :] = jnp.where(ranks < cnt, y, 0.0).astype(BF16)
            return carry

        lax.fori_loop(0, (cnt + MOE_CAP - 1) // MOE_CAP, block, 0)

    @pl.when(s == N_EXPERTS // EXPERTS_PER_STEP - 1)
    def _():
        group_rows = lax.broadcasted_iota(jnp.int32, (SEG_ALIGN, 1), 0)
        groups_per_block = COMBINE_BLOCK // SEG_ALIGN

        def combine(r, carry):
            base = pl.multiple_of(r * COMBINE_BLOCK, COMBINE_BLOCK)

            def onehot_groups(c, carry):
                for u in range(GROUP_UNROLL):
                    gi = c * GROUP_UNROLL + u
                    g0 = pl.multiple_of(gi * SEG_ALIGN, SEG_ALIGN)
                    e = own_s[i, r * groups_per_block + gi]
                    rank = (group_rows + (base + g0 - off_s[i, e])).astype(F32)
                    hit = pos_ref[0, pl.ds(e, 1), :] == rank
                    p_ref[pl.ds(g0, SEG_ALIGN), :] = jnp.where(hit, 1.0, 0.0).astype(BF16)
                return carry

            lax.fori_loop(0, groups_per_block // GROUP_UNROLL, onehot_groups, 0)
            for n in range(d // ROW_TILE):
                cols = slice(n * ROW_TILE, (n + 1) * ROW_TILE)
                o_ref[0, :, cols] += lax.dot_general(p_ref[...], y_ref[pl.ds(base, COMBINE_BLOCK), cols],
                                                     (((0,), (0,)), ((), ())), preferred_element_type=F32)
            return carry

        lax.fori_loop(0, MOE_SLOTS // COMBINE_BLOCK, combine, 0)

        ctx = mod_ref[bsz:bsz + 1, 5 * d:6 * d]
        if tiles_per_batch is None:
            r = i * rows + lax.broadcasted_iota(jnp.int32, (rows, 1), 0)
            g2 = ctx
            for b in range(bsz):
                lo = b * rows_per_batch
                inside = jnp.where(r >= lo, jnp.where(r < lo + n_lat, 1, 0), 0)
                g2 = jnp.where(inside == 1, mod_ref[b:b + 1, 5 * d:6 * d], g2)
        else:
            g2 = mod_ref[pl.ds(i // tiles_per_batch, 1), 5 * d:6 * d]
        y = x_ref[0] + g2 * o_ref[0]
        if final:
            y = _rms(y) * fn_ref[...]
        o_ref[0] = y


def _moe(h, gates, xs, mod, wg, wu, wd, sg, su, sd, final_norm, layer, n_lat, final):
    bsz, m, d = xs.shape
    t = MOE_TILE
    if final:
        assert n_lat % t == 0
        tiles_per_batch = n_lat // t
        n_tiles = bsz * tiles_per_batch
        where = lambda i: (i // tiles_per_batch, i % tiles_per_batch, 0)
        out_shape = (bsz, n_lat, d)
    else:
        assert (bsz * m) % t == 0
        tiles_per_batch = None
        n_tiles = bsz * m // t
        where = lambda i: (0, i, 0)
        out_shape = (1, bsz * m, d)
        h, gates, xs = (a.reshape(1, bsz * m, a.shape[-1]) for a in (h, gates, xs))
    once = pl.Buffered(1)

    tok = jnp.arange(t)
    tri = (tok[None, :] < tok[:, None]).astype(BF16)
    lane = jnp.arange(LANE)
    upper = (lane[:, None] < lane[None, :]).astype(F32)
    groups = -(-MOE_SLOTS // SEG_ALIGN // LANE) * LANE
    em = jax.ShapeDtypeStruct((n_tiles, LANE, t), F32)
    ints = lambda n: jax.ShapeDtypeStruct((n_tiles, SUBLANE, n), jnp.int32)
    per_tile = lambda a, b: pl.BlockSpec((1, a, b), lambda i: (i, 0, 0))
    pos, gem, cnt, off, own = pl.pallas_call(
        _plan_kernel,
        grid=(n_tiles,),
        in_specs=[pl.BlockSpec((1, t, LANE), lambda i: where(i)),
                  pl.BlockSpec((t, t), lambda i: (0, 0)), pl.BlockSpec((LANE, LANE), lambda i: (0, 0))],
        out_specs=[per_tile(LANE, t), per_tile(LANE, t), per_tile(SUBLANE, LANE), per_tile(SUBLANE, LANE),
                   per_tile(SUBLANE, groups)],
        out_shape=[em, em, ints(LANE), ints(LANE), ints(groups)],
        compiler_params=_params(("parallel",)),
        name="moe_plan",
    )(gates, tri, upper)

    tile = lambda n: pl.BlockSpec((1, t, n), lambda i, s, *_: where(i), pipeline_mode=once)
    planned = lambda a, b: pl.BlockSpec((1, a, b), lambda i, s, *_: (i, 0, 0), pipeline_mode=once)
    whole = lambda a: pl.BlockSpec(a.shape, lambda i, s, *_: (0,) * a.ndim, pipeline_mode=once)
    experts = lambda a: pl.BlockSpec((None, EXPERTS_PER_STEP) + a.shape[2:], lambda i, s, *_: (layer, s, 0, 0))
    out = pl.pallas_call(
        functools.partial(_moe_kernel, bsz=bsz, rows_per_batch=m, n_lat=n_lat,
                          tiles_per_batch=tiles_per_batch, final=final),
        grid_spec=pltpu.PrefetchScalarGridSpec(
            num_scalar_prefetch=3,
            grid=(n_tiles, N_EXPERTS // EXPERTS_PER_STEP),
            in_specs=[tile(d), planned(LANE, t), planned(LANE, t),
                      experts(wg), experts(wu), experts(wd), whole(sg), whole(su), whole(sd),
                      tile(d), whole(mod), whole(final_norm)],
            out_specs=pl.BlockSpec((1, t, d), lambda i, s, *_: where(i)),
            scratch_shapes=[pltpu.VMEM((MOE_SLOTS + MOE_CAP, d), BF16), pltpu.VMEM((COMBINE_BLOCK, t), BF16)]),
        out_shape=jax.ShapeDtypeStruct(out_shape, F32),
        compiler_params=_params(("parallel", "arbitrary"), MOE_VMEM_LIMIT),
        name="moe_final" if final else "moe",
    )(cnt[:, 0], off[:, 0], own[:, 0], h, pos, gem, wg, wu, wd, sg, su, sd, xs, mod, final_norm)
    return out if final else out.reshape(bsz, m, d)


def _rope_tables(n_lat, m, head_dim, width):
    pos = jnp.arange(n_lat, dtype=jnp.int32)
    row = (pos // GRID_W).astype(F32)
    col = (pos % GRID_W).astype(F32)
    half = head_dim // 2
    quarter = half // 2
    lane = jnp.arange(head_dim)
    freq = (lane % quarter).astype(F32)
    inv = ROPE_THETA ** (-(2.0 * freq) / half)
    ang = jnp.where(lane[None, :] < half, row[:, None], col[:, None]) * inv[None, :]
    low = (lane % half) < quarter
    cos = jnp.cos(ang)
    sin = jnp.sin(ang)
    slo = jnp.where(low[None, :], -sin, 0.0)
    shi = jnp.where(low[None, :], 0.0, sin)
    reps = width // head_dim
    pad = lambda a, fill: jnp.concatenate(
        [jnp.tile(a, (1, reps)), jnp.full((m - n_lat, width), fill, F32)], axis=0)
    return pad(cos, 1.0), pad(slo, 0.0), pad(shi, 0.0)


def _mla_tables(tabs16):
    outs = []
    for a, fill in zip(tabs16, (1.0, 0.0, 0.0)):
        m = a.shape[0]
        slot = jnp.concatenate([jnp.full((m, MLA_NOPE), fill, F32), a,
                                jnp.full((m, QK_SLOT - MLA_QK), fill, F32)], axis=1)
        outs.append(jnp.tile(slot, (1, MLA_HEADS)))
    return tuple(outs)


def _slots(w, heads, width):
    w = w.reshape(w.shape[0], heads, width)
    pad = jnp.zeros((w.shape[0], heads, QK_SLOT - width), w.dtype)
    return jnp.concatenate([w, pad], axis=-1).reshape(w.shape[0], heads * QK_SLOT)


def _block_diag(w):
    dirs = w.shape[0]
    eye = jnp.eye(LRU_BLOCKS, dtype=w.dtype)
    full = jnp.einsum('dnij,nm->dnimj', w, eye).reshape(dirs, LRU_WIDTH, LRU_WIDTH)
    nl = LRU_WIDTH // LRU_LANES
    return jnp.stack([full[:, j * LRU_LANES:(j + 1) * LRU_LANES, j * LRU_LANES:(j + 1) * LRU_LANES]
                      for j in range(nl)], axis=1)


def kernel(x, c, ctx, c_ctx, mod_w, mod_b, ab_w_in, ab_w_out, diff_lambda_q, diff_lambda_k, diff_subln, lru_conv_w, lru_conv_b, lru_w_a, lru_b_a, lru_w_x, lru_b_x, lru_lambda, cd_w_in, cd_w_out, gqa_q_norm, gqa_k_norm, mla_q_norm, mla_kv_norm, mla_w_uq, mla_w_ukv, router_w, router_bias, exp_w_gate, exp_w_up, exp_w_down, sh_w_gate, sh_w_up, sh_w_down, final_norm):
    bsz, n_lat, d = x.shape
    n_ctx = ctx.shape[1]
    m = n_lat + n_ctx
    depth = mod_w.shape[0]
    assert depth == 2 and bsz < MOD_ROWS
    assert n_lat % ROW_TILE == 0 and n_ctx % ROW_TILE == 0 and n_lat % GRID_W == 0
    ctx_row = bsz

    xs = jnp.concatenate([x, ctx], axis=1)
    c_all = jnp.concatenate([c, c_ctx[None, :], jnp.zeros((MOD_ROWS - bsz - 1, d), F32)], axis=0)
    mods = _modulation(c_all, mod_w, mod_b)

    tabs64 = _rope_tables(n_lat, m, DIFF_HEAD_DIM, DIFF_HEADS * 2 * DIFF_HEAD_DIM)
    tabs_m = _mla_tables(_rope_tables(n_lat, m, MLA_ROPE, MLA_ROPE))

    exp_w = tuple(w.astype(BF16) for w in (exp_w_gate, exp_w_up, exp_w_down))

    def moe_weights(i):
        return exp_w + (sh_w_gate[i], sh_w_up[i], sh_w_down[i])

    fn = final_norm.reshape(1, d)

    mod = mods[0]
    qt, k, vt, u, g = _proj_ab(xs, mod, ab_w_in[0].astype(BF16), tabs64, n_lat, ctx_row)
    lambda_init = 0.8 - 0.6 * math.exp(-0.3 * 0)
    dv = 2 * DIFF_HEAD_DIM
    jobs = tuple(_Job((2 * h + j) * QK_SLOT, h, j, h * dv, dv) for h in range(DIFF_HEADS) for j in range(2))
    o_att = _attention(qt, k, vt, (diff_lambda_q[0], diff_lambda_k[0], diff_subln[0].reshape(-1, 1)),
                       jobs, n_lat, lambda_init, "diff_attention")
    o_rec = _lru(u, g, lru_conv_w[0], lru_conv_b[0].reshape(1, -1),
                 _block_diag(lru_w_a[0]).astype(BF16), lru_b_a[0].reshape(2, 1, LRU_WIDTH),
                 _block_diag(lru_w_x[0]).astype(BF16), lru_b_x[0].reshape(2, 1, LRU_WIDTH),
                 lru_lambda[0].reshape(2, 1, LRU_WIDTH), n_lat)
    xs, h2, gates = _out_proj(o_att, o_rec, xs, mod, ab_w_out[0].astype(BF16),
                              router_w[0].T, router_bias[0].reshape(-1, 1), n_lat, ctx_row)
    xs = _moe(h2, gates, xs, mod, *moe_weights(0), fn, 0, n_lat, False)

    mod = mods[1]
    w_in = cd_w_in[0]
    wq, wk = GQA_HEADS * GQA_HEAD_DIM, GQA_KV_HEADS * GQA_HEAD_DIM
    o_kr = wq + 2 * wk + MLA_Q_RANK + MLA_KV_RANK
    kr_slots = jnp.concatenate([jnp.zeros((d, MLA_NOPE), F32), w_in[:, o_kr:o_kr + MLA_ROPE],
                                jnp.zeros((d, QK_SLOT - MLA_QK), F32)], axis=1)
    w_in = jnp.concatenate([w_in[:, :o_kr], jnp.tile(kr_slots, (1, MLA_HEADS))], axis=1).astype(BF16)
    ukv = mla_w_ukv[0].reshape(MLA_KV_RANK, MLA_HEADS, MLA_NOPE + MLA_V)
    w_kn = _slots(ukv[:, :, :MLA_NOPE].reshape(MLA_KV_RANK, -1), MLA_HEADS, MLA_NOPE).astype(BF16)
    w_v = ukv[:, :, MLA_NOPE:].reshape(MLA_KV_RANK, MLA_HEADS * MLA_V).astype(BF16)
    w_uq = _slots(mla_w_uq[0], MLA_HEADS, MLA_QK).astype(BF16)
    lane = jnp.arange(wq)
    ones_bd = (lane[:, None] // GQA_HEAD_DIM == lane[None, :] // GQA_HEAD_DIM).astype(BF16)
    qgt, kg, vgt, qmt, km, vmt = _proj_cd(
        xs, mod, w_in, ones_bd, jnp.tile(gqa_q_norm[0], GQA_HEADS).reshape(1, -1),
        jnp.tile(gqa_k_norm[0], GQA_KV_HEADS).reshape(1, -1), mla_q_norm[0].reshape(1, -1),
        mla_kv_norm[0].reshape(1, -1), w_uq, w_kn, w_v, tabs64, tabs_m, n_lat, ctx_row)
    groups = GQA_HEADS // GQA_KV_HEADS
    jobs = tuple(_Job(h * QK_SLOT, 0, h // groups, (h // groups) * GQA_HEAD_DIM, GQA_HEAD_DIM)
                 for h in range(GQA_HEADS))
    o_gqa = _attention(qgt, kg, vgt, (), jobs, n_lat, None, "gqa_attention")
    jobs = tuple(_Job(h * QK_SLOT, h // 2, h % 2, h * MLA_V, MLA_V) for h in range(MLA_HEADS))
    o_mla = _attention(qmt, km, vmt, (), jobs, n_lat, None, "mla_attention")
    xs, h2, gates = _out_proj(o_gqa, o_mla, xs, mod, cd_w_out[0].astype(BF16),
                              router_w[1].T, router_bias[1].reshape(-1, 1), n_lat, ctx_row)
    return _moe(h2, gates, xs, mod, *moe_weights(1), fn, 1, n_lat, True)
```

```python
import functools
import math
from typing import NamedTuple

import jax
import jax.numpy as jnp
from jax import lax
from jax.experimental import pallas as pl
from jax.experimental.pallas import tpu as pltpu

F32 = jnp.float32
BF16 = jnp.bfloat16

GRID_W = 64
ROPE_THETA = 10000.0
EPS = 1e-6
DIFF_HEADS = 4
DIFF_HEAD_DIM = 64
LRU_WIDTH = 512
LRU_BLOCKS = 8
LRU_BW = LRU_WIDTH // LRU_BLOCKS
CONV_W = 4
LRU_C = 8.0
GQA_HEADS = 8
GQA_KV_HEADS = 2
GQA_HEAD_DIM = 64
MLA_HEADS = 8
MLA_Q_RANK = 256
MLA_KV_RANK = 128
MLA_NOPE = 32
MLA_ROPE = 16
MLA_V = 64
MLA_QK = MLA_NOPE + MLA_ROPE
N_EXPERTS = 64
N_GROUPS = 8
PER_GROUP = N_EXPERTS // N_GROUPS
TOPK_GROUPS = 4
TOP_K = 8
EXPERT_FF = 256
ROUTED_SCALE = 2.5

QK_SLOT = 64
LOG2E = math.log2(math.e)

ROW_TILE = 256
LRU_CHUNK = 128
LRU_LANES = 256
EXPERTS_PER_STEP = 4
LANE = 128
SUBLANE = 8
MOD_ROWS = 8
VMEM_LIMIT = 56 * 1024 * 1024
MOE_VMEM_LIMIT = 62 * 1024 * 1024
MOE_TILE = 1024
MOE_CAP = 160
SEG_ALIGN = 16
MOE_SLOTS = MOE_TILE * TOP_K + N_EXPERTS * SEG_ALIGN
COMBINE_BLOCK = 1024
GROUP_UNROLL = 4


def _params(sem, vmem=VMEM_LIMIT):
    return pltpu.CompilerParams(dimension_semantics=sem, vmem_limit_bytes=vmem)


def _silu(x):
    return x * jax.nn.sigmoid(x)


def _rms(x):
    return x * lax.rsqrt(jnp.mean(x * x, axis=-1, keepdims=True) + EPS)


def _mod_rows(mod_ref, ctx_row, b, k, d, row0, rows, n_lat):
    lat = mod_ref[pl.ds(b, 1), k * d:(k + 1) * d]
    ctx = mod_ref[ctx_row:ctx_row + 1, k * d:(k + 1) * d]
    if n_lat % rows == 0:
        return jnp.where(row0 >= n_lat, ctx, lat)
    r = row0 + lax.broadcasted_iota(jnp.int32, (rows, 1), 0)
    return jnp.where(r >= n_lat, ctx, lat)


def _rope(x, cos, sin_lo, sin_hi, half):
    n = x.shape[-1]
    return x * cos + pltpu.roll(x, n - half, 1) * sin_lo + pltpu.roll(x, half, 1) * sin_hi


def _mod_kernel(c_ref, w_ref, b_ref, o_ref):
    s = _silu(c_ref[...])
    o_ref[0] = jnp.dot(s, w_ref[0], preferred_element_type=F32, precision=lax.Precision.HIGHEST) + b_ref[0]


def _modulation(c_all, mod_w, mod_b):
    depth, d, n = mod_w.shape
    tn = n // 4
    return pl.pallas_call(
        _mod_kernel,
        grid=(depth, n // tn),
        in_specs=[pl.BlockSpec((MOD_ROWS, d), lambda i, j: (0, 0)),
                  pl.BlockSpec((1, d, tn), lambda i, j: (i, 0, j)),
                  pl.BlockSpec((1, 1, tn), lambda i, j: (i, 0, j))],
        out_specs=pl.BlockSpec((1, MOD_ROWS, tn), lambda i, j: (i, 0, j)),
        out_shape=jax.ShapeDtypeStruct((depth, MOD_ROWS, n), F32),
        compiler_params=_params(("arbitrary", "arbitrary")),
        name="modulation",
    )(c_all, mod_w, mod_b.reshape(depth, 1, n))


def _proj_ab_kernel(x_ref, mod_ref, w_ref, cos_ref, slo_ref, shi_ref,
                    qt_ref, k_ref, vt_ref, u_ref, g_ref, *, n_lat, ctx_row):
    b, t = pl.program_id(0), pl.program_id(1)
    d = x_ref.shape[-1]
    row0 = t * ROW_TILE
    shift = _mod_rows(mod_ref, ctx_row, b, 0, d, row0, ROW_TILE, n_lat)
    scale = _mod_rows(mod_ref, ctx_row, b, 1, d, row0, ROW_TILE, n_lat)
    h = (_rms(x_ref[0]) * (1.0 + scale) + shift).astype(BF16)
    acc = jnp.dot(h, w_ref[...], preferred_element_type=F32)
    w = DIFF_HEADS * 2 * DIFF_HEAD_DIM
    cos, slo, shi = cos_ref[...], slo_ref[...], shi_ref[...]
    q = _rope(acc[:, 0:w], cos, slo, shi, DIFF_HEAD_DIM // 4) * (DIFF_HEAD_DIM ** -0.5 * LOG2E)
    k = _rope(acc[:, w:2 * w], cos, slo, shi, DIFF_HEAD_DIM // 4)
    qt_ref[0, 0] = q.T.astype(BF16)
    k_ref[0] = k.astype(BF16)
    vt_ref[0, 0] = acc[:, 2 * w:3 * w].T.astype(BF16)
    u_ref[0] = acc[:, 3 * w:3 * w + LRU_WIDTH]
    g_ref[0] = jax.nn.gelu(acc[:, 3 * w + LRU_WIDTH:3 * w + 2 * LRU_WIDTH]).astype(BF16)


def _proj_ab(xs, mod, w_in, tabs, n_lat, ctx_row):
    bsz, m, d = xs.shape
    nt = m // ROW_TILE
    w = DIFF_HEADS * 2 * DIFF_HEAD_DIM
    n_in = w_in.shape[1]
    tile = lambda n: pl.BlockSpec((1, ROW_TILE, n), lambda b, t: (b, t, 0))
    ttile = pl.BlockSpec((1, 1, w, ROW_TILE), lambda b, t: (b, t, 0, 0))
    tab = pl.BlockSpec((ROW_TILE, w), lambda b, t: (t, 0))
    return pl.pallas_call(
        functools.partial(_proj_ab_kernel, n_lat=n_lat, ctx_row=ctx_row),
        grid=(bsz, nt),
        in_specs=[tile(d),
                  pl.BlockSpec(mod.shape, lambda b, t: (0, 0)),
                  pl.BlockSpec((d, n_in), lambda b, t: (0, 0)),
                  tab, tab, tab],
        out_specs=[ttile, tile(w), ttile, tile(LRU_WIDTH), tile(LRU_WIDTH)],
        out_shape=[jax.ShapeDtypeStruct((bsz, nt, w, ROW_TILE), BF16),
                   jax.ShapeDtypeStruct((bsz, m, w), BF16),
                   jax.ShapeDtypeStruct((bsz, nt, w, ROW_TILE), BF16),
                   jax.ShapeDtypeStruct((bsz, m, LRU_WIDTH), F32),
                   jax.ShapeDtypeStruct((bsz, m, LRU_WIDTH), BF16)],
        compiler_params=_params(("parallel", "parallel")),
        name="proj_ab",
    )(xs, mod, w_in, *tabs)


def _group_rms(x, ones_bd):
    x2 = x * x
    hi = x2.astype(BF16)
    lo = (x2 - hi.astype(F32)).astype(BF16)
    ss = jnp.dot(hi, ones_bd, preferred_element_type=F32) + jnp.dot(lo, ones_bd, preferred_element_type=F32)
    return x * lax.rsqrt(ss * (1.0 / GQA_HEAD_DIM) + EPS)


def _proj_cd_kernel(x_ref, mod_ref, w_ref, bd_ref, qn_ref, kn_ref, cqn_ref, ckvn_ref, wuq_ref, wkn_ref, wv_ref,
                    cos_ref, slo_ref, shi_ref, cosm_ref, slom_ref, shim_ref,
                    qgt_ref, kg_ref, vgt_ref, qmt_ref, km_ref, vmt_ref, *, n_lat, ctx_row):
    b, t = pl.program_id(0), pl.program_id(1)
    d = x_ref.shape[-1]
    row0 = t * ROW_TILE
    shift = _mod_rows(mod_ref, ctx_row, b, 0, d, row0, ROW_TILE, n_lat)
    scale = _mod_rows(mod_ref, ctx_row, b, 1, d, row0, ROW_TILE, n_lat)
    h = (_rms(x_ref[0]) * (1.0 + scale) + shift).astype(BF16)
    acc = jnp.dot(h, w_ref[...], preferred_element_type=F32)
    wq = GQA_HEADS * GQA_HEAD_DIM
    wk = GQA_KV_HEADS * GQA_HEAD_DIM
    o = 0
    q = acc[:, o:o + wq]; o += wq
    k = acc[:, o:o + wk]; o += wk
    v = acc[:, o:o + wk]; o += wk
    cq = acc[:, o:o + MLA_Q_RANK]; o += MLA_Q_RANK
    ckv = acc[:, o:o + MLA_KV_RANK]; o += MLA_KV_RANK
    kr = acc[:, o:o + MLA_HEADS * QK_SLOT]
    cos, slo, shi = cos_ref[...], slo_ref[...], shi_ref[...]
    bd = bd_ref[...]
    q = (_rope(_group_rms(q, bd) * qn_ref[...], cos, slo, shi, GQA_HEAD_DIM // 4)
         * (GQA_HEAD_DIM ** -0.5 * LOG2E))
    k = _rope(_group_rms(k, bd[:wk, :wk]) * kn_ref[...], cos[:, :wk], slo[:, :wk], shi[:, :wk], GQA_HEAD_DIM // 4)
    qgt_ref[0, 0] = q.T.astype(BF16)
    kg_ref[0] = k.astype(BF16)
    vgt_ref[0, 0] = v.T.astype(BF16)
    cosm, slom, shim = cosm_ref[...], slom_ref[...], shim_ref[...]
    cqn = (_rms(cq) * cqn_ref[...]).astype(BF16)
    mq = jnp.dot(cqn, wuq_ref[...], preferred_element_type=F32)
    qmt_ref[0, 0] = (_rope(mq, cosm, slom, shim, MLA_ROPE // 4) * (MLA_QK ** -0.5 * LOG2E)).T.astype(BF16)
    ckvn = (_rms(ckv) * ckvn_ref[...]).astype(BF16)
    km = jnp.dot(ckvn, wkn_ref[...], preferred_element_type=F32) + _rope(kr, cosm, slom, shim, MLA_ROPE // 4)
    km_ref[0] = km.astype(BF16)
    vmt_ref[0, 0] = jnp.dot(ckvn, wv_ref[...], preferred_element_type=F32).T.astype(BF16)


def _proj_cd(xs, mod, w_in, bd, qn, kn, cqn, ckvn, wuq, wkn, wv, tabs, tabs_m, n_lat, ctx_row):
    bsz, m, d = xs.shape
    nt = m // ROW_TILE
    wq = GQA_HEADS * GQA_HEAD_DIM
    wk = GQA_KV_HEADS * GQA_HEAD_DIM
    wm = MLA_HEADS * QK_SLOT
    wmv = MLA_HEADS * MLA_V
    tile = lambda n: pl.BlockSpec((1, ROW_TILE, n), lambda b, t: (b, t, 0))
    ttile = lambda n: pl.BlockSpec((1, 1, n, ROW_TILE), lambda b, t: (b, t, 0, 0))
    whole = lambda a: pl.BlockSpec(a.shape, lambda b, t: (0,) * a.ndim)
    tab = pl.BlockSpec((ROW_TILE, wq), lambda b, t: (t, 0))
    tabm = pl.BlockSpec((ROW_TILE, wm), lambda b, t: (t, 0))
    return pl.pallas_call(
        functools.partial(_proj_cd_kernel, n_lat=n_lat, ctx_row=ctx_row),
        grid=(bsz, nt),
        in_specs=[tile(d), whole(mod), whole(w_in), whole(bd), whole(qn), whole(kn), whole(cqn), whole(ckvn),
                  whole(wuq), whole(wkn), whole(wv), tab, tab, tab, tabm, tabm, tabm],
        out_specs=[ttile(wq), tile(wk), ttile(wk), ttile(wm), tile(wm), ttile(wmv)],
        out_shape=[jax.ShapeDtypeStruct((bsz, nt, wq, ROW_TILE), BF16),
                   jax.ShapeDtypeStruct((bsz, m, wk), BF16),
                   jax.ShapeDtypeStruct((bsz, nt, wk, ROW_TILE), BF16),
                   jax.ShapeDtypeStruct((bsz, nt, wm, ROW_TILE), BF16),
                   jax.ShapeDtypeStruct((bsz, m, wm), BF16),
                   jax.ShapeDtypeStruct((bsz, nt, wmv, ROW_TILE), BF16)],
        compiler_params=_params(("parallel", "parallel")),
        name="proj_cd",
    )(xs, mod, w_in, bd, qn, kn, cqn, ckvn, wuq, wkn, wv, *tabs, *tabs_m)


class _Job(NamedTuple):
    q_row: int
    k_group: int
    k_half: int
    v_row: int
    dv: int


def _attend(qt_ref, qnext_ref, k_ref, vt_ref, s_ref, mp_ref, jobs, lo, hi, chained, stack):
    assert len(jobs) % 2 == 0
    tq = qt_ref.shape[-1]
    n = hi - lo
    group = next(g for g in (11, 3, 2, 1) if n % g == 0)
    steps = n // group
    row_groups = ROW_TILE // SUBLANE

    def weights(job, ref=qt_ref):
        qh = ref[0, 0, job.q_row:job.q_row + QK_SLOT, :]
        z = jnp.zeros_like(qh)
        return jnp.concatenate([z, qh] if job.k_half else [qh, z], axis=0)

    def score(c, buf, qw, job, mp):
        r0 = pl.multiple_of(c * ROW_TILE, ROW_TILE)
        kc = k_ref[0, pl.ds(r0, ROW_TILE), job.k_group * LANE:(job.k_group + 1) * LANE]
        s = jnp.dot(kc, qw, preferred_element_type=F32)
        s_ref[buf, c] = s
        return jnp.maximum(mp, jnp.max(s.reshape(row_groups, SUBLANE, tq), axis=0))

    def value(c, buf, m, job, lp, acc):
        p = jnp.exp2(s_ref[buf, c] - m)
        lp = lp + jnp.sum(p.reshape(row_groups, SUBLANE, tq), axis=0)
        acc = acc + jnp.dot(vt_ref[0, c, job.v_row:job.v_row + job.dv, :], p.astype(BF16),
                            preferred_element_type=F32)
        return lp, acc

    def loop(body, init):
        return body(0, init) if steps == 1 else lax.fori_loop(0, steps, body, init)

    neg = jnp.full((SUBLANE, tq), -jnp.inf, F32)

    def prologue():
        qw = weights(jobs[0])

        def first(i, mp):
            for g in range(group):
                mp = score(lo + i * group + g, 0, qw, jobs[0], mp)
            return mp

        mp_ref[...] = loop(first, neg)

    if chained:
        pl.when(pl.program_id(1) == 0)(prologue)
    else:
        prologue()
    mp = mp_ref[...]
    results = []
    for j, job in enumerate(jobs):
        buf = j % 2
        m = jnp.max(mp, axis=0, keepdims=True)
        if j + 1 < len(jobs):
            nxt, qw = jobs[j + 1], weights(jobs[j + 1])
        elif chained:
            nxt, qw = jobs[0], weights(jobs[0], qnext_ref)
        else:
            nxt = None

        def body(i, carry):
            mp, lp, acc = carry
            if stack and nxt is not None:
                c0 = lo + i * group
                r0 = pl.multiple_of(c0 * ROW_TILE, ROW_TILE)
                kc = k_ref[0, pl.ds(r0, group * ROW_TILE), nxt.k_group * LANE:(nxt.k_group + 1) * LANE]
                s_all = jnp.dot(kc, qw, preferred_element_type=F32)
                for g in range(group):
                    s = s_all[g * ROW_TILE:(g + 1) * ROW_TILE]
                    s_ref[1 - buf, c0 + g] = s
                    mp = jnp.maximum(mp, jnp.max(s.reshape(row_groups, SUBLANE, tq), axis=0))
            for g in range(group):
                c = lo + i * group + g
                if nxt is not None and not stack:
                    mp = score(c, 1 - buf, qw, nxt, mp)
                lp, acc = value(c, buf, m, job, lp, acc)
            return mp, lp, acc

        mp, lp, acc = loop(body, (neg, jnp.zeros((SUBLANE, tq), F32), jnp.zeros((job.dv, tq), F32)))
        results.append((acc, jnp.sum(lp, axis=0, keepdims=True)))
    if chained:
        mp_ref[...] = mp
    return results


def _attn_kernel(*refs, jobs, n_lat, lambda_init, stack):
    if lambda_init is None:
        qt_ref, qnext_ref, k_ref, vt_ref, o_ref, s_ref, mp_ref = refs
    else:
        qt_ref, qnext_ref, k_ref, vt_ref, lq_ref, lk_ref, sub_ref, o_ref, s_ref, mp_ref = refs
    n_chunks = vt_ref.shape[1]
    first_ctx = n_lat // ROW_TILE
    t = pl.program_id(1)

    def run(lo, hi, chained):
        res = _attend(qt_ref, qnext_ref, k_ref, vt_ref, s_ref, mp_ref, jobs, lo, hi, chained, stack)
        if lambda_init is None:
            outs = [acc / l for acc, l in res]
        else:
            lq, lk = lq_ref[...], lk_ref[...]
            lam = (jnp.exp(jnp.sum(lq[0:1] * lk[0:1], axis=-1, keepdims=True))
                   - jnp.exp(jnp.sum(lq[1:2] * lk[1:2], axis=-1, keepdims=True)) + lambda_init)
            outs = []
            for h in range(len(res) // 2):
                (a0, l0), (a1, l1) = res[2 * h], res[2 * h + 1]
                o = a0 / l0 - lam * (a1 / l1)
                o = o * lax.rsqrt(jnp.mean(o * o, axis=0, keepdims=True) + EPS)
                outs.append(o * sub_ref[...] * (1.0 - lambda_init))
        o_ref[0] = jnp.concatenate(outs, axis=0).T.astype(BF16)

    @pl.when(t < first_ctx)
    def _():
        run(0, n_chunks, True)

    @pl.when(t >= first_ctx)
    def _():
        run(first_ctx, n_chunks, False)


def _attention(qt, k, vt, extra, jobs, n_lat, lambda_init, name, stack=False):
    bsz, nt, wq, tq = qt.shape
    m = k.shape[1]
    wo = sum(j.dv for j in jobs) if lambda_init is None else sum(j.dv for j in jobs) // 2
    once = pl.Buffered(1)
    whole = lambda a: pl.BlockSpec(a.shape, lambda b, t: (0,) * a.ndim)
    return pl.pallas_call(
        functools.partial(_attn_kernel, jobs=jobs, n_lat=n_lat, lambda_init=lambda_init, stack=stack),
        grid=(bsz, nt),
        in_specs=[pl.BlockSpec((1, 1, wq, tq), lambda b, t: (b, t, 0, 0)),
                  pl.BlockSpec((1, 1, wq, tq), lambda b, t: (b, jnp.minimum(t + 1, nt - 1), 0, 0)),
                  pl.BlockSpec((1,) + k.shape[1:], lambda b, t: (b, 0, 0), pipeline_mode=once),
                  pl.BlockSpec((1,) + vt.shape[1:], lambda b, t: (b, 0, 0, 0), pipeline_mode=once)]
                 + [whole(a) for a in extra],
        out_specs=pl.BlockSpec((1, tq, wo), lambda b, t: (b, t, 0)),
        out_shape=jax.ShapeDtypeStruct((bsz, m, wo), BF16),
        scratch_shapes=[pltpu.VMEM((2, nt, ROW_TILE, tq), F32), pltpu.VMEM((SUBLANE, tq), F32)],
        compiler_params=_params(("arbitrary", "arbitrary")),
        name=name,
    )(qt, qt, k, vt, *extra)


def _scan_chunk(a, b, carry, reverse):
    n = a.shape[0]
    rows = lax.broadcasted_iota(jnp.int32, (n, 1), 0)
    d = 1
    while d < n:
        sh = n - d if reverse else d
        valid = rows < n - d if reverse else rows >= d
        b = jnp.where(valid, a * pltpu.roll(b, sh, 0) + b, b)
        a = jnp.where(valid, a * pltpu.roll(a, sh, 0), a)
        d *= 2
    h = a * carry + b
    return h, (h[0:1] if reverse else h[n - 1:n])


def _lru_kernel(u_ref, g_ref, cw_ref, cb_ref, wa_ref, ba_ref, wx_ref, bx_ref, lam_ref, o_ref, up_ref, hf_ref, *, n_lat):
    m = u_ref.shape[1]
    lanes = u_ref.shape[2]
    n_chunks = m // LRU_CHUNK
    n_ctx_chunks = (m - n_lat) // LRU_CHUNK
    pad = SUBLANE

    up_ref[0:pad, :] = jnp.zeros((pad, lanes), F32)
    up_ref[pad + m:pad + m + pad, :] = jnp.zeros((pad, lanes), F32)

    def copy(c, _):
        r0 = pl.multiple_of(c * LRU_CHUNK, LRU_CHUNK)
        up_ref[pl.ds(pad + r0, LRU_CHUNK), :] = u_ref[0, pl.ds(r0, LRU_CHUNK), :]
        return 0

    lax.fori_loop(0, n_chunks, copy, 0)

    cw = cw_ref[...]
    cb = cb_ref[...]
    win_rows = LRU_CHUNK + 2 * pad

    def coeffs(c, d):
        r0 = pl.multiple_of(c * LRU_CHUNK, LRU_CHUNK)
        win = up_ref[pl.ds(r0, win_rows), :]
        r = r0 + lax.broadcasted_iota(jnp.int32, (LRU_CHUNK, 1), 0)
        at = lambda k: pltpu.roll(win, (win_rows - k) % win_rows, 0)[pad:pad + LRU_CHUNK]
        y = (cw[0:1] * jnp.where((r == n_lat) | (r == n_lat + 1), 0.0, at(-2))
             + cw[1:2] * jnp.where(r == n_lat, 0.0, at(-1))
             + cw[2:3] * win[pad:pad + LRU_CHUNK]
             + cw[3:4] * jnp.where(r == n_lat - 1, 0.0, at(1))
             + cb)
        yb = y.astype(BF16)
        ra = jax.nn.sigmoid(jnp.dot(yb, wa_ref[d, 0], preferred_element_type=F32) + ba_ref[d])
        ix = jax.nn.sigmoid(jnp.dot(yb, wx_ref[d, 0], preferred_element_type=F32) + bx_ref[d])
        z = -lam_ref[d]
        softplus = jnp.maximum(z, 0.0) + jnp.log(1.0 + jnp.exp(-jnp.abs(z)))
        a = jnp.exp(-LRU_C * ra * softplus)
        return a, jnp.sqrt(1.0 - a * a) * ix * y

    def fwd(s, carry):
        c = lax.rem(s + (n_chunks - n_ctx_chunks), n_chunks)
        a, b = coeffs(c, 0)
        h, carry = _scan_chunk(a, b, carry, False)
        r0 = pl.multiple_of(c * LRU_CHUNK, LRU_CHUNK)
        hf_ref[pl.ds(r0, LRU_CHUNK), :] = h
        return carry

    lax.fori_loop(0, n_chunks, fwd, jnp.zeros((1, lanes), F32))

    def bwd(s, carry):
        c = n_chunks - 1 - s
        a, b = coeffs(c, 1)
        h, carry = _scan_chunk(a, b, carry, True)
        r0 = pl.multiple_of(c * LRU_CHUNK, LRU_CHUNK)
        tot = hf_ref[pl.ds(r0, LRU_CHUNK), :] + h
        o_ref[0, pl.ds(r0, LRU_CHUNK), :] = (tot * g_ref[0, pl.ds(r0, LRU_CHUNK), :].astype(F32)).astype(BF16)
        return carry

    lax.fori_loop(0, n_chunks, bwd, jnp.zeros((1, lanes), F32))


def _lru(u, g, conv_w, conv_b, wa_bd, ba, wx_bd, bx, lam, n_lat):
    bsz, m, c = u.shape
    nl = c // LRU_LANES
    seq = lambda: pl.BlockSpec((1, m, LRU_LANES), lambda b, j: (b, 0, j))
    vec = lambda a: pl.BlockSpec(a.shape[:-1] + (LRU_LANES,), lambda b, j: (0,) * (a.ndim - 1) + (j,))
    mat = pl.BlockSpec((2, 1, LRU_LANES, LRU_LANES), lambda b, j: (0, j, 0, 0))
    return pl.pallas_call(
        functools.partial(_lru_kernel, n_lat=n_lat),
        grid=(bsz, nl),
        in_specs=[seq(), seq(), vec(conv_w), vec(conv_b), mat, vec(ba), mat, vec(bx), vec(lam)],
        out_specs=seq(),
        out_shape=jax.ShapeDtypeStruct((bsz, m, c), BF16),
        scratch_shapes=[pltpu.VMEM((m + 2 * SUBLANE, LRU_LANES), F32), pltpu.VMEM((m, LRU_LANES), F32)],
        compiler_params=_params(("parallel", "parallel")),
        name="rglru",
    )(u, g, conv_w, conv_b, wa_bd, ba, wx_bd, bx, lam)


def _route(logits_t, bias):
    n = logits_t.shape[-1]
    scores = jax.nn.sigmoid(logits_t)
    choice = scores + bias
    sub = lax.broadcasted_iota(jnp.int32, (PER_GROUP, n), 0)
    neg = -jnp.inf
    groups, gs = [], []
    for g in range(N_GROUPS):
        cg = choice[g * PER_GROUP:(g + 1) * PER_GROUP]
        m1 = jnp.max(cg, axis=0, keepdims=True)
        i1 = jnp.min(jnp.where(cg == m1, sub, PER_GROUP), axis=0, keepdims=True)
        m2 = jnp.max(jnp.where(sub == i1, neg, cg), axis=0, keepdims=True)
        groups.append(cg)
        gs.append(m1 + m2)
    masked = []
    for g in range(N_GROUPS):
        rank = jnp.zeros((1, n), jnp.int32)
        for o in range(N_GROUPS):
            if o != g:
                ahead = (gs[o] >= gs[g]) if o < g else (gs[o] > gs[g])
                rank = rank + jnp.where(ahead, 1, 0)
        masked.append(jnp.where(rank < TOPK_GROUPS, groups[g], neg))
    masked = jnp.concatenate(masked, axis=0)
    eidx = lax.broadcasted_iota(jnp.int32, (N_EXPERTS, n), 0)
    picked = jnp.zeros((N_EXPERTS, n), F32)
    for _ in range(TOP_K):
        mx = jnp.max(masked, axis=0, keepdims=True)
        first = jnp.min(jnp.where(masked == mx, eidx, N_EXPERTS), axis=0, keepdims=True)
        hit = eidx == first
        picked = jnp.where(hit, 1.0, picked)
        masked = jnp.where(hit, neg, masked)
    w = picked * scores
    return w / jnp.sum(w, axis=0, keepdims=True) * ROUTED_SCALE


def _out_proj_kernel(oa_ref, ob_ref, x_ref, mod_ref, wa_ref, wb_ref, rw_ref, rb_ref,
                     xo_ref, h_ref, gates_ref, *, n_lat, ctx_row):
    b, t = pl.program_id(0), pl.program_id(1)
    d = x_ref.shape[-1]
    row0 = t * ROW_TILE
    mv = lambda k: _mod_rows(mod_ref, ctx_row, b, k, d, row0, ROW_TILE, n_lat)
    y = (jnp.dot(oa_ref[0], wa_ref[...], preferred_element_type=F32)
         + jnp.dot(ob_ref[0], wb_ref[...], preferred_element_type=F32))
    x = x_ref[0] + mv(2) * y
    xo_ref[0] = x
    h = _rms(x) * (1.0 + mv(4)) + mv(3)
    h_ref[0] = h.astype(BF16)
    logits_t = lax.dot_general(rw_ref[...], h, (((1,), (1,)), ((), ())),
                               preferred_element_type=F32, precision=lax.Precision.HIGHEST)
    gates_t = _route(logits_t, rb_ref[...])
    gates_t = jnp.concatenate([gates_t, jnp.zeros((LANE - N_EXPERTS, ROW_TILE), F32)], axis=0)
    gates_ref[0] = gates_t.T


def _out_proj(oa, ob, xs, mod, w_out, router_wt, router_b, n_lat, ctx_row):
    bsz, m, d = xs.shape
    na, nb = oa.shape[-1], ob.shape[-1]
    wa, wb = w_out[:na], w_out[na:]
    tile = lambda n: pl.BlockSpec((1, ROW_TILE, n), lambda b, t: (b, t, 0))
    whole = lambda a: pl.BlockSpec(a.shape, lambda b, t: (0,) * a.ndim)
    return pl.pallas_call(
        functools.partial(_out_proj_kernel, n_lat=n_lat, ctx_row=ctx_row),
        grid=(bsz, m // ROW_TILE),
        in_specs=[tile(na), tile(nb), tile(d), whole(mod), whole(wa), whole(wb), whole(router_wt), whole(router_b)],
        out_specs=[tile(d), tile(d), tile(LANE)],
        out_shape=[jax.ShapeDtypeStruct((bsz, m, d), F32),
                   jax.ShapeDtypeStruct((bsz, m, d), BF16),
                   jax.ShapeDtypeStruct((bsz, m, LANE), F32)],
        compiler_params=_params(("parallel", "parallel")),
        name="out_proj_router",
    )(oa, ob, xs, mod, wa, wb, router_wt, router_b)


def _plan_kernel(gates_ref, tri_ref, upper_ref, pos_ref, gem_ref, cnt_ref, off_ref, own_ref):
    g = gates_ref[0]
    sel = g != 0.0
    one = jnp.where(sel, 1.0, 0.0)
    pos = jnp.dot(tri_ref[...], one.astype(BF16), preferred_element_type=F32)
    cnt = jnp.sum(one, axis=0, keepdims=True)
    cpad = jnp.floor((cnt + (SEG_ALIGN - 1)) * (1.0 / SEG_ALIGN)) * SEG_ALIGN
    off = jnp.dot(jnp.broadcast_to(cpad, (SUBLANE, LANE)), upper_ref[...], preferred_element_type=F32,
                  precision=lax.Precision.HIGHEST)[0:1]
    pos_ref[0] = jnp.where(sel, pos, -1.0).T
    gem_ref[0] = g.T
    cnt_ref[0] = jnp.broadcast_to(cnt, (SUBLANE, LANE)).astype(jnp.int32)
    off_ref[0] = jnp.broadcast_to(off, (SUBLANE, LANE)).astype(jnp.int32)
    groups = own_ref.shape[-1]
    ends = jnp.broadcast_to(off + cpad, (LANE, LANE)).T
    ends = jnp.concatenate([ends] * (groups // LANE), axis=1)
    start = lax.broadcasted_iota(jnp.int32, (LANE, groups), 1).astype(F32) * SEG_ALIGN
    real = lax.broadcasted_iota(jnp.int32, (LANE, groups), 0) < N_EXPERTS
    before = jnp.where(real, jnp.where(ends <= start, 1.0, 0.0), 0.0)
    owner = jnp.minimum(jnp.sum(before, axis=0, keepdims=True), N_EXPERTS - 1.0)
    own_ref[0] = jnp.broadcast_to(owner, (SUBLANE, groups)).astype(jnp.int32)


def _moe_kernel(cnt_s, off_s, own_s, h_ref, pos_ref, gem_ref, wg_ref, wu_ref, wd_ref,
                sg_ref, su_ref, sd_ref, x_ref, mod_ref, fn_ref, o_ref, y_ref, p_ref,
                *, bsz, rows_per_batch, n_lat, tiles_per_batch, final, cap, stacked):
    i, s = pl.program_id(0), pl.program_id(1)
    rows, d = h_ref.shape[1], h_ref.shape[2]
    h = h_ref[0]

    def ffn(x, wg, wu, wd, gate):
        a = _silu(jnp.dot(x, wg, preferred_element_type=F32)) * jnp.dot(x, wu, preferred_element_type=F32)
        if gate is not None:
            a = a * gate
        return jnp.dot(a.astype(BF16), wd, preferred_element_type=F32)

    @pl.when(s == 0)
    def _():
        y_ref[...] = jnp.zeros(y_ref.shape, BF16)
        o_ref[0] = ffn(h, sg_ref[...].astype(BF16), su_ref[...].astype(BF16), sd_ref[...].astype(BF16), None)

    cap_rows = lax.broadcasted_iota(jnp.int32, (cap, 1), 0)

    def onehot(pos_row, q):
        ranks = cap_rows + q * cap
        hit = pos_row == ranks.astype(F32)
        return ranks, hit, jnp.where(hit, 1.0, 0.0).astype(BF16)

    def expert_rows(x, ranks, hit, q, gate_row, w, cnt, off):
        gate = jnp.sum(jnp.where(hit, gate_row, 0.0), axis=1, keepdims=True)
        y = ffn(x, *w, gate)
        dst = pl.multiple_of(off + q * cap, SEG_ALIGN)
        y_ref[pl.ds(dst, cap), :] = jnp.where(ranks < cnt, y, 0.0).astype(BF16)

    experts = []
    for j in range(EXPERTS_PER_STEP):
        e = s * EXPERTS_PER_STEP + j
        experts.append((cnt_s[i, e], off_s[i, e],
                        pos_ref[0, pl.ds(e, 1), :],
                        gem_ref[0, pl.ds(e, 1), :],
                        (wg_ref[j], wu_ref[j], wd_ref[j])))

    first_block = 0
    if stacked:
        hots = [onehot(pos_row, 0) for _, _, pos_row, _, _ in experts]
        xs = jnp.dot(jnp.concatenate([p for _, _, p in hots], axis=0), h,
                     preferred_element_type=F32).astype(BF16)
        for j, ((cnt, off, _, gate_row, w), (ranks, hit, _)) in enumerate(zip(experts, hots)):
            expert_rows(xs[j * cap:(j + 1) * cap], ranks, hit, 0, gate_row, w, cnt, off)
        first_block = 1

    for cnt, off, pos_row, gate_row, w in experts:
        def block(q, carry):
            ranks, hit, p = onehot(pos_row, q)
            x = jnp.dot(p, h, preferred_element_type=F32).astype(BF16)
            expert_rows(x, ranks, hit, q, gate_row, w, cnt, off)
            return carry

        lax.fori_loop(first_block, (cnt + cap - 1) // cap, block, 0)

    @pl.when(s == N_EXPERTS // EXPERTS_PER_STEP - 1)
    def _():
        group_rows = lax.broadcasted_iota(jnp.int32, (SEG_ALIGN, 1), 0)
        groups_per_block = COMBINE_BLOCK // SEG_ALIGN

        def combine(r, carry):
            base = pl.multiple_of(r * COMBINE_BLOCK, COMBINE_BLOCK)

            def onehot_groups(c, carry):
                for u in range(GROUP_UNROLL):
                    gi = c * GROUP_UNROLL + u
                    g0 = pl.multiple_of(gi * SEG_ALIGN, SEG_ALIGN)
                    e = own_s[i, r * groups_per_block + gi]
                    rank = (group_rows + (base + g0 - off_s[i, e])).astype(F32)
                    hit = pos_ref[0, pl.ds(e, 1), :] == rank
                    p_ref[pl.ds(g0, SEG_ALIGN), :] = jnp.where(hit, 1.0, 0.0).astype(BF16)
                return carry

            lax.fori_loop(0, groups_per_block // GROUP_UNROLL, onehot_groups, 0)
            for n in range(d // ROW_TILE):
                cols = slice(n * ROW_TILE, (n + 1) * ROW_TILE)
                o_ref[0, :, cols] += lax.dot_general(p_ref[...], y_ref[pl.ds(base, COMBINE_BLOCK), cols],
                                                     (((0,), (0,)), ((), ())), preferred_element_type=F32)
            return carry

        lax.fori_loop(0, MOE_SLOTS // COMBINE_BLOCK, combine, 0)

        ctx = mod_ref[bsz:bsz + 1, 5 * d:6 * d]
        if tiles_per_batch is None:
            r = i * rows + lax.broadcasted_iota(jnp.int32, (rows, 1), 0)
            g2 = ctx
            for b in range(bsz):
                lo = b * rows_per_batch
                inside = jnp.where(r >= lo, jnp.where(r < lo + n_lat, 1, 0), 0)
                g2 = jnp.where(inside == 1, mod_ref[b:b + 1, 5 * d:6 * d], g2)
        else:
            g2 = mod_ref[pl.ds(i // tiles_per_batch, 1), 5 * d:6 * d]
        y = x_ref[0] + g2 * o_ref[0]
        if final:
            y = _rms(y) * fn_ref[...]
        o_ref[0] = y


def _moe(h, gates, xs, mod, wg, wu, wd, sg, su, sd, final_norm, layer, n_lat, final, cap=MOE_CAP, stacked=False):
    bsz, m, d = xs.shape
    t = MOE_TILE
    if final:
        assert n_lat % t == 0
        tiles_per_batch = n_lat // t
        n_tiles = bsz * tiles_per_batch
        where = lambda i: (i // tiles_per_batch, i % tiles_per_batch, 0)
        out_shape = (bsz, n_lat, d)
    else:
        assert (bsz * m) % t == 0
        tiles_per_batch = None
        n_tiles = bsz * m // t
        where = lambda i: (0, i, 0)
        out_shape = (1, bsz * m, d)
        h, gates, xs = (a.reshape(1, bsz * m, a.shape[-1]) for a in (h, gates, xs))
    once = pl.Buffered(1)

    tok = jnp.arange(t)
    tri = (tok[None, :] < tok[:, None]).astype(BF16)
    lane = jnp.arange(LANE)
    upper = (lane[:, None] < lane[None, :]).astype(F32)
    groups = -(-MOE_SLOTS // SEG_ALIGN // LANE) * LANE
    em = jax.ShapeDtypeStruct((n_tiles, LANE, t), F32)
    ints = lambda n: jax.ShapeDtypeStruct((n_tiles, SUBLANE, n), jnp.int32)
    per_tile = lambda a, b: pl.BlockSpec((1, a, b), lambda i: (i, 0, 0))
    pos, gem, cnt, off, own = pl.pallas_call(
        _plan_kernel,
        grid=(n_tiles,),
        in_specs=[pl.BlockSpec((1, t, LANE), lambda i: where(i)),
                  pl.BlockSpec((t, t), lambda i: (0, 0)), pl.BlockSpec((LANE, LANE), lambda i: (0, 0))],
        out_specs=[per_tile(LANE, t), per_tile(LANE, t), per_tile(SUBLANE, LANE), per_tile(SUBLANE, LANE),
                   per_tile(SUBLANE, groups)],
        out_shape=[em, em, ints(LANE), ints(LANE), ints(groups)],
        compiler_params=_params(("parallel",)),
        name="moe_plan",
    )(gates, tri, upper)

    tile = lambda n: pl.BlockSpec((1, t, n), lambda i, s, *_: where(i), pipeline_mode=once)
    planned = lambda a, b: pl.BlockSpec((1, a, b), lambda i, s, *_: (i, 0, 0), pipeline_mode=once)
    whole = lambda a: pl.BlockSpec(a.shape, lambda i, s, *_: (0,) * a.ndim, pipeline_mode=once)
    experts = lambda a: pl.BlockSpec((None, EXPERTS_PER_STEP) + a.shape[2:], lambda i, s, *_: (layer, s, 0, 0))
    out = pl.pallas_call(
        functools.partial(_moe_kernel, bsz=bsz, rows_per_batch=m, n_lat=n_lat,
                          tiles_per_batch=tiles_per_batch, final=final, cap=cap, stacked=stacked),
        grid_spec=pltpu.PrefetchScalarGridSpec(
            num_scalar_prefetch=3,
            grid=(n_tiles, N_EXPERTS // EXPERTS_PER_STEP),
            in_specs=[tile(d), planned(LANE, t), planned(LANE, t),
                      experts(wg), experts(wu), experts(wd), whole(sg), whole(su), whole(sd),
                      tile(d), whole(mod), whole(final_norm)],
            out_specs=pl.BlockSpec((1, t, d), lambda i, s, *_: where(i)),
            scratch_shapes=[pltpu.VMEM((MOE_SLOTS + cap, d), BF16), pltpu.VMEM((COMBINE_BLOCK, t), BF16)]),
        out_shape=jax.ShapeDtypeStruct(out_shape, F32),
        compiler_params=_params(("parallel", "arbitrary"), MOE_VMEM_LIMIT),
        name="moe_final" if final else "moe",
    )(cnt[:, 0], off[:, 0], own[:, 0], h, pos, gem, wg, wu, wd, sg, su, sd, xs, mod, final_norm)
    return out if final else out.reshape(bsz, m, d)


def _rope_tables(n_lat, m, head_dim, width):
    pos = jnp.arange(n_lat, dtype=jnp.int32)
    row = (pos // GRID_W).astype(F32)
    col = (pos % GRID_W).astype(F32)
    half = head_dim // 2
    quarter = half // 2
    lane = jnp.arange(head_dim)
    freq = (lane % quarter).astype(F32)
    inv = ROPE_THETA ** (-(2.0 * freq) / half)
    ang = jnp.where(lane[None, :] < half, row[:, None], col[:, None]) * inv[None, :]
    low = (lane % half) < quarter
    cos = jnp.cos(ang)
    sin = jnp.sin(ang)
    slo = jnp.where(low[None, :], -sin, 0.0)
    shi = jnp.where(low[None, :], 0.0, sin)
    reps = width // head_dim
    pad = lambda a, fill: jnp.concatenate(
        [jnp.tile(a, (1, reps)), jnp.full((m - n_lat, width), fill, F32)], axis=0)
    return pad(cos, 1.0), pad(slo, 0.0), pad(shi, 0.0)


def _mla_tables(tabs16):
    outs = []
    for a, fill in zip(tabs16, (1.0, 0.0, 0.0)):
        m = a.shape[0]
        slot = jnp.concatenate([jnp.full((m, MLA_NOPE), fill, F32), a,
                                jnp.full((m, QK_SLOT - MLA_QK), fill, F32)], axis=1)
        outs.append(jnp.tile(slot, (1, MLA_HEADS)))
    return tuple(outs)


def _slots(w, heads, width):
    w = w.reshape(w.shape[0], heads, width)
    pad = jnp.zeros((w.shape[0], heads, QK_SLOT - width), w.dtype)
    return jnp.concatenate([w, pad], axis=-1).reshape(w.shape[0], heads * QK_SLOT)


def _block_diag(w):
    dirs = w.shape[0]
    eye = jnp.eye(LRU_BLOCKS, dtype=w.dtype)
    full = jnp.einsum('dnij,nm->dnimj', w, eye).reshape(dirs, LRU_WIDTH, LRU_WIDTH)
    nl = LRU_WIDTH // LRU_LANES
    return jnp.stack([full[:, j * LRU_LANES:(j + 1) * LRU_LANES, j * LRU_LANES:(j + 1) * LRU_LANES]
                      for j in range(nl)], axis=1)


def kernel(x, c, ctx, c_ctx, mod_w, mod_b, ab_w_in, ab_w_out, diff_lambda_q, diff_lambda_k, diff_subln, lru_conv_w, lru_conv_b, lru_w_a, lru_b_a, lru_w_x, lru_b_x, lru_lambda, cd_w_in, cd_w_out, gqa_q_norm, gqa_k_norm, mla_q_norm, mla_kv_norm, mla_w_uq, mla_w_ukv, router_w, router_bias, exp_w_gate, exp_w_up, exp_w_down, sh_w_gate, sh_w_up, sh_w_down, final_norm):
    bsz, n_lat, d = x.shape
    n_ctx = ctx.shape[1]
    m = n_lat + n_ctx
    depth = mod_w.shape[0]
    assert depth == 2 and bsz < MOD_ROWS
    assert n_lat % ROW_TILE == 0 and n_ctx % ROW_TILE == 0 and n_lat % GRID_W == 0
    ctx_row = bsz

    xs = jnp.concatenate([x, ctx], axis=1)
    c_all = jnp.concatenate([c, c_ctx[None, :], jnp.zeros((MOD_ROWS - bsz - 1, d), F32)], axis=0)
    mods = _modulation(c_all, mod_w, mod_b)

    tabs64 = _rope_tables(n_lat, m, DIFF_HEAD_DIM, DIFF_HEADS * 2 * DIFF_HEAD_DIM)
    tabs_m = _mla_tables(_rope_tables(n_lat, m, MLA_ROPE, MLA_ROPE))

    exp_w = tuple(w.astype(BF16) for w in (exp_w_gate, exp_w_up, exp_w_down))

    def moe_weights(i):
        return exp_w + (sh_w_gate[i], sh_w_up[i], sh_w_down[i])

    fn = final_norm.reshape(1, d)

    mod = mods[0]
    qt, k, vt, u, g = _proj_ab(xs, mod, ab_w_in[0].astype(BF16), tabs64, n_lat, ctx_row)
    lambda_init = 0.8 - 0.6 * math.exp(-0.3 * 0)
    dv = 2 * DIFF_HEAD_DIM
    jobs = tuple(_Job((2 * h + j) * QK_SLOT, h, j, h * dv, dv) for h in range(DIFF_HEADS) for j in range(2))
    o_att = _attention(qt, k, vt, (diff_lambda_q[0], diff_lambda_k[0], diff_subln[0].reshape(-1, 1)),
                       jobs, n_lat, lambda_init, "diff_attention")
    o_rec = _lru(u, g, lru_conv_w[0], lru_conv_b[0].reshape(1, -1),
                 _block_diag(lru_w_a[0]).astype(BF16), lru_b_a[0].reshape(2, 1, LRU_WIDTH),
                 _block_diag(lru_w_x[0]).astype(BF16), lru_b_x[0].reshape(2, 1, LRU_WIDTH),
                 lru_lambda[0].reshape(2, 1, LRU_WIDTH), n_lat)
    xs, h2, gates = _out_proj(o_att, o_rec, xs, mod, ab_w_out[0].astype(BF16),
                              router_w[0].T, router_bias[0].reshape(-1, 1), n_lat, ctx_row)
    xs = _moe(h2, gates, xs, mod, *moe_weights(0), fn, 0, n_lat, False, cap=256, stacked=False)

    mod = mods[1]
    w_in = cd_w_in[0]
    wq, wk = GQA_HEADS * GQA_HEAD_DIM, GQA_KV_HEADS * GQA_HEAD_DIM
    o_kr = wq + 2 * wk + MLA_Q_RANK + MLA_KV_RANK
    kr_slots = jnp.concatenate([jnp.zeros((d, MLA_NOPE), F32), w_in[:, o_kr:o_kr + MLA_ROPE],
                                jnp.zeros((d, QK_SLOT - MLA_QK), F32)], axis=1)
    w_in = jnp.concatenate([w_in[:, :o_kr], jnp.tile(kr_slots, (1, MLA_HEADS))], axis=1).astype(BF16)
    ukv = mla_w_ukv[0].reshape(MLA_KV_RANK, MLA_HEADS, MLA_NOPE + MLA_V)
    w_kn = _slots(ukv[:, :, :MLA_NOPE].reshape(MLA_KV_RANK, -1), MLA_HEADS, MLA_NOPE).astype(BF16)
    w_v = ukv[:, :, MLA_NOPE:].reshape(MLA_KV_RANK, MLA_HEADS * MLA_V).astype(BF16)
    w_uq = _slots(mla_w_uq[0], MLA_HEADS, MLA_QK).astype(BF16)
    lane = jnp.arange(wq)
    ones_bd = (lane[:, None] // GQA_HEAD_DIM == lane[None, :] // GQA_HEAD_DIM).astype(BF16)
    qgt, kg, vgt, qmt, km, vmt = _proj_cd(
        xs, mod, w_in, ones_bd, jnp.tile(gqa_q_norm[0], GQA_HEADS).reshape(1, -1),
        jnp.tile(gqa_k_norm[0], GQA_KV_HEADS).reshape(1, -1), mla_q_norm[0].reshape(1, -1),
        mla_kv_norm[0].reshape(1, -1), w_uq, w_kn, w_v, tabs64, tabs_m, n_lat, ctx_row)
    groups = GQA_HEADS // GQA_KV_HEADS
    jobs = tuple(_Job(h * QK_SLOT, 0, h // groups, (h // groups) * GQA_HEAD_DIM, GQA_HEAD_DIM)
                 for h in range(GQA_HEADS))
    o_gqa = _attention(qgt, kg, vgt, (), jobs, n_lat, None, "gqa_attention", stack=True)
    jobs = tuple(_Job(h * QK_SLOT, h // 2, h % 2, h * MLA_V, MLA_V) for h in range(MLA_HEADS))
    o_mla = _attention(qmt, km, vmt, (), jobs, n_lat, None, "mla_attention")
    xs, h2, gates = _out_proj(o_gqa, o_mla, xs, mod, cd_w_out[0].astype(BF16),
                              router_w[1].T, router_bias[1].reshape(-1, 1), n_lat, ctx_row)
    return _moe(h2, gates, xs, mod, *moe_weights(1), fn, 1, n_lat, True, cap=160, stacked=True)
```

```python
import functools
import math
from typing import NamedTuple

import jax
import jax.numpy as jnp
from jax import lax
from jax.experimental import pallas as pl
from jax.experimental.pallas import tpu as pltpu

F32 = jnp.float32
BF16 = jnp.bfloat16

GRID_W = 64
ROPE_THETA = 10000.0
EPS = 1e-6
DIFF_HEADS = 4
DIFF_HEAD_DIM = 64
LRU_WIDTH = 512
LRU_BLOCKS = 8
LRU_BW = LRU_WIDTH // LRU_BLOCKS
CONV_W = 4
LRU_C = 8.0
GQA_HEADS = 8
GQA_KV_HEADS = 2
GQA_HEAD_DIM = 64
MLA_HEADS = 8
MLA_Q_RANK = 256
MLA_KV_RANK = 128
MLA_NOPE = 32
MLA_ROPE = 16
MLA_V = 64
MLA_QK = MLA_NOPE + MLA_ROPE
N_EXPERTS = 64
N_GROUPS = 8
PER_GROUP = N_EXPERTS // N_GROUPS
TOPK_GROUPS = 4
TOP_K = 8
EXPERT_FF = 256
ROUTED_SCALE = 2.5

QK_SLOT = 64
LOG2E = math.log2(math.e)

ROW_TILE = 256
LRU_CHUNK = 128
LRU_LANES = 256
EXPERTS_PER_STEP = 4
LANE = 128
SUBLANE = 8
MOD_ROWS = 8
VMEM_LIMIT = 56 * 1024 * 1024
MOE_VMEM_LIMIT = 62 * 1024 * 1024
MOE_TILE = 1024
MOE_CAP = 160
SEG_ALIGN = 16
MOE_SLOTS = MOE_TILE * TOP_K + N_EXPERTS * SEG_ALIGN
COMBINE_BLOCK = 512
GROUP_UNROLL = 4


def _params(sem, vmem=VMEM_LIMIT):
    return pltpu.CompilerParams(dimension_semantics=sem, vmem_limit_bytes=vmem)


def _silu(x):
    return x * jax.nn.sigmoid(x)


def _rms(x):
    return x * lax.rsqrt(jnp.mean(x * x, axis=-1, keepdims=True) + EPS)


def _mod_rows(mod_ref, ctx_row, b, k, d, row0, rows, n_lat):
    lat = mod_ref[pl.ds(b, 1), k * d:(k + 1) * d]
    ctx = mod_ref[ctx_row:ctx_row + 1, k * d:(k + 1) * d]
    if n_lat % rows == 0:
        return jnp.where(row0 >= n_lat, ctx, lat)
    r = row0 + lax.broadcasted_iota(jnp.int32, (rows, 1), 0)
    return jnp.where(r >= n_lat, ctx, lat)


def _rope(x, cos, sin_lo, sin_hi, half):
    n = x.shape[-1]
    return x * cos + pltpu.roll(x, n - half, 1) * sin_lo + pltpu.roll(x, half, 1) * sin_hi


def _mod_kernel(c_ref, w_ref, b_ref, o_ref):
    s = _silu(c_ref[...])
    o_ref[0] = jnp.dot(s, w_ref[0], preferred_element_type=F32, precision=lax.Precision.HIGHEST) + b_ref[0]


def _modulation(c_all, mod_w, mod_b):
    depth, d, n = mod_w.shape
    tn = n // 4
    return pl.pallas_call(
        _mod_kernel,
        grid=(depth, n // tn),
        in_specs=[pl.BlockSpec((MOD_ROWS, d), lambda i, j: (0, 0)),
                  pl.BlockSpec((1, d, tn), lambda i, j: (i, 0, j)),
                  pl.BlockSpec((1, 1, tn), lambda i, j: (i, 0, j))],
        out_specs=pl.BlockSpec((1, MOD_ROWS, tn), lambda i, j: (i, 0, j)),
        out_shape=jax.ShapeDtypeStruct((depth, MOD_ROWS, n), F32),
        compiler_params=_params(("arbitrary", "arbitrary")),
        name="modulation",
    )(c_all, mod_w, mod_b.reshape(depth, 1, n))


def _proj_ab_kernel(x_ref, mod_ref, w_ref, cos_ref, slo_ref, shi_ref,
                    qt_ref, k_ref, vt_ref, u_ref, g_ref, *, n_lat, ctx_row):
    b, t = pl.program_id(0), pl.program_id(1)
    d = x_ref.shape[-1]
    row0 = t * ROW_TILE
    shift = _mod_rows(mod_ref, ctx_row, b, 0, d, row0, ROW_TILE, n_lat)
    scale = _mod_rows(mod_ref, ctx_row, b, 1, d, row0, ROW_TILE, n_lat)
    h = (_rms(x_ref[0]) * (1.0 + scale) + shift).astype(BF16)
    acc = jnp.dot(h, w_ref[...], preferred_element_type=F32)
    w = DIFF_HEADS * 2 * DIFF_HEAD_DIM
    cos, slo, shi = cos_ref[...], slo_ref[...], shi_ref[...]
    q = _rope(acc[:, 0:w], cos, slo, shi, DIFF_HEAD_DIM // 4) * (DIFF_HEAD_DIM ** -0.5 * LOG2E)
    k = _rope(acc[:, w:2 * w], cos, slo, shi, DIFF_HEAD_DIM // 4)
    qt_ref[0, 0] = q.T.astype(BF16)
    k_ref[0] = k.astype(BF16)
    vt_ref[0, 0] = acc[:, 2 * w:3 * w].T.astype(BF16)
    u_ref[0] = acc[:, 3 * w:3 * w + LRU_WIDTH]
    g_ref[0] = jax.nn.gelu(acc[:, 3 * w + LRU_WIDTH:3 * w + 2 * LRU_WIDTH]).astype(BF16)


def _proj_ab(xs, mod, w_in, tabs, n_lat, ctx_row):
    bsz, m, d = xs.shape
    nt = m // ROW_TILE
    w = DIFF_HEADS * 2 * DIFF_HEAD_DIM
    n_in = w_in.shape[1]
    tile = lambda n: pl.BlockSpec((1, ROW_TILE, n), lambda b, t: (b, t, 0))
    ttile = pl.BlockSpec((1, 1, w, ROW_TILE), lambda b, t: (b, t, 0, 0))
    tab = pl.BlockSpec((ROW_TILE, w), lambda b, t: (t, 0))
    return pl.pallas_call(
        functools.partial(_proj_ab_kernel, n_lat=n_lat, ctx_row=ctx_row),
        grid=(bsz, nt),
        in_specs=[tile(d),
                  pl.BlockSpec(mod.shape, lambda b, t: (0, 0)),
                  pl.BlockSpec((d, n_in), lambda b, t: (0, 0)),
                  tab, tab, tab],
        out_specs=[ttile, tile(w), ttile, tile(LRU_WIDTH), tile(LRU_WIDTH)],
        out_shape=[jax.ShapeDtypeStruct((bsz, nt, w, ROW_TILE), BF16),
                   jax.ShapeDtypeStruct((bsz, m, w), BF16),
                   jax.ShapeDtypeStruct((bsz, nt, w, ROW_TILE), BF16),
                   jax.ShapeDtypeStruct((bsz, m, LRU_WIDTH), F32),
                   jax.ShapeDtypeStruct((bsz, m, LRU_WIDTH), BF16)],
        compiler_params=_params(("parallel", "parallel")),
        name="proj_ab",
    )(xs, mod, w_in, *tabs)


def _group_rms(x, ones_bd):
    x2 = x * x
    hi = x2.astype(BF16)
    lo = (x2 - hi.astype(F32)).astype(BF16)
    ss = jnp.dot(hi, ones_bd, preferred_element_type=F32) + jnp.dot(lo, ones_bd, preferred_element_type=F32)
    return x * lax.rsqrt(ss * (1.0 / GQA_HEAD_DIM) + EPS)


def _proj_cd_kernel(x_ref, mod_ref, w_ref, bd_ref, qn_ref, kn_ref, cqn_ref, ckvn_ref, wuq_ref, wkn_ref, wv_ref,
                    cos_ref, slo_ref, shi_ref, cosm_ref, slom_ref, shim_ref,
                    qgt_ref, kg_ref, vgt_ref, qmt_ref, km_ref, vmt_ref, *, n_lat, ctx_row):
    b, t = pl.program_id(0), pl.program_id(1)
    d = x_ref.shape[-1]
    row0 = t * ROW_TILE
    shift = _mod_rows(mod_ref, ctx_row, b, 0, d, row0, ROW_TILE, n_lat)
    scale = _mod_rows(mod_ref, ctx_row, b, 1, d, row0, ROW_TILE, n_lat)
    h = (_rms(x_ref[0]) * (1.0 + scale) + shift).astype(BF16)
    acc = jnp.dot(h, w_ref[...], preferred_element_type=F32)
    wq = GQA_HEADS * GQA_HEAD_DIM
    wk = GQA_KV_HEADS * GQA_HEAD_DIM
    o = 0
    q = acc[:, o:o + wq]; o += wq
    k = acc[:, o:o + wk]; o += wk
    v = acc[:, o:o + wk]; o += wk
    cq = acc[:, o:o + MLA_Q_RANK]; o += MLA_Q_RANK
    ckv = acc[:, o:o + MLA_KV_RANK]; o += MLA_KV_RANK
    kr = acc[:, o:o + MLA_HEADS * QK_SLOT]
    cos, slo, shi = cos_ref[...], slo_ref[...], shi_ref[...]
    bd = bd_ref[...]
    q = (_rope(_group_rms(q, bd) * qn_ref[...], cos, slo, shi, GQA_HEAD_DIM // 4)
         * (GQA_HEAD_DIM ** -0.5 * LOG2E))
    k = _rope(_group_rms(k, bd[:wk, :wk]) * kn_ref[...], cos[:, :wk], slo[:, :wk], shi[:, :wk], GQA_HEAD_DIM // 4)
    qgt_ref[0, 0] = q.T.astype(BF16)
    kg_ref[0] = k.astype(BF16)
    vgt_ref[0, 0] = v.T.astype(BF16)
    cosm, slom, shim = cosm_ref[...], slom_ref[...], shim_ref[...]
    cqn = (_rms(cq) * cqn_ref[...]).astype(BF16)
    mq = jnp.dot(cqn, wuq_ref[...], preferred_element_type=F32)
    qmt_ref[0, 0] = (_rope(mq, cosm, slom, shim, MLA_ROPE // 4) * (MLA_QK ** -0.5 * LOG2E)).T.astype(BF16)
    ckvn = (_rms(ckv) * ckvn_ref[...]).astype(BF16)
    km = jnp.dot(ckvn, wkn_ref[...], preferred_element_type=F32) + _rope(kr, cosm, slom, shim, MLA_ROPE // 4)
    km_ref[0] = km.astype(BF16)
    vmt_ref[0, 0] = jnp.dot(ckvn, wv_ref[...], preferred_element_type=F32).T.astype(BF16)


def _proj_cd(xs, mod, w_in, bd, qn, kn, cqn, ckvn, wuq, wkn, wv, tabs, tabs_m, n_lat, ctx_row):
    bsz, m, d = xs.shape
    nt = m // ROW_TILE
    wq = GQA_HEADS * GQA_HEAD_DIM
    wk = GQA_KV_HEADS * GQA_HEAD_DIM
    wm = MLA_HEADS * QK_SLOT
    wmv = MLA_HEADS * MLA_V
    tile = lambda n: pl.BlockSpec((1, ROW_TILE, n), lambda b, t: (b, t, 0))
    ttile = lambda n: pl.BlockSpec((1, 1, n, ROW_TILE), lambda b, t: (b, t, 0, 0))
    whole = lambda a: pl.BlockSpec(a.shape, lambda b, t: (0,) * a.ndim)
    tab = pl.BlockSpec((ROW_TILE, wq), lambda b, t: (t, 0))
    tabm = pl.BlockSpec((ROW_TILE, wm), lambda b, t: (t, 0))
    return pl.pallas_call(
        functools.partial(_proj_cd_kernel, n_lat=n_lat, ctx_row=ctx_row),
        grid=(bsz, nt),
        in_specs=[tile(d), whole(mod), whole(w_in), whole(bd), whole(qn), whole(kn), whole(cqn), whole(ckvn),
                  whole(wuq), whole(wkn), whole(wv), tab, tab, tab, tabm, tabm, tabm],
        out_specs=[ttile(wq), tile(wk), ttile(wk), ttile(wm), tile(wm), ttile(wmv)],
        out_shape=[jax.ShapeDtypeStruct((bsz, nt, wq, ROW_TILE), BF16),
                   jax.ShapeDtypeStruct((bsz, m, wk), BF16),
                   jax.ShapeDtypeStruct((bsz, nt, wk, ROW_TILE), BF16),
                   jax.ShapeDtypeStruct((bsz, nt, wm, ROW_TILE), BF16),
                   jax.ShapeDtypeStruct((bsz, m, wm), BF16),
                   jax.ShapeDtypeStruct((bsz, nt, wmv, ROW_TILE), BF16)],
        compiler_params=_params(("parallel", "parallel")),
        name="proj_cd",
    )(xs, mod, w_in, bd, qn, kn, cqn, ckvn, wuq, wkn, wv, *tabs, *tabs_m)


class _Job(NamedTuple):
    q_row: int
    k_group: int
    k_half: int
    v_row: int
    dv: int


def _attend(qt_ref, qnext_ref, k_ref, vt_ref, s_ref, mp_ref, jobs, lo, hi, chained):
    assert len(jobs) % 2 == 0
    tq = qt_ref.shape[-1]
    n = hi - lo
    group = next(g for g in (11, 3, 2, 1) if n % g == 0)
    steps = n // group
    row_groups = ROW_TILE // SUBLANE

    def weights(job, ref=qt_ref):
        qh = ref[0, 0, job.q_row:job.q_row + QK_SLOT, :]
        z = jnp.zeros_like(qh)
        return jnp.concatenate([z, qh] if job.k_half else [qh, z], axis=0)

    def score(c0, buf, qw, job, mp):
        r0 = pl.multiple_of(c0 * ROW_TILE, ROW_TILE)
        kc = k_ref[0, pl.ds(r0, group * ROW_TILE), job.k_group * LANE:(job.k_group + 1) * LANE]
        s_all = jnp.dot(kc, qw, preferred_element_type=F32)
        for g in range(group):
            s = s_all[g * ROW_TILE:(g + 1) * ROW_TILE]
            s_ref[buf, c0 + g] = s
            mp = jnp.maximum(mp, jnp.max(s.reshape(row_groups, SUBLANE, tq), axis=0))
        return mp

    def value(c, buf, m, job, lp, acc):
        p = jnp.exp2(s_ref[buf, c] - m)
        lp = lp + jnp.sum(p.reshape(row_groups, SUBLANE, tq), axis=0)
        acc = acc + jnp.dot(vt_ref[0, c, job.v_row:job.v_row + job.dv, :], p.astype(BF16),
                            preferred_element_type=F32)
        return lp, acc

    def loop(body, init):
        return body(0, init) if steps == 1 else lax.fori_loop(0, steps, body, init)

    neg = jnp.full((SUBLANE, tq), -jnp.inf, F32)

    def prologue():
        qw = weights(jobs[0])

        def first(i, mp):
            return score(lo + i * group, 0, qw, jobs[0], mp)

        mp_ref[...] = loop(first, neg)

    if chained:
        pl.when(pl.program_id(1) == 0)(prologue)
    else:
        prologue()
    mp = mp_ref[...]
    results = []
    for j, job in enumerate(jobs):
        buf = j % 2
        m = jnp.max(mp, axis=0, keepdims=True)
        if j + 1 < len(jobs):
            nxt, qw = jobs[j + 1], weights(jobs[j + 1])
        elif chained:
            nxt, qw = jobs[0], weights(jobs[0], qnext_ref)
        else:
            nxt = None

        def body(i, carry):
            mp, lp, acc = carry
            if nxt is not None:
                mp = score(lo + i * group, 1 - buf, qw, nxt, mp)
            for g in range(group):
                lp, acc = value(lo + i * group + g, buf, m, job, lp, acc)
            return mp, lp, acc

        mp, lp, acc = loop(body, (neg, jnp.zeros((SUBLANE, tq), F32), jnp.zeros((job.dv, tq), F32)))
        results.append((acc, jnp.sum(lp, axis=0, keepdims=True)))
    if chained:
        mp_ref[...] = mp
    return results


def _attn_kernel(*refs, jobs, n_lat, lambda_init):
    if lambda_init is None:
        qt_ref, qnext_ref, k_ref, vt_ref, o_ref, s_ref, mp_ref = refs
    else:
        qt_ref, qnext_ref, k_ref, vt_ref, lq_ref, lk_ref, sub_ref, o_ref, s_ref, mp_ref = refs
    n_chunks = vt_ref.shape[1]
    first_ctx = n_lat // ROW_TILE
    t = pl.program_id(1)

    def run(lo, hi, chained):
        res = _attend(qt_ref, qnext_ref, k_ref, vt_ref, s_ref, mp_ref, jobs, lo, hi, chained)
        if lambda_init is None:
            outs = [acc / l for acc, l in res]
        else:
            lq, lk = lq_ref[...], lk_ref[...]
            lam = (jnp.exp(jnp.sum(lq[0:1] * lk[0:1], axis=-1, keepdims=True))
                   - jnp.exp(jnp.sum(lq[1:2] * lk[1:2], axis=-1, keepdims=True)) + lambda_init)
            outs = []
            for h in range(len(res) // 2):
                (a0, l0), (a1, l1) = res[2 * h], res[2 * h + 1]
                o = a0 / l0 - lam * (a1 / l1)
                o = o * lax.rsqrt(jnp.mean(o * o, axis=0, keepdims=True) + EPS)
                outs.append(o * sub_ref[...] * (1.0 - lambda_init))
        o_ref[0] = jnp.concatenate(outs, axis=0).T.astype(BF16)

    @pl.when(t < first_ctx)
    def _():
        run(0, n_chunks, True)

    @pl.when(t >= first_ctx)
    def _():
        run(first_ctx, n_chunks, False)


def _attention(qt, k, vt, extra, jobs, n_lat, lambda_init, name):
    bsz, nt, wq, tq = qt.shape
    m = k.shape[1]
    wo = sum(j.dv for j in jobs) if lambda_init is None else sum(j.dv for j in jobs) // 2
    once = pl.Buffered(1)
    whole = lambda a: pl.BlockSpec(a.shape, lambda b, t: (0,) * a.ndim)
    return pl.pallas_call(
        functools.partial(_attn_kernel, jobs=jobs, n_lat=n_lat, lambda_init=lambda_init),
        grid=(bsz, nt),
        in_specs=[pl.BlockSpec((1, 1, wq, tq), lambda b, t: (b, t, 0, 0)),
                  pl.BlockSpec((1, 1, wq, tq), lambda b, t: (b, jnp.minimum(t + 1, nt - 1), 0, 0)),
                  pl.BlockSpec((1,) + k.shape[1:], lambda b, t: (b, 0, 0), pipeline_mode=once),
                  pl.BlockSpec((1,) + vt.shape[1:], lambda b, t: (b, 0, 0, 0), pipeline_mode=once)]
                 + [whole(a) for a in extra],
        out_specs=pl.BlockSpec((1, tq, wo), lambda b, t: (b, t, 0)),
        out_shape=jax.ShapeDtypeStruct((bsz, m, wo), BF16),
        scratch_shapes=[pltpu.VMEM((2, nt, ROW_TILE, tq), F32), pltpu.VMEM((SUBLANE, tq), F32)],
        compiler_params=_params(("arbitrary", "arbitrary")),
        name=name,
    )(qt, qt, k, vt, *extra)


def _scan_chunk(a, b, carry, reverse):
    n = a.shape[0]
    rows = lax.broadcasted_iota(jnp.int32, (n, 1), 0)
    d = 1
    while d < n:
        sh = n - d if reverse else d
        valid = rows < n - d if reverse else rows >= d
        b = jnp.where(valid, a * pltpu.roll(b, sh, 0) + b, b)
        a = jnp.where(valid, a * pltpu.roll(a, sh, 0), a)
        d *= 2
    h = a * carry + b
    return h, (h[0:1] if reverse else h[n - 1:n])


def _lru_kernel(u_ref, g_ref, cw_ref, cb_ref, wa_ref, ba_ref, wx_ref, bx_ref, lam_ref, o_ref, up_ref, hf_ref, *, n_lat):
    m = u_ref.shape[1]
    lanes = u_ref.shape[2]
    n_chunks = m // LRU_CHUNK
    n_ctx_chunks = (m - n_lat) // LRU_CHUNK
    pad = SUBLANE

    up_ref[0:pad, :] = jnp.zeros((pad, lanes), F32)
    up_ref[pad + m:pad + m + pad, :] = jnp.zeros((pad, lanes), F32)

    def copy(c, _):
        r0 = pl.multiple_of(c * LRU_CHUNK, LRU_CHUNK)
        up_ref[pl.ds(pad + r0, LRU_CHUNK), :] = u_ref[0, pl.ds(r0, LRU_CHUNK), :]
        return 0

    lax.fori_loop(0, n_chunks, copy, 0)

    cw = cw_ref[...]
    cb = cb_ref[...]
    win_rows = LRU_CHUNK + 2 * pad

    def coeffs(c, d):
        r0 = pl.multiple_of(c * LRU_CHUNK, LRU_CHUNK)
        win = up_ref[pl.ds(r0, win_rows), :]
        r = r0 + lax.broadcasted_iota(jnp.int32, (LRU_CHUNK, 1), 0)
        at = lambda k: pltpu.roll(win, (win_rows - k) % win_rows, 0)[pad:pad + LRU_CHUNK]
        y = (cw[0:1] * jnp.where((r == n_lat) | (r == n_lat + 1), 0.0, at(-2))
             + cw[1:2] * jnp.where(r == n_lat, 0.0, at(-1))
             + cw[2:3] * win[pad:pad + LRU_CHUNK]
             + cw[3:4] * jnp.where(r == n_lat - 1, 0.0, at(1))
             + cb)
        yb = y.astype(BF16)
        ra = jax.nn.sigmoid(jnp.dot(yb, wa_ref[d, 0], preferred_element_type=F32) + ba_ref[d])
        ix = jax.nn.sigmoid(jnp.dot(yb, wx_ref[d, 0], preferred_element_type=F32) + bx_ref[d])
        z = -lam_ref[d]
        softplus = jnp.maximum(z, 0.0) + jnp.log(1.0 + jnp.exp(-jnp.abs(z)))
        a = jnp.exp(-LRU_C * ra * softplus)
        return a, jnp.sqrt(1.0 - a * a) * ix * y

    def fwd(s, carry):
        c = lax.rem(s + (n_chunks - n_ctx_chunks), n_chunks)
        a, b = coeffs(c, 0)
        h, carry = _scan_chunk(a, b, carry, False)
        r0 = pl.multiple_of(c * LRU_CHUNK, LRU_CHUNK)
        hf_ref[pl.ds(r0, LRU_CHUNK), :] = h
        return carry

    lax.fori_loop(0, n_chunks, fwd, jnp.zeros((1, lanes), F32))

    def bwd(s, carry):
        c = n_chunks - 1 - s
        a, b = coeffs(c, 1)
        h, carry = _scan_chunk(a, b, carry, True)
        r0 = pl.multiple_of(c * LRU_CHUNK, LRU_CHUNK)
        tot = hf_ref[pl.ds(r0, LRU_CHUNK), :] + h
        o_ref[0, pl.ds(r0, LRU_CHUNK), :] = (tot * g_ref[0, pl.ds(r0, LRU_CHUNK), :].astype(F32)).astype(BF16)
        return carry

    lax.fori_loop(0, n_chunks, bwd, jnp.zeros((1, lanes), F32))


def _lru(u, g, conv_w, conv_b, wa_bd, ba, wx_bd, bx, lam, n_lat):
    bsz, m, c = u.shape
    nl = c // LRU_LANES
    seq = lambda: pl.BlockSpec((1, m, LRU_LANES), lambda b, j: (b, 0, j))
    vec = lambda a: pl.BlockSpec(a.shape[:-1] + (LRU_LANES,), lambda b, j: (0,) * (a.ndim - 1) + (j,))
    mat = pl.BlockSpec((2, 1, LRU_LANES, LRU_LANES), lambda b, j: (0, j, 0, 0))
    return pl.pallas_call(
        functools.partial(_lru_kernel, n_lat=n_lat),
        grid=(bsz, nl),
        in_specs=[seq(), seq(), vec(conv_w), vec(conv_b), mat, vec(ba), mat, vec(bx), vec(lam)],
        out_specs=seq(),
        out_shape=jax.ShapeDtypeStruct((bsz, m, c), BF16),
        scratch_shapes=[pltpu.VMEM((m + 2 * SUBLANE, LRU_LANES), F32), pltpu.VMEM((m, LRU_LANES), F32)],
        compiler_params=_params(("parallel", "parallel")),
        name="rglru",
    )(u, g, conv_w, conv_b, wa_bd, ba, wx_bd, bx, lam)


def _route(logits_t, bias):
    n = logits_t.shape[-1]
    scores = jax.nn.sigmoid(logits_t)
    choice = scores + bias
    sub = lax.broadcasted_iota(jnp.int32, (PER_GROUP, n), 0)
    neg = -jnp.inf
    groups, gs = [], []
    for g in range(N_GROUPS):
        cg = choice[g * PER_GROUP:(g + 1) * PER_GROUP]
        m1 = jnp.max(cg, axis=0, keepdims=True)
        i1 = jnp.min(jnp.where(cg == m1, sub, PER_GROUP), axis=0, keepdims=True)
        m2 = jnp.max(jnp.where(sub == i1, neg, cg), axis=0, keepdims=True)
        groups.append(cg)
        gs.append(m1 + m2)
    masked = []
    for g in range(N_GROUPS):
        rank = jnp.zeros((1, n), jnp.int32)
        for o in range(N_GROUPS):
            if o != g:
                ahead = (gs[o] >= gs[g]) if o < g else (gs[o] > gs[g])
                rank = rank + jnp.where(ahead, 1, 0)
        masked.append(jnp.where(rank < TOPK_GROUPS, groups[g], neg))
    masked = jnp.concatenate(masked, axis=0)
    eidx = lax.broadcasted_iota(jnp.int32, (N_EXPERTS, n), 0)
    picked = jnp.zeros((N_EXPERTS, n), F32)
    for _ in range(TOP_K):
        mx = jnp.max(masked, axis=0, keepdims=True)
        first = jnp.min(jnp.where(masked == mx, eidx, N_EXPERTS), axis=0, keepdims=True)
        hit = eidx == first
        picked = jnp.where(hit, 1.0, picked)
        masked = jnp.where(hit, neg, masked)
    w = picked * scores
    return w / jnp.sum(w, axis=0, keepdims=True) * ROUTED_SCALE


def _out_proj_kernel(oa_ref, ob_ref, x_ref, mod_ref, wa_ref, wb_ref, rw_ref, rb_ref,
                     xo_ref, h_ref, gates_ref, *, n_lat, ctx_row):
    b, t = pl.program_id(0), pl.program_id(1)
    d = x_ref.shape[-1]
    row0 = t * ROW_TILE
    mv = lambda k: _mod_rows(mod_ref, ctx_row, b, k, d, row0, ROW_TILE, n_lat)
    y = (jnp.dot(oa_ref[0], wa_ref[...], preferred_element_type=F32)
         + jnp.dot(ob_ref[0], wb_ref[...], preferred_element_type=F32))
    x = x_ref[0] + mv(2) * y
    xo_ref[0] = x
    h = _rms(x) * (1.0 + mv(4)) + mv(3)
    h_ref[0] = h.astype(BF16)
    logits_t = lax.dot_general(rw_ref[...], h, (((1,), (1,)), ((), ())),
                               preferred_element_type=F32, precision=lax.Precision.HIGHEST)
    gates_t = _route(logits_t, rb_ref[...])
    gates_t = jnp.concatenate([gates_t, jnp.zeros((LANE - N_EXPERTS, ROW_TILE), F32)], axis=0)
    gates_ref[0] = gates_t.T


def _out_proj(oa, ob, xs, mod, w_out, router_wt, router_b, n_lat, ctx_row):
    bsz, m, d = xs.shape
    na, nb = oa.shape[-1], ob.shape[-1]
    wa, wb = w_out[:na], w_out[na:]
    tile = lambda n: pl.BlockSpec((1, ROW_TILE, n), lambda b, t: (b, t, 0))
    whole = lambda a: pl.BlockSpec(a.shape, lambda b, t: (0,) * a.ndim)
    return pl.pallas_call(
        functools.partial(_out_proj_kernel, n_lat=n_lat, ctx_row=ctx_row),
        grid=(bsz, m // ROW_TILE),
        in_specs=[tile(na), tile(nb), tile(d), whole(mod), whole(wa), whole(wb), whole(router_wt), whole(router_b)],
        out_specs=[tile(d), tile(d), tile(LANE)],
        out_shape=[jax.ShapeDtypeStruct((bsz, m, d), F32),
                   jax.ShapeDtypeStruct((bsz, m, d), BF16),
                   jax.ShapeDtypeStruct((bsz, m, LANE), F32)],
        compiler_params=_params(("parallel", "parallel")),
        name="out_proj_router",
    )(oa, ob, xs, mod, wa, wb, router_wt, router_b)


def _plan_kernel(gates_ref, tri_ref, upper_ref, pos_ref, gem_ref, cnt_ref, off_ref, own_ref):
    g = gates_ref[0]
    sel = g != 0.0
    one = jnp.where(sel, 1.0, 0.0)
    pos = jnp.dot(tri_ref[...], one.astype(BF16), preferred_element_type=F32)
    cnt = jnp.sum(one, axis=0, keepdims=True)
    cpad = jnp.floor((cnt + (SEG_ALIGN - 1)) * (1.0 / SEG_ALIGN)) * SEG_ALIGN
    off = jnp.dot(jnp.broadcast_to(cpad, (SUBLANE, LANE)), upper_ref[...], preferred_element_type=F32,
                  precision=lax.Precision.HIGHEST)[0:1]
    pos_ref[0] = jnp.where(sel, pos, -1.0).T
    gem_ref[0] = g.T
    cnt_ref[0] = jnp.broadcast_to(cnt, (SUBLANE, LANE)).astype(jnp.int32)
    off_ref[0] = jnp.broadcast_to(off, (SUBLANE, LANE)).astype(jnp.int32)
    groups = own_ref.shape[-1]
    ends = jnp.broadcast_to(off + cpad, (LANE, LANE)).T
    ends = jnp.concatenate([ends] * (groups // LANE), axis=1)
    start = lax.broadcasted_iota(jnp.int32, (LANE, groups), 1).astype(F32) * SEG_ALIGN
    real = lax.broadcasted_iota(jnp.int32, (LANE, groups), 0) < N_EXPERTS
    before = jnp.where(real, jnp.where(ends <= start, 1.0, 0.0), 0.0)
    owner = jnp.minimum(jnp.sum(before, axis=0, keepdims=True), N_EXPERTS - 1.0)
    own_ref[0] = jnp.broadcast_to(owner, (SUBLANE, groups)).astype(jnp.int32)


def _moe_kernel(cnt_s, off_s, own_s, h_ref, pos_ref, gem_ref, wg_ref, wu_ref, wd_ref,
                sg_ref, su_ref, sd_ref, x_ref, mod_ref, fn_ref, o_ref, y_ref, p_ref,
                *, bsz, rows_per_batch, n_lat, tiles_per_batch, final):
    i, s = pl.program_id(0), pl.program_id(1)
    rows, d = h_ref.shape[1], h_ref.shape[2]
    h = h_ref[0]

    def ffn(x, wg, wu, wd, gate):
        a = _silu(jnp.dot(x, wg, preferred_element_type=F32)) * jnp.dot(x, wu, preferred_element_type=F32)
        if gate is not None:
            a = a * gate
        return jnp.dot(a.astype(BF16), wd, preferred_element_type=F32)

    @pl.when(s == 0)
    def _():
        y_ref[...] = jnp.zeros(y_ref.shape, BF16)
        o_ref[0] = ffn(h, sg_ref[...].astype(BF16), su_ref[...].astype(BF16), sd_ref[...].astype(BF16), None)

    cap = MOE_CAP
    cap_rows = lax.broadcasted_iota(jnp.int32, (cap, 1), 0)

    def onehot(pos_row, q):
        ranks = cap_rows + q * cap
        hit = pos_row == ranks.astype(F32)
        return ranks, hit, jnp.where(hit, 1.0, 0.0).astype(BF16)

    def expert_rows(x, ranks, hit, q, gate_row, w, cnt, off):
        gate = jnp.sum(jnp.where(hit, gate_row, 0.0), axis=1, keepdims=True)
        y = ffn(x, *w, gate)
        dst = pl.multiple_of(off + q * cap, SEG_ALIGN)
        y_ref[pl.ds(dst, cap), :] = jnp.where(ranks < cnt, y, 0.0).astype(BF16)

    experts = []
    for j in range(EXPERTS_PER_STEP):
        e = s * EXPERTS_PER_STEP + j
        experts.append((cnt_s[i, e], off_s[i, e],
                        pos_ref[0, pl.ds(e, 1), :],
                        gem_ref[0, pl.ds(e, 1), :],
                        (wg_ref[j], wu_ref[j], wd_ref[j])))

    hots = [onehot(pos_row, 0) for _, _, pos_row, _, _ in experts]
    xs = jnp.dot(jnp.concatenate([p for _, _, p in hots], axis=0), h, preferred_element_type=F32).astype(BF16)
    for j, ((cnt, off, pos_row, gate_row, w), (ranks, hit, _)) in enumerate(zip(experts, hots)):
        expert_rows(xs[j * cap:(j + 1) * cap], ranks, hit, 0, gate_row, w, cnt, off)

        def block(q, carry):
            ranks, hit, p = onehot(pos_row, q)
            x = jnp.dot(p, h, preferred_element_type=F32).astype(BF16)
            expert_rows(x, ranks, hit, q, gate_row, w, cnt, off)
            return carry

        lax.fori_loop(1, (cnt + cap - 1) // cap, block, 0)

    @pl.when(s == N_EXPERTS // EXPERTS_PER_STEP - 1)
    def _():
        group_rows = lax.broadcasted_iota(jnp.int32, (SEG_ALIGN, 1), 0)
        groups_per_block = COMBINE_BLOCK // SEG_ALIGN

        def combine(r, carry):
            base = pl.multiple_of(r * COMBINE_BLOCK, COMBINE_BLOCK)

            def onehot_groups(c, carry):
                for u in range(GROUP_UNROLL):
                    gi = c * GROUP_UNROLL + u
                    g0 = pl.multiple_of(gi * SEG_ALIGN, SEG_ALIGN)
                    e = own_s[i, r * groups_per_block + gi]
                    rank = (group_rows + (base + g0 - off_s[i, e])).astype(F32)
                    hit = pos_ref[0, pl.ds(e, 1), :] == rank
                    p_ref[pl.ds(g0, SEG_ALIGN), :] = jnp.where(hit, 1.0, 0.0).astype(BF16)
                return carry

            lax.fori_loop(0, groups_per_block // GROUP_UNROLL, onehot_groups, 0)
            for n in range(d // ROW_TILE):
                cols = slice(n * ROW_TILE, (n + 1) * ROW_TILE)
                o_ref[0, :, cols] += lax.dot_general(p_ref[...], y_ref[pl.ds(base, COMBINE_BLOCK), cols],
                                                     (((0,), (0,)), ((), ())), preferred_element_type=F32)
            return carry

        lax.fori_loop(0, MOE_SLOTS // COMBINE_BLOCK, combine, 0)

        ctx = mod_ref[bsz:bsz + 1, 5 * d:6 * d]
        if tiles_per_batch is None:
            r = i * rows + lax.broadcasted_iota(jnp.int32, (rows, 1), 0)
            g2 = ctx
            for b in range(bsz):
                lo = b * rows_per_batch
                inside = jnp.where(r >= lo, jnp.where(r < lo + n_lat, 1, 0), 0)
                g2 = jnp.where(inside == 1, mod_ref[b:b + 1, 5 * d:6 * d], g2)
        else:
            g2 = mod_ref[pl.ds(i // tiles_per_batch, 1), 5 * d:6 * d]
        y = x_ref[0] + g2 * o_ref[0]
        if final:
            y = _rms(y) * fn_ref[...]
        o_ref[0] = y


def _moe(h, gates, xs, mod, wg, wu, wd, sg, su, sd, final_norm, layer, n_lat, final):
    bsz, m, d = xs.shape
    t = MOE_TILE
    if final:
        assert n_lat % t == 0
        tiles_per_batch = n_lat // t
        n_tiles = bsz * tiles_per_batch
        where = lambda i: (i // tiles_per_batch, i % tiles_per_batch, 0)
        out_shape = (bsz, n_lat, d)
    else:
        assert (bsz * m) % t == 0
        tiles_per_batch = None
        n_tiles = bsz * m // t
        where = lambda i: (0, i, 0)
        out_shape = (1, bsz * m, d)
        h, gates, xs = (a.reshape(1, bsz * m, a.shape[-1]) for a in (h, gates, xs))
    once = pl.Buffered(1)

    tok = jnp.arange(t)
    tri = (tok[None, :] < tok[:, None]).astype(BF16)
    lane = jnp.arange(LANE)
    upper = (lane[:, None] < lane[None, :]).astype(F32)
    groups = -(-MOE_SLOTS // SEG_ALIGN // LANE) * LANE
    em = jax.ShapeDtypeStruct((n_tiles, LANE, t), F32)
    ints = lambda n: jax.ShapeDtypeStruct((n_tiles, SUBLANE, n), jnp.int32)
    per_tile = lambda a, b: pl.BlockSpec((1, a, b), lambda i: (i, 0, 0))
    pos, gem, cnt, off, own = pl.pallas_call(
        _plan_kernel,
        grid=(n_tiles,),
        in_specs=[pl.BlockSpec((1, t, LANE), lambda i: where(i)),
                  pl.BlockSpec((t, t), lambda i: (0, 0)), pl.BlockSpec((LANE, LANE), lambda i: (0, 0))],
        out_specs=[per_tile(LANE, t), per_tile(LANE, t), per_tile(SUBLANE, LANE), per_tile(SUBLANE, LANE),
                   per_tile(SUBLANE, groups)],
        out_shape=[em, em, ints(LANE), ints(LANE), ints(groups)],
        compiler_params=_params(("parallel",)),
        name="moe_plan",
    )(gates, tri, upper)

    tile = lambda n: pl.BlockSpec((1, t, n), lambda i, s, *_: where(i), pipeline_mode=once)
    planned = lambda a, b: pl.BlockSpec((1, a, b), lambda i, s, *_: (i, 0, 0), pipeline_mode=once)
    whole = lambda a: pl.BlockSpec(a.shape, lambda i, s, *_: (0,) * a.ndim, pipeline_mode=once)
    experts = lambda a: pl.BlockSpec((None, EXPERTS_PER_STEP) + a.shape[2:], lambda i, s, *_: (layer, s, 0, 0))
    out = pl.pallas_call(
        functools.partial(_moe_kernel, bsz=bsz, rows_per_batch=m, n_lat=n_lat,
                          tiles_per_batch=tiles_per_batch, final=final),
        grid_spec=pltpu.PrefetchScalarGridSpec(
            num_scalar_prefetch=3,
            grid=(n_tiles, N_EXPERTS // EXPERTS_PER_STEP),
            in_specs=[tile(d), planned(LANE, t), planned(LANE, t),
                      experts(wg), experts(wu), experts(wd), whole(sg), whole(su), whole(sd),
                      tile(d), whole(mod), whole(final_norm)],
            out_specs=pl.BlockSpec((1, t, d), lambda i, s, *_: where(i)),
            scratch_shapes=[pltpu.VMEM((MOE_SLOTS + MOE_CAP, d), BF16), pltpu.VMEM((COMBINE_BLOCK, t), BF16)]),
        out_shape=jax.ShapeDtypeStruct(out_shape, F32),
        compiler_params=_params(("parallel", "arbitrary"), MOE_VMEM_LIMIT),
        name="moe_final" if final else "moe",
    )(cnt[:, 0], off[:, 0], own[:, 0], h, pos, gem, wg, wu, wd, sg, su, sd, xs, mod, final_norm)
    return out if final else out.reshape(bsz, m, d)


def _rope_tables(n_lat, m, head_dim, width):
    pos = jnp.arange(n_lat, dtype=jnp.int32)
    row = (pos // GRID_W).astype(F32)
    col = (pos % GRID_W).astype(F32)
    half = head_dim // 2
    quarter = half // 2
    lane = jnp.arange(head_dim)
    freq = (lane % quarter).astype(F32)
    inv = ROPE_THETA ** (-(2.0 * freq) / half)
    ang = jnp.where(lane[None, :] < half, row[:, None], col[:, None]) * inv[None, :]
    low = (lane % half) < quarter
    cos = jnp.cos(ang)
    sin = jnp.sin(ang)
    slo = jnp.where(low[None, :], -sin, 0.0)
    shi = jnp.where(low[None, :], 0.0, sin)
    reps = width // head_dim
    pad = lambda a, fill: jnp.concatenate(
        [jnp.tile(a, (1, reps)), jnp.full((m - n_lat, width), fill, F32)], axis=0)
    return pad(cos, 1.0), pad(slo, 0.0), pad(shi, 0.0)


def _mla_tables(tabs16):
    outs = []
    for a, fill in zip(tabs16, (1.0, 0.0, 0.0)):
        m = a.shape[0]
        slot = jnp.concatenate([jnp.full((m, MLA_NOPE), fill, F32), a,
                                jnp.full((m, QK_SLOT - MLA_QK), fill, F32)], axis=1)
        outs.append(jnp.tile(slot, (1, MLA_HEADS)))
    return tuple(outs)


def _slots(w, heads, width):
    w = w.reshape(w.shape[0], heads, width)
    pad = jnp.zeros((w.shape[0], heads, QK_SLOT - width), w.dtype)
    return jnp.concatenate([w, pad], axis=-1).reshape(w.shape[0], heads * QK_SLOT)


def _block_diag(w):
    dirs = w.shape[0]
    eye = jnp.eye(LRU_BLOCKS, dtype=w.dtype)
    full = jnp.einsum('dnij,nm->dnimj', w, eye).reshape(dirs, LRU_WIDTH, LRU_WIDTH)
    nl = LRU_WIDTH // LRU_LANES
    return jnp.stack([full[:, j * LRU_LANES:(j + 1) * LRU_LANES, j * LRU_LANES:(j + 1) * LRU_LANES]
                      for j in range(nl)], axis=1)


def kernel(x, c, ctx, c_ctx, mod_w, mod_b, ab_w_in, ab_w_out, diff_lambda_q, diff_lambda_k, diff_subln, lru_conv_w, lru_conv_b, lru_w_a, lru_b_a, lru_w_x, lru_b_x, lru_lambda, cd_w_in, cd_w_out, gqa_q_norm, gqa_k_norm, mla_q_norm, mla_kv_norm, mla_w_uq, mla_w_ukv, router_w, router_bias, exp_w_gate, exp_w_up, exp_w_down, sh_w_gate, sh_w_up, sh_w_down, final_norm):
    bsz, n_lat, d = x.shape
    n_ctx = ctx.shape[1]
    m = n_lat + n_ctx
    depth = mod_w.shape[0]
    assert depth == 2 and bsz < MOD_ROWS
    assert n_lat % ROW_TILE == 0 and n_ctx % ROW_TILE == 0 and n_lat % GRID_W == 0
    ctx_row = bsz

    xs = jnp.concatenate([x, ctx], axis=1)
    c_all = jnp.concatenate([c, c_ctx[None, :], jnp.zeros((MOD_ROWS - bsz - 1, d), F32)], axis=0)
    mods = _modulation(c_all, mod_w, mod_b)

    tabs64 = _rope_tables(n_lat, m, DIFF_HEAD_DIM, DIFF_HEADS * 2 * DIFF_HEAD_DIM)
    tabs_m = _mla_tables(_rope_tables(n_lat, m, MLA_ROPE, MLA_ROPE))

    exp_w = tuple(w.astype(BF16) for w in (exp_w_gate, exp_w_up, exp_w_down))

    def moe_weights(i):
        return exp_w + (sh_w_gate[i], sh_w_up[i], sh_w_down[i])

    fn = final_norm.reshape(1, d)

    mod = mods[0]
    qt, k, vt, u, g = _proj_ab(xs, mod, ab_w_in[0].astype(BF16), tabs64, n_lat, ctx_row)
    lambda_init = 0.8 - 0.6 * math.exp(-0.3 * 0)
    dv = 2 * DIFF_HEAD_DIM
    jobs = tuple(_Job((2 * h + j) * QK_SLOT, h, j, h * dv, dv) for h in range(DIFF_HEADS) for j in range(2))
    o_att = _attention(qt, k, vt, (diff_lambda_q[0], diff_lambda_k[0], diff_subln[0].reshape(-1, 1)),
                       jobs, n_lat, lambda_init, "diff_attention")
    o_rec = _lru(u, g, lru_conv_w[0], lru_conv_b[0].reshape(1, -1),
                 _block_diag(lru_w_a[0]).astype(BF16), lru_b_a[0].reshape(2, 1, LRU_WIDTH),
                 _block_diag(lru_w_x[0]).astype(BF16), lru_b_x[0].reshape(2, 1, LRU_WIDTH),
                 lru_lambda[0].reshape(2, 1, LRU_WIDTH), n_lat)
    xs, h2, gates = _out_proj(o_att, o_rec, xs, mod, ab_w_out[0].astype(BF16),
                              router_w[0].T, router_bias[0].reshape(-1, 1), n_lat, ctx_row)
    xs = _moe(h2, gates, xs, mod, *moe_weights(0), fn, 0, n_lat, False)

    mod = mods[1]
    w_in = cd_w_in[0]
    wq, wk = GQA_HEADS * GQA_HEAD_DIM, GQA_KV_HEADS * GQA_HEAD_DIM
    o_kr = wq + 2 * wk + MLA_Q_RANK + MLA_KV_RANK
    kr_slots = jnp.concatenate([jnp.zeros((d, MLA_NOPE), F32), w_in[:, o_kr:o_kr + MLA_ROPE],
                                jnp.zeros((d, QK_SLOT - MLA_QK), F32)], axis=1)
    w_in = jnp.concatenate([w_in[:, :o_kr], jnp.tile(kr_slots, (1, MLA_HEADS))], axis=1).astype(BF16)
    ukv = mla_w_ukv[0].reshape(MLA_KV_RANK, MLA_HEADS, MLA_NOPE + MLA_V)
    w_kn = _slots(ukv[:, :, :MLA_NOPE].reshape(MLA_KV_RANK, -1), MLA_HEADS, MLA_NOPE).astype(BF16)
    w_v = ukv[:, :, MLA_NOPE:].reshape(MLA_KV_RANK, MLA_HEADS * MLA_V).astype(BF16)
    w_uq = _slots(mla_w_uq[0], MLA_HEADS, MLA_QK).astype(BF16)
    lane = jnp.arange(wq)
    ones_bd = (lane[:, None] // GQA_HEAD_DIM == lane[None, :] // GQA_HEAD_DIM).astype(BF16)
    qgt, kg, vgt, qmt, km, vmt = _proj_cd(
        xs, mod, w_in, ones_bd, jnp.tile(gqa_q_norm[0], GQA_HEADS).reshape(1, -1),
        jnp.tile(gqa_k_norm[0], GQA_KV_HEADS).reshape(1, -1), mla_q_norm[0].reshape(1, -1),
        mla_kv_norm[0].reshape(1, -1), w_uq, w_kn, w_v, tabs64, tabs_m, n_lat, ctx_row)
    groups = GQA_HEADS // GQA_KV_HEADS
    jobs = tuple(_Job(h * QK_SLOT, 0, h // groups, (h // groups) * GQA_HEAD_DIM, GQA_HEAD_DIM)
                 for h in range(GQA_HEADS))
    o_gqa = _attention(qgt, kg, vgt, (), jobs, n_lat, None, "gqa_attention")
    jobs = tuple(_Job(h * QK_SLOT, h // 2, h % 2, h * MLA_V, MLA_V) for h in range(MLA_HEADS))
    o_mla = _attention(qmt, km, vmt, (), jobs, n_lat, None, "mla_attention")
    xs, h2, gates = _out_proj(o_gqa, o_mla, xs, mod, cd_w_out[0].astype(BF16),
                              router_w[1].T, router_bias[1].reshape(-1, 1), n_lat, ctx_row)
    return _moe(h2, gates, xs, mod, *moe_weights(1), fn, 1, n_lat, True)
```

```python
import functools
import math
from typing import NamedTuple

import jax
import jax.numpy as jnp
from jax import lax
from jax.experimental import pallas as pl
from jax.experimental.pallas import tpu as pltpu

F32 = jnp.float32
BF16 = jnp.bfloat16

GRID_W = 64
ROPE_THETA = 10000.0
EPS = 1e-6
DIFF_HEADS = 4
DIFF_HEAD_DIM = 64
LRU_WIDTH = 512
LRU_BLOCKS = 8
LRU_BW = LRU_WIDTH // LRU_BLOCKS
CONV_W = 4
LRU_C = 8.0
GQA_HEADS = 8
GQA_KV_HEADS = 2
GQA_HEAD_DIM = 64
MLA_HEADS = 8
MLA_Q_RANK = 256
MLA_KV_RANK = 128
MLA_NOPE = 32
MLA_ROPE = 16
MLA_V = 64
MLA_QK = MLA_NOPE + MLA_ROPE
N_EXPERTS = 64
N_GROUPS = 8
PER_GROUP = N_EXPERTS // N_GROUPS
TOPK_GROUPS = 4
TOP_K = 8
EXPERT_FF = 256
ROUTED_SCALE = 2.5

QK_SLOT = 64
LOG2E = math.log2(math.e)

ROW_TILE = 256
LRU_CHUNK = 128
LRU_LANES = 256
EXPERTS_PER_STEP = 4
LANE = 128
SUBLANE = 8
MOD_ROWS = 8
VMEM_LIMIT = 56 * 1024 * 1024
MOE_VMEM_LIMIT = 62 * 1024 * 1024
MOE_TILE = 1024
MOE_CAP = 160
SEG_ALIGN = 16
MOE_SLOTS = MOE_TILE * TOP_K + N_EXPERTS * SEG_ALIGN
COMBINE_BLOCK = 1024
GROUP_UNROLL = 4


def _params(sem, vmem=VMEM_LIMIT):
    return pltpu.CompilerParams(dimension_semantics=sem, vmem_limit_bytes=vmem)


def _silu(x):
    return x * jax.nn.sigmoid(x)


def _rms(x):
    return x * lax.rsqrt(jnp.mean(x * x, axis=-1, keepdims=True) + EPS)


def _mod_rows(mod_ref, ctx_row, b, k, d, row0, rows, n_lat):
    lat = mod_ref[pl.ds(b, 1), k * d:(k + 1) * d]
    ctx = mod_ref[ctx_row:ctx_row + 1, k * d:(k + 1) * d]
    if n_lat % rows == 0:
        return jnp.where(row0 >= n_lat, ctx, lat)
    r = row0 + lax.broadcasted_iota(jnp.int32, (rows, 1), 0)
    return jnp.where(r >= n_lat, ctx, lat)


def _rope(x, cos, sin_lo, sin_hi, half):
    n = x.shape[-1]
    return x * cos + pltpu.roll(x, n - half, 1) * sin_lo + pltpu.roll(x, half, 1) * sin_hi


def _mod_kernel(c_ref, w_ref, b_ref, o_ref):
    s = _silu(c_ref[...])
    o_ref[0] = jnp.dot(s, w_ref[0], preferred_element_type=F32, precision=lax.Precision.HIGHEST) + b_ref[0]


def _modulation(c_all, mod_w, mod_b):
    depth, d, n = mod_w.shape
    tn = n // 4
    return pl.pallas_call(
        _mod_kernel,
        grid=(depth, n // tn),
        in_specs=[pl.BlockSpec((MOD_ROWS, d), lambda i, j: (0, 0)),
                  pl.BlockSpec((1, d, tn), lambda i, j: (i, 0, j)),
                  pl.BlockSpec((1, 1, tn), lambda i, j: (i, 0, j))],
        out_specs=pl.BlockSpec((1, MOD_ROWS, tn), lambda i, j: (i, 0, j)),
        out_shape=jax.ShapeDtypeStruct((depth, MOD_ROWS, n), F32),
        compiler_params=_params(("arbitrary", "arbitrary")),
        name="modulation",
    )(c_all, mod_w, mod_b.reshape(depth, 1, n))


def _proj_ab_kernel(x_ref, mod_ref, w_ref, cos_ref, slo_ref, shi_ref,
                    qt_ref, k_ref, vt_ref, u_ref, g_ref, *, n_lat, ctx_row):
    b, t = pl.program_id(0), pl.program_id(1)
    d = x_ref.shape[-1]
    row0 = t * ROW_TILE
    shift = _mod_rows(mod_ref, ctx_row, b, 0, d, row0, ROW_TILE, n_lat)
    scale = _mod_rows(mod_ref, ctx_row, b, 1, d, row0, ROW_TILE, n_lat)
    h = (_rms(x_ref[0]) * (1.0 + scale) + shift).astype(BF16)
    acc = jnp.dot(h, w_ref[...], preferred_element_type=F32)
    w = DIFF_HEADS * 2 * DIFF_HEAD_DIM
    cos, slo, shi = cos_ref[...], slo_ref[...], shi_ref[...]
    q = _rope(acc[:, 0:w], cos, slo, shi, DIFF_HEAD_DIM // 4) * (DIFF_HEAD_DIM ** -0.5 * LOG2E)
    k = _rope(acc[:, w:2 * w], cos, slo, shi, DIFF_HEAD_DIM // 4)
    qt_ref[0, 0] = q.T.astype(BF16)
    k_ref[0] = k.astype(BF16)
    vt_ref[0, 0] = acc[:, 2 * w:3 * w].T.astype(BF16)
    u_ref[0] = acc[:, 3 * w:3 * w + LRU_WIDTH]
    g_ref[0] = jax.nn.gelu(acc[:, 3 * w + LRU_WIDTH:3 * w + 2 * LRU_WIDTH]).astype(BF16)


def _proj_ab(xs, mod, w_in, tabs, n_lat, ctx_row):
    bsz, m, d = xs.shape
    nt = m // ROW_TILE
    w = DIFF_HEADS * 2 * DIFF_HEAD_DIM
    n_in = w_in.shape[1]
    tile = lambda n: pl.BlockSpec((1, ROW_TILE, n), lambda b, t: (b, t, 0))
    ttile = pl.BlockSpec((1, 1, w, ROW_TILE), lambda b, t: (b, t, 0, 0))
    tab = pl.BlockSpec((ROW_TILE, w), lambda b, t: (t, 0))
    return pl.pallas_call(
        functools.partial(_proj_ab_kernel, n_lat=n_lat, ctx_row=ctx_row),
        grid=(bsz, nt),
        in_specs=[tile(d),
                  pl.BlockSpec(mod.shape, lambda b, t: (0, 0)),
                  pl.BlockSpec((d, n_in), lambda b, t: (0, 0)),
                  tab, tab, tab],
        out_specs=[ttile, tile(w), ttile, tile(LRU_WIDTH), tile(LRU_WIDTH)],
        out_shape=[jax.ShapeDtypeStruct((bsz, nt, w, ROW_TILE), BF16),
                   jax.ShapeDtypeStruct((bsz, m, w), BF16),
                   jax.ShapeDtypeStruct((bsz, nt, w, ROW_TILE), BF16),
                   jax.ShapeDtypeStruct((bsz, m, LRU_WIDTH), F32),
                   jax.ShapeDtypeStruct((bsz, m, LRU_WIDTH), BF16)],
        compiler_params=_params(("parallel", "parallel")),
        name="proj_ab",
    )(xs, mod, w_in, *tabs)


def _group_rms(x, ones_bd):
    x2 = x * x
    hi = x2.astype(BF16)
    lo = (x2 - hi.astype(F32)).astype(BF16)
    ss = jnp.dot(hi, ones_bd, preferred_element_type=F32) + jnp.dot(lo, ones_bd, preferred_element_type=F32)
    return x * lax.rsqrt(ss * (1.0 / GQA_HEAD_DIM) + EPS)


def _proj_cd_kernel(x_ref, mod_ref, w_ref, bd_ref, qn_ref, kn_ref, cqn_ref, ckvn_ref, wuq_ref, wkn_ref, wv_ref,
                    cos_ref, slo_ref, shi_ref, cosm_ref, slom_ref, shim_ref,
                    qgt_ref, kg_ref, vgt_ref, qmt_ref, km_ref, vmt_ref, *, n_lat, ctx_row):
    b, t = pl.program_id(0), pl.program_id(1)
    d = x_ref.shape[-1]
    row0 = t * ROW_TILE
    shift = _mod_rows(mod_ref, ctx_row, b, 0, d, row0, ROW_TILE, n_lat)
    scale = _mod_rows(mod_ref, ctx_row, b, 1, d, row0, ROW_TILE, n_lat)
    h = (_rms(x_ref[0]) * (1.0 + scale) + shift).astype(BF16)
    acc = jnp.dot(h, w_ref[...], preferred_element_type=F32)
    wq = GQA_HEADS * GQA_HEAD_DIM
    wk = GQA_KV_HEADS * GQA_HEAD_DIM
    o = 0
    q = acc[:, o:o + wq]; o += wq
    k = acc[:, o:o + wk]; o += wk
    v = acc[:, o:o + wk]; o += wk
    cq = acc[:, o:o + MLA_Q_RANK]; o += MLA_Q_RANK
    ckv = acc[:, o:o + MLA_KV_RANK]; o += MLA_KV_RANK
    kr = acc[:, o:o + MLA_HEADS * QK_SLOT]
    cos, slo, shi = cos_ref[...], slo_ref[...], shi_ref[...]
    bd = bd_ref[...]
    q = (_rope(_group_rms(q, bd) * qn_ref[...], cos, slo, shi, GQA_HEAD_DIM // 4)
         * (GQA_HEAD_DIM ** -0.5 * LOG2E))
    k = _rope(_group_rms(k, bd[:wk, :wk]) * kn_ref[...], cos[:, :wk], slo[:, :wk], shi[:, :wk], GQA_HEAD_DIM // 4)
    qgt_ref[0, 0] = q.T.astype(BF16)
    kg_ref[0] = k.astype(BF16)
    vgt_ref[0, 0] = v.T.astype(BF16)
    cosm, slom, shim = cosm_ref[...], slom_ref[...], shim_ref[...]
    cqn = (_rms(cq) * cqn_ref[...]).astype(BF16)
    mq = jnp.dot(cqn, wuq_ref[...], preferred_element_type=F32)
    qmt_ref[0, 0] = (_rope(mq, cosm, slom, shim, MLA_ROPE // 4) * (MLA_QK ** -0.5 * LOG2E)).T.astype(BF16)
    ckvn = (_rms(ckv) * ckvn_ref[...]).astype(BF16)
    km = jnp.dot(ckvn, wkn_ref[...], preferred_element_type=F32) + _rope(kr, cosm, slom, shim, MLA_ROPE // 4)
    km_ref[0] = km.astype(BF16)
    vmt_ref[0, 0] = jnp.dot(ckvn, wv_ref[...], preferred_element_type=F32).T.astype(BF16)


def _proj_cd(xs, mod, w_in, bd, qn, kn, cqn, ckvn, wuq, wkn, wv, tabs, tabs_m, n_lat, ctx_row):
    bsz, m, d = xs.shape
    nt = m // ROW_TILE
    wq = GQA_HEADS * GQA_HEAD_DIM
    wk = GQA_KV_HEADS * GQA_HEAD_DIM
    wm = MLA_HEADS * QK_SLOT
    wmv = MLA_HEADS * MLA_V
    tile = lambda n: pl.BlockSpec((1, ROW_TILE, n), lambda b, t: (b, t, 0))
    ttile = lambda n: pl.BlockSpec((1, 1, n, ROW_TILE), lambda b, t: (b, t, 0, 0))
    whole = lambda a: pl.BlockSpec(a.shape, lambda b, t: (0,) * a.ndim)
    tab = pl.BlockSpec((ROW_TILE, wq), lambda b, t: (t, 0))
    tabm = pl.BlockSpec((ROW_TILE, wm), lambda b, t: (t, 0))
    return pl.pallas_call(
        functools.partial(_proj_cd_kernel, n_lat=n_lat, ctx_row=ctx_row),
        grid=(bsz, nt),
        in_specs=[tile(d), whole(mod), whole(w_in), whole(bd), whole(qn), whole(kn), whole(cqn), whole(ckvn),
                  whole(wuq), whole(wkn), whole(wv), tab, tab, tab, tabm, tabm, tabm],
        out_specs=[ttile(wq), tile(wk), ttile(wk), ttile(wm), tile(wm), ttile(wmv)],
        out_shape=[jax.ShapeDtypeStruct((bsz, nt, wq, ROW_TILE), BF16),
                   jax.ShapeDtypeStruct((bsz, m, wk), BF16),
                   jax.ShapeDtypeStruct((bsz, nt, wk, ROW_TILE), BF16),
                   jax.ShapeDtypeStruct((bsz, nt, wm, ROW_TILE), BF16),
                   jax.ShapeDtypeStruct((bsz, m, wm), BF16),
                   jax.ShapeDtypeStruct((bsz, nt, wmv, ROW_TILE), BF16)],
        compiler_params=_params(("parallel", "parallel")),
        name="proj_cd",
    )(xs, mod, w_in, bd, qn, kn, cqn, ckvn, wuq, wkn, wv, *tabs, *tabs_m)


class _Job(NamedTuple):
    q_row: int
    k_group: int
    k_half: int
    v_row: int
    dv: int


def _attend(qt_ref, qnext_ref, k_ref, vt_ref, s_ref, mp_ref, jobs, lo, hi, chained):
    assert len(jobs) % 2 == 0
    tq = qt_ref.shape[-1]
    n = hi - lo
    group = next(g for g in (11, 3, 2, 1) if n % g == 0)
    steps = n // group
    row_groups = ROW_TILE // SUBLANE

    def weights(job, ref=qt_ref):
        qh = ref[0, 0, job.q_row:job.q_row + QK_SLOT, :]
        z = jnp.zeros_like(qh)
        return jnp.concatenate([z, qh] if job.k_half else [qh, z], axis=0)

    def score(c0, buf, qw, job, mp):
        r0 = pl.multiple_of(c0 * ROW_TILE, ROW_TILE)
        kc = k_ref[0, pl.ds(r0, group * ROW_TILE), job.k_group * LANE:(job.k_group + 1) * LANE]
        s_all = jnp.dot(kc, qw, preferred_element_type=F32)
        for g in range(group):
            s = s_all[g * ROW_TILE:(g + 1) * ROW_TILE]
            s_ref[buf, c0 + g] = s
            mp = jnp.maximum(mp, jnp.max(s.reshape(row_groups, SUBLANE, tq), axis=0))
        return mp

    def value(c, buf, m, job, lp, acc):
        p = jnp.exp2(s_ref[buf, c] - m)
        lp = lp + jnp.sum(p.reshape(row_groups, SUBLANE, tq), axis=0)
        acc = acc + jnp.dot(vt_ref[0, c, job.v_row:job.v_row + job.dv, :], p.astype(BF16),
                            preferred_element_type=F32)
        return lp, acc

    def loop(body, init):
        return body(0, init) if steps == 1 else lax.fori_loop(0, steps, body, init)

    neg = jnp.full((SUBLANE, tq), -jnp.inf, F32)

    def prologue():
        qw = weights(jobs[0])

        def first(i, mp):
            return score(lo + i * group, 0, qw, jobs[0], mp)

        mp_ref[...] = loop(first, neg)

    if chained:
        pl.when(pl.program_id(1) == 0)(prologue)
    else:
        prologue()
    mp = mp_ref[...]
    results = []
    for j, job in enumerate(jobs):
        buf = j % 2
        m = jnp.max(mp, axis=0, keepdims=True)
        if j + 1 < len(jobs):
            nxt, qw = jobs[j + 1], weights(jobs[j + 1])
        elif chained:
            nxt, qw = jobs[0], weights(jobs[0], qnext_ref)
        else:
            nxt = None

        def body(i, carry):
            mp, lp, acc = carry
            if nxt is not None:
                mp = score(lo + i * group, 1 - buf, qw, nxt, mp)
            for g in range(group):
                lp, acc = value(lo + i * group + g, buf, m, job, lp, acc)
            return mp, lp, acc

        mp, lp, acc = loop(body, (neg, jnp.zeros((SUBLANE, tq), F32), jnp.zeros((job.dv, tq), F32)))
        results.append((acc, jnp.sum(lp, axis=0, keepdims=True)))
    if chained:
        mp_ref[...] = mp
    return results


def _attn_kernel(*refs, jobs, n_lat, lambda_init):
    if lambda_init is None:
        qt_ref, qnext_ref, k_ref, vt_ref, o_ref, s_ref, mp_ref = refs
    else:
        qt_ref, qnext_ref, k_ref, vt_ref, lq_ref, lk_ref, sub_ref, o_ref, s_ref, mp_ref = refs
    n_chunks = vt_ref.shape[1]
    first_ctx = n_lat // ROW_TILE
    t = pl.program_id(1)

    def run(lo, hi, chained):
        res = _attend(qt_ref, qnext_ref, k_ref, vt_ref, s_ref, mp_ref, jobs, lo, hi, chained)
        if lambda_init is None:
            outs = [acc / l for acc, l in res]
        else:
            lq, lk = lq_ref[...], lk_ref[...]
            lam = (jnp.exp(jnp.sum(lq[0:1] * lk[0:1], axis=-1, keepdims=True))
                   - jnp.exp(jnp.sum(lq[1:2] * lk[1:2], axis=-1, keepdims=True)) + lambda_init)
            outs = []
            for h in range(len(res) // 2):
                (a0, l0), (a1, l1) = res[2 * h], res[2 * h + 1]
                o = a0 / l0 - lam * (a1 / l1)
                o = o * lax.rsqrt(jnp.mean(o * o, axis=0, keepdims=True) + EPS)
                outs.append(o * sub_ref[...] * (1.0 - lambda_init))
        o_ref[0] = jnp.concatenate(outs, axis=0).T.astype(BF16)

    @pl.when(t < first_ctx)
    def _():
        run(0, n_chunks, True)

    @pl.when(t >= first_ctx)
    def _():
        run(first_ctx, n_chunks, False)


def _attention(qt, k, vt, extra, jobs, n_lat, lambda_init, name):
    bsz, nt, wq, tq = qt.shape
    m = k.shape[1]
    wo = sum(j.dv for j in jobs) if lambda_init is None else sum(j.dv for j in jobs) // 2
    once = pl.Buffered(1)
    whole = lambda a: pl.BlockSpec(a.shape, lambda b, t: (0,) * a.ndim)
    return pl.pallas_call(
        functools.partial(_attn_kernel, jobs=jobs, n_lat=n_lat, lambda_init=lambda_init),
        grid=(bsz, nt),
        in_specs=[pl.BlockSpec((1, 1, wq, tq), lambda b, t: (b, t, 0, 0)),
                  pl.BlockSpec((1, 1, wq, tq), lambda b, t: (b, jnp.minimum(t + 1, nt - 1), 0, 0)),
                  pl.BlockSpec((1,) + k.shape[1:], lambda b, t: (b, 0, 0), pipeline_mode=once),
                  pl.BlockSpec((1,) + vt.shape[1:], lambda b, t: (b, 0, 0, 0), pipeline_mode=once)]
                 + [whole(a) for a in extra],
        out_specs=pl.BlockSpec((1, tq, wo), lambda b, t: (b, t, 0)),
        out_shape=jax.ShapeDtypeStruct((bsz, m, wo), BF16),
        scratch_shapes=[pltpu.VMEM((2, nt, ROW_TILE, tq), F32), pltpu.VMEM((SUBLANE, tq), F32)],
        compiler_params=_params(("arbitrary", "arbitrary")),
        name=name,
    )(qt, qt, k, vt, *extra)


def _scan_chunk(a, b, carry, reverse):
    n = a.shape[0]
    rows = lax.broadcasted_iota(jnp.int32, (n, 1), 0)
    d = 1
    while d < n:
        sh = n - d if reverse else d
        valid = rows < n - d if reverse else rows >= d
        b = jnp.where(valid, a * pltpu.roll(b, sh, 0) + b, b)
        a = jnp.where(valid, a * pltpu.roll(a, sh, 0), a)
        d *= 2
    h = a * carry + b
    return h, (h[0:1] if reverse else h[n - 1:n])


def _lru_kernel(u_ref, g_ref, cw_ref, cb_ref, wa_ref, ba_ref, wx_ref, bx_ref, lam_ref, o_ref, up_ref, hf_ref, *, n_lat):
    m = u_ref.shape[1]
    lanes = u_ref.shape[2]
    n_chunks = m // LRU_CHUNK
    n_ctx_chunks = (m - n_lat) // LRU_CHUNK
    pad = SUBLANE

    up_ref[0:pad, :] = jnp.zeros((pad, lanes), F32)
    up_ref[pad + m:pad + m + pad, :] = jnp.zeros((pad, lanes), F32)

    def copy(c, _):
        r0 = pl.multiple_of(c * LRU_CHUNK, LRU_CHUNK)
        up_ref[pl.ds(pad + r0, LRU_CHUNK), :] = u_ref[0, pl.ds(r0, LRU_CHUNK), :]
        return 0

    lax.fori_loop(0, n_chunks, copy, 0)

    cw = cw_ref[...]
    cb = cb_ref[...]
    win_rows = LRU_CHUNK + 2 * pad

    def coeffs(c, d):
        r0 = pl.multiple_of(c * LRU_CHUNK, LRU_CHUNK)
        win = up_ref[pl.ds(r0, win_rows), :]
        r = r0 + lax.broadcasted_iota(jnp.int32, (LRU_CHUNK, 1), 0)
        at = lambda k: pltpu.roll(win, (win_rows - k) % win_rows, 0)[pad:pad + LRU_CHUNK]
        y = (cw[0:1] * jnp.where((r == n_lat) | (r == n_lat + 1), 0.0, at(-2))
             + cw[1:2] * jnp.where(r == n_lat, 0.0, at(-1))
             + cw[2:3] * win[pad:pad + LRU_CHUNK]
             + cw[3:4] * jnp.where(r == n_lat - 1, 0.0, at(1))
             + cb)
        yb = y.astype(BF16)
        ra = jax.nn.sigmoid(jnp.dot(yb, wa_ref[d, 0], preferred_element_type=F32) + ba_ref[d])
        ix = jax.nn.sigmoid(jnp.dot(yb, wx_ref[d, 0], preferred_element_type=F32) + bx_ref[d])
        z = -lam_ref[d]
        softplus = jnp.maximum(z, 0.0) + jnp.log(1.0 + jnp.exp(-jnp.abs(z)))
        a = jnp.exp(-LRU_C * ra * softplus)
        return a, jnp.sqrt(1.0 - a * a) * ix * y

    def fwd(s, carry):
        c = lax.rem(s + (n_chunks - n_ctx_chunks), n_chunks)
        a, b = coeffs(c, 0)
        h, carry = _scan_chunk(a, b, carry, False)
        r0 = pl.multiple_of(c * LRU_CHUNK, LRU_CHUNK)
        hf_ref[pl.ds(r0, LRU_CHUNK), :] = h
        return carry

    lax.fori_loop(0, n_chunks, fwd, jnp.zeros((1, lanes), F32))

    def bwd(s, carry):
        c = n_chunks - 1 - s
        a, b = coeffs(c, 1)
        h, carry = _scan_chunk(a, b, carry, True)
        r0 = pl.multiple_of(c * LRU_CHUNK, LRU_CHUNK)
        tot = hf_ref[pl.ds(r0, LRU_CHUNK), :] + h
        o_ref[0, pl.ds(r0, LRU_CHUNK), :] = (tot * g_ref[0, pl.ds(r0, LRU_CHUNK), :].astype(F32)).astype(BF16)
        return carry

    lax.fori_loop(0, n_chunks, bwd, jnp.zeros((1, lanes), F32))


def _lru(u, g, conv_w, conv_b, wa_bd, ba, wx_bd, bx, lam, n_lat):
    bsz, m, c = u.shape
    nl = c // LRU_LANES
    seq = lambda: pl.BlockSpec((1, m, LRU_LANES), lambda b, j: (b, 0, j))
    vec = lambda a: pl.BlockSpec(a.shape[:-1] + (LRU_LANES,), lambda b, j: (0,) * (a.ndim - 1) + (j,))
    mat = pl.BlockSpec((2, 1, LRU_LANES, LRU_LANES), lambda b, j: (0, j, 0, 0))
    return pl.pallas_call(
        functools.partial(_lru_kernel, n_lat=n_lat),
        grid=(bsz, nl),
        in_specs=[seq(), seq(), vec(conv_w), vec(conv_b), mat, vec(ba), mat, vec(bx), vec(lam)],
        out_specs=seq(),
        out_shape=jax.ShapeDtypeStruct((bsz, m, c), BF16),
        scratch_shapes=[pltpu.VMEM((m + 2 * SUBLANE, LRU_LANES), F32), pltpu.VMEM((m, LRU_LANES), F32)],
        compiler_params=_params(("parallel", "parallel")),
        name="rglru",
    )(u, g, conv_w, conv_b, wa_bd, ba, wx_bd, bx, lam)


def _route(logits_t, bias):
    n = logits_t.shape[-1]
    scores = jax.nn.sigmoid(logits_t)
    choice = scores + bias
    sub = lax.broadcasted_iota(jnp.int32, (PER_GROUP, n), 0)
    neg = -jnp.inf
    groups, gs = [], []
    for g in range(N_GROUPS):
        cg = choice[g * PER_GROUP:(g + 1) * PER_GROUP]
        m1 = jnp.max(cg, axis=0, keepdims=True)
        i1 = jnp.min(jnp.where(cg == m1, sub, PER_GROUP), axis=0, keepdims=True)
        m2 = jnp.max(jnp.where(sub == i1, neg, cg), axis=0, keepdims=True)
        groups.append(cg)
        gs.append(m1 + m2)
    masked = []
    for g in range(N_GROUPS):
        rank = jnp.zeros((1, n), jnp.int32)
        for o in range(N_GROUPS):
            if o != g:
                ahead = (gs[o] >= gs[g]) if o < g else (gs[o] > gs[g])
                rank = rank + jnp.where(ahead, 1, 0)
        masked.append(jnp.where(rank < TOPK_GROUPS, groups[g], neg))
    masked = jnp.concatenate(masked, axis=0)
    eidx = lax.broadcasted_iota(jnp.int32, (N_EXPERTS, n), 0)
    picked = jnp.zeros((N_EXPERTS, n), F32)
    for _ in range(TOP_K):
        mx = jnp.max(masked, axis=0, keepdims=True)
        first = jnp.min(jnp.where(masked == mx, eidx, N_EXPERTS), axis=0, keepdims=True)
        hit = eidx == first
        picked = jnp.where(hit, 1.0, picked)
        masked = jnp.where(hit, neg, masked)
    w = picked * scores
    return w / jnp.sum(w, axis=0, keepdims=True) * ROUTED_SCALE


def _out_proj_kernel(oa_ref, ob_ref, x_ref, mod_ref, wa_ref, wb_ref, rw_ref, rb_ref,
                     xo_ref, h_ref, gates_ref, *, n_lat, ctx_row):
    b, t = pl.program_id(0), pl.program_id(1)
    d = x_ref.shape[-1]
    row0 = t * ROW_TILE
    mv = lambda k: _mod_rows(mod_ref, ctx_row, b, k, d, row0, ROW_TILE, n_lat)
    y = (jnp.dot(oa_ref[0], wa_ref[...], preferred_element_type=F32)
         + jnp.dot(ob_ref[0], wb_ref[...], preferred_element_type=F32))
    x = x_ref[0] + mv(2) * y
    xo_ref[0] = x
    h = _rms(x) * (1.0 + mv(4)) + mv(3)
    h_ref[0] = h.astype(BF16)
    logits_t = lax.dot_general(rw_ref[...], h, (((1,), (1,)), ((), ())),
                               preferred_element_type=F32, precision=lax.Precision.HIGHEST)
    gates_t = _route(logits_t, rb_ref[...])
    gates_t = jnp.concatenate([gates_t, jnp.zeros((LANE - N_EXPERTS, ROW_TILE), F32)], axis=0)
    gates_ref[0] = gates_t.T


def _out_proj(oa, ob, xs, mod, w_out, router_wt, router_b, n_lat, ctx_row):
    bsz, m, d = xs.shape
    na, nb = oa.shape[-1], ob.shape[-1]
    wa, wb = w_out[:na], w_out[na:]
    tile = lambda n: pl.BlockSpec((1, ROW_TILE, n), lambda b, t: (b, t, 0))
    whole = lambda a: pl.BlockSpec(a.shape, lambda b, t: (0,) * a.ndim)
    return pl.pallas_call(
        functools.partial(_out_proj_kernel, n_lat=n_lat, ctx_row=ctx_row),
        grid=(bsz, m // ROW_TILE),
        in_specs=[tile(na), tile(nb), tile(d), whole(mod), whole(wa), whole(wb), whole(router_wt), whole(router_b)],
        out_specs=[tile(d), tile(d), tile(LANE)],
        out_shape=[jax.ShapeDtypeStruct((bsz, m, d), F32),
                   jax.ShapeDtypeStruct((bsz, m, d), BF16),
                   jax.ShapeDtypeStruct((bsz, m, LANE), F32)],
        compiler_params=_params(("parallel", "parallel")),
        name="out_proj_router",
    )(oa, ob, xs, mod, wa, wb, router_wt, router_b)


def _plan_kernel(gates_ref, tri_ref, upper_ref, pos_ref, gem_ref, cnt_ref, off_ref, own_ref):
    g = gates_ref[0]
    sel = g != 0.0
    one = jnp.where(sel, 1.0, 0.0)
    pos = jnp.dot(tri_ref[...], one.astype(BF16), preferred_element_type=F32)
    cnt = jnp.sum(one, axis=0, keepdims=True)
    cpad = jnp.floor((cnt + (SEG_ALIGN - 1)) * (1.0 / SEG_ALIGN)) * SEG_ALIGN
    off = jnp.dot(jnp.broadcast_to(cpad, (SUBLANE, LANE)), upper_ref[...], preferred_element_type=F32,
                  precision=lax.Precision.HIGHEST)[0:1]
    pos_ref[0] = jnp.where(sel, pos, -1.0).T
    gem_ref[0] = g.T
    cnt_ref[0] = jnp.broadcast_to(cnt, (SUBLANE, LANE)).astype(jnp.int32)
    off_ref[0] = jnp.broadcast_to(off, (SUBLANE, LANE)).astype(jnp.int32)
    groups = own_ref.shape[-1]
    ends = jnp.broadcast_to(off + cpad, (LANE, LANE)).T
    ends = jnp.concatenate([ends] * (groups // LANE), axis=1)
    start = lax.broadcasted_iota(jnp.int32, (LANE, groups), 1).astype(F32) * SEG_ALIGN
    real = lax.broadcasted_iota(jnp.int32, (LANE, groups), 0) < N_EXPERTS
    before = jnp.where(real, jnp.where(ends <= start, 1.0, 0.0), 0.0)
    owner = jnp.minimum(jnp.sum(before, axis=0, keepdims=True), N_EXPERTS - 1.0)
    own_ref[0] = jnp.broadcast_to(owner, (SUBLANE, groups)).astype(jnp.int32)


def _moe_kernel(cnt_s, off_s, own_s, h_ref, pos_ref, gem_ref, wg_ref, wu_ref, wd_ref,
                sg_ref, su_ref, sd_ref, x_ref, mod_ref, fn_ref, o_ref, y_ref, p_ref,
                *, bsz, rows_per_batch, n_lat, tiles_per_batch, final):
    i, s = pl.program_id(0), pl.program_id(1)
    rows, d = h_ref.shape[1], h_ref.shape[2]
    h = h_ref[0]

    def ffn(x, wg, wu, wd, gate):
        a = _silu(jnp.dot(x, wg, preferred_element_type=F32)) * jnp.dot(x, wu, preferred_element_type=F32)
        if gate is not None:
            a = a * gate
        return jnp.dot(a.astype(BF16), wd, preferred_element_type=F32)

    @pl.when(s == 0)
    def _():
        y_ref[...] = jnp.zeros(y_ref.shape, BF16)
        o_ref[0] = ffn(h, sg_ref[...].astype(BF16), su_ref[...].astype(BF16), sd_ref[...].astype(BF16), None)

    cap = MOE_CAP
    cap_rows = lax.broadcasted_iota(jnp.int32, (cap, 1), 0)

    def onehot(pos_row, q):
        ranks = cap_rows + q * cap
        hit = pos_row == ranks.astype(F32)
        return ranks, hit, jnp.where(hit, 1.0, 0.0).astype(BF16)

    def expert_rows(x, ranks, hit, q, gate_row, w, cnt, off):
        gate = jnp.sum(jnp.where(hit, gate_row, 0.0), axis=1, keepdims=True)
        y = ffn(x, *w, gate)
        dst = pl.multiple_of(off + q * cap, SEG_ALIGN)
        y_ref[pl.ds(dst, cap), :] = jnp.where(ranks < cnt, y.astype(BF16), y_ref[pl.ds(dst, cap), :])

    experts = []
    for j in range(EXPERTS_PER_STEP):
        e = s * EXPERTS_PER_STEP + j
        experts.append((cnt_s[i, e], off_s[i, e],
                        pos_ref[0, pl.ds(e, 1), :],
                        gem_ref[0, pl.ds(e, 1), :],
                        (wg_ref[j], wu_ref[j], wd_ref[j])))

    hots = [onehot(pos_row, 0) for _, _, pos_row, _, _ in experts]
    xs = jnp.dot(jnp.concatenate([p for _, _, p in hots], axis=0), h, preferred_element_type=F32).astype(BF16)
    for j, ((cnt, off, _, gate_row, w), (ranks, hit, _)) in enumerate(zip(experts, hots)):
        expert_rows(xs[j * cap:(j + 1) * cap], ranks, hit, 0, gate_row, w, cnt, off)

    for cnt, off, pos_row, gate_row, w in experts:
        def block(q, carry):
            ranks, hit, p = onehot(pos_row, q)
            x = jnp.dot(p, h, preferred_element_type=F32).astype(BF16)
            expert_rows(x, ranks, hit, q, gate_row, w, cnt, off)
            return carry

        lax.fori_loop(1, (cnt + cap - 1) // cap, block, 0)

    @pl.when(s == N_EXPERTS // EXPERTS_PER_STEP - 1)
    def _():
        group_rows = lax.broadcasted_iota(jnp.int32, (SEG_ALIGN, 1), 0)
        groups_per_block = COMBINE_BLOCK // SEG_ALIGN

        def combine(r, carry):
            base = pl.multiple_of(r * COMBINE_BLOCK, COMBINE_BLOCK)

            def onehot_groups(c, carry):
                for u in range(GROUP_UNROLL):
                    gi = c * GROUP_UNROLL + u
                    g0 = pl.multiple_of(gi * SEG_ALIGN, SEG_ALIGN)
                    e = own_s[i, r * groups_per_block + gi]
                    rank = (group_rows + (base + g0 - off_s[i, e])).astype(F32)
                    hit = pos_ref[0, pl.ds(e, 1), :] == rank
                    p_ref[pl.ds(g0, SEG_ALIGN), :] = jnp.where(hit, 1.0, 0.0).astype(BF16)
                return carry

            lax.fori_loop(0, groups_per_block // GROUP_UNROLL, onehot_groups, 0)
            for n in range(d // ROW_TILE):
                cols = slice(n * ROW_TILE, (n + 1) * ROW_TILE)
                o_ref[0, :, cols] += lax.dot_general(p_ref[...], y_ref[pl.ds(base, COMBINE_BLOCK), cols],
                                                     (((0,), (0,)), ((), ())), preferred_element_type=F32)
            return carry

        lax.fori_loop(0, MOE_SLOTS // COMBINE_BLOCK, combine, 0)

        ctx = mod_ref[bsz:bsz + 1, 5 * d:6 * d]
        if tiles_per_batch is None:
            r = i * rows + lax.broadcasted_iota(jnp.int32, (rows, 1), 0)
            g2 = ctx
            for b in range(bsz):
                lo = b * rows_per_batch
                inside = jnp.where(r >= lo, jnp.where(r < lo + n_lat, 1, 0), 0)
                g2 = jnp.where(inside == 1, mod_ref[b:b + 1, 5 * d:6 * d], g2)
        else:
            g2 = mod_ref[pl.ds(i // tiles_per_batch, 1), 5 * d:6 * d]
        y = x_ref[0] + g2 * o_ref[0]
        if final:
            y = _rms(y) * fn_ref[...]
        o_ref[0] = y


def _moe(h, gates, xs, mod, wg, wu, wd, sg, su, sd, final_norm, layer, n_lat, final):
    bsz, m, d = xs.shape
    t = MOE_TILE
    if final:
        assert n_lat % t == 0
        tiles_per_batch = n_lat // t
        n_tiles = bsz * tiles_per_batch
        where = lambda i: (i // tiles_per_batch, i % tiles_per_batch, 0)
        out_shape = (bsz, n_lat, d)
    else:
        assert (bsz * m) % t == 0
        tiles_per_batch = None
        n_tiles = bsz * m // t
        where = lambda i: (0, i, 0)
        out_shape = (1, bsz * m, d)
        h, gates, xs = (a.reshape(1, bsz * m, a.shape[-1]) for a in (h, gates, xs))
    once = pl.Buffered(1)

    tok = jnp.arange(t)
    tri = (tok[None, :] < tok[:, None]).astype(BF16)
    lane = jnp.arange(LANE)
    upper = (lane[:, None] < lane[None, :]).astype(F32)
    groups = -(-MOE_SLOTS // SEG_ALIGN // LANE) * LANE
    em = jax.ShapeDtypeStruct((n_tiles, LANE, t), F32)
    ints = lambda n: jax.ShapeDtypeStruct((n_tiles, SUBLANE, n), jnp.int32)
    per_tile = lambda a, b: pl.BlockSpec((1, a, b), lambda i: (i, 0, 0))
    pos, gem, cnt, off, own = pl.pallas_call(
        _plan_kernel,
        grid=(n_tiles,),
        in_specs=[pl.BlockSpec((1, t, LANE), lambda i: where(i)),
                  pl.BlockSpec((t, t), lambda i: (0, 0)), pl.BlockSpec((LANE, LANE), lambda i: (0, 0))],
        out_specs=[per_tile(LANE, t), per_tile(LANE, t), per_tile(SUBLANE, LANE), per_tile(SUBLANE, LANE),
                   per_tile(SUBLANE, groups)],
        out_shape=[em, em, ints(LANE), ints(LANE), ints(groups)],
        compiler_params=_params(("parallel",)),
        name="moe_plan",
    )(gates, tri, upper)

    tile = lambda n: pl.BlockSpec((1, t, n), lambda i, s, *_: where(i), pipeline_mode=once)
    planned = lambda a, b: pl.BlockSpec((1, a, b), lambda i, s, *_: (i, 0, 0), pipeline_mode=once)
    whole = lambda a: pl.BlockSpec(a.shape, lambda i, s, *_: (0,) * a.ndim, pipeline_mode=once)
    experts = lambda a: pl.BlockSpec((None, EXPERTS_PER_STEP) + a.shape[2:], lambda i, s, *_: (layer, s, 0, 0))
    out = pl.pallas_call(
        functools.partial(_moe_kernel, bsz=bsz, rows_per_batch=m, n_lat=n_lat,
                          tiles_per_batch=tiles_per_batch, final=final),
        grid_spec=pltpu.PrefetchScalarGridSpec(
            num_scalar_prefetch=3,
            grid=(n_tiles, N_EXPERTS // EXPERTS_PER_STEP),
            in_specs=[tile(d), planned(LANE, t), planned(LANE, t),
                      experts(wg), experts(wu), experts(wd), whole(sg), whole(su), whole(sd),
                      tile(d), whole(mod), whole(final_norm)],
            out_specs=pl.BlockSpec((1, t, d), lambda i, s, *_: where(i)),
            scratch_shapes=[pltpu.VMEM((MOE_SLOTS + MOE_CAP, d), BF16), pltpu.VMEM((COMBINE_BLOCK, t), BF16)]),
        out_shape=jax.ShapeDtypeStruct(out_shape, F32),
        compiler_params=_params(("parallel", "arbitrary"), MOE_VMEM_LIMIT),
        name="moe_final" if final else "moe",
    )(cnt[:, 0], off[:, 0], own[:, 0], h, pos, gem, wg, wu, wd, sg, su, sd, xs, mod, final_norm)
    return out if final else out.reshape(bsz, m, d)


def _rope_tables(n_lat, m, head_dim, width):
    pos = jnp.arange(n_lat, dtype=jnp.int32)
    row = (pos // GRID_W).astype(F32)
    col = (pos % GRID_W).astype(F32)
    half = head_dim // 2
    quarter = half // 2
    lane = jnp.arange(head_dim)
    freq = (lane % quarter).astype(F32)
    inv = ROPE_THETA ** (-(2.0 * freq) / half)
    ang = jnp.where(lane[None, :] < half, row[:, None], col[:, None]) * inv[None, :]
    low = (lane % half) < quarter
    cos = jnp.cos(ang)
    sin = jnp.sin(ang)
    slo = jnp.where(low[None, :], -sin, 0.0)
    shi = jnp.where(low[None, :], 0.0, sin)
    reps = width // head_dim
    pad = lambda a, fill: jnp.concatenate(
        [jnp.tile(a, (1, reps)), jnp.full((m - n_lat, width), fill, F32)], axis=0)
    return pad(cos, 1.0), pad(slo, 0.0), pad(shi, 0.0)


def _mla_tables(tabs16):
    outs = []
    for a, fill in zip(tabs16, (1.0, 0.0, 0.0)):
        m = a.shape[0]
        slot = jnp.concatenate([jnp.full((m, MLA_NOPE), fill, F32), a,
                                jnp.full((m, QK_SLOT - MLA_QK), fill, F32)], axis=1)
        outs.append(jnp.tile(slot, (1, MLA_HEADS)))
    return tuple(outs)


def _slots(w, heads, width):
    w = w.reshape(w.shape[0], heads, width)
    pad = jnp.zeros((w.shape[0], heads, QK_SLOT - width), w.dtype)
    return jnp.concatenate([w, pad], axis=-1).reshape(w.shape[0], heads * QK_SLOT)


def _block_diag(w):
    dirs = w.shape[0]
    eye = jnp.eye(LRU_BLOCKS, dtype=w.dtype)
    full = jnp.einsum('dnij,nm->dnimj', w, eye).reshape(dirs, LRU_WIDTH, LRU_WIDTH)
    nl = LRU_WIDTH // LRU_LANES
    return jnp.stack([full[:, j * LRU_LANES:(j + 1) * LRU_LANES, j * LRU_LANES:(j + 1) * LRU_LANES]
                      for j in range(nl)], axis=1)


def kernel(x, c, ctx, c_ctx, mod_w, mod_b, ab_w_in, ab_w_out, diff_lambda_q, diff_lambda_k, diff_subln, lru_conv_w, lru_conv_b, lru_w_a, lru_b_a, lru_w_x, lru_b_x, lru_lambda, cd_w_in, cd_w_out, gqa_q_norm, gqa_k_norm, mla_q_norm, mla_kv_norm, mla_w_uq, mla_w_ukv, router_w, router_bias, exp_w_gate, exp_w_up, exp_w_down, sh_w_gate, sh_w_up, sh_w_down, final_norm):
    bsz, n_lat, d = x.shape
    n_ctx = ctx.shape[1]
    m = n_lat + n_ctx
    depth = mod_w.shape[0]
    assert depth == 2 and bsz < MOD_ROWS
    assert n_lat % ROW_TILE == 0 and n_ctx % ROW_TILE == 0 and n_lat % GRID_W == 0
    ctx_row = bsz

    xs = jnp.concatenate([x, ctx], axis=1)
    c_all = jnp.concatenate([c, c_ctx[None, :], jnp.zeros((MOD_ROWS - bsz - 1, d), F32)], axis=0)
    mods = _modulation(c_all, mod_w, mod_b)

    tabs64 = _rope_tables(n_lat, m, DIFF_HEAD_DIM, DIFF_HEADS * 2 * DIFF_HEAD_DIM)
    tabs_m = _mla_tables(_rope_tables(n_lat, m, MLA_ROPE, MLA_ROPE))

    exp_w = tuple(w.astype(BF16) for w in (exp_w_gate, exp_w_up, exp_w_down))

    def moe_weights(i):
        return exp_w + (sh_w_gate[i], sh_w_up[i], sh_w_down[i])

    fn = final_norm.reshape(1, d)

    mod = mods[0]
    qt, k, vt, u, g = _proj_ab(xs, mod, ab_w_in[0].astype(BF16), tabs64, n_lat, ctx_row)
    lambda_init = 0.8 - 0.6 * math.exp(-0.3 * 0)
    dv = 2 * DIFF_HEAD_DIM
    jobs = tuple(_Job((2 * h + j) * QK_SLOT, h, j, h * dv, dv) for h in range(DIFF_HEADS) for j in range(2))
    o_att = _attention(qt, k, vt, (diff_lambda_q[0], diff_lambda_k[0], diff_subln[0].reshape(-1, 1)),
                       jobs, n_lat, lambda_init, "diff_attention")
    o_rec = _lru(u, g, lru_conv_w[0], lru_conv_b[0].reshape(1, -1),
                 _block_diag(lru_w_a[0]).astype(BF16), lru_b_a[0].reshape(2, 1, LRU_WIDTH),
                 _block_diag(lru_w_x[0]).astype(BF16), lru_b_x[0].reshape(2, 1, LRU_WIDTH),
                 lru_lambda[0].reshape(2, 1, LRU_WIDTH), n_lat)
    xs, h2, gates = _out_proj(o_att, o_rec, xs, mod, ab_w_out[0].astype(BF16),
                              router_w[0].T, router_bias[0].reshape(-1, 1), n_lat, ctx_row)
    xs = _moe(h2, gates, xs, mod, *moe_weights(0), fn, 0, n_lat, False)

    mod = mods[1]
    w_in = cd_w_in[0]
    wq, wk = GQA_HEADS * GQA_HEAD_DIM, GQA_KV_HEADS * GQA_HEAD_DIM
    o_kr = wq + 2 * wk + MLA_Q_RANK + MLA_KV_RANK
    kr_slots = jnp.concatenate([jnp.zeros((d, MLA_NOPE), F32), w_in[:, o_kr:o_kr + MLA_ROPE],
                                jnp.zeros((d, QK_SLOT - MLA_QK), F32)], axis=1)
    w_in = jnp.concatenate([w_in[:, :o_kr], jnp.tile(kr_slots, (1, MLA_HEADS))], axis=1).astype(BF16)
    ukv = mla_w_ukv[0].reshape(MLA_KV_RANK, MLA_HEADS, MLA_NOPE + MLA_V)
    w_kn = _slots(ukv[:, :, :MLA_NOPE].reshape(MLA_KV_RANK, -1), MLA_HEADS, MLA_NOPE).astype(BF16)
    w_v = ukv[:, :, MLA_NOPE:].reshape(MLA_KV_RANK, MLA_HEADS * MLA_V).astype(BF16)
    w_uq = _slots(mla_w_uq[0], MLA_HEADS, MLA_QK).astype(BF16)
    lane = jnp.arange(wq)
    ones_bd = (lane[:, None] // GQA_HEAD_DIM == lane[None, :] // GQA_HEAD_DIM).astype(BF16)
    qgt, kg, vgt, qmt, km, vmt = _proj_cd(
        xs, mod, w_in, ones_bd, jnp.tile(gqa_q_norm[0], GQA_HEADS).reshape(1, -1),
        jnp.tile(gqa_k_norm[0], GQA_KV_HEADS).reshape(1, -1), mla_q_norm[0].reshape(1, -1),
        mla_kv_norm[0].reshape(1, -1), w_uq, w_kn, w_v, tabs64, tabs_m, n_lat, ctx_row)
    groups = GQA_HEADS // GQA_KV_HEADS
    jobs = tuple(_Job(h * QK_SLOT, 0, h // groups, (h // groups) * GQA_HEAD_DIM, GQA_HEAD_DIM)
                 for h in range(GQA_HEADS))
    o_gqa = _attention(qgt, kg, vgt, (), jobs, n_lat, None, "gqa_attention")
    jobs = tuple(_Job(h * QK_SLOT, h // 2, h % 2, h * MLA_V, MLA_V) for h in range(MLA_HEADS))
    o_mla = _attention(qmt, km, vmt, (), jobs, n_lat, None, "mla_attention")
    xs, h2, gates = _out_proj(o_gqa, o_mla, xs, mod, cd_w_out[0].astype(BF16),
                              router_w[1].T, router_bias[1].reshape(-1, 1), n_lat, ctx_row)
    return _moe(h2, gates, xs, mod, *moe_weights(1), fn, 1, n_lat, True)
```

```python
import functools
import math
from typing import NamedTuple

import jax
import jax.numpy as jnp
from jax import lax
from jax.experimental import pallas as pl
from jax.experimental.pallas import tpu as pltpu

F32 = jnp.float32
BF16 = jnp.bfloat16

GRID_W = 64
ROPE_THETA = 10000.0
EPS = 1e-6
DIFF_HEADS = 4
DIFF_HEAD_DIM = 64
LRU_WIDTH = 512
LRU_BLOCKS = 8
LRU_BW = LRU_WIDTH // LRU_BLOCKS
CONV_W = 4
LRU_C = 8.0
GQA_HEADS = 8
GQA_KV_HEADS = 2
GQA_HEAD_DIM = 64
MLA_HEADS = 8
MLA_Q_RANK = 256
MLA_KV_RANK = 128
MLA_NOPE = 32
MLA_ROPE = 16
MLA_V = 64
MLA_QK = MLA_NOPE + MLA_ROPE
N_EXPERTS = 64
N_GROUPS = 8
PER_GROUP = N_EXPERTS // N_GROUPS
TOPK_GROUPS = 4
TOP_K = 8
EXPERT_FF = 256
ROUTED_SCALE = 2.5

QK_SLOT = 64
LOG2E = math.log2(math.e)

ROW_TILE = 256
LRU_CHUNK = 128
LRU_LANES = 256
EXPERTS_PER_STEP = 4
LANE = 128
SUBLANE = 8
MOD_ROWS = 8
VMEM_LIMIT = 56 * 1024 * 1024
MOE_VMEM_LIMIT = 62 * 1024 * 1024
MOE_TILE = 1024
MOE_CAP = 160
SEG_ALIGN = 16
MOE_SLOTS = MOE_TILE * TOP_K + N_EXPERTS * SEG_ALIGN
COMBINE_BLOCK = 1024
GROUP_UNROLL = 4


def _params(sem, vmem=VMEM_LIMIT):
    return pltpu.CompilerParams(dimension_semantics=sem, vmem_limit_bytes=vmem)


def _silu(x):
    return x * jax.nn.sigmoid(x)


def _rms(x):
    return x * lax.rsqrt(jnp.mean(x * x, axis=-1, keepdims=True) + EPS)


def _mod_rows(mod_ref, ctx_row, b, k, d, row0, rows, n_lat):
    lat = mod_ref[pl.ds(b, 1), k * d:(k + 1) * d]
    ctx = mod_ref[ctx_row:ctx_row + 1, k * d:(k + 1) * d]
    if n_lat % rows == 0:
        return jnp.where(row0 >= n_lat, ctx, lat)
    r = row0 + lax.broadcasted_iota(jnp.int32, (rows, 1), 0)
    return jnp.where(r >= n_lat, ctx, lat)


def _rope(x, cos, sin_lo, sin_hi, half):
    n = x.shape[-1]
    return x * cos + pltpu.roll(x, n - half, 1) * sin_lo + pltpu.roll(x, half, 1) * sin_hi


def _mod_kernel(c_ref, w_ref, b_ref, o_ref):
    s = _silu(c_ref[...])
    o_ref[0] = jnp.dot(s, w_ref[0], preferred_element_type=F32, precision=lax.Precision.HIGHEST) + b_ref[0]


def _modulation(c_all, mod_w, mod_b):
    depth, d, n = mod_w.shape
    tn = n // 4
    return pl.pallas_call(
        _mod_kernel,
        grid=(depth, n // tn),
        in_specs=[pl.BlockSpec((MOD_ROWS, d), lambda i, j: (0, 0)),
                  pl.BlockSpec((1, d, tn), lambda i, j: (i, 0, j)),
                  pl.BlockSpec((1, 1, tn), lambda i, j: (i, 0, j))],
        out_specs=pl.BlockSpec((1, MOD_ROWS, tn), lambda i, j: (i, 0, j)),
        out_shape=jax.ShapeDtypeStruct((depth, MOD_ROWS, n), F32),
        compiler_params=_params(("arbitrary", "arbitrary")),
        name="modulation",
    )(c_all, mod_w, mod_b.reshape(depth, 1, n))


def _proj_ab_kernel(x_ref, mod_ref, w_ref, cos_ref, slo_ref, shi_ref,
                    qt_ref, k_ref, vt_ref, u_ref, g_ref, *, n_lat, ctx_row):
    b, t = pl.program_id(0), pl.program_id(1)
    d = x_ref.shape[-1]
    row0 = t * ROW_TILE
    shift = _mod_rows(mod_ref, ctx_row, b, 0, d, row0, ROW_TILE, n_lat)
    scale = _mod_rows(mod_ref, ctx_row, b, 1, d, row0, ROW_TILE, n_lat)
    h = (_rms(x_ref[0]) * (1.0 + scale) + shift).astype(BF16)
    acc = jnp.dot(h, w_ref[...], preferred_element_type=F32)
    w = DIFF_HEADS * 2 * DIFF_HEAD_DIM
    cos, slo, shi = cos_ref[...], slo_ref[...], shi_ref[...]
    q = _rope(acc[:, 0:w], cos, slo, shi, DIFF_HEAD_DIM // 4) * (DIFF_HEAD_DIM ** -0.5 * LOG2E)
    k = _rope(acc[:, w:2 * w], cos, slo, shi, DIFF_HEAD_DIM // 4)
    qt_ref[0, 0] = q.T.astype(BF16)
    k_ref[0] = k.astype(BF16)
    vt_ref[0] = acc[:, 2 * w:3 * w].T.astype(BF16)
    u_ref[0] = acc[:, 3 * w:3 * w + LRU_WIDTH]
    g_ref[0] = jax.nn.gelu(acc[:, 3 * w + LRU_WIDTH:3 * w + 2 * LRU_WIDTH]).astype(BF16)


def _proj_ab(xs, mod, w_in, tabs, n_lat, ctx_row):
    bsz, m, d = xs.shape
    nt = m // ROW_TILE
    w = DIFF_HEADS * 2 * DIFF_HEAD_DIM
    n_in = w_in.shape[1]
    tile = lambda n: pl.BlockSpec((1, ROW_TILE, n), lambda b, t: (b, t, 0))
    ttile = pl.BlockSpec((1, 1, w, ROW_TILE), lambda b, t: (b, t, 0, 0))
    vtile = pl.BlockSpec((1, w, ROW_TILE), lambda b, t: (b, 0, t))
    tab = pl.BlockSpec((ROW_TILE, w), lambda b, t: (t, 0))
    return pl.pallas_call(
        functools.partial(_proj_ab_kernel, n_lat=n_lat, ctx_row=ctx_row),
        grid=(bsz, nt),
        in_specs=[tile(d),
                  pl.BlockSpec(mod.shape, lambda b, t: (0, 0)),
                  pl.BlockSpec((d, n_in), lambda b, t: (0, 0)),
                  tab, tab, tab],
        out_specs=[ttile, tile(w), vtile, tile(LRU_WIDTH), tile(LRU_WIDTH)],
        out_shape=[jax.ShapeDtypeStruct((bsz, nt, w, ROW_TILE), BF16),
                   jax.ShapeDtypeStruct((bsz, m, w), BF16),
                   jax.ShapeDtypeStruct((bsz, w, m), BF16),
                   jax.ShapeDtypeStruct((bsz, m, LRU_WIDTH), F32),
                   jax.ShapeDtypeStruct((bsz, m, LRU_WIDTH), BF16)],
        compiler_params=_params(("parallel", "parallel")),
        name="proj_ab",
    )(xs, mod, w_in, *tabs)


def _group_rms(x, ones_bd):
    x2 = x * x
    hi = x2.astype(BF16)
    lo = (x2 - hi.astype(F32)).astype(BF16)
    ss = jnp.dot(hi, ones_bd, preferred_element_type=F32) + jnp.dot(lo, ones_bd, preferred_element_type=F32)
    return x * lax.rsqrt(ss * (1.0 / GQA_HEAD_DIM) + EPS)


def _proj_cd_kernel(x_ref, mod_ref, w_ref, bd_ref, qn_ref, kn_ref, cqn_ref, ckvn_ref, wuq_ref, wkn_ref, wv_ref,
                    cos_ref, slo_ref, shi_ref, cosm_ref, slom_ref, shim_ref,
                    qgt_ref, kg_ref, vgt_ref, qmt_ref, km_ref, vmt_ref, *, n_lat, ctx_row):
    b, t = pl.program_id(0), pl.program_id(1)
    d = x_ref.shape[-1]
    row0 = t * ROW_TILE
    shift = _mod_rows(mod_ref, ctx_row, b, 0, d, row0, ROW_TILE, n_lat)
    scale = _mod_rows(mod_ref, ctx_row, b, 1, d, row0, ROW_TILE, n_lat)
    h = (_rms(x_ref[0]) * (1.0 + scale) + shift).astype(BF16)
    acc = jnp.dot(h, w_ref[...], preferred_element_type=F32)
    wq = GQA_HEADS * GQA_HEAD_DIM
    wk = GQA_KV_HEADS * GQA_HEAD_DIM
    o = 0
    q = acc[:, o:o + wq]; o += wq
    k = acc[:, o:o + wk]; o += wk
    v = acc[:, o:o + wk]; o += wk
    cq = acc[:, o:o + MLA_Q_RANK]; o += MLA_Q_RANK
    ckv = acc[:, o:o + MLA_KV_RANK]; o += MLA_KV_RANK
    kr = acc[:, o:o + MLA_HEADS * QK_SLOT]
    cos, slo, shi = cos_ref[...], slo_ref[...], shi_ref[...]
    bd = bd_ref[...]
    q = (_rope(_group_rms(q, bd) * qn_ref[...], cos, slo, shi, GQA_HEAD_DIM // 4)
         * (GQA_HEAD_DIM ** -0.5 * LOG2E))
    k = _rope(_group_rms(k, bd[:wk, :wk]) * kn_ref[...], cos[:, :wk], slo[:, :wk], shi[:, :wk], GQA_HEAD_DIM // 4)
    qgt_ref[0, 0] = q.T.astype(BF16)
    kg_ref[0] = k.astype(BF16)
    vgt_ref[0] = v.T.astype(BF16)
    cosm, slom, shim = cosm_ref[...], slom_ref[...], shim_ref[...]
    cqn = (_rms(cq) * cqn_ref[...]).astype(BF16)
    mq = jnp.dot(cqn, wuq_ref[...], preferred_element_type=F32)
    qmt_ref[0, 0] = (_rope(mq, cosm, slom, shim, MLA_ROPE // 4) * (MLA_QK ** -0.5 * LOG2E)).T.astype(BF16)
    ckvn = (_rms(ckv) * ckvn_ref[...]).astype(BF16)
    km = jnp.dot(ckvn, wkn_ref[...], preferred_element_type=F32) + _rope(kr, cosm, slom, shim, MLA_ROPE // 4)
    km_ref[0] = km.astype(BF16)
    vmt_ref[0] = jnp.dot(ckvn, wv_ref[...], preferred_element_type=F32).T.astype(BF16)


def _proj_cd(xs, mod, w_in, bd, qn, kn, cqn, ckvn, wuq, wkn, wv, tabs, tabs_m, n_lat, ctx_row):
    bsz, m, d = xs.shape
    nt = m // ROW_TILE
    wq = GQA_HEADS * GQA_HEAD_DIM
    wk = GQA_KV_HEADS * GQA_HEAD_DIM
    wm = MLA_HEADS * QK_SLOT
    wmv = MLA_HEADS * MLA_V
    tile = lambda n: pl.BlockSpec((1, ROW_TILE, n), lambda b, t: (b, t, 0))
    ttile = lambda n: pl.BlockSpec((1, 1, n, ROW_TILE), lambda b, t: (b, t, 0, 0))
    vtile = lambda n: pl.BlockSpec((1, n, ROW_TILE), lambda b, t: (b, 0, t))
    whole = lambda a: pl.BlockSpec(a.shape, lambda b, t: (0,) * a.ndim)
    tab = pl.BlockSpec((ROW_TILE, wq), lambda b, t: (t, 0))
    tabm = pl.BlockSpec((ROW_TILE, wm), lambda b, t: (t, 0))
    return pl.pallas_call(
        functools.partial(_proj_cd_kernel, n_lat=n_lat, ctx_row=ctx_row),
        grid=(bsz, nt),
        in_specs=[tile(d), whole(mod), whole(w_in), whole(bd), whole(qn), whole(kn), whole(cqn), whole(ckvn),
                  whole(wuq), whole(wkn), whole(wv), tab, tab, tab, tabm, tabm, tabm],
        out_specs=[ttile(wq), tile(wk), vtile(wk), ttile(wm), tile(wm), vtile(wmv)],
        out_shape=[jax.ShapeDtypeStruct((bsz, nt, wq, ROW_TILE), BF16),
                   jax.ShapeDtypeStruct((bsz, m, wk), BF16),
                   jax.ShapeDtypeStruct((bsz, wk, m), BF16),
                   jax.ShapeDtypeStruct((bsz, nt, wm, ROW_TILE), BF16),
                   jax.ShapeDtypeStruct((bsz, m, wm), BF16),
                   jax.ShapeDtypeStruct((bsz, wmv, m), BF16)],
        compiler_params=_params(("parallel", "parallel")),
        name="proj_cd",
    )(xs, mod, w_in, bd, qn, kn, cqn, ckvn, wuq, wkn, wv, *tabs, *tabs_m)


class _Job(NamedTuple):
    q_row: int
    k_group: int
    k_half: int
    v_row: int
    dv: int


def _attend(qt_ref, qnext_ref, k_ref, vt_ref, s_ref, mp_ref, jobs, lo, hi, chained):
    assert len(jobs) % 2 == 0
    tq = qt_ref.shape[-1]
    n = hi - lo
    group = next(g for g in (11, 3, 2, 1) if n % g == 0)
    steps = n // group
    row_groups = ROW_TILE // SUBLANE

    def weights(job, ref=qt_ref):
        qh = ref[0, 0, job.q_row:job.q_row + QK_SLOT, :]
        z = jnp.zeros_like(qh)
        return jnp.concatenate([z, qh] if job.k_half else [qh, z], axis=0)

    def score(c0, buf, qw, job, mp):
        r0 = pl.multiple_of(c0 * ROW_TILE, ROW_TILE)
        kc = k_ref[0, pl.ds(r0, group * ROW_TILE), job.k_group * LANE:(job.k_group + 1) * LANE]
        s_all = jnp.dot(kc, qw, preferred_element_type=F32)
        for g in range(group):
            s = s_all[g * ROW_TILE:(g + 1) * ROW_TILE]
            s_ref[buf, c0 + g] = s
            mp = jnp.maximum(mp, jnp.max(s.reshape(row_groups, SUBLANE, tq), axis=0))
        return mp

    def value(c0, buf, m, job, lp, acc):
        ps = []
        for g in range(group):
            p = jnp.exp2(s_ref[buf, c0 + g] - m)
            lp = lp + jnp.sum(p.reshape(row_groups, SUBLANE, tq), axis=0)
            ps.append(p.astype(BF16))
        k0 = pl.multiple_of(c0 * ROW_TILE, ROW_TILE)
        v = vt_ref[0, job.v_row:job.v_row + job.dv, pl.ds(k0, group * ROW_TILE)]
        return lp, acc + jnp.dot(v, jnp.concatenate(ps, axis=0), preferred_element_type=F32)

    def loop(body, init):
        return body(0, init) if steps == 1 else lax.fori_loop(0, steps, body, init)

    neg = jnp.full((SUBLANE, tq), -jnp.inf, F32)

    def prologue():
        qw = weights(jobs[0])

        def first(i, mp):
            return score(lo + i * group, 0, qw, jobs[0], mp)

        mp_ref[...] = loop(first, neg)

    if chained:
        pl.when(pl.program_id(1) == 0)(prologue)
    else:
        prologue()
    mp = mp_ref[...]
    results = []
    for j, job in enumerate(jobs):
        buf = j % 2
        m = jnp.max(mp, axis=0, keepdims=True)
        if j + 1 < len(jobs):
            nxt, qw = jobs[j + 1], weights(jobs[j + 1])
        elif chained:
            nxt, qw = jobs[0], weights(jobs[0], qnext_ref)
        else:
            nxt = None

        def body(i, carry):
            mp, lp, acc = carry
            if nxt is not None:
                mp = score(lo + i * group, 1 - buf, qw, nxt, mp)
            lp, acc = value(lo + i * group, buf, m, job, lp, acc)
            return mp, lp, acc

        mp, lp, acc = loop(body, (neg, jnp.zeros((SUBLANE, tq), F32), jnp.zeros((job.dv, tq), F32)))
        results.append((acc, jnp.sum(lp, axis=0, keepdims=True)))
    if chained:
        mp_ref[...] = mp
    return results


def _attn_kernel(*refs, jobs, n_lat, lambda_init):
    if lambda_init is None:
        qt_ref, qnext_ref, k_ref, vt_ref, o_ref, s_ref, mp_ref = refs
    else:
        qt_ref, qnext_ref, k_ref, vt_ref, lq_ref, lk_ref, sub_ref, o_ref, s_ref, mp_ref = refs
    n_chunks = k_ref.shape[1] // ROW_TILE
    first_ctx = n_lat // ROW_TILE
    t = pl.program_id(1)

    def run(lo, hi, chained):
        res = _attend(qt_ref, qnext_ref, k_ref, vt_ref, s_ref, mp_ref, jobs, lo, hi, chained)
        if lambda_init is None:
            outs = [acc / l for acc, l in res]
        else:
            lq, lk = lq_ref[...], lk_ref[...]
            lam = (jnp.exp(jnp.sum(lq[0:1] * lk[0:1], axis=-1, keepdims=True))
                   - jnp.exp(jnp.sum(lq[1:2] * lk[1:2], axis=-1, keepdims=True)) + lambda_init)
            outs = []
            for h in range(len(res) // 2):
                (a0, l0), (a1, l1) = res[2 * h], res[2 * h + 1]
                o = a0 / l0 - lam * (a1 / l1)
                o = o * lax.rsqrt(jnp.mean(o * o, axis=0, keepdims=True) + EPS)
                outs.append(o * sub_ref[...] * (1.0 - lambda_init))
        o_ref[0] = jnp.concatenate(outs, axis=0).T.astype(BF16)

    @pl.when(t < first_ctx)
    def _():
        run(0, n_chunks, True)

    @pl.when(t >= first_ctx)
    def _():
        run(first_ctx, n_chunks, False)


def _attention(qt, k, vt, extra, jobs, n_lat, lambda_init, name):
    bsz, nt, wq, tq = qt.shape
    m = k.shape[1]
    wo = sum(j.dv for j in jobs) if lambda_init is None else sum(j.dv for j in jobs) // 2
    once = pl.Buffered(1)
    whole = lambda a: pl.BlockSpec(a.shape, lambda b, t: (0,) * a.ndim)
    return pl.pallas_call(
        functools.partial(_attn_kernel, jobs=jobs, n_lat=n_lat, lambda_init=lambda_init),
        grid=(bsz, nt),
        in_specs=[pl.BlockSpec((1, 1, wq, tq), lambda b, t: (b, t, 0, 0)),
                  pl.BlockSpec((1, 1, wq, tq), lambda b, t: (b, jnp.minimum(t + 1, nt - 1), 0, 0)),
                  pl.BlockSpec((1,) + k.shape[1:], lambda b, t: (b, 0, 0), pipeline_mode=once),
                  pl.BlockSpec((1,) + vt.shape[1:], lambda b, t: (b, 0, 0), pipeline_mode=once)]
                 + [whole(a) for a in extra],
        out_specs=pl.BlockSpec((1, tq, wo), lambda b, t: (b, t, 0)),
        out_shape=jax.ShapeDtypeStruct((bsz, m, wo), BF16),
        scratch_shapes=[pltpu.VMEM((2, nt, ROW_TILE, tq), F32), pltpu.VMEM((SUBLANE, tq), F32)],
        compiler_params=_params(("arbitrary", "arbitrary")),
        name=name,
    )(qt, qt, k, vt, *extra)


def _scan_chunk(a, b, carry, reverse):
    n = a.shape[0]
    rows = lax.broadcasted_iota(jnp.int32, (n, 1), 0)
    d = 1
    while d < n:
        sh = n - d if reverse else d
        valid = rows < n - d if reverse else rows >= d
        b = jnp.where(valid, a * pltpu.roll(b, sh, 0) + b, b)
        a = jnp.where(valid, a * pltpu.roll(a, sh, 0), a)
        d *= 2
    h = a * carry + b
    return h, (h[0:1] if reverse else h[n - 1:n])


def _lru_kernel(u_ref, g_ref, cw_ref, cb_ref, wa_ref, ba_ref, wx_ref, bx_ref, lam_ref, o_ref, up_ref, hf_ref, *, n_lat):
    m = u_ref.shape[1]
    lanes = u_ref.shape[2]
    n_chunks = m // LRU_CHUNK
    n_ctx_chunks = (m - n_lat) // LRU_CHUNK
    pad = SUBLANE

    up_ref[0:pad, :] = jnp.zeros((pad, lanes), F32)
    up_ref[pad + m:pad + m + pad, :] = jnp.zeros((pad, lanes), F32)

    def copy(c, _):
        r0 = pl.multiple_of(c * LRU_CHUNK, LRU_CHUNK)
        up_ref[pl.ds(pad + r0, LRU_CHUNK), :] = u_ref[0, pl.ds(r0, LRU_CHUNK), :]
        return 0

    lax.fori_loop(0, n_chunks, copy, 0)

    cw = cw_ref[...]
    cb = cb_ref[...]
    win_rows = LRU_CHUNK + 2 * pad

    def coeffs(c, d):
        r0 = pl.multiple_of(c * LRU_CHUNK, LRU_CHUNK)
        win = up_ref[pl.ds(r0, win_rows), :]
        r = r0 + lax.broadcasted_iota(jnp.int32, (LRU_CHUNK, 1), 0)
        at = lambda k: pltpu.roll(win, (win_rows - k) % win_rows, 0)[pad:pad + LRU_CHUNK]
        y = (cw[0:1] * jnp.where((r == n_lat) | (r == n_lat + 1), 0.0, at(-2))
             + cw[1:2] * jnp.where(r == n_lat, 0.0, at(-1))
             + cw[2:3] * win[pad:pad + LRU_CHUNK]
             + cw[3:4] * jnp.where(r == n_lat - 1, 0.0, at(1))
             + cb)
        yb = y.astype(BF16)
        ra = jax.nn.sigmoid(jnp.dot(yb, wa_ref[d, 0], preferred_element_type=F32) + ba_ref[d])
        ix = jax.nn.sigmoid(jnp.dot(yb, wx_ref[d, 0], preferred_element_type=F32) + bx_ref[d])
        z = -lam_ref[d]
        softplus = jnp.maximum(z, 0.0) + jnp.log(1.0 + jnp.exp(-jnp.abs(z)))
        a = jnp.exp(-LRU_C * ra * softplus)
        return a, jnp.sqrt(1.0 - a * a) * ix * y

    def fwd(s, carry):
        c = lax.rem(s + (n_chunks - n_ctx_chunks), n_chunks)
        a, b = coeffs(c, 0)
        h, carry = _scan_chunk(a, b, carry, False)
        r0 = pl.multiple_of(c * LRU_CHUNK, LRU_CHUNK)
        hf_ref[pl.ds(r0, LRU_CHUNK), :] = h
        return carry

    lax.fori_loop(0, n_chunks, fwd, jnp.zeros((1, lanes), F32))

    def bwd(s, carry):
        c = n_chunks - 1 - s
        a, b = coeffs(c, 1)
        h, carry = _scan_chunk(a, b, carry, True)
        r0 = pl.multiple_of(c * LRU_CHUNK, LRU_CHUNK)
        tot = hf_ref[pl.ds(r0, LRU_CHUNK), :] + h
        o_ref[0, pl.ds(r0, LRU_CHUNK), :] = (tot * g_ref[0, pl.ds(r0, LRU_CHUNK), :].astype(F32)).astype(BF16)
        return carry

    lax.fori_loop(0, n_chunks, bwd, jnp.zeros((1, lanes), F32))


def _lru(u, g, conv_w, conv_b, wa_bd, ba, wx_bd, bx, lam, n_lat):
    bsz, m, c = u.shape
    nl = c // LRU_LANES
    seq = lambda: pl.BlockSpec((1, m, LRU_LANES), lambda b, j: (b, 0, j))
    vec = lambda a: pl.BlockSpec(a.shape[:-1] + (LRU_LANES,), lambda b, j: (0,) * (a.ndim - 1) + (j,))
    mat = pl.BlockSpec((2, 1, LRU_LANES, LRU_LANES), lambda b, j: (0, j, 0, 0))
    return pl.pallas_call(
        functools.partial(_lru_kernel, n_lat=n_lat),
        grid=(bsz, nl),
        in_specs=[seq(), seq(), vec(conv_w), vec(conv_b), mat, vec(ba), mat, vec(bx), vec(lam)],
        out_specs=seq(),
        out_shape=jax.ShapeDtypeStruct((bsz, m, c), BF16),
        scratch_shapes=[pltpu.VMEM((m + 2 * SUBLANE, LRU_LANES), F32), pltpu.VMEM((m, LRU_LANES), F32)],
        compiler_params=_params(("parallel", "parallel")),
        name="rglru",
    )(u, g, conv_w, conv_b, wa_bd, ba, wx_bd, bx, lam)


def _route(logits_t, bias):
    n = logits_t.shape[-1]
    scores = jax.nn.sigmoid(logits_t)
    choice = scores + bias
    sub = lax.broadcasted_iota(jnp.int32, (PER_GROUP, n), 0)
    neg = -jnp.inf
    groups, gs = [], []
    for g in range(N_GROUPS):
        cg = choice[g * PER_GROUP:(g + 1) * PER_GROUP]
        m1 = jnp.max(cg, axis=0, keepdims=True)
        i1 = jnp.min(jnp.where(cg == m1, sub, PER_GROUP), axis=0, keepdims=True)
        m2 = jnp.max(jnp.where(sub == i1, neg, cg), axis=0, keepdims=True)
        groups.append(cg)
        gs.append(m1 + m2)
    masked = []
    for g in range(N_GROUPS):
        rank = jnp.zeros((1, n), jnp.int32)
        for o in range(N_GROUPS):
            if o != g:
                ahead = (gs[o] >= gs[g]) if o < g else (gs[o] > gs[g])
                rank = rank + jnp.where(ahead, 1, 0)
        masked.append(jnp.where(rank < TOPK_GROUPS, groups[g], neg))
    masked = jnp.concatenate(masked, axis=0)
    eidx = lax.broadcasted_iota(jnp.int32, (N_EXPERTS, n), 0)
    picked = jnp.zeros((N_EXPERTS, n), F32)
    for _ in range(TOP_K):
        mx = jnp.max(masked, axis=0, keepdims=True)
        first = jnp.min(jnp.where(masked == mx, eidx, N_EXPERTS), axis=0, keepdims=True)
        hit = eidx == first
        picked = jnp.where(hit, 1.0, picked)
        masked = jnp.where(hit, neg, masked)
    w = picked * scores
    return w / jnp.sum(w, axis=0, keepdims=True) * ROUTED_SCALE


def _out_proj_kernel(oa_ref, ob_ref, x_ref, mod_ref, wa_ref, wb_ref, rwh_ref, rwl_ref, rb_ref,
                     xo_ref, h_ref, gates_ref, *, n_lat, ctx_row):
    b, t = pl.program_id(0), pl.program_id(1)
    d = x_ref.shape[-1]
    row0 = t * ROW_TILE
    mv = lambda k: _mod_rows(mod_ref, ctx_row, b, k, d, row0, ROW_TILE, n_lat)
    y = (jnp.dot(oa_ref[0], wa_ref[...], preferred_element_type=F32)
         + jnp.dot(ob_ref[0], wb_ref[...], preferred_element_type=F32))
    x = x_ref[0] + mv(2) * y
    xo_ref[0] = x
    h = _rms(x) * (1.0 + mv(4)) + mv(3)
    hb = h.astype(BF16)
    h_ref[0] = hb
    h_lo = (h - hb.astype(F32)).astype(BF16)
    nt = lambda a, b: lax.dot_general(a, b, (((1,), (1,)), ((), ())), preferred_element_type=F32)
    logits_t = nt(rwh_ref[...], hb) + (nt(rwh_ref[...], h_lo) + nt(rwl_ref[...], hb))
    gates_t = _route(logits_t, rb_ref[...])
    gates_t = jnp.concatenate([gates_t, jnp.zeros((LANE - N_EXPERTS, ROW_TILE), F32)], axis=0)
    gates_ref[0] = gates_t.T


def _out_proj(oa, ob, xs, mod, w_out, router_wt, router_b, n_lat, ctx_row):
    bsz, m, d = xs.shape
    na, nb = oa.shape[-1], ob.shape[-1]
    wa, wb = w_out[:na], w_out[na:]
    rw_hi = router_wt.astype(BF16)
    rw_lo = (router_wt - rw_hi.astype(F32)).astype(BF16)
    tile = lambda n: pl.BlockSpec((1, ROW_TILE, n), lambda b, t: (b, t, 0))
    whole = lambda a: pl.BlockSpec(a.shape, lambda b, t: (0,) * a.ndim)
    return pl.pallas_call(
        functools.partial(_out_proj_kernel, n_lat=n_lat, ctx_row=ctx_row),
        grid=(bsz, m // ROW_TILE),
        in_specs=[tile(na), tile(nb), tile(d), whole(mod), whole(wa), whole(wb), whole(rw_hi), whole(rw_lo),
                  whole(router_b)],
        out_specs=[tile(d), tile(d), tile(LANE)],
        out_shape=[jax.ShapeDtypeStruct((bsz, m, d), F32),
                   jax.ShapeDtypeStruct((bsz, m, d), BF16),
                   jax.ShapeDtypeStruct((bsz, m, LANE), F32)],
        compiler_params=_params(("parallel", "parallel")),
        name="out_proj_router",
    )(oa, ob, xs, mod, wa, wb, rw_hi, rw_lo, router_b)


def _plan_kernel(gates_ref, tri_ref, upper_ref, pos_ref, gem_ref, cnt_ref, off_ref, own_ref):
    g = gates_ref[0]
    sel = g != 0.0
    one = jnp.where(sel, 1.0, 0.0)
    pos = jnp.dot(tri_ref[...], one.astype(BF16), preferred_element_type=F32)
    cnt = jnp.sum(one, axis=0, keepdims=True)
    cpad = jnp.floor((cnt + (SEG_ALIGN - 1)) * (1.0 / SEG_ALIGN)) * SEG_ALIGN
    off = jnp.dot(jnp.broadcast_to(cpad, (SUBLANE, LANE)), upper_ref[...], preferred_element_type=F32,
                  precision=lax.Precision.HIGHEST)[0:1]
    pos_ref[0] = jnp.where(sel, pos, -1.0).T
    gem_ref[0] = g.T
    cnt_ref[0] = jnp.broadcast_to(cnt, (SUBLANE, LANE)).astype(jnp.int32)
    off_ref[0] = jnp.broadcast_to(off, (SUBLANE, LANE)).astype(jnp.int32)
    groups = own_ref.shape[-1]
    ends = jnp.broadcast_to(off + cpad, (LANE, LANE)).T
    ends = jnp.concatenate([ends] * (groups // LANE), axis=1)
    start = lax.broadcasted_iota(jnp.int32, (LANE, groups), 1).astype(F32) * SEG_ALIGN
    real = lax.broadcasted_iota(jnp.int32, (LANE, groups), 0) < N_EXPERTS
    before = jnp.where(real, jnp.where(ends <= start, 1.0, 0.0), 0.0)
    owner = jnp.minimum(jnp.sum(before, axis=0, keepdims=True), N_EXPERTS - 1.0)
    own_ref[0] = jnp.broadcast_to(owner, (SUBLANE, groups)).astype(jnp.int32)


def _moe_kernel(cnt_s, off_s, own_s, h_ref, pos_ref, gem_ref, wg_ref, wu_ref, wd_ref,
                sg_ref, su_ref, sd_ref, x_ref, mod_ref, fn_ref, o_ref, y_ref, p_ref,
                *, bsz, rows_per_batch, n_lat, tiles_per_batch, final):
    i, s = pl.program_id(0), pl.program_id(1)
    rows, d = h_ref.shape[1], h_ref.shape[2]
    h = h_ref[0]

    def ffn(x, wg, wu, wd, gate):
        a = _silu(jnp.dot(x, wg, preferred_element_type=F32)) * jnp.dot(x, wu, preferred_element_type=F32)
        if gate is not None:
            a = a * gate
        return jnp.dot(a.astype(BF16), wd, preferred_element_type=F32)

    @pl.when((s == 0) & (i == 0))
    def _():
        y_ref[...] = jnp.zeros(y_ref.shape, BF16)

    @pl.when(s == 0)
    def _():
        o_ref[0] = ffn(h, sg_ref[...].astype(BF16), su_ref[...].astype(BF16), sd_ref[...].astype(BF16), None)

    cap = MOE_CAP
    cap_rows = lax.broadcasted_iota(jnp.int32, (cap, 1), 0)

    def onehot(pos_row, q):
        ranks = cap_rows + q * cap
        hit = pos_row == ranks.astype(F32)
        return ranks, hit, jnp.where(hit, 1.0, 0.0).astype(BF16)

    def expert_rows(x, ranks, hit, q, gate_row, w, cnt, off):
        gate = jnp.sum(jnp.where(hit, gate_row, 0.0), axis=1, keepdims=True)
        y = ffn(x, *w, gate)
        dst = pl.multiple_of(off + q * cap, SEG_ALIGN)
        y_ref[pl.ds(dst, cap), :] = jnp.where(ranks < cnt, y.astype(BF16), y_ref[pl.ds(dst, cap), :])

    experts = []
    for j in range(EXPERTS_PER_STEP):
        e = s * EXPERTS_PER_STEP + j
        experts.append((cnt_s[i, e], off_s[i, e],
                        pos_ref[0, pl.ds(e, 1), :],
                        gem_ref[0, pl.ds(e, 1), :],
                        (wg_ref[j], wu_ref[j], wd_ref[j])))

    hots = [onehot(pos_row, 0) for _, _, pos_row, _, _ in experts]
    xs = jnp.dot(jnp.concatenate([p for _, _, p in hots], axis=0), h, preferred_element_type=F32).astype(BF16)
    for j, ((cnt, off, _, gate_row, w), (ranks, hit, _)) in enumerate(zip(experts, hots)):
        expert_rows(xs[j * cap:(j + 1) * cap], ranks, hit, 0, gate_row, w, cnt, off)

    for cnt, off, pos_row, gate_row, w in experts:
        def block(q, carry):
            ranks, hit, p = onehot(pos_row, q)
            x = jnp.dot(p, h, preferred_element_type=F32).astype(BF16)
            expert_rows(x, ranks, hit, q, gate_row, w, cnt, off)
            return carry

        lax.fori_loop(1, (cnt + cap - 1) // cap, block, 0)

    @pl.when(s == N_EXPERTS // EXPERTS_PER_STEP - 1)
    def _():
        group_rows = lax.broadcasted_iota(jnp.int32, (SEG_ALIGN, 1), 0)
        groups_per_block = COMBINE_BLOCK // SEG_ALIGN

        def combine(r, carry):
            base = pl.multiple_of(r * COMBINE_BLOCK, COMBINE_BLOCK)

            def onehot_groups(c, carry):
                for u in range(GROUP_UNROLL):
                    gi = c * GROUP_UNROLL + u
                    g0 = pl.multiple_of(gi * SEG_ALIGN, SEG_ALIGN)
                    e = own_s[i, r * groups_per_block + gi]
                    rank = (group_rows + (base + g0 - off_s[i, e])).astype(F32)
                    hit = pos_ref[0, pl.ds(e, 1), :] == rank
                    p_ref[pl.ds(g0, SEG_ALIGN), :] = jnp.where(hit, 1.0, 0.0).astype(BF16)
                return carry

            lax.fori_loop(0, groups_per_block // GROUP_UNROLL, onehot_groups, 0)
            for n in range(d // ROW_TILE):
                cols = slice(n * ROW_TILE, (n + 1) * ROW_TILE)
                o_ref[0, :, cols] += lax.dot_general(p_ref[...], y_ref[pl.ds(base, COMBINE_BLOCK), cols],
                                                     (((0,), (0,)), ((), ())), preferred_element_type=F32)
            return carry

        lax.fori_loop(0, MOE_SLOTS // COMBINE_BLOCK, combine, 0)

        ctx = mod_ref[bsz:bsz + 1, 5 * d:6 * d]
        if tiles_per_batch is None:
            r = i * rows + lax.broadcasted_iota(jnp.int32, (rows, 1), 0)
            g2 = ctx
            for b in range(bsz):
                lo = b * rows_per_batch
                inside = jnp.where(r >= lo, jnp.where(r < lo + n_lat, 1, 0), 0)
                g2 = jnp.where(inside == 1, mod_ref[b:b + 1, 5 * d:6 * d], g2)
        else:
            g2 = mod_ref[pl.ds(i // tiles_per_batch, 1), 5 * d:6 * d]
        y = x_ref[0] + g2 * o_ref[0]
        if final:
            y = _rms(y) * fn_ref[...]
        o_ref[0] = y


def _moe(h, gates, xs, mod, wg, wu, wd, sg, su, sd, final_norm, layer, n_lat, final):
    bsz, m, d = xs.shape
    t = MOE_TILE
    if final:
        assert n_lat % t == 0
        tiles_per_batch = n_lat // t
        n_tiles = bsz * tiles_per_batch
        where = lambda i: (i // tiles_per_batch, i % tiles_per_batch, 0)
        out_shape = (bsz, n_lat, d)
    else:
        assert (bsz * m) % t == 0
        tiles_per_batch = None
        n_tiles = bsz * m // t
        where = lambda i: (0, i, 0)
        out_shape = (1, bsz * m, d)
        h, gates, xs = (a.reshape(1, bsz * m, a.shape[-1]) for a in (h, gates, xs))
    once = pl.Buffered(1)

    tok = jnp.arange(t)
    tri = (tok[None, :] < tok[:, None]).astype(BF16)
    lane = jnp.arange(LANE)
    upper = (lane[:, None] < lane[None, :]).astype(F32)
    groups = -(-MOE_SLOTS // SEG_ALIGN // LANE) * LANE
    em = jax.ShapeDtypeStruct((n_tiles, LANE, t), F32)
    ints = lambda n: jax.ShapeDtypeStruct((n_tiles, SUBLANE, n), jnp.int32)
    per_tile = lambda a, b: pl.BlockSpec((1, a, b), lambda i: (i, 0, 0))
    pos, gem, cnt, off, own = pl.pallas_call(
        _plan_kernel,
        grid=(n_tiles,),
        in_specs=[pl.BlockSpec((1, t, LANE), lambda i: where(i)),
                  pl.BlockSpec((t, t), lambda i: (0, 0)), pl.BlockSpec((LANE, LANE), lambda i: (0, 0))],
        out_specs=[per_tile(LANE, t), per_tile(LANE, t), per_tile(SUBLANE, LANE), per_tile(SUBLANE, LANE),
                   per_tile(SUBLANE, groups)],
        out_shape=[em, em, ints(LANE), ints(LANE), ints(groups)],
        compiler_params=_params(("parallel",)),
        name="moe_plan",
    )(gates, tri, upper)

    tile = lambda n: pl.BlockSpec((1, t, n), lambda i, s, *_: where(i), pipeline_mode=once)
    planned = lambda a, b: pl.BlockSpec((1, a, b), lambda i, s, *_: (i, 0, 0), pipeline_mode=once)
    whole = lambda a: pl.BlockSpec(a.shape, lambda i, s, *_: (0,) * a.ndim, pipeline_mode=once)
    experts = lambda a: pl.BlockSpec((None, EXPERTS_PER_STEP) + a.shape[2:], lambda i, s, *_: (layer, s, 0, 0))
    out = pl.pallas_call(
        functools.partial(_moe_kernel, bsz=bsz, rows_per_batch=m, n_lat=n_lat,
                          tiles_per_batch=tiles_per_batch, final=final),
        grid_spec=pltpu.PrefetchScalarGridSpec(
            num_scalar_prefetch=3,
            grid=(n_tiles, N_EXPERTS // EXPERTS_PER_STEP),
            in_specs=[tile(d), planned(LANE, t), planned(LANE, t),
                      experts(wg), experts(wu), experts(wd), whole(sg), whole(su), whole(sd),
                      tile(d), whole(mod), whole(final_norm)],
            out_specs=pl.BlockSpec((1, t, d), lambda i, s, *_: where(i)),
            scratch_shapes=[pltpu.VMEM((MOE_SLOTS + MOE_CAP, d), BF16), pltpu.VMEM((COMBINE_BLOCK, t), BF16)]),
        out_shape=jax.ShapeDtypeStruct(out_shape, F32),
        compiler_params=_params(("arbitrary", "arbitrary"), MOE_VMEM_LIMIT),
        name="moe_final" if final else "moe",
    )(cnt[:, 0], off[:, 0], own[:, 0], h, pos, gem, wg, wu, wd, sg, su, sd, xs, mod, final_norm)
    return out if final else out.reshape(bsz, m, d)


def _rope_tables(n_lat, m, head_dim, width):
    pos = jnp.arange(n_lat, dtype=jnp.int32)
    row = (pos // GRID_W).astype(F32)
    col = (pos % GRID_W).astype(F32)
    half = head_dim // 2
    quarter = half // 2
    lane = jnp.arange(head_dim)
    freq = (lane % quarter).astype(F32)
    inv = ROPE_THETA ** (-(2.0 * freq) / half)
    ang = jnp.where(lane[None, :] < half, row[:, None], col[:, None]) * inv[None, :]
    low = (lane % half) < quarter
    cos = jnp.cos(ang)
    sin = jnp.sin(ang)
    slo = jnp.where(low[None, :], -sin, 0.0)
    shi = jnp.where(low[None, :], 0.0, sin)
    reps = width // head_dim
    pad = lambda a, fill: jnp.concatenate(
        [jnp.tile(a, (1, reps)), jnp.full((m - n_lat, width), fill, F32)], axis=0)
    return pad(cos, 1.0), pad(slo, 0.0), pad(shi, 0.0)


def _mla_tables(tabs16):
    outs = []
    for a, fill in zip(tabs16, (1.0, 0.0, 0.0)):
        m = a.shape[0]
        slot = jnp.concatenate([jnp.full((m, MLA_NOPE), fill, F32), a,
                                jnp.full((m, QK_SLOT - MLA_QK), fill, F32)], axis=1)
        outs.append(jnp.tile(slot, (1, MLA_HEADS)))
    return tuple(outs)


def _slots(w, heads, width):
    w = w.reshape(w.shape[0], heads, width)
    pad = jnp.zeros((w.shape[0], heads, QK_SLOT - width), w.dtype)
    return jnp.concatenate([w, pad], axis=-1).reshape(w.shape[0], heads * QK_SLOT)


def _block_diag(w):
    dirs = w.shape[0]
    eye = jnp.eye(LRU_BLOCKS, dtype=w.dtype)
    full = jnp.einsum('dnij,nm->dnimj', w, eye).reshape(dirs, LRU_WIDTH, LRU_WIDTH)
    nl = LRU_WIDTH // LRU_LANES
    return jnp.stack([full[:, j * LRU_LANES:(j + 1) * LRU_LANES, j * LRU_LANES:(j + 1) * LRU_LANES]
                      for j in range(nl)], axis=1)


def kernel(x, c, ctx, c_ctx, mod_w, mod_b, ab_w_in, ab_w_out, diff_lambda_q, diff_lambda_k, diff_subln, lru_conv_w, lru_conv_b, lru_w_a, lru_b_a, lru_w_x, lru_b_x, lru_lambda, cd_w_in, cd_w_out, gqa_q_norm, gqa_k_norm, mla_q_norm, mla_kv_norm, mla_w_uq, mla_w_ukv, router_w, router_bias, exp_w_gate, exp_w_up, exp_w_down, sh_w_gate, sh_w_up, sh_w_down, final_norm):
    bsz, n_lat, d = x.shape
    n_ctx = ctx.shape[1]
    m = n_lat + n_ctx
    depth = mod_w.shape[0]
    assert depth == 2 and bsz < MOD_ROWS
    assert n_lat % ROW_TILE == 0 and n_ctx % ROW_TILE == 0 and n_lat % GRID_W == 0
    ctx_row = bsz

    xs = jnp.concatenate([x, ctx], axis=1)
    c_all = jnp.concatenate([c, c_ctx[None, :], jnp.zeros((MOD_ROWS - bsz - 1, d), F32)], axis=0)
    mods = _modulation(c_all, mod_w, mod_b)

    tabs64 = _rope_tables(n_lat, m, DIFF_HEAD_DIM, DIFF_HEADS * 2 * DIFF_HEAD_DIM)
    tabs_m = _mla_tables(_rope_tables(n_lat, m, MLA_ROPE, MLA_ROPE))

    exp_w = tuple(w.astype(BF16) for w in (exp_w_gate, exp_w_up, exp_w_down))

    def moe_weights(i):
        return exp_w + (sh_w_gate[i], sh_w_up[i], sh_w_down[i])

    fn = final_norm.reshape(1, d)

    mod = mods[0]
    qt, k, vt, u, g = _proj_ab(xs, mod, ab_w_in[0].astype(BF16), tabs64, n_lat, ctx_row)
    lambda_init = 0.8 - 0.6 * math.exp(-0.3 * 0)
    dv = 2 * DIFF_HEAD_DIM
    jobs = tuple(_Job((2 * h + j) * QK_SLOT, h, j, h * dv, dv) for h in range(DIFF_HEADS) for j in range(2))
    o_att = _attention(qt, k, vt, (diff_lambda_q[0], diff_lambda_k[0], diff_subln[0].reshape(-1, 1)),
                       jobs, n_lat, lambda_init, "diff_attention")
    o_rec = _lru(u, g, lru_conv_w[0], lru_conv_b[0].reshape(1, -1),
                 _block_diag(lru_w_a[0]).astype(BF16), lru_b_a[0].reshape(2, 1, LRU_WIDTH),
                 _block_diag(lru_w_x[0]).astype(BF16), lru_b_x[0].reshape(2, 1, LRU_WIDTH),
                 lru_lambda[0].reshape(2, 1, LRU_WIDTH), n_lat)
    xs, h2, gates = _out_proj(o_att, o_rec, xs, mod, ab_w_out[0].astype(BF16),
                              router_w[0].T, router_bias[0].reshape(-1, 1), n_lat, ctx_row)
    xs = _moe(h2, gates, xs, mod, *moe_weights(0), fn, 0, n_lat, False)

    mod = mods[1]
    w_in = cd_w_in[0]
    wq, wk = GQA_HEADS * GQA_HEAD_DIM, GQA_KV_HEADS * GQA_HEAD_DIM
    o_kr = wq + 2 * wk + MLA_Q_RANK + MLA_KV_RANK
    kr_slots = jnp.concatenate([jnp.zeros((d, MLA_NOPE), F32), w_in[:, o_kr:o_kr + MLA_ROPE],
                                jnp.zeros((d, QK_SLOT - MLA_QK), F32)], axis=1)
    w_in = jnp.concatenate([w_in[:, :o_kr], jnp.tile(kr_slots, (1, MLA_HEADS))], axis=1).astype(BF16)
    ukv = mla_w_ukv[0].reshape(MLA_KV_RANK, MLA_HEADS, MLA_NOPE + MLA_V)
    w_kn = _slots(ukv[:, :, :MLA_NOPE].reshape(MLA_KV_RANK, -1), MLA_HEADS, MLA_NOPE).astype(BF16)
    w_v = ukv[:, :, MLA_NOPE:].reshape(MLA_KV_RANK, MLA_HEADS * MLA_V).astype(BF16)
    w_uq = _slots(mla_w_uq[0], MLA_HEADS, MLA_QK).astype(BF16)
    lane = jnp.arange(wq)
    ones_bd = (lane[:, None] // GQA_HEAD_DIM == lane[None, :] // GQA_HEAD_DIM).astype(BF16)
    qgt, kg, vgt, qmt, km, vmt = _proj_cd(
        xs, mod, w_in, ones_bd, jnp.tile(gqa_q_norm[0], GQA_HEADS).reshape(1, -1),
        jnp.tile(gqa_k_norm[0], GQA_KV_HEADS).reshape(1, -1), mla_q_norm[0].reshape(1, -1),
        mla_kv_norm[0].reshape(1, -1), w_uq, w_kn, w_v, tabs64, tabs_m, n_lat, ctx_row)
    groups = GQA_HEADS // GQA_KV_HEADS
    jobs = tuple(_Job(h * QK_SLOT, 0, h // groups, (h // groups) * GQA_HEAD_DIM, GQA_HEAD_DIM)
                 for h in range(GQA_HEADS))
    o_gqa = _attention(qgt, kg, vgt, (), jobs, n_lat, None, "gqa_attention")
    jobs = tuple(_Job(h * QK_SLOT, h // 2, h % 2, h * MLA_V, MLA_V) for h in range(MLA_HEADS))
    o_mla = _attention(qmt, km, vmt, (), jobs, n_lat, None, "mla_attention")
    xs, h2, gates = _out_proj(o_gqa, o_mla, xs, mod, cd_w_out[0].astype(BF16),
                              router_w[1].T, router_bias[1].reshape(-1, 1), n_lat, ctx_row)
    return _moe(h2, gates, xs, mod, *moe_weights(1), fn, 1, n_lat, True)
```

```python
import functools
import math
from typing import NamedTuple

import jax
import jax.numpy as jnp
from jax import lax
from jax.experimental import pallas as pl
from jax.experimental.pallas import tpu as pltpu

F32 = jnp.float32
BF16 = jnp.bfloat16

GRID_W = 64
ROPE_THETA = 10000.0
EPS = 1e-6
DIFF_HEADS = 4
DIFF_HEAD_DIM = 64
LRU_WIDTH = 512
LRU_BLOCKS = 8
LRU_BW = LRU_WIDTH // LRU_BLOCKS
CONV_W = 4
LRU_C = 8.0
GQA_HEADS = 8
GQA_KV_HEADS = 2
GQA_HEAD_DIM = 64
MLA_HEADS = 8
MLA_Q_RANK = 256
MLA_KV_RANK = 128
MLA_NOPE = 32
MLA_ROPE = 16
MLA_V = 64
MLA_QK = MLA_NOPE + MLA_ROPE
N_EXPERTS = 64
N_GROUPS = 8
PER_GROUP = N_EXPERTS // N_GROUPS
TOPK_GROUPS = 4
TOP_K = 8
EXPERT_FF = 256
ROUTED_SCALE = 2.5

QK_SLOT = 64
NORM_ROWS = 16
LOG2E = math.log2(math.e)

ROW_TILE = 256
LRU_CHUNK = 128
LRU_LANES = 256
EXPERTS_PER_STEP = 4
LANE = 128
SUBLANE = 8
MOD_ROWS = 8
VMEM_LIMIT = 56 * 1024 * 1024
MOE_VMEM_LIMIT = 62 * 1024 * 1024
MOE_TILE = 1024
MOE_CAP = 160
SEG_ALIGN = 16
MOE_SLOTS = MOE_TILE * TOP_K + N_EXPERTS * SEG_ALIGN
COMBINE_BLOCK = 1024
GROUP_UNROLL = 4


def _params(sem, vmem=VMEM_LIMIT):
    return pltpu.CompilerParams(dimension_semantics=sem, vmem_limit_bytes=vmem)


def _silu(x):
    return x * jax.nn.sigmoid(x)


def _rms(x):
    return x * lax.rsqrt(jnp.mean(x * x, axis=-1, keepdims=True) + EPS)


def _mod_rows(mod_ref, ctx_row, b, k, d, row0, rows, n_lat):
    lat = mod_ref[pl.ds(b, 1), k * d:(k + 1) * d]
    ctx = mod_ref[ctx_row:ctx_row + 1, k * d:(k + 1) * d]
    if n_lat % rows == 0:
        return jnp.where(row0 >= n_lat, ctx, lat)
    r = row0 + lax.broadcasted_iota(jnp.int32, (rows, 1), 0)
    return jnp.where(r >= n_lat, ctx, lat)


def _rope(x, cos, sin_lo, sin_hi, half):
    n = x.shape[-1]
    return x * cos + pltpu.roll(x, n - half, 1) * sin_lo + pltpu.roll(x, half, 1) * sin_hi


def _mod_kernel(c_ref, w_ref, b_ref, o_ref):
    s = _silu(c_ref[...])
    o_ref[0] = jnp.dot(s, w_ref[0], preferred_element_type=F32, precision=lax.Precision.HIGHEST) + b_ref[0]


def _modulation(c_all, mod_w, mod_b):
    depth, d, n = mod_w.shape
    tn = n // 4
    return pl.pallas_call(
        _mod_kernel,
        grid=(depth, n // tn),
        in_specs=[pl.BlockSpec((MOD_ROWS, d), lambda i, j: (0, 0)),
                  pl.BlockSpec((1, d, tn), lambda i, j: (i, 0, j)),
                  pl.BlockSpec((1, 1, tn), lambda i, j: (i, 0, j))],
        out_specs=pl.BlockSpec((1, MOD_ROWS, tn), lambda i, j: (i, 0, j)),
        out_shape=jax.ShapeDtypeStruct((depth, MOD_ROWS, n), F32),
        compiler_params=_params(("arbitrary", "arbitrary")),
        name="modulation",
    )(c_all, mod_w, mod_b.reshape(depth, 1, n))


def _proj_ab_kernel(x_ref, mod_ref, w_ref, cos_ref, slo_ref, shi_ref,
                    qt_ref, k_ref, vt_ref, u_ref, g_ref, *, n_lat, ctx_row):
    b, t = pl.program_id(0), pl.program_id(1)
    d = x_ref.shape[-1]
    row0 = t * ROW_TILE
    shift = _mod_rows(mod_ref, ctx_row, b, 0, d, row0, ROW_TILE, n_lat)
    scale = _mod_rows(mod_ref, ctx_row, b, 1, d, row0, ROW_TILE, n_lat)
    h = (_rms(x_ref[0]) * (1.0 + scale) + shift).astype(BF16)
    acc = jnp.dot(h, w_ref[...], preferred_element_type=F32)
    w = DIFF_HEADS * 2 * DIFF_HEAD_DIM
    cos, slo, shi = cos_ref[...], slo_ref[...], shi_ref[...]
    q = _rope(acc[:, 0:w], cos, slo, shi, DIFF_HEAD_DIM // 4) * (DIFF_HEAD_DIM ** -0.5 * LOG2E)
    k = _rope(acc[:, w:2 * w], cos, slo, shi, DIFF_HEAD_DIM // 4)
    qt_ref[0, 0] = q.T.astype(BF16)
    k_ref[0] = k.astype(BF16)
    vt_ref[0] = acc[:, 2 * w:3 * w].T.astype(BF16)
    u_ref[0] = acc[:, 3 * w:3 * w + LRU_WIDTH]
    g_ref[0] = jax.nn.gelu(acc[:, 3 * w + LRU_WIDTH:3 * w + 2 * LRU_WIDTH]).astype(BF16)


def _proj_ab(xs, mod, w_in, tabs, n_lat, ctx_row):
    bsz, m, d = xs.shape
    nt = m // ROW_TILE
    w = DIFF_HEADS * 2 * DIFF_HEAD_DIM
    n_in = w_in.shape[1]
    tile = lambda n: pl.BlockSpec((1, ROW_TILE, n), lambda b, t: (b, t, 0))
    ttile = pl.BlockSpec((1, 1, w, ROW_TILE), lambda b, t: (b, t, 0, 0))
    vtile = pl.BlockSpec((1, w, ROW_TILE), lambda b, t: (b, 0, t))
    tab = pl.BlockSpec((ROW_TILE, w), lambda b, t: (t, 0))
    return pl.pallas_call(
        functools.partial(_proj_ab_kernel, n_lat=n_lat, ctx_row=ctx_row),
        grid=(bsz, nt),
        in_specs=[tile(d),
                  pl.BlockSpec(mod.shape, lambda b, t: (0, 0)),
                  pl.BlockSpec((d, n_in), lambda b, t: (0, 0)),
                  tab, tab, tab],
        out_specs=[ttile, tile(w), vtile, tile(LRU_WIDTH), tile(LRU_WIDTH)],
        out_shape=[jax.ShapeDtypeStruct((bsz, nt, w, ROW_TILE), BF16),
                   jax.ShapeDtypeStruct((bsz, m, w), BF16),
                   jax.ShapeDtypeStruct((bsz, w, m), BF16),
                   jax.ShapeDtypeStruct((bsz, m, LRU_WIDTH), F32),
                   jax.ShapeDtypeStruct((bsz, m, LRU_WIDTH), BF16)],
        compiler_params=_params(("parallel", "parallel")),
        name="proj_ab",
    )(xs, mod, w_in, *tabs)


def _group_rms(x, ones_bd):
    x2 = x * x
    hi = x2.astype(BF16)
    lo = (x2 - hi.astype(F32)).astype(BF16)
    ss = jnp.dot(hi, ones_bd, preferred_element_type=F32) + jnp.dot(lo, ones_bd, preferred_element_type=F32)
    return x * lax.rsqrt(ss * (1.0 / GQA_HEAD_DIM) + EPS)


def _proj_cd_kernel(x_ref, mod_ref, w_ref, bd_ref, qn_ref, kn_ref, cqn_ref, ckvn_ref, wuq_ref, wkn_ref, wv_ref,
                    cos_ref, slo_ref, shi_ref, cosm_ref, slom_ref, shim_ref,
                    qgt_ref, kg_ref, vgt_ref, qmt_ref, km_ref, vmt_ref, *, n_lat, ctx_row):
    b, t = pl.program_id(0), pl.program_id(1)
    d = x_ref.shape[-1]
    row0 = t * ROW_TILE
    shift = _mod_rows(mod_ref, ctx_row, b, 0, d, row0, ROW_TILE, n_lat)
    scale = _mod_rows(mod_ref, ctx_row, b, 1, d, row0, ROW_TILE, n_lat)
    h = (_rms(x_ref[0]) * (1.0 + scale) + shift).astype(BF16)
    acc = jnp.dot(h, w_ref[...], preferred_element_type=F32)
    wq = GQA_HEADS * GQA_HEAD_DIM
    wk = GQA_KV_HEADS * GQA_HEAD_DIM
    o = 0
    q = acc[:, o:o + wq]; o += wq
    k = acc[:, o:o + wk]; o += wk
    v = acc[:, o:o + wk]; o += wk
    cq = acc[:, o:o + MLA_Q_RANK]; o += MLA_Q_RANK
    ckv = acc[:, o:o + MLA_KV_RANK]; o += MLA_KV_RANK
    kr = acc[:, o:o + MLA_HEADS * QK_SLOT]
    cos, slo, shi = cos_ref[...], slo_ref[...], shi_ref[...]
    bd = bd_ref[...]
    q = (_rope(_group_rms(q, bd) * qn_ref[...], cos, slo, shi, GQA_HEAD_DIM // 4)
         * (GQA_HEAD_DIM ** -0.5 * LOG2E))
    k = _rope(_group_rms(k, bd[:wk, :wk]) * kn_ref[...], cos[:, :wk], slo[:, :wk], shi[:, :wk], GQA_HEAD_DIM // 4)
    qgt_ref[0, 0] = q.T.astype(BF16)
    kg_ref[0] = k.astype(BF16)
    vt = v.T
    ones = jnp.ones((NORM_ROWS, ROW_TILE), F32)
    vgt_ref[0] = jnp.concatenate(
        [piece for kv in range(GQA_KV_HEADS) for piece in (vt[kv * GQA_HEAD_DIM:(kv + 1) * GQA_HEAD_DIM], ones)],
        axis=0).astype(BF16)
    cosm, slom, shim = cosm_ref[...], slom_ref[...], shim_ref[...]
    cqn = (_rms(cq) * cqn_ref[...]).astype(BF16)
    mq = jnp.dot(cqn, wuq_ref[...], preferred_element_type=F32)
    qmt_ref[0, 0] = (_rope(mq, cosm, slom, shim, MLA_ROPE // 4) * (MLA_QK ** -0.5 * LOG2E)).T.astype(BF16)
    ckvn = (_rms(ckv) * ckvn_ref[...]).astype(BF16)
    km = jnp.dot(ckvn, wkn_ref[...], preferred_element_type=F32) + _rope(kr, cosm, slom, shim, MLA_ROPE // 4)
    km_ref[0] = km.astype(BF16)
    vmt_ref[0] = jnp.dot(ckvn, wv_ref[...], preferred_element_type=F32).T.astype(BF16)


def _proj_cd(xs, mod, w_in, bd, qn, kn, cqn, ckvn, wuq, wkn, wv, tabs, tabs_m, n_lat, ctx_row):
    bsz, m, d = xs.shape
    nt = m // ROW_TILE
    wq = GQA_HEADS * GQA_HEAD_DIM
    wk = GQA_KV_HEADS * GQA_HEAD_DIM
    wm = MLA_HEADS * QK_SLOT
    wmv = MLA_HEADS * MLA_V
    wgv = GQA_KV_HEADS * (GQA_HEAD_DIM + NORM_ROWS)
    tile = lambda n: pl.BlockSpec((1, ROW_TILE, n), lambda b, t: (b, t, 0))
    ttile = lambda n: pl.BlockSpec((1, 1, n, ROW_TILE), lambda b, t: (b, t, 0, 0))
    vtile = lambda n: pl.BlockSpec((1, n, ROW_TILE), lambda b, t: (b, 0, t))
    whole = lambda a: pl.BlockSpec(a.shape, lambda b, t: (0,) * a.ndim)
    tab = pl.BlockSpec((ROW_TILE, wq), lambda b, t: (t, 0))
    tabm = pl.BlockSpec((ROW_TILE, wm), lambda b, t: (t, 0))
    return pl.pallas_call(
        functools.partial(_proj_cd_kernel, n_lat=n_lat, ctx_row=ctx_row),
        grid=(bsz, nt),
        in_specs=[tile(d), whole(mod), whole(w_in), whole(bd), whole(qn), whole(kn), whole(cqn), whole(ckvn),
                  whole(wuq), whole(wkn), whole(wv), tab, tab, tab, tabm, tabm, tabm],
        out_specs=[ttile(wq), tile(wk), vtile(wgv), ttile(wm), tile(wm), vtile(wmv)],
        out_shape=[jax.ShapeDtypeStruct((bsz, nt, wq, ROW_TILE), BF16),
                   jax.ShapeDtypeStruct((bsz, m, wk), BF16),
                   jax.ShapeDtypeStruct((bsz, wgv, m), BF16),
                   jax.ShapeDtypeStruct((bsz, nt, wm, ROW_TILE), BF16),
                   jax.ShapeDtypeStruct((bsz, m, wm), BF16),
                   jax.ShapeDtypeStruct((bsz, wmv, m), BF16)],
        compiler_params=_params(("parallel", "parallel")),
        name="proj_cd",
    )(xs, mod, w_in, bd, qn, kn, cqn, ckvn, wuq, wkn, wv, *tabs, *tabs_m)


class _Job(NamedTuple):
    q_row: int
    k_group: int
    k_half: int
    v_row: int
    dv: int
    ones: int = 0


def _attend(qt_ref, qnext_ref, k_ref, vt_ref, s_ref, mp_ref, jobs, lo, hi, chained):
    assert len(jobs) % 2 == 0
    tq = qt_ref.shape[-1]
    n = hi - lo
    group = next(g for g in (11, 3, 2, 1) if n % g == 0)
    steps = n // group
    row_groups = ROW_TILE // SUBLANE

    def weights(job, ref=qt_ref):
        qh = ref[0, 0, job.q_row:job.q_row + QK_SLOT, :]
        z = jnp.zeros_like(qh)
        return jnp.concatenate([z, qh] if job.k_half else [qh, z], axis=0)

    def score(c0, buf, qw, job, mp):
        r0 = pl.multiple_of(c0 * ROW_TILE, ROW_TILE)
        kc = k_ref[0, pl.ds(r0, group * ROW_TILE), job.k_group * LANE:(job.k_group + 1) * LANE]
        s_all = jnp.dot(kc, qw, preferred_element_type=F32)
        for g in range(group):
            s = s_all[g * ROW_TILE:(g + 1) * ROW_TILE]
            s_ref[buf, c0 + g] = s
            mp = jnp.maximum(mp, jnp.max(s.reshape(row_groups, SUBLANE, tq), axis=0))
        return mp

    def value(c0, buf, m, job, lp, acc):
        ps = []
        for g in range(group):
            p = jnp.exp2(s_ref[buf, c0 + g] - m)
            if not job.ones:
                lp = lp + jnp.sum(p.reshape(row_groups, SUBLANE, tq), axis=0)
            ps.append(p.astype(BF16))
        k0 = pl.multiple_of(c0 * ROW_TILE, ROW_TILE)
        v = vt_ref[0, job.v_row:job.v_row + job.dv, pl.ds(k0, group * ROW_TILE)]
        return lp, acc + jnp.dot(v, jnp.concatenate(ps, axis=0), preferred_element_type=F32)

    def loop(body, init):
        return body(0, init) if steps == 1 else lax.fori_loop(0, steps, body, init)

    neg = jnp.full((SUBLANE, tq), -jnp.inf, F32)

    def prologue():
        qw = weights(jobs[0])

        def first(i, mp):
            return score(lo + i * group, 0, qw, jobs[0], mp)

        mp_ref[...] = loop(first, neg)

    if chained:
        pl.when(pl.program_id(1) == 0)(prologue)
    else:
        prologue()
    mp = mp_ref[...]
    results = []
    for j, job in enumerate(jobs):
        buf = j % 2
        m = jnp.max(mp, axis=0, keepdims=True)
        if j + 1 < len(jobs):
            nxt, qw = jobs[j + 1], weights(jobs[j + 1])
        elif chained:
            nxt, qw = jobs[0], weights(jobs[0], qnext_ref)
        else:
            nxt = None

        def body(i, carry):
            mp, lp, acc = carry
            if nxt is not None:
                mp = score(lo + i * group, 1 - buf, qw, nxt, mp)
            lp, acc = value(lo + i * group, buf, m, job, lp, acc)
            return mp, lp, acc

        mp, lp, acc = loop(body, (neg, jnp.zeros((SUBLANE, tq), F32), jnp.zeros((job.dv, tq), F32)))
        if job.ones:
            results.append((acc[:job.dv - job.ones], acc[job.dv - job.ones:job.dv - job.ones + 1]))
        else:
            results.append((acc, jnp.sum(lp, axis=0, keepdims=True)))
    if chained:
        mp_ref[...] = mp
    return results


def _attn_kernel(*refs, jobs, n_lat, lambda_init):
    if lambda_init is None:
        qt_ref, qnext_ref, k_ref, vt_ref, o_ref, s_ref, mp_ref = refs
    else:
        qt_ref, qnext_ref, k_ref, vt_ref, lq_ref, lk_ref, sub_ref, o_ref, s_ref, mp_ref = refs
    n_chunks = k_ref.shape[1] // ROW_TILE
    first_ctx = n_lat // ROW_TILE
    t = pl.program_id(1)

    def run(lo, hi, chained):
        res = _attend(qt_ref, qnext_ref, k_ref, vt_ref, s_ref, mp_ref, jobs, lo, hi, chained)
        if lambda_init is None:
            outs = [acc / l for acc, l in res]
        else:
            lq, lk = lq_ref[...], lk_ref[...]
            lam = (jnp.exp(jnp.sum(lq[0:1] * lk[0:1], axis=-1, keepdims=True))
                   - jnp.exp(jnp.sum(lq[1:2] * lk[1:2], axis=-1, keepdims=True)) + lambda_init)
            outs = []
            for h in range(len(res) // 2):
                (a0, l0), (a1, l1) = res[2 * h], res[2 * h + 1]
                o = a0 / l0 - lam * (a1 / l1)
                o = o * lax.rsqrt(jnp.mean(o * o, axis=0, keepdims=True) + EPS)
                outs.append(o * sub_ref[...] * (1.0 - lambda_init))
        o_ref[0] = jnp.concatenate(outs, axis=0).T.astype(BF16)

    @pl.when(t < first_ctx)
    def _():
        run(0, n_chunks, True)

    @pl.when(t >= first_ctx)
    def _():
        run(first_ctx, n_chunks, False)


def _attention(qt, k, vt, extra, jobs, n_lat, lambda_init, name):
    bsz, nt, wq, tq = qt.shape
    m = k.shape[1]
    wo = sum(j.dv - j.ones for j in jobs) // (1 if lambda_init is None else 2)
    once = pl.Buffered(1)
    whole = lambda a: pl.BlockSpec(a.shape, lambda b, t: (0,) * a.ndim)
    return pl.pallas_call(
        functools.partial(_attn_kernel, jobs=jobs, n_lat=n_lat, lambda_init=lambda_init),
        grid=(bsz, nt),
        in_specs=[pl.BlockSpec((1, 1, wq, tq), lambda b, t: (b, t, 0, 0)),
                  pl.BlockSpec((1, 1, wq, tq), lambda b, t: (b, jnp.minimum(t + 1, nt - 1), 0, 0)),
                  pl.BlockSpec((1,) + k.shape[1:], lambda b, t: (b, 0, 0), pipeline_mode=once),
                  pl.BlockSpec((1,) + vt.shape[1:], lambda b, t: (b, 0, 0), pipeline_mode=once)]
                 + [whole(a) for a in extra],
        out_specs=pl.BlockSpec((1, tq, wo), lambda b, t: (b, t, 0)),
        out_shape=jax.ShapeDtypeStruct((bsz, m, wo), BF16),
        scratch_shapes=[pltpu.VMEM((2, nt, ROW_TILE, tq), F32), pltpu.VMEM((SUBLANE, tq), F32)],
        compiler_params=_params(("arbitrary", "arbitrary")),
        name=name,
    )(qt, qt, k, vt, *extra)


def _scan_chunk(a, b, carry, reverse):
    n = a.shape[0]
    rows = lax.broadcasted_iota(jnp.int32, (n, 1), 0)
    d = 1
    while d < n:
        sh = n - d if reverse else d
        valid = rows < n - d if reverse else rows >= d
        b = jnp.where(valid, a * pltpu.roll(b, sh, 0) + b, b)
        a = jnp.where(valid, a * pltpu.roll(a, sh, 0), a)
        d *= 2
    h = a * carry + b
    return h, (h[0:1] if reverse else h[n - 1:n])


def _lru_kernel(u_ref, g_ref, cw_ref, cb_ref, wa_ref, ba_ref, wx_ref, bx_ref, lam_ref, o_ref, up_ref, hf_ref, *, n_lat):
    m = u_ref.shape[1]
    lanes = u_ref.shape[2]
    n_chunks = m // LRU_CHUNK
    n_ctx_chunks = (m - n_lat) // LRU_CHUNK
    pad = SUBLANE

    up_ref[0:pad, :] = jnp.zeros((pad, lanes), F32)
    up_ref[pad + m:pad + m + pad, :] = jnp.zeros((pad, lanes), F32)

    def copy(c, _):
        r0 = pl.multiple_of(c * LRU_CHUNK, LRU_CHUNK)
        up_ref[pl.ds(pad + r0, LRU_CHUNK), :] = u_ref[0, pl.ds(r0, LRU_CHUNK), :]
        return 0

    lax.fori_loop(0, n_chunks, copy, 0)

    cw = cw_ref[...]
    cb = cb_ref[...]
    win_rows = LRU_CHUNK + 2 * pad

    def coeffs(c, d):
        r0 = pl.multiple_of(c * LRU_CHUNK, LRU_CHUNK)
        win = up_ref[pl.ds(r0, win_rows), :]
        r = r0 + lax.broadcasted_iota(jnp.int32, (LRU_CHUNK, 1), 0)
        at = lambda k: pltpu.roll(win, (win_rows - k) % win_rows, 0)[pad:pad + LRU_CHUNK]
        y = (cw[0:1] * jnp.where((r == n_lat) | (r == n_lat + 1), 0.0, at(-2))
             + cw[1:2] * jnp.where(r == n_lat, 0.0, at(-1))
             + cw[2:3] * win[pad:pad + LRU_CHUNK]
             + cw[3:4] * jnp.where(r == n_lat - 1, 0.0, at(1))
             + cb)
        yb = y.astype(BF16)
        ra = jax.nn.sigmoid(jnp.dot(yb, wa_ref[d, 0], preferred_element_type=F32) + ba_ref[d])
        ix = jax.nn.sigmoid(jnp.dot(yb, wx_ref[d, 0], preferred_element_type=F32) + bx_ref[d])
        z = -lam_ref[d]
        softplus = jnp.maximum(z, 0.0) + jnp.log(1.0 + jnp.exp(-jnp.abs(z)))
        a = jnp.exp(-LRU_C * ra * softplus)
        return a, jnp.sqrt(1.0 - a * a) * ix * y

    def fwd(s, carry):
        c = lax.rem(s + (n_chunks - n_ctx_chunks), n_chunks)
        a, b = coeffs(c, 0)
        h, carry = _scan_chunk(a, b, carry, False)
        r0 = pl.multiple_of(c * LRU_CHUNK, LRU_CHUNK)
        hf_ref[pl.ds(r0, LRU_CHUNK), :] = h
        return carry

    lax.fori_loop(0, n_chunks, fwd, jnp.zeros((1, lanes), F32))

    def bwd(s, carry):
        c = n_chunks - 1 - s
        a, b = coeffs(c, 1)
        h, carry = _scan_chunk(a, b, carry, True)
        r0 = pl.multiple_of(c * LRU_CHUNK, LRU_CHUNK)
        tot = hf_ref[pl.ds(r0, LRU_CHUNK), :] + h
        o_ref[0, pl.ds(r0, LRU_CHUNK), :] = (tot * g_ref[0, pl.ds(r0, LRU_CHUNK), :].astype(F32)).astype(BF16)
        return carry

    lax.fori_loop(0, n_chunks, bwd, jnp.zeros((1, lanes), F32))


def _lru(u, g, conv_w, conv_b, wa_bd, ba, wx_bd, bx, lam, n_lat):
    bsz, m, c = u.shape
    nl = c // LRU_LANES
    seq = lambda: pl.BlockSpec((1, m, LRU_LANES), lambda b, j: (b, 0, j))
    vec = lambda a: pl.BlockSpec(a.shape[:-1] + (LRU_LANES,), lambda b, j: (0,) * (a.ndim - 1) + (j,))
    mat = pl.BlockSpec((2, 1, LRU_LANES, LRU_LANES), lambda b, j: (0, j, 0, 0))
    return pl.pallas_call(
        functools.partial(_lru_kernel, n_lat=n_lat),
        grid=(bsz, nl),
        in_specs=[seq(), seq(), vec(conv_w), vec(conv_b), mat, vec(ba), mat, vec(bx), vec(lam)],
        out_specs=seq(),
        out_shape=jax.ShapeDtypeStruct((bsz, m, c), BF16),
        scratch_shapes=[pltpu.VMEM((m + 2 * SUBLANE, LRU_LANES), F32), pltpu.VMEM((m, LRU_LANES), F32)],
        compiler_params=_params(("parallel", "parallel")),
        name="rglru",
    )(u, g, conv_w, conv_b, wa_bd, ba, wx_bd, bx, lam)


def _route(logits_t, bias):
    n = logits_t.shape[-1]
    scores = jax.nn.sigmoid(logits_t)
    choice = scores + bias
    sub = lax.broadcasted_iota(jnp.int32, (PER_GROUP, n), 0)
    neg = -jnp.inf
    groups, gs = [], []
    for g in range(N_GROUPS):
        cg = choice[g * PER_GROUP:(g + 1) * PER_GROUP]
        m1 = jnp.max(cg, axis=0, keepdims=True)
        i1 = jnp.min(jnp.where(cg == m1, sub, PER_GROUP), axis=0, keepdims=True)
        m2 = jnp.max(jnp.where(sub == i1, neg, cg), axis=0, keepdims=True)
        groups.append(cg)
        gs.append(m1 + m2)
    masked = []
    for g in range(N_GROUPS):
        rank = jnp.zeros((1, n), jnp.int32)
        for o in range(N_GROUPS):
            if o != g:
                ahead = (gs[o] >= gs[g]) if o < g else (gs[o] > gs[g])
                rank = rank + jnp.where(ahead, 1, 0)
        masked.append(jnp.where(rank < TOPK_GROUPS, groups[g], neg))
    masked = jnp.concatenate(masked, axis=0)
    eidx = lax.broadcasted_iota(jnp.int32, (N_EXPERTS, n), 0)
    picked = jnp.zeros((N_EXPERTS, n), F32)
    for _ in range(TOP_K):
        mx = jnp.max(masked, axis=0, keepdims=True)
        first = jnp.min(jnp.where(masked == mx, eidx, N_EXPERTS), axis=0, keepdims=True)
        hit = eidx == first
        picked = jnp.where(hit, 1.0, picked)
        masked = jnp.where(hit, neg, masked)
    w = picked * scores
    return w / jnp.sum(w, axis=0, keepdims=True) * ROUTED_SCALE


def _out_proj_kernel(oa_ref, ob_ref, x_ref, mod_ref, wa_ref, wb_ref, rwh_ref, rwl_ref, rb_ref,
                     xo_ref, h_ref, gates_ref, *, n_lat, ctx_row):
    b, t = pl.program_id(0), pl.program_id(1)
    d = x_ref.shape[-1]
    row0 = t * ROW_TILE
    mv = lambda k: _mod_rows(mod_ref, ctx_row, b, k, d, row0, ROW_TILE, n_lat)
    y = (jnp.dot(oa_ref[0], wa_ref[...], preferred_element_type=F32)
         + jnp.dot(ob_ref[0], wb_ref[...], preferred_element_type=F32))
    x = x_ref[0] + mv(2) * y
    xo_ref[0] = x
    h = _rms(x) * (1.0 + mv(4)) + mv(3)
    hb = h.astype(BF16)
    h_ref[0] = hb
    h_lo = (h - hb.astype(F32)).astype(BF16)
    nt = lambda a, b: lax.dot_general(a, b, (((1,), (1,)), ((), ())), preferred_element_type=F32)
    logits_t = nt(rwh_ref[...], hb) + (nt(rwh_ref[...], h_lo) + nt(rwl_ref[...], hb))
    gates_t = _route(logits_t, rb_ref[...])
    gates_t = jnp.concatenate([gates_t, jnp.zeros((LANE - N_EXPERTS, ROW_TILE), F32)], axis=0)
    gates_ref[0] = gates_t.T


def _out_proj(oa, ob, xs, mod, w_out, router_wt, router_b, n_lat, ctx_row):
    bsz, m, d = xs.shape
    na, nb = oa.shape[-1], ob.shape[-1]
    wa, wb = w_out[:na], w_out[na:]
    rw_hi = router_wt.astype(BF16)
    rw_lo = (router_wt - rw_hi.astype(F32)).astype(BF16)
    tile = lambda n: pl.BlockSpec((1, ROW_TILE, n), lambda b, t: (b, t, 0))
    whole = lambda a: pl.BlockSpec(a.shape, lambda b, t: (0,) * a.ndim)
    return pl.pallas_call(
        functools.partial(_out_proj_kernel, n_lat=n_lat, ctx_row=ctx_row),
        grid=(bsz, m // ROW_TILE),
        in_specs=[tile(na), tile(nb), tile(d), whole(mod), whole(wa), whole(wb), whole(rw_hi), whole(rw_lo),
                  whole(router_b)],
        out_specs=[tile(d), tile(d), tile(LANE)],
        out_shape=[jax.ShapeDtypeStruct((bsz, m, d), F32),
                   jax.ShapeDtypeStruct((bsz, m, d), BF16),
                   jax.ShapeDtypeStruct((bsz, m, LANE), F32)],
        compiler_params=_params(("parallel", "parallel")),
        name="out_proj_router",
    )(oa, ob, xs, mod, wa, wb, rw_hi, rw_lo, router_b)


def _plan_kernel(gates_ref, tri_ref, upper_ref, pos_ref, gem_ref, cnt_ref, off_ref, own_ref):
    g = gates_ref[0]
    sel = g != 0.0
    one = jnp.where(sel, 1.0, 0.0)
    pos = jnp.dot(tri_ref[...], one.astype(BF16), preferred_element_type=F32)
    cnt = jnp.sum(one, axis=0, keepdims=True)
    cpad = jnp.floor((cnt + (SEG_ALIGN - 1)) * (1.0 / SEG_ALIGN)) * SEG_ALIGN
    off = jnp.dot(jnp.broadcast_to(cpad, (SUBLANE, LANE)), upper_ref[...], preferred_element_type=F32,
                  precision=lax.Precision.HIGHEST)[0:1]
    pos_ref[0] = jnp.where(sel, pos, -1.0).T
    gem_ref[0] = g.T
    cnt_ref[0] = jnp.broadcast_to(cnt, (SUBLANE, LANE)).astype(jnp.int32)
    off_ref[0] = jnp.broadcast_to(off, (SUBLANE, LANE)).astype(jnp.int32)
    groups = own_ref.shape[-1]
    ends = jnp.broadcast_to(off + cpad, (LANE, LANE)).T
    ends = jnp.concatenate([ends] * (groups // LANE), axis=1)
    start = lax.broadcasted_iota(jnp.int32, (LANE, groups), 1).astype(F32) * SEG_ALIGN
    real = lax.broadcasted_iota(jnp.int32, (LANE, groups), 0) < N_EXPERTS
    before = jnp.where(real, jnp.where(ends <= start, 1.0, 0.0), 0.0)
    owner = jnp.minimum(jnp.sum(before, axis=0, keepdims=True), N_EXPERTS - 1.0)
    own_ref[0] = jnp.broadcast_to(owner, (SUBLANE, groups)).astype(jnp.int32)


def _moe_kernel(cnt_s, off_s, own_s, h_ref, pos_ref, gem_ref, wg_ref, wu_ref, wd_ref,
                sg_ref, su_ref, sd_ref, x_ref, mod_ref, fn_ref, o_ref, y_ref, p_ref,
                *, bsz, rows_per_batch, n_lat, tiles_per_batch, final):
    i, s = pl.program_id(0), pl.program_id(1)
    rows, d = h_ref.shape[1], h_ref.shape[2]
    h = h_ref[0]

    def ffn(x, wg, wu, wd, gate):
        a = _silu(jnp.dot(x, wg, preferred_element_type=F32)) * jnp.dot(x, wu, preferred_element_type=F32)
        if gate is not None:
            a = a * gate
        return jnp.dot(a.astype(BF16), wd, preferred_element_type=F32)

    @pl.when((s == 0) & (i == 0))
    def _():
        y_ref[...] = jnp.zeros(y_ref.shape, BF16)

    @pl.when(s == 0)
    def _():
        o_ref[0] = ffn(h, sg_ref[...].astype(BF16), su_ref[...].astype(BF16), sd_ref[...].astype(BF16), None)

    cap = MOE_CAP
    cap_rows = lax.broadcasted_iota(jnp.int32, (cap, 1), 0)

    def onehot(pos_row, q):
        ranks = cap_rows + q * cap
        hit = pos_row == ranks.astype(F32)
        return ranks, hit, jnp.where(hit, 1.0, 0.0).astype(BF16)

    def expert_rows(x, ranks, hit, q, gate_row, w, cnt, off):
        gate = jnp.sum(jnp.where(hit, gate_row, 0.0), axis=1, keepdims=True)
        y = ffn(x, *w, gate)
        dst = pl.multiple_of(off + q * cap, SEG_ALIGN)
        y_ref[pl.ds(dst, cap), :] = jnp.where(ranks < cnt, y.astype(BF16), y_ref[pl.ds(dst, cap), :])

    experts = []
    for j in range(EXPERTS_PER_STEP):
        e = s * EXPERTS_PER_STEP + j
        experts.append((cnt_s[i, e], off_s[i, e],
                        pos_ref[0, pl.ds(e, 1), :],
                        gem_ref[0, pl.ds(e, 1), :],
                        (wg_ref[j], wu_ref[j], wd_ref[j])))

    hots = [onehot(pos_row, 0) for _, _, pos_row, _, _ in experts]
    xs = jnp.dot(jnp.concatenate([p for _, _, p in hots], axis=0), h, preferred_element_type=F32).astype(BF16)
    for j, ((cnt, off, _, gate_row, w), (ranks, hit, _)) in enumerate(zip(experts, hots)):
        expert_rows(xs[j * cap:(j + 1) * cap], ranks, hit, 0, gate_row, w, cnt, off)

    for cnt, off, pos_row, gate_row, w in experts:
        def block(q, carry):
            ranks, hit, p = onehot(pos_row, q)
            x = jnp.dot(p, h, preferred_element_type=F32).astype(BF16)
            expert_rows(x, ranks, hit, q, gate_row, w, cnt, off)
            return carry

        lax.fori_loop(1, (cnt + cap - 1) // cap, block, 0)

    @pl.when(s == N_EXPERTS // EXPERTS_PER_STEP - 1)
    def _():
        group_rows = lax.broadcasted_iota(jnp.int32, (SEG_ALIGN, 1), 0)
        groups_per_block = COMBINE_BLOCK // SEG_ALIGN

        def combine(r, carry):
            base = pl.multiple_of(r * COMBINE_BLOCK, COMBINE_BLOCK)

            def onehot_groups(c, carry):
                for u in range(GROUP_UNROLL):
                    gi = c * GROUP_UNROLL + u
                    g0 = pl.multiple_of(gi * SEG_ALIGN, SEG_ALIGN)
                    e = own_s[i, r * groups_per_block + gi]
                    rank = (group_rows + (base + g0 - off_s[i, e])).astype(F32)
                    hit = pos_ref[0, pl.ds(e, 1), :] == rank
                    p_ref[pl.ds(g0, SEG_ALIGN), :] = jnp.where(hit, 1.0, 0.0).astype(BF16)
                return carry

            lax.fori_loop(0, groups_per_block // GROUP_UNROLL, onehot_groups, 0)
            for n in range(d // ROW_TILE):
                cols = slice(n * ROW_TILE, (n + 1) * ROW_TILE)
                o_ref[0, :, cols] += lax.dot_general(p_ref[...], y_ref[pl.ds(base, COMBINE_BLOCK), cols],
                                                     (((0,), (0,)), ((), ())), preferred_element_type=F32)
            return carry

        lax.fori_loop(0, MOE_SLOTS // COMBINE_BLOCK, combine, 0)

        ctx = mod_ref[bsz:bsz + 1, 5 * d:6 * d]
        if tiles_per_batch is None:
            r = i * rows + lax.broadcasted_iota(jnp.int32, (rows, 1), 0)
            g2 = ctx
            for b in range(bsz):
                lo = b * rows_per_batch
                inside = jnp.where(r >= lo, jnp.where(r < lo + n_lat, 1, 0), 0)
                g2 = jnp.where(inside == 1, mod_ref[b:b + 1, 5 * d:6 * d], g2)
        else:
            g2 = mod_ref[pl.ds(i // tiles_per_batch, 1), 5 * d:6 * d]
        y = x_ref[0] + g2 * o_ref[0]
        if final:
            y = _rms(y) * fn_ref[...]
        o_ref[0] = y


def _moe(h, gates, xs, mod, wg, wu, wd, sg, su, sd, final_norm, layer, n_lat, final):
    bsz, m, d = xs.shape
    t = MOE_TILE
    if final:
        assert n_lat % t == 0
        tiles_per_batch = n_lat // t
        n_tiles = bsz * tiles_per_batch
        where = lambda i: (i // tiles_per_batch, i % tiles_per_batch, 0)
        out_shape = (bsz, n_lat, d)
    else:
        assert (bsz * m) % t == 0
        tiles_per_batch = None
        n_tiles = bsz * m // t
        where = lambda i: (0, i, 0)
        out_shape = (1, bsz * m, d)
        h, gates, xs = (a.reshape(1, bsz * m, a.shape[-1]) for a in (h, gates, xs))
    once = pl.Buffered(1)

    tok = jnp.arange(t)
    tri = (tok[None, :] < tok[:, None]).astype(BF16)
    lane = jnp.arange(LANE)
    upper = (lane[:, None] < lane[None, :]).astype(F32)
    groups = -(-MOE_SLOTS // SEG_ALIGN // LANE) * LANE
    em = jax.ShapeDtypeStruct((n_tiles, LANE, t), F32)
    ints = lambda n: jax.ShapeDtypeStruct((n_tiles, SUBLANE, n), jnp.int32)
    per_tile = lambda a, b: pl.BlockSpec((1, a, b), lambda i: (i, 0, 0))
    pos, gem, cnt, off, own = pl.pallas_call(
        _plan_kernel,
        grid=(n_tiles,),
        in_specs=[pl.BlockSpec((1, t, LANE), lambda i: where(i)),
                  pl.BlockSpec((t, t), lambda i: (0, 0)), pl.BlockSpec((LANE, LANE), lambda i: (0, 0))],
        out_specs=[per_tile(LANE, t), per_tile(LANE, t), per_tile(SUBLANE, LANE), per_tile(SUBLANE, LANE),
                   per_tile(SUBLANE, groups)],
        out_shape=[em, em, ints(LANE), ints(LANE), ints(groups)],
        compiler_params=_params(("parallel",)),
        name="moe_plan",
    )(gates, tri, upper)

    tile = lambda n: pl.BlockSpec((1, t, n), lambda i, s, *_: where(i), pipeline_mode=once)
    planned = lambda a, b: pl.BlockSpec((1, a, b), lambda i, s, *_: (i, 0, 0), pipeline_mode=once)
    whole = lambda a: pl.BlockSpec(a.shape, lambda i, s, *_: (0,) * a.ndim, pipeline_mode=once)
    experts = lambda a: pl.BlockSpec((None, EXPERTS_PER_STEP) + a.shape[2:], lambda i, s, *_: (layer, s, 0, 0))
    out = pl.pallas_call(
        functools.partial(_moe_kernel, bsz=bsz, rows_per_batch=m, n_lat=n_lat,
                          tiles_per_batch=tiles_per_batch, final=final),
        grid_spec=pltpu.PrefetchScalarGridSpec(
            num_scalar_prefetch=3,
            grid=(n_tiles, N_EXPERTS // EXPERTS_PER_STEP),
            in_specs=[tile(d), planned(LANE, t), planned(LANE, t),
                      experts(wg), experts(wu), experts(wd), whole(sg), whole(su), whole(sd),
                      tile(d), whole(mod), whole(final_norm)],
            out_specs=pl.BlockSpec((1, t, d), lambda i, s, *_: where(i)),
            scratch_shapes=[pltpu.VMEM((MOE_SLOTS + MOE_CAP, d), BF16), pltpu.VMEM((COMBINE_BLOCK, t), BF16)]),
        out_shape=jax.ShapeDtypeStruct(out_shape, F32),
        compiler_params=_params(("arbitrary", "arbitrary"), MOE_VMEM_LIMIT),
        name="moe_final" if final else "moe",
    )(cnt[:, 0], off[:, 0], own[:, 0], h, pos, gem, wg, wu, wd, sg, su, sd, xs, mod, final_norm)
    return out if final else out.reshape(bsz, m, d)


def _rope_tables(n_lat, m, head_dim, width):
    pos = jnp.arange(n_lat, dtype=jnp.int32)
    row = (pos // GRID_W).astype(F32)
    col = (pos % GRID_W).astype(F32)
    half = head_dim // 2
    quarter = half // 2
    lane = jnp.arange(head_dim)
    freq = (lane % quarter).astype(F32)
    inv = ROPE_THETA ** (-(2.0 * freq) / half)
    ang = jnp.where(lane[None, :] < half, row[:, None], col[:, None]) * inv[None, :]
    low = (lane % half) < quarter
    cos = jnp.cos(ang)
    sin = jnp.sin(ang)
    slo = jnp.where(low[None, :], -sin, 0.0)
    shi = jnp.where(low[None, :], 0.0, sin)
    reps = width // head_dim
    pad = lambda a, fill: jnp.concatenate(
        [jnp.tile(a, (1, reps)), jnp.full((m - n_lat, width), fill, F32)], axis=0)
    return pad(cos, 1.0), pad(slo, 0.0), pad(shi, 0.0)


def _mla_tables(tabs16):
    outs = []
    for a, fill in zip(tabs16, (1.0, 0.0, 0.0)):
        m = a.shape[0]
        slot = jnp.concatenate([jnp.full((m, MLA_NOPE), fill, F32), a,
                                jnp.full((m, QK_SLOT - MLA_QK), fill, F32)], axis=1)
        outs.append(jnp.tile(slot, (1, MLA_HEADS)))
    return tuple(outs)


def _slots(w, heads, width):
    w = w.reshape(w.shape[0], heads, width)
    pad = jnp.zeros((w.shape[0], heads, QK_SLOT - width), w.dtype)
    return jnp.concatenate([w, pad], axis=-1).reshape(w.shape[0], heads * QK_SLOT)


def _block_diag(w):
    dirs = w.shape[0]
    eye = jnp.eye(LRU_BLOCKS, dtype=w.dtype)
    full = jnp.einsum('dnij,nm->dnimj', w, eye).reshape(dirs, LRU_WIDTH, LRU_WIDTH)
    nl = LRU_WIDTH // LRU_LANES
    return jnp.stack([full[:, j * LRU_LANES:(j + 1) * LRU_LANES, j * LRU_LANES:(j + 1) * LRU_LANES]
                      for j in range(nl)], axis=1)


def kernel(x, c, ctx, c_ctx, mod_w, mod_b, ab_w_in, ab_w_out, diff_lambda_q, diff_lambda_k, diff_subln, lru_conv_w, lru_conv_b, lru_w_a, lru_b_a, lru_w_x, lru_b_x, lru_lambda, cd_w_in, cd_w_out, gqa_q_norm, gqa_k_norm, mla_q_norm, mla_kv_norm, mla_w_uq, mla_w_ukv, router_w, router_bias, exp_w_gate, exp_w_up, exp_w_down, sh_w_gate, sh_w_up, sh_w_down, final_norm):
    bsz, n_lat, d = x.shape
    n_ctx = ctx.shape[1]
    m = n_lat + n_ctx
    depth = mod_w.shape[0]
    assert depth == 2 and bsz < MOD_ROWS
    assert n_lat % ROW_TILE == 0 and n_ctx % ROW_TILE == 0 and n_lat % GRID_W == 0
    ctx_row = bsz

    xs = jnp.concatenate([x, ctx], axis=1)
    c_all = jnp.concatenate([c, c_ctx[None, :], jnp.zeros((MOD_ROWS - bsz - 1, d), F32)], axis=0)
    mods = _modulation(c_all, mod_w, mod_b)

    tabs64 = _rope_tables(n_lat, m, DIFF_HEAD_DIM, DIFF_HEADS * 2 * DIFF_HEAD_DIM)
    tabs_m = _mla_tables(_rope_tables(n_lat, m, MLA_ROPE, MLA_ROPE))

    exp_w = tuple(w.astype(BF16) for w in (exp_w_gate, exp_w_up, exp_w_down))

    def moe_weights(i):
        return exp_w + (sh_w_gate[i], sh_w_up[i], sh_w_down[i])

    fn = final_norm.reshape(1, d)

    mod = mods[0]
    qt, k, vt, u, g = _proj_ab(xs, mod, ab_w_in[0].astype(BF16), tabs64, n_lat, ctx_row)
    lambda_init = 0.8 - 0.6 * math.exp(-0.3 * 0)
    dv = 2 * DIFF_HEAD_DIM
    jobs = tuple(_Job((2 * h + j) * QK_SLOT, h, j, h * dv, dv) for h in range(DIFF_HEADS) for j in range(2))
    o_att = _attention(qt, k, vt, (diff_lambda_q[0], diff_lambda_k[0], diff_subln[0].reshape(-1, 1)),
                       jobs, n_lat, lambda_init, "diff_attention")
    o_rec = _lru(u, g, lru_conv_w[0], lru_conv_b[0].reshape(1, -1),
                 _block_diag(lru_w_a[0]).astype(BF16), lru_b_a[0].reshape(2, 1, LRU_WIDTH),
                 _block_diag(lru_w_x[0]).astype(BF16), lru_b_x[0].reshape(2, 1, LRU_WIDTH),
                 lru_lambda[0].reshape(2, 1, LRU_WIDTH), n_lat)
    xs, h2, gates = _out_proj(o_att, o_rec, xs, mod, ab_w_out[0].astype(BF16),
                              router_w[0].T, router_bias[0].reshape(-1, 1), n_lat, ctx_row)
    xs = _moe(h2, gates, xs, mod, *moe_weights(0), fn, 0, n_lat, False)

    mod = mods[1]
    w_in = cd_w_in[0]
    wq, wk = GQA_HEADS * GQA_HEAD_DIM, GQA_KV_HEADS * GQA_HEAD_DIM
    o_kr = wq + 2 * wk + MLA_Q_RANK + MLA_KV_RANK
    kr_slots = jnp.concatenate([jnp.zeros((d, MLA_NOPE), F32), w_in[:, o_kr:o_kr + MLA_ROPE],
                                jnp.zeros((d, QK_SLOT - MLA_QK), F32)], axis=1)
    w_in = jnp.concatenate([w_in[:, :o_kr], jnp.tile(kr_slots, (1, MLA_HEADS))], axis=1).astype(BF16)
    ukv = mla_w_ukv[0].reshape(MLA_KV_RANK, MLA_HEADS, MLA_NOPE + MLA_V)
    w_kn = _slots(ukv[:, :, :MLA_NOPE].reshape(MLA_KV_RANK, -1), MLA_HEADS, MLA_NOPE).astype(BF16)
    w_v = ukv[:, :, MLA_NOPE:].reshape(MLA_KV_RANK, MLA_HEADS * MLA_V).astype(BF16)
    w_uq = _slots(mla_w_uq[0], MLA_HEADS, MLA_QK).astype(BF16)
    lane = jnp.arange(wq)
    ones_bd = (lane[:, None] // GQA_HEAD_DIM == lane[None, :] // GQA_HEAD_DIM).astype(BF16)
    qgt, kg, vgt, qmt, km, vmt = _proj_cd(
        xs, mod, w_in, ones_bd, jnp.tile(gqa_q_norm[0], GQA_HEADS).reshape(1, -1),
        jnp.tile(gqa_k_norm[0], GQA_KV_HEADS).reshape(1, -1), mla_q_norm[0].reshape(1, -1),
        mla_kv_norm[0].reshape(1, -1), w_uq, w_kn, w_v, tabs64, tabs_m, n_lat, ctx_row)
    groups = GQA_HEADS // GQA_KV_HEADS
    dvn = GQA_HEAD_DIM + NORM_ROWS
    jobs = tuple(_Job(h * QK_SLOT, 0, h // groups, (h // groups) * dvn, dvn, NORM_ROWS) for h in range(GQA_HEADS))
    o_gqa = _attention(qgt, kg, vgt, (), jobs, n_lat, None, "gqa_attention")
    jobs = tuple(_Job(h * QK_SLOT, h // 2, h % 2, h * MLA_V, MLA_V) for h in range(MLA_HEADS))
    o_mla = _attention(qmt, km, vmt, (), jobs, n_lat, None, "mla_attention")
    xs, h2, gates = _out_proj(o_gqa, o_mla, xs, mod, cd_w_out[0].astype(BF16),
                              router_w[1].T, router_bias[1].reshape(-1, 1), n_lat, ctx_row)
    return _moe(h2, gates, xs, mod, *moe_weights(1), fn, 1, n_lat, True)
```

```python
import functools
import math
from typing import NamedTuple

import jax
import jax.numpy as jnp
from jax import lax
from jax.experimental import pallas as pl
from jax.experimental.pallas import tpu as pltpu

F32 = jnp.float32
BF16 = jnp.bfloat16

GRID_W = 64
ROPE_THETA = 10000.0
EPS = 1e-6
DIFF_HEADS = 4
DIFF_HEAD_DIM = 64
LRU_WIDTH = 512
LRU_BLOCKS = 8
LRU_BW = LRU_WIDTH // LRU_BLOCKS
CONV_W = 4
LRU_C = 8.0
GQA_HEADS = 8
GQA_KV_HEADS = 2
GQA_HEAD_DIM = 64
MLA_HEADS = 8
MLA_Q_RANK = 256
MLA_KV_RANK = 128
MLA_NOPE = 32
MLA_ROPE = 16
MLA_V = 64
MLA_QK = MLA_NOPE + MLA_ROPE
N_EXPERTS = 64
N_GROUPS = 8
PER_GROUP = N_EXPERTS // N_GROUPS
TOPK_GROUPS = 4
TOP_K = 8
EXPERT_FF = 256
ROUTED_SCALE = 2.5

QK_SLOT = 64
NORM_ROWS = 16
LOG2E = math.log2(math.e)

ROW_TILE = 256
LRU_CHUNK = 128
LRU_LANES = 256
EXPERTS_PER_STEP = 4
LANE = 128
SUBLANE = 8
MOD_ROWS = 8
VMEM_LIMIT = 56 * 1024 * 1024
MOE_VMEM_LIMIT = 62 * 1024 * 1024
MOE_TILE = 1024
MOE_CAP = 160
SEG_ALIGN = 16
MOE_SLOTS = MOE_TILE * TOP_K + N_EXPERTS * SEG_ALIGN
COMBINE_BLOCK = 1024
GROUP_UNROLL = 4


def _params(sem, vmem=VMEM_LIMIT):
    return pltpu.CompilerParams(dimension_semantics=sem, vmem_limit_bytes=vmem)


def _silu(x):
    return x * jax.nn.sigmoid(x)


def _rms(x):
    return x * lax.rsqrt(jnp.mean(x * x, axis=-1, keepdims=True) + EPS)


def _mod_rows(mod_ref, ctx_row, b, k, d, row0, rows, n_lat):
    lat = mod_ref[pl.ds(b, 1), k * d:(k + 1) * d]
    ctx = mod_ref[ctx_row:ctx_row + 1, k * d:(k + 1) * d]
    if n_lat % rows == 0:
        return jnp.where(row0 >= n_lat, ctx, lat)
    r = row0 + lax.broadcasted_iota(jnp.int32, (rows, 1), 0)
    return jnp.where(r >= n_lat, ctx, lat)


def _values_t(v, heads):
    vt = v.T
    dv = vt.shape[0] // heads
    ones = jnp.ones((NORM_ROWS, vt.shape[1]), F32)
    return jnp.concatenate([piece for h in range(heads) for piece in (vt[h * dv:(h + 1) * dv], ones)],
                           axis=0).astype(BF16)


def _rope(x, cos, sin_lo, sin_hi, half):
    n = x.shape[-1]
    return x * cos + pltpu.roll(x, n - half, 1) * sin_lo + pltpu.roll(x, half, 1) * sin_hi


def _mod_kernel(c_ref, w_ref, b_ref, o_ref):
    s = _silu(c_ref[...])
    o_ref[0] = jnp.dot(s, w_ref[0], preferred_element_type=F32, precision=lax.Precision.HIGHEST) + b_ref[0]


def _modulation(c_all, mod_w, mod_b):
    depth, d, n = mod_w.shape
    tn = n // 4
    return pl.pallas_call(
        _mod_kernel,
        grid=(depth, n // tn),
        in_specs=[pl.BlockSpec((MOD_ROWS, d), lambda i, j: (0, 0)),
                  pl.BlockSpec((1, d, tn), lambda i, j: (i, 0, j)),
                  pl.BlockSpec((1, 1, tn), lambda i, j: (i, 0, j))],
        out_specs=pl.BlockSpec((1, MOD_ROWS, tn), lambda i, j: (i, 0, j)),
        out_shape=jax.ShapeDtypeStruct((depth, MOD_ROWS, n), F32),
        compiler_params=_params(("arbitrary", "arbitrary")),
        name="modulation",
    )(c_all, mod_w, mod_b.reshape(depth, 1, n))


def _proj_ab_kernel(x_ref, mod_ref, w_ref, cos_ref, slo_ref, shi_ref,
                    qt_ref, k_ref, vt_ref, u_ref, g_ref, *, n_lat, ctx_row):
    b, t = pl.program_id(0), pl.program_id(1)
    d = x_ref.shape[-1]
    row0 = t * ROW_TILE
    shift = _mod_rows(mod_ref, ctx_row, b, 0, d, row0, ROW_TILE, n_lat)
    scale = _mod_rows(mod_ref, ctx_row, b, 1, d, row0, ROW_TILE, n_lat)
    h = (_rms(x_ref[0]) * (1.0 + scale) + shift).astype(BF16)
    acc = jnp.dot(h, w_ref[...], preferred_element_type=F32)
    w = DIFF_HEADS * 2 * DIFF_HEAD_DIM
    cos, slo, shi = cos_ref[...], slo_ref[...], shi_ref[...]
    q = _rope(acc[:, 0:w], cos, slo, shi, DIFF_HEAD_DIM // 4) * (DIFF_HEAD_DIM ** -0.5 * LOG2E)
    k = _rope(acc[:, w:2 * w], cos, slo, shi, DIFF_HEAD_DIM // 4)
    qt_ref[0, 0] = q.T.astype(BF16)
    k_ref[0] = k.astype(BF16)
    vt_ref[0] = _values_t(acc[:, 2 * w:3 * w], DIFF_HEADS)
    u_ref[0] = acc[:, 3 * w:3 * w + LRU_WIDTH]
    g_ref[0] = jax.nn.gelu(acc[:, 3 * w + LRU_WIDTH:3 * w + 2 * LRU_WIDTH]).astype(BF16)


def _proj_ab(xs, mod, w_in, tabs, n_lat, ctx_row):
    bsz, m, d = xs.shape
    nt = m // ROW_TILE
    w = DIFF_HEADS * 2 * DIFF_HEAD_DIM
    n_in = w_in.shape[1]
    tile = lambda n: pl.BlockSpec((1, ROW_TILE, n), lambda b, t: (b, t, 0))
    ttile = pl.BlockSpec((1, 1, w, ROW_TILE), lambda b, t: (b, t, 0, 0))
    wv = w + DIFF_HEADS * NORM_ROWS
    vtile = pl.BlockSpec((1, wv, ROW_TILE), lambda b, t: (b, 0, t))
    tab = pl.BlockSpec((ROW_TILE, w), lambda b, t: (t, 0))
    return pl.pallas_call(
        functools.partial(_proj_ab_kernel, n_lat=n_lat, ctx_row=ctx_row),
        grid=(bsz, nt),
        in_specs=[tile(d),
                  pl.BlockSpec(mod.shape, lambda b, t: (0, 0)),
                  pl.BlockSpec((d, n_in), lambda b, t: (0, 0)),
                  tab, tab, tab],
        out_specs=[ttile, tile(w), vtile, tile(LRU_WIDTH), tile(LRU_WIDTH)],
        out_shape=[jax.ShapeDtypeStruct((bsz, nt, w, ROW_TILE), BF16),
                   jax.ShapeDtypeStruct((bsz, m, w), BF16),
                   jax.ShapeDtypeStruct((bsz, wv, m), BF16),
                   jax.ShapeDtypeStruct((bsz, m, LRU_WIDTH), F32),
                   jax.ShapeDtypeStruct((bsz, m, LRU_WIDTH), BF16)],
        compiler_params=_params(("parallel", "parallel")),
        name="proj_ab",
    )(xs, mod, w_in, *tabs)


def _group_rms(x, ones_bd):
    x2 = x * x
    hi = x2.astype(BF16)
    lo = (x2 - hi.astype(F32)).astype(BF16)
    ss = jnp.dot(hi, ones_bd, preferred_element_type=F32) + jnp.dot(lo, ones_bd, preferred_element_type=F32)
    return x * lax.rsqrt(ss * (1.0 / GQA_HEAD_DIM) + EPS)


def _proj_cd_kernel(x_ref, mod_ref, w_ref, bd_ref, qn_ref, kn_ref, cqn_ref, ckvn_ref, wuq_ref, wkn_ref, wv_ref,
                    cos_ref, slo_ref, shi_ref, cosm_ref, slom_ref, shim_ref,
                    qgt_ref, kg_ref, vgt_ref, qmt_ref, km_ref, vmt_ref, *, n_lat, ctx_row):
    b, t = pl.program_id(0), pl.program_id(1)
    d = x_ref.shape[-1]
    row0 = t * ROW_TILE
    shift = _mod_rows(mod_ref, ctx_row, b, 0, d, row0, ROW_TILE, n_lat)
    scale = _mod_rows(mod_ref, ctx_row, b, 1, d, row0, ROW_TILE, n_lat)
    h = (_rms(x_ref[0]) * (1.0 + scale) + shift).astype(BF16)
    acc = jnp.dot(h, w_ref[...], preferred_element_type=F32)
    wq = GQA_HEADS * GQA_HEAD_DIM
    wk = GQA_KV_HEADS * GQA_HEAD_DIM
    o = 0
    q = acc[:, o:o + wq]; o += wq
    k = acc[:, o:o + wk]; o += wk
    v = acc[:, o:o + wk]; o += wk
    cq = acc[:, o:o + MLA_Q_RANK]; o += MLA_Q_RANK
    ckv = acc[:, o:o + MLA_KV_RANK]; o += MLA_KV_RANK
    kr = acc[:, o:o + MLA_HEADS * QK_SLOT]
    cos, slo, shi = cos_ref[...], slo_ref[...], shi_ref[...]
    bd = bd_ref[...]
    q = (_rope(_group_rms(q, bd) * qn_ref[...], cos, slo, shi, GQA_HEAD_DIM // 4)
         * (GQA_HEAD_DIM ** -0.5 * LOG2E))
    k = _rope(_group_rms(k, bd[:wk, :wk]) * kn_ref[...], cos[:, :wk], slo[:, :wk], shi[:, :wk], GQA_HEAD_DIM // 4)
    qgt_ref[0, 0] = q.T.astype(BF16)
    kg_ref[0] = k.astype(BF16)
    vgt_ref[0] = _values_t(v, GQA_KV_HEADS)
    cosm, slom, shim = cosm_ref[...], slom_ref[...], shim_ref[...]
    cqn = (_rms(cq) * cqn_ref[...]).astype(BF16)
    mq = jnp.dot(cqn, wuq_ref[...], preferred_element_type=F32)
    qmt_ref[0, 0] = (_rope(mq, cosm, slom, shim, MLA_ROPE // 4) * (MLA_QK ** -0.5 * LOG2E)).T.astype(BF16)
    ckvn = (_rms(ckv) * ckvn_ref[...]).astype(BF16)
    km = jnp.dot(ckvn, wkn_ref[...], preferred_element_type=F32) + _rope(kr, cosm, slom, shim, MLA_ROPE // 4)
    km_ref[0] = km.astype(BF16)
    vmt_ref[0] = _values_t(jnp.dot(ckvn, wv_ref[...], preferred_element_type=F32), MLA_HEADS)


def _proj_cd(xs, mod, w_in, bd, qn, kn, cqn, ckvn, wuq, wkn, wv, tabs, tabs_m, n_lat, ctx_row):
    bsz, m, d = xs.shape
    nt = m // ROW_TILE
    wq = GQA_HEADS * GQA_HEAD_DIM
    wk = GQA_KV_HEADS * GQA_HEAD_DIM
    wm = MLA_HEADS * QK_SLOT
    wmv = MLA_HEADS * (MLA_V + NORM_ROWS)
    wgv = GQA_KV_HEADS * (GQA_HEAD_DIM + NORM_ROWS)
    tile = lambda n: pl.BlockSpec((1, ROW_TILE, n), lambda b, t: (b, t, 0))
    ttile = lambda n: pl.BlockSpec((1, 1, n, ROW_TILE), lambda b, t: (b, t, 0, 0))
    vtile = lambda n: pl.BlockSpec((1, n, ROW_TILE), lambda b, t: (b, 0, t))
    whole = lambda a: pl.BlockSpec(a.shape, lambda b, t: (0,) * a.ndim)
    tab = pl.BlockSpec((ROW_TILE, wq), lambda b, t: (t, 0))
    tabm = pl.BlockSpec((ROW_TILE, wm), lambda b, t: (t, 0))
    return pl.pallas_call(
        functools.partial(_proj_cd_kernel, n_lat=n_lat, ctx_row=ctx_row),
        grid=(bsz, nt),
        in_specs=[tile(d), whole(mod), whole(w_in), whole(bd), whole(qn), whole(kn), whole(cqn), whole(ckvn),
                  whole(wuq), whole(wkn), whole(wv), tab, tab, tab, tabm, tabm, tabm],
        out_specs=[ttile(wq), tile(wk), vtile(wgv), ttile(wm), tile(wm), vtile(wmv)],
        out_shape=[jax.ShapeDtypeStruct((bsz, nt, wq, ROW_TILE), BF16),
                   jax.ShapeDtypeStruct((bsz, m, wk), BF16),
                   jax.ShapeDtypeStruct((bsz, wgv, m), BF16),
                   jax.ShapeDtypeStruct((bsz, nt, wm, ROW_TILE), BF16),
                   jax.ShapeDtypeStruct((bsz, m, wm), BF16),
                   jax.ShapeDtypeStruct((bsz, wmv, m), BF16)],
        compiler_params=_params(("parallel", "parallel")),
        name="proj_cd",
    )(xs, mod, w_in, bd, qn, kn, cqn, ckvn, wuq, wkn, wv, *tabs, *tabs_m)


class _Job(NamedTuple):
    q_row: int
    k_group: int
    k_half: int
    v_row: int
    dv: int


def _attend(qt_ref, qnext_ref, k_ref, vt_ref, s_ref, mp_ref, jobs, lo, hi, chained):
    assert len(jobs) % 2 == 0
    tq = qt_ref.shape[-1]
    n = hi - lo
    group = next(g for g in (11, 3, 2, 1) if n % g == 0)
    steps = n // group
    row_groups = ROW_TILE // SUBLANE

    def weights(job, ref=qt_ref):
        qh = ref[0, 0, job.q_row:job.q_row + QK_SLOT, :]
        z = jnp.zeros_like(qh)
        return jnp.concatenate([z, qh] if job.k_half else [qh, z], axis=0)

    def score(c0, buf, qw, job, mp):
        r0 = pl.multiple_of(c0 * ROW_TILE, ROW_TILE)
        kc = k_ref[0, pl.ds(r0, group * ROW_TILE), job.k_group * LANE:(job.k_group + 1) * LANE]
        s_all = jnp.dot(kc, qw, preferred_element_type=F32)
        for g in range(group):
            s = s_all[g * ROW_TILE:(g + 1) * ROW_TILE]
            s_ref[buf, c0 + g] = s
            mp = jnp.maximum(mp, jnp.max(s.reshape(row_groups, SUBLANE, tq), axis=0))
        return mp

    def value(c0, buf, m, job, acc):
        ps = [jnp.exp2(s_ref[buf, c0 + g] - m).astype(BF16) for g in range(group)]
        k0 = pl.multiple_of(c0 * ROW_TILE, ROW_TILE)
        v = vt_ref[0, job.v_row:job.v_row + job.dv, pl.ds(k0, group * ROW_TILE)]
        return acc + jnp.dot(v, jnp.concatenate(ps, axis=0), preferred_element_type=F32)

    def loop(body, init):
        return body(0, init) if steps == 1 else lax.fori_loop(0, steps, body, init)

    neg = jnp.full((SUBLANE, tq), -jnp.inf, F32)

    def prologue():
        qw = weights(jobs[0])

        def first(i, mp):
            return score(lo + i * group, 0, qw, jobs[0], mp)

        mp_ref[...] = loop(first, neg)

    if chained:
        pl.when(pl.program_id(1) == 0)(prologue)
    else:
        prologue()
    mp = mp_ref[...]
    results = []
    for j, job in enumerate(jobs):
        buf = j % 2
        m = jnp.max(mp, axis=0, keepdims=True)
        if j + 1 < len(jobs):
            nxt, qw = jobs[j + 1], weights(jobs[j + 1])
        elif chained:
            nxt, qw = jobs[0], weights(jobs[0], qnext_ref)
        else:
            nxt = None

        def body(i, carry):
            mp, acc = carry
            if nxt is not None:
                mp = score(lo + i * group, 1 - buf, qw, nxt, mp)
            return mp, value(lo + i * group, buf, m, job, acc)

        mp, acc = loop(body, (neg, jnp.zeros((job.dv, tq), F32)))
        dv = job.dv - NORM_ROWS
        results.append((acc[:dv], acc[dv:dv + 1]))
    if chained:
        mp_ref[...] = mp
    return results


def _attn_kernel(*refs, jobs, n_lat, lambda_init):
    if lambda_init is None:
        qt_ref, qnext_ref, k_ref, vt_ref, o_ref, s_ref, mp_ref = refs
    else:
        qt_ref, qnext_ref, k_ref, vt_ref, lq_ref, lk_ref, sub_ref, o_ref, s_ref, mp_ref = refs
    n_chunks = k_ref.shape[1] // ROW_TILE
    first_ctx = n_lat // ROW_TILE
    t = pl.program_id(1)

    def run(lo, hi, chained):
        res = _attend(qt_ref, qnext_ref, k_ref, vt_ref, s_ref, mp_ref, jobs, lo, hi, chained)
        if lambda_init is None:
            outs = [acc / l for acc, l in res]
        else:
            lq, lk = lq_ref[...], lk_ref[...]
            lam = (jnp.exp(jnp.sum(lq[0:1] * lk[0:1], axis=-1, keepdims=True))
                   - jnp.exp(jnp.sum(lq[1:2] * lk[1:2], axis=-1, keepdims=True)) + lambda_init)
            outs = []
            for h in range(len(res) // 2):
                (a0, l0), (a1, l1) = res[2 * h], res[2 * h + 1]
                o = a0 / l0 - lam * (a1 / l1)
                o = o * lax.rsqrt(jnp.mean(o * o, axis=0, keepdims=True) + EPS)
                outs.append(o * sub_ref[...] * (1.0 - lambda_init))
        o_ref[0] = jnp.concatenate(outs, axis=0).T.astype(BF16)

    @pl.when(t < first_ctx)
    def _():
        run(0, n_chunks, True)

    @pl.when(t >= first_ctx)
    def _():
        run(first_ctx, n_chunks, False)


def _attention(qt, k, vt, extra, jobs, n_lat, lambda_init, name):
    bsz, nt, wq, tq = qt.shape
    m = k.shape[1]
    wo = sum(j.dv - NORM_ROWS for j in jobs) // (1 if lambda_init is None else 2)
    once = pl.Buffered(1)
    whole = lambda a: pl.BlockSpec(a.shape, lambda b, t: (0,) * a.ndim)
    return pl.pallas_call(
        functools.partial(_attn_kernel, jobs=jobs, n_lat=n_lat, lambda_init=lambda_init),
        grid=(bsz, nt),
        in_specs=[pl.BlockSpec((1, 1, wq, tq), lambda b, t: (b, t, 0, 0)),
                  pl.BlockSpec((1, 1, wq, tq), lambda b, t: (b, jnp.minimum(t + 1, nt - 1), 0, 0)),
                  pl.BlockSpec((1,) + k.shape[1:], lambda b, t: (b, 0, 0), pipeline_mode=once),
                  pl.BlockSpec((1,) + vt.shape[1:], lambda b, t: (b, 0, 0), pipeline_mode=once)]
                 + [whole(a) for a in extra],
        out_specs=pl.BlockSpec((1, tq, wo), lambda b, t: (b, t, 0)),
        out_shape=jax.ShapeDtypeStruct((bsz, m, wo), BF16),
        scratch_shapes=[pltpu.VMEM((2, nt, ROW_TILE, tq), F32), pltpu.VMEM((SUBLANE, tq), F32)],
        compiler_params=_params(("arbitrary", "arbitrary")),
        name=name,
    )(qt, qt, k, vt, *extra)


def _scan_chunk(a, b, carry, reverse):
    n = a.shape[0]
    rows = lax.broadcasted_iota(jnp.int32, (n, 1), 0)
    d = 1
    while d < n:
        sh = n - d if reverse else d
        valid = rows < n - d if reverse else rows >= d
        b = jnp.where(valid, a * pltpu.roll(b, sh, 0) + b, b)
        a = jnp.where(valid, a * pltpu.roll(a, sh, 0), a)
        d *= 2
    h = a * carry + b
    return h, (h[0:1] if reverse else h[n - 1:n])


def _lru_kernel(u_ref, g_ref, cw_ref, cb_ref, wa_ref, ba_ref, wx_ref, bx_ref, lam_ref, o_ref, up_ref, hf_ref, *, n_lat):
    m = u_ref.shape[1]
    lanes = u_ref.shape[2]
    n_chunks = m // LRU_CHUNK
    n_ctx_chunks = (m - n_lat) // LRU_CHUNK
    pad = SUBLANE

    up_ref[0:pad, :] = jnp.zeros((pad, lanes), F32)
    up_ref[pad + m:pad + m + pad, :] = jnp.zeros((pad, lanes), F32)

    def copy(c, _):
        r0 = pl.multiple_of(c * LRU_CHUNK, LRU_CHUNK)
        up_ref[pl.ds(pad + r0, LRU_CHUNK), :] = u_ref[0, pl.ds(r0, LRU_CHUNK), :]
        return 0

    lax.fori_loop(0, n_chunks, copy, 0)

    cw = cw_ref[...]
    cb = cb_ref[...]
    win_rows = LRU_CHUNK + 2 * pad

    def coeffs(c, d):
        r0 = pl.multiple_of(c * LRU_CHUNK, LRU_CHUNK)
        win = up_ref[pl.ds(r0, win_rows), :]
        r = r0 + lax.broadcasted_iota(jnp.int32, (LRU_CHUNK, 1), 0)
        at = lambda k: pltpu.roll(win, (win_rows - k) % win_rows, 0)[pad:pad + LRU_CHUNK]
        y = (cw[0:1] * jnp.where((r == n_lat) | (r == n_lat + 1), 0.0, at(-2))
             + cw[1:2] * jnp.where(r == n_lat, 0.0, at(-1))
             + cw[2:3] * win[pad:pad + LRU_CHUNK]
             + cw[3:4] * jnp.where(r == n_lat - 1, 0.0, at(1))
             + cb)
        yb = y.astype(BF16)
        ra = jax.nn.sigmoid(jnp.dot(yb, wa_ref[d, 0], preferred_element_type=F32) + ba_ref[d])
        ix = jax.nn.sigmoid(jnp.dot(yb, wx_ref[d, 0], preferred_element_type=F32) + bx_ref[d])
        z = -lam_ref[d]
        softplus = jnp.maximum(z, 0.0) + jnp.log(1.0 + jnp.exp(-jnp.abs(z)))
        a = jnp.exp(-LRU_C * ra * softplus)
        return a, jnp.sqrt(1.0 - a * a) * ix * y

    def fwd(s, carry):
        c = lax.rem(s + (n_chunks - n_ctx_chunks), n_chunks)
        a, b = coeffs(c, 0)
        h, carry = _scan_chunk(a, b, carry, False)
        r0 = pl.multiple_of(c * LRU_CHUNK, LRU_CHUNK)
        hf_ref[pl.ds(r0, LRU_CHUNK), :] = h
        return carry

    lax.fori_loop(0, n_chunks, fwd, jnp.zeros((1, lanes), F32))

    def bwd(s, carry):
        c = n_chunks - 1 - s
        a, b = coeffs(c, 1)
        h, carry = _scan_chunk(a, b, carry, True)
        r0 = pl.multiple_of(c * LRU_CHUNK, LRU_CHUNK)
        tot = hf_ref[pl.ds(r0, LRU_CHUNK), :] + h
        o_ref[0, pl.ds(r0, LRU_CHUNK), :] = (tot * g_ref[0, pl.ds(r0, LRU_CHUNK), :].astype(F32)).astype(BF16)
        return carry

    lax.fori_loop(0, n_chunks, bwd, jnp.zeros((1, lanes), F32))


def _lru(u, g, conv_w, conv_b, wa_bd, ba, wx_bd, bx, lam, n_lat):
    bsz, m, c = u.shape
    nl = c // LRU_LANES
    seq = lambda: pl.BlockSpec((1, m, LRU_LANES), lambda b, j: (b, 0, j))
    vec = lambda a: pl.BlockSpec(a.shape[:-1] + (LRU_LANES,), lambda b, j: (0,) * (a.ndim - 1) + (j,))
    mat = pl.BlockSpec((2, 1, LRU_LANES, LRU_LANES), lambda b, j: (0, j, 0, 0))
    return pl.pallas_call(
        functools.partial(_lru_kernel, n_lat=n_lat),
        grid=(bsz, nl),
        in_specs=[seq(), seq(), vec(conv_w), vec(conv_b), mat, vec(ba), mat, vec(bx), vec(lam)],
        out_specs=seq(),
        out_shape=jax.ShapeDtypeStruct((bsz, m, c), BF16),
        scratch_shapes=[pltpu.VMEM((m + 2 * SUBLANE, LRU_LANES), F32), pltpu.VMEM((m, LRU_LANES), F32)],
        compiler_params=_params(("parallel", "parallel")),
        name="rglru",
    )(u, g, conv_w, conv_b, wa_bd, ba, wx_bd, bx, lam)


def _route(logits_t, bias):
    n = logits_t.shape[-1]
    scores = jax.nn.sigmoid(logits_t)
    choice = scores + bias
    sub = lax.broadcasted_iota(jnp.int32, (PER_GROUP, n), 0)
    neg = -jnp.inf
    groups, gs = [], []
    for g in range(N_GROUPS):
        cg = choice[g * PER_GROUP:(g + 1) * PER_GROUP]
        m1 = jnp.max(cg, axis=0, keepdims=True)
        i1 = jnp.min(jnp.where(cg == m1, sub, PER_GROUP), axis=0, keepdims=True)
        m2 = jnp.max(jnp.where(sub == i1, neg, cg), axis=0, keepdims=True)
        groups.append(cg)
        gs.append(m1 + m2)
    masked = []
    for g in range(N_GROUPS):
        rank = jnp.zeros((1, n), jnp.int32)
        for o in range(N_GROUPS):
            if o != g:
                ahead = (gs[o] >= gs[g]) if o < g else (gs[o] > gs[g])
                rank = rank + jnp.where(ahead, 1, 0)
        masked.append(jnp.where(rank < TOPK_GROUPS, groups[g], neg))
    masked = jnp.concatenate(masked, axis=0)
    eidx = lax.broadcasted_iota(jnp.int32, (N_EXPERTS, n), 0)
    picked = jnp.zeros((N_EXPERTS, n), F32)
    for _ in range(TOP_K):
        mx = jnp.max(masked, axis=0, keepdims=True)
        first = jnp.min(jnp.where(masked == mx, eidx, N_EXPERTS), axis=0, keepdims=True)
        hit = eidx == first
        picked = jnp.where(hit, 1.0, picked)
        masked = jnp.where(hit, neg, masked)
    w = picked * scores
    return w / jnp.sum(w, axis=0, keepdims=True) * ROUTED_SCALE


def _out_proj_kernel(oa_ref, ob_ref, x_ref, mod_ref, wa_ref, wb_ref, rwh_ref, rwl_ref, rb_ref,
                     xo_ref, h_ref, gates_ref, *, n_lat, ctx_row):
    b, t = pl.program_id(0), pl.program_id(1)
    d = x_ref.shape[-1]
    row0 = t * ROW_TILE
    mv = lambda k: _mod_rows(mod_ref, ctx_row, b, k, d, row0, ROW_TILE, n_lat)
    y = (jnp.dot(oa_ref[0], wa_ref[...], preferred_element_type=F32)
         + jnp.dot(ob_ref[0], wb_ref[...], preferred_element_type=F32))
    x = x_ref[0] + mv(2) * y
    xo_ref[0] = x
    h = _rms(x) * (1.0 + mv(4)) + mv(3)
    hb = h.astype(BF16)
    h_ref[0] = hb
    h_lo = (h - hb.astype(F32)).astype(BF16)
    nt = lambda a, b: lax.dot_general(a, b, (((1,), (1,)), ((), ())), preferred_element_type=F32)
    logits_t = nt(rwh_ref[...], hb) + (nt(rwh_ref[...], h_lo) + nt(rwl_ref[...], hb))
    gates_t = _route(logits_t, rb_ref[...])
    gates_t = jnp.concatenate([gates_t, jnp.zeros((LANE - N_EXPERTS, ROW_TILE), F32)], axis=0)
    gates_ref[0] = gates_t.T


def _out_proj(oa, ob, xs, mod, w_out, router_wt, router_b, n_lat, ctx_row):
    bsz, m, d = xs.shape
    na, nb = oa.shape[-1], ob.shape[-1]
    wa, wb = w_out[:na], w_out[na:]
    rw_hi = router_wt.astype(BF16)
    rw_lo = (router_wt - rw_hi.astype(F32)).astype(BF16)
    tile = lambda n: pl.BlockSpec((1, ROW_TILE, n), lambda b, t: (b, t, 0))
    whole = lambda a: pl.BlockSpec(a.shape, lambda b, t: (0,) * a.ndim)
    return pl.pallas_call(
        functools.partial(_out_proj_kernel, n_lat=n_lat, ctx_row=ctx_row),
        grid=(bsz, m // ROW_TILE),
        in_specs=[tile(na), tile(nb), tile(d), whole(mod), whole(wa), whole(wb), whole(rw_hi), whole(rw_lo),
                  whole(router_b)],
        out_specs=[tile(d), tile(d), tile(LANE)],
        out_shape=[jax.ShapeDtypeStruct((bsz, m, d), F32),
                   jax.ShapeDtypeStruct((bsz, m, d), BF16),
                   jax.ShapeDtypeStruct((bsz, m, LANE), F32)],
        compiler_params=_params(("parallel", "parallel")),
        name="out_proj_router",
    )(oa, ob, xs, mod, wa, wb, rw_hi, rw_lo, router_b)


def _plan_kernel(gates_ref, tri_ref, upper_ref, pos_ref, gem_ref, cnt_ref, off_ref, own_ref):
    g = gates_ref[0]
    sel = g != 0.0
    one = jnp.where(sel, 1.0, 0.0)
    pos = jnp.dot(tri_ref[...], one.astype(BF16), preferred_element_type=F32)
    cnt = jnp.sum(one, axis=0, keepdims=True)
    cpad = jnp.floor((cnt + (SEG_ALIGN - 1)) * (1.0 / SEG_ALIGN)) * SEG_ALIGN
    off = jnp.dot(jnp.broadcast_to(cpad, (SUBLANE, LANE)), upper_ref[...], preferred_element_type=F32,
                  precision=lax.Precision.HIGHEST)[0:1]
    pos_ref[0] = jnp.where(sel, pos, -1.0).T
    gem_ref[0] = g.T
    cnt_ref[0] = jnp.broadcast_to(cnt, (SUBLANE, LANE)).astype(jnp.int32)
    off_ref[0] = jnp.broadcast_to(off, (SUBLANE, LANE)).astype(jnp.int32)
    groups = own_ref.shape[-1]
    ends = jnp.broadcast_to(off + cpad, (LANE, LANE)).T
    ends = jnp.concatenate([ends] * (groups // LANE), axis=1)
    start = lax.broadcasted_iota(jnp.int32, (LANE, groups), 1).astype(F32) * SEG_ALIGN
    real = lax.broadcasted_iota(jnp.int32, (LANE, groups), 0) < N_EXPERTS
    before = jnp.where(real, jnp.where(ends <= start, 1.0, 0.0), 0.0)
    owner = jnp.minimum(jnp.sum(before, axis=0, keepdims=True), N_EXPERTS - 1.0)
    own_ref[0] = jnp.broadcast_to(owner, (SUBLANE, groups)).astype(jnp.int32)


def _moe_kernel(cnt_s, off_s, own_s, h_ref, pos_ref, gem_ref, wg_ref, wu_ref, wd_ref,
                sg_ref, su_ref, sd_ref, x_ref, mod_ref, fn_ref, o_ref, y_ref, p_ref,
                *, bsz, rows_per_batch, n_lat, tiles_per_batch, final):
    i, s = pl.program_id(0), pl.program_id(1)
    rows, d = h_ref.shape[1], h_ref.shape[2]
    h = h_ref[0]

    def ffn(x, wg, wu, wd, gate):
        a = _silu(jnp.dot(x, wg, preferred_element_type=F32)) * jnp.dot(x, wu, preferred_element_type=F32)
        if gate is not None:
            a = a * gate
        return jnp.dot(a.astype(BF16), wd, preferred_element_type=F32)

    @pl.when((s == 0) & (i == 0))
    def _():
        y_ref[...] = jnp.zeros(y_ref.shape, BF16)

    @pl.when(s == 0)
    def _():
        o_ref[0] = ffn(h, sg_ref[...].astype(BF16), su_ref[...].astype(BF16), sd_ref[...].astype(BF16), None)

    cap = y_ref.shape[0] - MOE_SLOTS
    cap_rows = lax.broadcasted_iota(jnp.int32, (cap, 1), 0)

    def onehot(pos_row, q):
        ranks = cap_rows + q * cap
        hit = pos_row == ranks.astype(F32)
        return ranks, hit, jnp.where(hit, 1.0, 0.0).astype(BF16)

    def expert_rows(x, ranks, hit, q, gate_row, w, cnt, off):
        gate = jnp.sum(jnp.where(hit, gate_row, 0.0), axis=1, keepdims=True)
        y = ffn(x, *w, gate)
        dst = pl.multiple_of(off + q * cap, SEG_ALIGN)
        y_ref[pl.ds(dst, cap), :] = jnp.where(ranks < cnt, y.astype(BF16), y_ref[pl.ds(dst, cap), :])

    experts = []
    for j in range(EXPERTS_PER_STEP):
        e = s * EXPERTS_PER_STEP + j
        experts.append((cnt_s[i, e], off_s[i, e],
                        pos_ref[0, pl.ds(e, 1), :],
                        gem_ref[0, pl.ds(e, 1), :],
                        (wg_ref[j], wu_ref[j], wd_ref[j])))

    hots = [onehot(pos_row, 0) for _, _, pos_row, _, _ in experts]
    xs = jnp.dot(jnp.concatenate([p for _, _, p in hots], axis=0), h, preferred_element_type=F32).astype(BF16)
    for j, ((cnt, off, _, gate_row, w), (ranks, hit, _)) in enumerate(zip(experts, hots)):
        expert_rows(xs[j * cap:(j + 1) * cap], ranks, hit, 0, gate_row, w, cnt, off)

    for cnt, off, pos_row, gate_row, w in experts:
        def block(q, carry):
            ranks, hit, p = onehot(pos_row, q)
            x = jnp.dot(p, h, preferred_element_type=F32).astype(BF16)
            expert_rows(x, ranks, hit, q, gate_row, w, cnt, off)
            return carry

        lax.fori_loop(1, (cnt + cap - 1) // cap, block, 0)

    @pl.when(s == N_EXPERTS // EXPERTS_PER_STEP - 1)
    def _():
        group_rows = lax.broadcasted_iota(jnp.int32, (SEG_ALIGN, 1), 0)
        groups_per_block = COMBINE_BLOCK // SEG_ALIGN

        def combine(r, carry):
            base = pl.multiple_of(r * COMBINE_BLOCK, COMBINE_BLOCK)

            def onehot_groups(c, carry):
                for u in range(GROUP_UNROLL):
                    gi = c * GROUP_UNROLL + u
                    g0 = pl.multiple_of(gi * SEG_ALIGN, SEG_ALIGN)
                    e = own_s[i, r * groups_per_block + gi]
                    rank = (group_rows + (base + g0 - off_s[i, e])).astype(F32)
                    hit = pos_ref[0, pl.ds(e, 1), :] == rank
                    p_ref[pl.ds(g0, SEG_ALIGN), :] = jnp.where(hit, 1.0, 0.0).astype(BF16)
                return carry

            lax.fori_loop(0, groups_per_block // GROUP_UNROLL, onehot_groups, 0)
            for n in range(d // ROW_TILE):
                cols = slice(n * ROW_TILE, (n + 1) * ROW_TILE)
                o_ref[0, :, cols] += lax.dot_general(p_ref[...], y_ref[pl.ds(base, COMBINE_BLOCK), cols],
                                                     (((0,), (0,)), ((), ())), preferred_element_type=F32)
            return carry

        lax.fori_loop(0, MOE_SLOTS // COMBINE_BLOCK, combine, 0)

        ctx = mod_ref[bsz:bsz + 1, 5 * d:6 * d]
        if tiles_per_batch is None:
            r = i * rows + lax.broadcasted_iota(jnp.int32, (rows, 1), 0)
            g2 = ctx
            for b in range(bsz):
                lo = b * rows_per_batch
                inside = jnp.where(r >= lo, jnp.where(r < lo + n_lat, 1, 0), 0)
                g2 = jnp.where(inside == 1, mod_ref[b:b + 1, 5 * d:6 * d], g2)
        else:
            g2 = mod_ref[pl.ds(i // tiles_per_batch, 1), 5 * d:6 * d]
        y = x_ref[0] + g2 * o_ref[0]
        if final:
            y = _rms(y) * fn_ref[...]
        o_ref[0] = y


def _moe(h, gates, xs, mod, wg, wu, wd, sg, su, sd, final_norm, layer, n_lat, final, cap=MOE_CAP):
    bsz, m, d = xs.shape
    t = MOE_TILE
    if final:
        assert n_lat % t == 0
        tiles_per_batch = n_lat // t
        n_tiles = bsz * tiles_per_batch
        where = lambda i: (i // tiles_per_batch, i % tiles_per_batch, 0)
        out_shape = (bsz, n_lat, d)
    else:
        assert (bsz * m) % t == 0
        tiles_per_batch = None
        n_tiles = bsz * m // t
        where = lambda i: (0, i, 0)
        out_shape = (1, bsz * m, d)
        h, gates, xs = (a.reshape(1, bsz * m, a.shape[-1]) for a in (h, gates, xs))
    once = pl.Buffered(1)

    tok = jnp.arange(t)
    tri = (tok[None, :] < tok[:, None]).astype(BF16)
    lane = jnp.arange(LANE)
    upper = (lane[:, None] < lane[None, :]).astype(F32)
    groups = -(-MOE_SLOTS // SEG_ALIGN // LANE) * LANE
    em = jax.ShapeDtypeStruct((n_tiles, LANE, t), F32)
    ints = lambda n: jax.ShapeDtypeStruct((n_tiles, SUBLANE, n), jnp.int32)
    per_tile = lambda a, b: pl.BlockSpec((1, a, b), lambda i: (i, 0, 0))
    pos, gem, cnt, off, own = pl.pallas_call(
        _plan_kernel,
        grid=(n_tiles,),
        in_specs=[pl.BlockSpec((1, t, LANE), lambda i: where(i)),
                  pl.BlockSpec((t, t), lambda i: (0, 0)), pl.BlockSpec((LANE, LANE), lambda i: (0, 0))],
        out_specs=[per_tile(LANE, t), per_tile(LANE, t), per_tile(SUBLANE, LANE), per_tile(SUBLANE, LANE),
                   per_tile(SUBLANE, groups)],
        out_shape=[em, em, ints(LANE), ints(LANE), ints(groups)],
        compiler_params=_params(("parallel",)),
        name="moe_plan",
    )(gates, tri, upper)

    tile = lambda n: pl.BlockSpec((1, t, n), lambda i, s, *_: where(i), pipeline_mode=once)
    planned = lambda a, b: pl.BlockSpec((1, a, b), lambda i, s, *_: (i, 0, 0), pipeline_mode=once)
    whole = lambda a: pl.BlockSpec(a.shape, lambda i, s, *_: (0,) * a.ndim, pipeline_mode=once)
    experts = lambda a: pl.BlockSpec((None, EXPERTS_PER_STEP) + a.shape[2:], lambda i, s, *_: (layer, s, 0, 0))
    out = pl.pallas_call(
        functools.partial(_moe_kernel, bsz=bsz, rows_per_batch=m, n_lat=n_lat,
                          tiles_per_batch=tiles_per_batch, final=final),
        grid_spec=pltpu.PrefetchScalarGridSpec(
            num_scalar_prefetch=3,
            grid=(n_tiles, N_EXPERTS // EXPERTS_PER_STEP),
            in_specs=[tile(d), planned(LANE, t), planned(LANE, t),
                      experts(wg), experts(wu), experts(wd), whole(sg), whole(su), whole(sd),
                      tile(d), whole(mod), whole(final_norm)],
            out_specs=pl.BlockSpec((1, t, d), lambda i, s, *_: where(i)),
            scratch_shapes=[pltpu.VMEM((MOE_SLOTS + cap, d), BF16), pltpu.VMEM((COMBINE_BLOCK, t), BF16)]),
        out_shape=jax.ShapeDtypeStruct(out_shape, F32),
        compiler_params=_params(("arbitrary", "arbitrary"), MOE_VMEM_LIMIT),
        name="moe_final" if final else "moe",
    )(cnt[:, 0], off[:, 0], own[:, 0], h, pos, gem, wg, wu, wd, sg, su, sd, xs, mod, final_norm)
    return out if final else out.reshape(bsz, m, d)


def _rope_tables(n_lat, m, head_dim, width):
    pos = jnp.arange(n_lat, dtype=jnp.int32)
    row = (pos // GRID_W).astype(F32)
    col = (pos % GRID_W).astype(F32)
    half = head_dim // 2
    quarter = half // 2
    lane = jnp.arange(head_dim)
    freq = (lane % quarter).astype(F32)
    inv = ROPE_THETA ** (-(2.0 * freq) / half)
    ang = jnp.where(lane[None, :] < half, row[:, None], col[:, None]) * inv[None, :]
    low = (lane % half) < quarter
    cos = jnp.cos(ang)
    sin = jnp.sin(ang)
    slo = jnp.where(low[None, :], -sin, 0.0)
    shi = jnp.where(low[None, :], 0.0, sin)
    reps = width // head_dim
    pad = lambda a, fill: jnp.concatenate(
        [jnp.tile(a, (1, reps)), jnp.full((m - n_lat, width), fill, F32)], axis=0)
    return pad(cos, 1.0), pad(slo, 0.0), pad(shi, 0.0)


def _mla_tables(tabs16):
    outs = []
    for a, fill in zip(tabs16, (1.0, 0.0, 0.0)):
        m = a.shape[0]
        slot = jnp.concatenate([jnp.full((m, MLA_NOPE), fill, F32), a,
                                jnp.full((m, QK_SLOT - MLA_QK), fill, F32)], axis=1)
        outs.append(jnp.tile(slot, (1, MLA_HEADS)))
    return tuple(outs)


def _slots(w, heads, width):
    w = w.reshape(w.shape[0], heads, width)
    pad = jnp.zeros((w.shape[0], heads, QK_SLOT - width), w.dtype)
    return jnp.concatenate([w, pad], axis=-1).reshape(w.shape[0], heads * QK_SLOT)


def _block_diag(w):
    dirs = w.shape[0]
    eye = jnp.eye(LRU_BLOCKS, dtype=w.dtype)
    full = jnp.einsum('dnij,nm->dnimj', w, eye).reshape(dirs, LRU_WIDTH, LRU_WIDTH)
    nl = LRU_WIDTH // LRU_LANES
    return jnp.stack([full[:, j * LRU_LANES:(j + 1) * LRU_LANES, j * LRU_LANES:(j + 1) * LRU_LANES]
                      for j in range(nl)], axis=1)


def kernel(x, c, ctx, c_ctx, mod_w, mod_b, ab_w_in, ab_w_out, diff_lambda_q, diff_lambda_k, diff_subln, lru_conv_w, lru_conv_b, lru_w_a, lru_b_a, lru_w_x, lru_b_x, lru_lambda, cd_w_in, cd_w_out, gqa_q_norm, gqa_k_norm, mla_q_norm, mla_kv_norm, mla_w_uq, mla_w_ukv, router_w, router_bias, exp_w_gate, exp_w_up, exp_w_down, sh_w_gate, sh_w_up, sh_w_down, final_norm):
    bsz, n_lat, d = x.shape
    n_ctx = ctx.shape[1]
    m = n_lat + n_ctx
    depth = mod_w.shape[0]
    assert depth == 2 and bsz < MOD_ROWS
    assert n_lat % ROW_TILE == 0 and n_ctx % ROW_TILE == 0 and n_lat % GRID_W == 0
    ctx_row = bsz

    xs = jnp.concatenate([x, ctx], axis=1)
    c_all = jnp.concatenate([c, c_ctx[None, :], jnp.zeros((MOD_ROWS - bsz - 1, d), F32)], axis=0)
    mods = _modulation(c_all, mod_w, mod_b)

    tabs64 = _rope_tables(n_lat, m, DIFF_HEAD_DIM, DIFF_HEADS * 2 * DIFF_HEAD_DIM)
    tabs_m = _mla_tables(_rope_tables(n_lat, m, MLA_ROPE, MLA_ROPE))

    exp_w = tuple(w.astype(BF16) for w in (exp_w_gate, exp_w_up, exp_w_down))

    def moe_weights(i):
        return exp_w + (sh_w_gate[i], sh_w_up[i], sh_w_down[i])

    fn = final_norm.reshape(1, d)

    mod = mods[0]
    qt, k, vt, u, g = _proj_ab(xs, mod, ab_w_in[0].astype(BF16), tabs64, n_lat, ctx_row)
    lambda_init = 0.8 - 0.6 * math.exp(-0.3 * 0)
    dv = 2 * DIFF_HEAD_DIM + NORM_ROWS
    jobs = tuple(_Job((2 * h + j) * QK_SLOT, h, j, h * dv, dv) for h in range(DIFF_HEADS) for j in range(2))
    o_att = _attention(qt, k, vt, (diff_lambda_q[0], diff_lambda_k[0], diff_subln[0].reshape(-1, 1)),
                       jobs, n_lat, lambda_init, "diff_attention")
    o_rec = _lru(u, g, lru_conv_w[0], lru_conv_b[0].reshape(1, -1),
                 _block_diag(lru_w_a[0]).astype(BF16), lru_b_a[0].reshape(2, 1, LRU_WIDTH),
                 _block_diag(lru_w_x[0]).astype(BF16), lru_b_x[0].reshape(2, 1, LRU_WIDTH),
                 lru_lambda[0].reshape(2, 1, LRU_WIDTH), n_lat)
    xs, h2, gates = _out_proj(o_att, o_rec, xs, mod, ab_w_out[0].astype(BF16),
                              router_w[0].T, router_bias[0].reshape(-1, 1), n_lat, ctx_row)
    xs = _moe(h2, gates, xs, mod, *moe_weights(0), fn, 0, n_lat, False, cap=128)

    mod = mods[1]
    w_in = cd_w_in[0]
    wq, wk = GQA_HEADS * GQA_HEAD_DIM, GQA_KV_HEADS * GQA_HEAD_DIM
    o_kr = wq + 2 * wk + MLA_Q_RANK + MLA_KV_RANK
    kr_slots = jnp.concatenate([jnp.zeros((d, MLA_NOPE), F32), w_in[:, o_kr:o_kr + MLA_ROPE],
                                jnp.zeros((d, QK_SLOT - MLA_QK), F32)], axis=1)
    w_in = jnp.concatenate([w_in[:, :o_kr], jnp.tile(kr_slots, (1, MLA_HEADS))], axis=1).astype(BF16)
    ukv = mla_w_ukv[0].reshape(MLA_KV_RANK, MLA_HEADS, MLA_NOPE + MLA_V)
    w_kn = _slots(ukv[:, :, :MLA_NOPE].reshape(MLA_KV_RANK, -1), MLA_HEADS, MLA_NOPE).astype(BF16)
    w_v = ukv[:, :, MLA_NOPE:].reshape(MLA_KV_RANK, MLA_HEADS * MLA_V).astype(BF16)
    w_uq = _slots(mla_w_uq[0], MLA_HEADS, MLA_QK).astype(BF16)
    lane = jnp.arange(wq)
    ones_bd = (lane[:, None] // GQA_HEAD_DIM == lane[None, :] // GQA_HEAD_DIM).astype(BF16)
    qgt, kg, vgt, qmt, km, vmt = _proj_cd(
        xs, mod, w_in, ones_bd, jnp.tile(gqa_q_norm[0], GQA_HEADS).reshape(1, -1),
        jnp.tile(gqa_k_norm[0], GQA_KV_HEADS).reshape(1, -1), mla_q_norm[0].reshape(1, -1),
        mla_kv_norm[0].reshape(1, -1), w_uq, w_kn, w_v, tabs64, tabs_m, n_lat, ctx_row)
    groups = GQA_HEADS // GQA_KV_HEADS
    dvn = GQA_HEAD_DIM + NORM_ROWS
    jobs = tuple(_Job(h * QK_SLOT, 0, h // groups, (h // groups) * dvn, dvn) for h in range(GQA_HEADS))
    o_gqa = _attention(qgt, kg, vgt, (), jobs, n_lat, None, "gqa_attention")
    dvn = MLA_V + NORM_ROWS
    jobs = tuple(_Job(h * QK_SLOT, h // 2, h % 2, h * dvn, dvn) for h in range(MLA_HEADS))
    o_mla = _attention(qmt, km, vmt, (), jobs, n_lat, None, "mla_attention")
    xs, h2, gates = _out_proj(o_gqa, o_mla, xs, mod, cd_w_out[0].astype(BF16),
                              router_w[1].T, router_bias[1].reshape(-1, 1), n_lat, ctx_row)
    return _moe(h2, gates, xs, mod, *moe_weights(1), fn, 1, n_lat, True, cap=192)
```

```python
import functools
import math
from typing import NamedTuple

import jax
import jax.numpy as jnp
from jax import lax
from jax.experimental import pallas as pl
from jax.experimental.pallas import tpu as pltpu

F32 = jnp.float32
BF16 = jnp.bfloat16

GRID_W = 64
ROPE_THETA = 10000.0
EPS = 1e-6
DIFF_HEADS = 4
DIFF_HEAD_DIM = 64
LRU_WIDTH = 512
LRU_BLOCKS = 8
LRU_BW = LRU_WIDTH // LRU_BLOCKS
CONV_W = 4
LRU_C = 8.0
GQA_HEADS = 8
GQA_KV_HEADS = 2
GQA_HEAD_DIM = 64
MLA_HEADS = 8
MLA_Q_RANK = 256
MLA_KV_RANK = 128
MLA_NOPE = 32
MLA_ROPE = 16
MLA_V = 64
MLA_QK = MLA_NOPE + MLA_ROPE
N_EXPERTS = 64
N_GROUPS = 8
PER_GROUP = N_EXPERTS // N_GROUPS
TOPK_GROUPS = 4
TOP_K = 8
EXPERT_FF = 256
ROUTED_SCALE = 2.5

QK_SLOT = 64
NORM_ROWS = 16
LOG2E = math.log2(math.e)

ROW_TILE = 256
LRU_CHUNK = 128
LRU_LANES = 256
EXPERTS_PER_STEP = 4
LANE = 128
SUBLANE = 8
MOD_ROWS = 8
VMEM_LIMIT = 56 * 1024 * 1024
MOE_VMEM_LIMIT = 62 * 1024 * 1024
MOE_TILE = 1024
MOE_CAP = 160
SEG_ALIGN = 16
MOE_SLOTS = MOE_TILE * TOP_K + N_EXPERTS * SEG_ALIGN
COMBINE_BLOCK = 1024
GROUP_UNROLL = 4


def _params(sem, vmem=VMEM_LIMIT):
    return pltpu.CompilerParams(dimension_semantics=sem, vmem_limit_bytes=vmem)


def _silu(x):
    return x * jax.nn.sigmoid(x)


def _rms(x):
    return x * lax.rsqrt(jnp.mean(x * x, axis=-1, keepdims=True) + EPS)


def _mod_rows(mod_ref, ctx_row, b, k, d, row0, rows, n_lat):
    lat = mod_ref[pl.ds(b, 1), k * d:(k + 1) * d]
    ctx = mod_ref[ctx_row:ctx_row + 1, k * d:(k + 1) * d]
    if n_lat % rows == 0:
        return jnp.where(row0 >= n_lat, ctx, lat)
    r = row0 + lax.broadcasted_iota(jnp.int32, (rows, 1), 0)
    return jnp.where(r >= n_lat, ctx, lat)


def _values_t(v, heads):
    vt = v.T
    dv = vt.shape[0] // heads
    ones = jnp.ones((NORM_ROWS, vt.shape[1]), F32)
    return jnp.concatenate([piece for h in range(heads) for piece in (vt[h * dv:(h + 1) * dv], ones)],
                           axis=0).astype(BF16)


def _rope(x, cos, sin_lo, sin_hi, half):
    n = x.shape[-1]
    return x * cos + pltpu.roll(x, n - half, 1) * sin_lo + pltpu.roll(x, half, 1) * sin_hi


def _mod_kernel(c_ref, w_ref, b_ref, o_ref):
    s = _silu(c_ref[...])
    o_ref[0] = jnp.dot(s, w_ref[0], preferred_element_type=F32, precision=lax.Precision.HIGHEST) + b_ref[0]


def _modulation(c_all, mod_w, mod_b):
    depth, d, n = mod_w.shape
    tn = n // 4
    return pl.pallas_call(
        _mod_kernel,
        grid=(depth, n // tn),
        in_specs=[pl.BlockSpec((MOD_ROWS, d), lambda i, j: (0, 0)),
                  pl.BlockSpec((1, d, tn), lambda i, j: (i, 0, j)),
                  pl.BlockSpec((1, 1, tn), lambda i, j: (i, 0, j))],
        out_specs=pl.BlockSpec((1, MOD_ROWS, tn), lambda i, j: (i, 0, j)),
        out_shape=jax.ShapeDtypeStruct((depth, MOD_ROWS, n), F32),
        compiler_params=_params(("arbitrary", "arbitrary")),
        name="modulation",
    )(c_all, mod_w, mod_b.reshape(depth, 1, n))


def _proj_ab_kernel(x_ref, mod_ref, w_ref, cos_ref, slo_ref, shi_ref,
                    qt_ref, k_ref, vt_ref, u_ref, g_ref, *, n_lat, ctx_row):
    b, t = pl.program_id(0), pl.program_id(1)
    d = x_ref.shape[-1]
    row0 = t * ROW_TILE
    shift = _mod_rows(mod_ref, ctx_row, b, 0, d, row0, ROW_TILE, n_lat)
    scale = _mod_rows(mod_ref, ctx_row, b, 1, d, row0, ROW_TILE, n_lat)
    h = (_rms(x_ref[0]) * (1.0 + scale) + shift).astype(BF16)
    acc = jnp.dot(h, w_ref[...], preferred_element_type=F32)
    w = DIFF_HEADS * 2 * DIFF_HEAD_DIM
    cos, slo, shi = (jnp.tile(r[...], (1, w // LANE)) for r in (cos_ref, slo_ref, shi_ref))
    q = _rope(acc[:, 0:w], cos, slo, shi, DIFF_HEAD_DIM // 4) * (DIFF_HEAD_DIM ** -0.5 * LOG2E)
    k = _rope(acc[:, w:2 * w], cos, slo, shi, DIFF_HEAD_DIM // 4)
    qt_ref[0, 0] = q.T.astype(BF16)
    k_ref[0] = k.astype(BF16)
    vt_ref[0] = _values_t(acc[:, 2 * w:3 * w], DIFF_HEADS)
    u_ref[0] = acc[:, 3 * w:3 * w + LRU_WIDTH]
    g_ref[0] = jax.nn.gelu(acc[:, 3 * w + LRU_WIDTH:3 * w + 2 * LRU_WIDTH]).astype(BF16)


def _proj_ab(xs, mod, w_in, tabs, n_lat, ctx_row):
    bsz, m, d = xs.shape
    nt = m // ROW_TILE
    w = DIFF_HEADS * 2 * DIFF_HEAD_DIM
    n_in = w_in.shape[1]
    tile = lambda n: pl.BlockSpec((1, ROW_TILE, n), lambda b, t: (b, t, 0))
    ttile = pl.BlockSpec((1, 1, w, ROW_TILE), lambda b, t: (b, t, 0, 0))
    wv = w + DIFF_HEADS * NORM_ROWS
    vtile = pl.BlockSpec((1, wv, ROW_TILE), lambda b, t: (b, 0, t))
    tab = pl.BlockSpec((ROW_TILE, LANE), lambda b, t: (t, 0))
    return pl.pallas_call(
        functools.partial(_proj_ab_kernel, n_lat=n_lat, ctx_row=ctx_row),
        grid=(bsz, nt),
        in_specs=[tile(d),
                  pl.BlockSpec(mod.shape, lambda b, t: (0, 0)),
                  pl.BlockSpec((d, n_in), lambda b, t: (0, 0)),
                  tab, tab, tab],
        out_specs=[ttile, tile(w), vtile, tile(LRU_WIDTH), tile(LRU_WIDTH)],
        out_shape=[jax.ShapeDtypeStruct((bsz, nt, w, ROW_TILE), BF16),
                   jax.ShapeDtypeStruct((bsz, m, w), BF16),
                   jax.ShapeDtypeStruct((bsz, wv, m), BF16),
                   jax.ShapeDtypeStruct((bsz, m, LRU_WIDTH), F32),
                   jax.ShapeDtypeStruct((bsz, m, LRU_WIDTH), BF16)],
        compiler_params=_params(("parallel", "parallel")),
        name="proj_ab",
    )(xs, mod, w_in, *tabs)


def _group_rms(x, ones_bd):
    x2 = x * x
    hi = x2.astype(BF16)
    lo = (x2 - hi.astype(F32)).astype(BF16)
    ss = jnp.dot(hi, ones_bd, preferred_element_type=F32) + jnp.dot(lo, ones_bd, preferred_element_type=F32)
    return x * lax.rsqrt(ss * (1.0 / GQA_HEAD_DIM) + EPS)


def _proj_cd_kernel(x_ref, mod_ref, w_ref, bd_ref, qn_ref, kn_ref, cqn_ref, ckvn_ref, wuq_ref, wkn_ref, wv_ref,
                    cos_ref, slo_ref, shi_ref, cosm_ref, slom_ref, shim_ref,
                    qgt_ref, kg_ref, vgt_ref, qmt_ref, km_ref, vmt_ref, *, n_lat, ctx_row):
    b, t = pl.program_id(0), pl.program_id(1)
    d = x_ref.shape[-1]
    row0 = t * ROW_TILE
    shift = _mod_rows(mod_ref, ctx_row, b, 0, d, row0, ROW_TILE, n_lat)
    scale = _mod_rows(mod_ref, ctx_row, b, 1, d, row0, ROW_TILE, n_lat)
    h = (_rms(x_ref[0]) * (1.0 + scale) + shift).astype(BF16)
    acc = jnp.dot(h, w_ref[...], preferred_element_type=F32)
    wq = GQA_HEADS * GQA_HEAD_DIM
    wk = GQA_KV_HEADS * GQA_HEAD_DIM
    o = 0
    q = acc[:, o:o + wq]; o += wq
    k = acc[:, o:o + wk]; o += wk
    v = acc[:, o:o + wk]; o += wk
    cq = acc[:, o:o + MLA_Q_RANK]; o += MLA_Q_RANK
    ckv = acc[:, o:o + MLA_KV_RANK]; o += MLA_KV_RANK
    kr = acc[:, o:o + MLA_HEADS * QK_SLOT]
    cos, slo, shi = (jnp.tile(r[...], (1, wq // LANE)) for r in (cos_ref, slo_ref, shi_ref))
    bd = bd_ref[...]
    q = (_rope(_group_rms(q, bd) * qn_ref[...], cos, slo, shi, GQA_HEAD_DIM // 4)
         * (GQA_HEAD_DIM ** -0.5 * LOG2E))
    k = _rope(_group_rms(k, bd[:wk, :wk]) * kn_ref[...], cos[:, :wk], slo[:, :wk], shi[:, :wk], GQA_HEAD_DIM // 4)
    qgt_ref[0, 0] = q.T.astype(BF16)
    kg_ref[0] = k.astype(BF16)
    vgt_ref[0] = _values_t(v, GQA_KV_HEADS)
    cosm, slom, shim = (jnp.tile(r[...], (1, MLA_HEADS * QK_SLOT // LANE)) for r in (cosm_ref, slom_ref, shim_ref))
    cqn = (_rms(cq) * cqn_ref[...]).astype(BF16)
    mq = jnp.dot(cqn, wuq_ref[...], preferred_element_type=F32)
    qmt_ref[0, 0] = (_rope(mq, cosm, slom, shim, MLA_ROPE // 4) * (MLA_QK ** -0.5 * LOG2E)).T.astype(BF16)
    ckvn = (_rms(ckv) * ckvn_ref[...]).astype(BF16)
    km = jnp.dot(ckvn, wkn_ref[...], preferred_element_type=F32) + _rope(kr, cosm, slom, shim, MLA_ROPE // 4)
    km_ref[0] = km.astype(BF16)
    vmt_ref[0] = _values_t(jnp.dot(ckvn, wv_ref[...], preferred_element_type=F32), MLA_HEADS)


def _proj_cd(xs, mod, w_in, bd, qn, kn, cqn, ckvn, wuq, wkn, wv, tabs, tabs_m, n_lat, ctx_row):
    bsz, m, d = xs.shape
    nt = m // ROW_TILE
    wq = GQA_HEADS * GQA_HEAD_DIM
    wk = GQA_KV_HEADS * GQA_HEAD_DIM
    wm = MLA_HEADS * QK_SLOT
    wmv = MLA_HEADS * (MLA_V + NORM_ROWS)
    wgv = GQA_KV_HEADS * (GQA_HEAD_DIM + NORM_ROWS)
    tile = lambda n: pl.BlockSpec((1, ROW_TILE, n), lambda b, t: (b, t, 0))
    ttile = lambda n: pl.BlockSpec((1, 1, n, ROW_TILE), lambda b, t: (b, t, 0, 0))
    vtile = lambda n: pl.BlockSpec((1, n, ROW_TILE), lambda b, t: (b, 0, t))
    whole = lambda a: pl.BlockSpec(a.shape, lambda b, t: (0,) * a.ndim)
    tab = pl.BlockSpec((ROW_TILE, LANE), lambda b, t: (t, 0))
    tabm = tab
    return pl.pallas_call(
        functools.partial(_proj_cd_kernel, n_lat=n_lat, ctx_row=ctx_row),
        grid=(bsz, nt),
        in_specs=[tile(d), whole(mod), whole(w_in), whole(bd), whole(qn), whole(kn), whole(cqn), whole(ckvn),
                  whole(wuq), whole(wkn), whole(wv), tab, tab, tab, tabm, tabm, tabm],
        out_specs=[ttile(wq), tile(wk), vtile(wgv), ttile(wm), tile(wm), vtile(wmv)],
        out_shape=[jax.ShapeDtypeStruct((bsz, nt, wq, ROW_TILE), BF16),
                   jax.ShapeDtypeStruct((bsz, m, wk), BF16),
                   jax.ShapeDtypeStruct((bsz, wgv, m), BF16),
                   jax.ShapeDtypeStruct((bsz, nt, wm, ROW_TILE), BF16),
                   jax.ShapeDtypeStruct((bsz, m, wm), BF16),
                   jax.ShapeDtypeStruct((bsz, wmv, m), BF16)],
        compiler_params=_params(("parallel", "parallel")),
        name="proj_cd",
    )(xs, mod, w_in, bd, qn, kn, cqn, ckvn, wuq, wkn, wv, *tabs, *tabs_m)


class _Job(NamedTuple):
    q_row: int
    k_group: int
    k_half: int
    v_row: int
    dv: int


def _attend(qt_ref, qnext_ref, k_ref, vt_ref, s_ref, mp_ref, jobs, lo, hi, chained):
    assert len(jobs) % 2 == 0
    tq = qt_ref.shape[-1]
    n = hi - lo
    group = next(g for g in (11, 3, 2, 1) if n % g == 0)
    steps = n // group
    row_groups = ROW_TILE // SUBLANE

    def weights(job, ref=qt_ref):
        qh = ref[0, 0, job.q_row:job.q_row + QK_SLOT, :]
        z = jnp.zeros_like(qh)
        return jnp.concatenate([z, qh] if job.k_half else [qh, z], axis=0)

    def score(c0, buf, qw, job, mp):
        r0 = pl.multiple_of(c0 * ROW_TILE, ROW_TILE)
        kc = k_ref[0, pl.ds(r0, group * ROW_TILE), job.k_group * LANE:(job.k_group + 1) * LANE]
        s_all = jnp.dot(kc, qw, preferred_element_type=F32)
        for g in range(group):
            s = s_all[g * ROW_TILE:(g + 1) * ROW_TILE]
            s_ref[buf, c0 + g] = s
            mp = jnp.maximum(mp, jnp.max(s.reshape(row_groups, SUBLANE, tq), axis=0))
        return mp

    def value(c0, buf, m, job, acc):
        ps = [jnp.exp2(s_ref[buf, c0 + g] - m).astype(BF16) for g in range(group)]
        k0 = pl.multiple_of(c0 * ROW_TILE, ROW_TILE)
        v = vt_ref[0, job.v_row:job.v_row + job.dv, pl.ds(k0, group * ROW_TILE)]
        return acc + jnp.dot(v, jnp.concatenate(ps, axis=0), preferred_element_type=F32)

    def loop(body, init):
        return body(0, init) if steps == 1 else lax.fori_loop(0, steps, body, init)

    neg = jnp.full((SUBLANE, tq), -jnp.inf, F32)

    def prologue():
        qw = weights(jobs[0])

        def first(i, mp):
            return score(lo + i * group, 0, qw, jobs[0], mp)

        mp_ref[...] = loop(first, neg)

    if chained:
        pl.when(pl.program_id(1) == 0)(prologue)
    else:
        prologue()
    mp = mp_ref[...]
    results = []
    for j, job in enumerate(jobs):
        buf = j % 2
        m = jnp.max(mp, axis=0, keepdims=True)
        if j + 1 < len(jobs):
            nxt, qw = jobs[j + 1], weights(jobs[j + 1])
        elif chained:
            nxt, qw = jobs[0], weights(jobs[0], qnext_ref)
        else:
            nxt = None

        def body(i, carry):
            mp, acc = carry
            if nxt is not None:
                mp = score(lo + i * group, 1 - buf, qw, nxt, mp)
            return mp, value(lo + i * group, buf, m, job, acc)

        mp, acc = loop(body, (neg, jnp.zeros((job.dv, tq), F32)))
        dv = job.dv - NORM_ROWS
        results.append((acc[:dv], acc[dv:dv + 1]))
    if chained:
        mp_ref[...] = mp
    return results


def _attn_kernel(*refs, jobs, n_lat, lambda_init):
    if lambda_init is None:
        qt_ref, qnext_ref, k_ref, vt_ref, o_ref, s_ref, mp_ref = refs
    else:
        qt_ref, qnext_ref, k_ref, vt_ref, lq_ref, lk_ref, sub_ref, o_ref, s_ref, mp_ref = refs
    n_chunks = k_ref.shape[1] // ROW_TILE
    first_ctx = n_lat // ROW_TILE
    t = pl.program_id(1)

    def run(lo, hi, chained):
        res = _attend(qt_ref, qnext_ref, k_ref, vt_ref, s_ref, mp_ref, jobs, lo, hi, chained)
        if lambda_init is None:
            outs = [acc / l for acc, l in res]
        else:
            lq, lk = lq_ref[...], lk_ref[...]
            lam = (jnp.exp(jnp.sum(lq[0:1] * lk[0:1], axis=-1, keepdims=True))
                   - jnp.exp(jnp.sum(lq[1:2] * lk[1:2], axis=-1, keepdims=True)) + lambda_init)
            outs = []
            for h in range(len(res) // 2):
                (a0, l0), (a1, l1) = res[2 * h], res[2 * h + 1]
                o = a0 / l0 - lam * (a1 / l1)
                o = o * lax.rsqrt(jnp.mean(o * o, axis=0, keepdims=True) + EPS)
                outs.append(o * sub_ref[...] * (1.0 - lambda_init))
        o_ref[0] = jnp.concatenate(outs, axis=0).T.astype(BF16)

    @pl.when(t < first_ctx)
    def _():
        run(0, n_chunks, True)

    @pl.when(t >= first_ctx)
    def _():
        run(first_ctx, n_chunks, False)


def _attention(qt, k, vt, extra, jobs, n_lat, lambda_init, name):
    bsz, nt, wq, tq = qt.shape
    m = k.shape[1]
    wo = sum(j.dv - NORM_ROWS for j in jobs) // (1 if lambda_init is None else 2)
    once = pl.Buffered(1)
    whole = lambda a: pl.BlockSpec(a.shape, lambda b, t: (0,) * a.ndim)
    return pl.pallas_call(
        functools.partial(_attn_kernel, jobs=jobs, n_lat=n_lat, lambda_init=lambda_init),
        grid=(bsz, nt),
        in_specs=[pl.BlockSpec((1, 1, wq, tq), lambda b, t: (b, t, 0, 0)),
                  pl.BlockSpec((1, 1, wq, tq), lambda b, t: (b, jnp.minimum(t + 1, nt - 1), 0, 0)),
                  pl.BlockSpec((1,) + k.shape[1:], lambda b, t: (b, 0, 0), pipeline_mode=once),
                  pl.BlockSpec((1,) + vt.shape[1:], lambda b, t: (b, 0, 0), pipeline_mode=once)]
                 + [whole(a) for a in extra],
        out_specs=pl.BlockSpec((1, tq, wo), lambda b, t: (b, t, 0)),
        out_shape=jax.ShapeDtypeStruct((bsz, m, wo), BF16),
        scratch_shapes=[pltpu.VMEM((2, nt, ROW_TILE, tq), F32), pltpu.VMEM((SUBLANE, tq), F32)],
        compiler_params=_params(("arbitrary", "arbitrary")),
        name=name,
    )(qt, qt, k, vt, *extra)


def _scan_chunk(a, b, carry, reverse):
    n = a.shape[0]
    rows = lax.broadcasted_iota(jnp.int32, (n, 1), 0)
    d = 1
    while d < n:
        sh = n - d if reverse else d
        valid = rows < n - d if reverse else rows >= d
        b = jnp.where(valid, a * pltpu.roll(b, sh, 0) + b, b)
        a = jnp.where(valid, a * pltpu.roll(a, sh, 0), a)
        d *= 2
    h = a * carry + b
    return h, (h[0:1] if reverse else h[n - 1:n])


def _lru_kernel(u_ref, g_ref, cw_ref, cb_ref, wa_ref, ba_ref, wx_ref, bx_ref, lam_ref, o_ref, up_ref, hf_ref, *, n_lat):
    m = u_ref.shape[1]
    lanes = u_ref.shape[2]
    n_chunks = m // LRU_CHUNK
    n_ctx_chunks = (m - n_lat) // LRU_CHUNK
    pad = SUBLANE

    up_ref[0:pad, :] = jnp.zeros((pad, lanes), F32)
    up_ref[pad + m:pad + m + pad, :] = jnp.zeros((pad, lanes), F32)

    def copy(c, _):
        r0 = pl.multiple_of(c * LRU_CHUNK, LRU_CHUNK)
        up_ref[pl.ds(pad + r0, LRU_CHUNK), :] = u_ref[0, pl.ds(r0, LRU_CHUNK), :]
        return 0

    lax.fori_loop(0, n_chunks, copy, 0)

    cw = cw_ref[...]
    cb = cb_ref[...]
    win_rows = LRU_CHUNK + 2 * pad

    def coeffs(c, d):
        r0 = pl.multiple_of(c * LRU_CHUNK, LRU_CHUNK)
        win = up_ref[pl.ds(r0, win_rows), :]
        r = r0 + lax.broadcasted_iota(jnp.int32, (LRU_CHUNK, 1), 0)
        at = lambda k: pltpu.roll(win, (win_rows - k) % win_rows, 0)[pad:pad + LRU_CHUNK]
        y = (cw[0:1] * jnp.where((r == n_lat) | (r == n_lat + 1), 0.0, at(-2))
             + cw[1:2] * jnp.where(r == n_lat, 0.0, at(-1))
             + cw[2:3] * win[pad:pad + LRU_CHUNK]
             + cw[3:4] * jnp.where(r == n_lat - 1, 0.0, at(1))
             + cb)
        yb = y.astype(BF16)
        ra = jax.nn.sigmoid(jnp.dot(yb, wa_ref[d, 0], preferred_element_type=F32) + ba_ref[d])
        ix = jax.nn.sigmoid(jnp.dot(yb, wx_ref[d, 0], preferred_element_type=F32) + bx_ref[d])
        z = -lam_ref[d]
        softplus = jnp.maximum(z, 0.0) + jnp.log(1.0 + jnp.exp(-jnp.abs(z)))
        a = jnp.exp(-LRU_C * ra * softplus)
        return a, jnp.sqrt(1.0 - a * a) * ix * y

    def fwd(s, carry):
        c = lax.rem(s + (n_chunks - n_ctx_chunks), n_chunks)
        a, b = coeffs(c, 0)
        h, carry = _scan_chunk(a, b, carry, False)
        r0 = pl.multiple_of(c * LRU_CHUNK, LRU_CHUNK)
        hf_ref[pl.ds(r0, LRU_CHUNK), :] = h
        return carry

    lax.fori_loop(0, n_chunks, fwd, jnp.zeros((1, lanes), F32))

    def bwd(s, carry):
        c = n_chunks - 1 - s
        a, b = coeffs(c, 1)
        h, carry = _scan_chunk(a, b, carry, True)
        r0 = pl.multiple_of(c * LRU_CHUNK, LRU_CHUNK)
        tot = hf_ref[pl.ds(r0, LRU_CHUNK), :] + h
        o_ref[0, pl.ds(r0, LRU_CHUNK), :] = (tot * g_ref[0, pl.ds(r0, LRU_CHUNK), :].astype(F32)).astype(BF16)
        return carry

    lax.fori_loop(0, n_chunks, bwd, jnp.zeros((1, lanes), F32))


def _lru(u, g, conv_w, conv_b, wa_bd, ba, wx_bd, bx, lam, n_lat):
    bsz, m, c = u.shape
    nl = c // LRU_LANES
    seq = lambda: pl.BlockSpec((1, m, LRU_LANES), lambda b, j: (b, 0, j))
    vec = lambda a: pl.BlockSpec(a.shape[:-1] + (LRU_LANES,), lambda b, j: (0,) * (a.ndim - 1) + (j,))
    mat = pl.BlockSpec((2, 1, LRU_LANES, LRU_LANES), lambda b, j: (0, j, 0, 0))
    return pl.pallas_call(
        functools.partial(_lru_kernel, n_lat=n_lat),
        grid=(bsz, nl),
        in_specs=[seq(), seq(), vec(conv_w), vec(conv_b), mat, vec(ba), mat, vec(bx), vec(lam)],
        out_specs=seq(),
        out_shape=jax.ShapeDtypeStruct((bsz, m, c), BF16),
        scratch_shapes=[pltpu.VMEM((m + 2 * SUBLANE, LRU_LANES), F32), pltpu.VMEM((m, LRU_LANES), F32)],
        compiler_params=_params(("parallel", "parallel")),
        name="rglru",
    )(u, g, conv_w, conv_b, wa_bd, ba, wx_bd, bx, lam)


def _route(logits_t, bias):
    n = logits_t.shape[-1]
    scores = jax.nn.sigmoid(logits_t)
    choice = scores + bias
    sub = lax.broadcasted_iota(jnp.int32, (PER_GROUP, n), 0)
    neg = -jnp.inf
    groups, gs = [], []
    for g in range(N_GROUPS):
        cg = choice[g * PER_GROUP:(g + 1) * PER_GROUP]
        m1 = jnp.max(cg, axis=0, keepdims=True)
        i1 = jnp.min(jnp.where(cg == m1, sub, PER_GROUP), axis=0, keepdims=True)
        m2 = jnp.max(jnp.where(sub == i1, neg, cg), axis=0, keepdims=True)
        groups.append(cg)
        gs.append(m1 + m2)
    masked = []
    for g in range(N_GROUPS):
        rank = jnp.zeros((1, n), jnp.int32)
        for o in range(N_GROUPS):
            if o != g:
                ahead = (gs[o] >= gs[g]) if o < g else (gs[o] > gs[g])
                rank = rank + jnp.where(ahead, 1, 0)
        masked.append(jnp.where(rank < TOPK_GROUPS, groups[g], neg))
    masked = jnp.concatenate(masked, axis=0)
    eidx = lax.broadcasted_iota(jnp.int32, (N_EXPERTS, n), 0)
    picked = jnp.zeros((N_EXPERTS, n), F32)
    for _ in range(TOP_K):
        mx = jnp.max(masked, axis=0, keepdims=True)
        first = jnp.min(jnp.where(masked == mx, eidx, N_EXPERTS), axis=0, keepdims=True)
        hit = eidx == first
        picked = jnp.where(hit, 1.0, picked)
        masked = jnp.where(hit, neg, masked)
    w = picked * scores
    return w / jnp.sum(w, axis=0, keepdims=True) * ROUTED_SCALE


def _out_proj_kernel(oa_ref, ob_ref, x_ref, mod_ref, wa_ref, wb_ref, rwh_ref, rwl_ref, rb_ref,
                     xo_ref, h_ref, gates_ref, *, n_lat, ctx_row):
    b, t = pl.program_id(0), pl.program_id(1)
    d = x_ref.shape[-1]
    row0 = t * ROW_TILE
    mv = lambda k: _mod_rows(mod_ref, ctx_row, b, k, d, row0, ROW_TILE, n_lat)
    y = (jnp.dot(oa_ref[0], wa_ref[...], preferred_element_type=F32)
         + jnp.dot(ob_ref[0], wb_ref[...], preferred_element_type=F32))
    x = x_ref[0] + mv(2) * y
    xo_ref[0] = x
    h = _rms(x) * (1.0 + mv(4)) + mv(3)
    hb = h.astype(BF16)
    h_ref[0] = hb
    h_lo = (h - hb.astype(F32)).astype(BF16)
    nt = lambda a, b: lax.dot_general(a, b, (((1,), (1,)), ((), ())), preferred_element_type=F32)
    logits_t = nt(rwh_ref[...], hb) + (nt(rwh_ref[...], h_lo) + nt(rwl_ref[...], hb))
    gates_t = _route(logits_t, rb_ref[...])
    gates_t = jnp.concatenate([gates_t, jnp.zeros((LANE - N_EXPERTS, ROW_TILE), F32)], axis=0)
    gates_ref[0] = gates_t.T


def _out_proj(oa, ob, xs, mod, w_out, router_wt, router_b, n_lat, ctx_row):
    bsz, m, d = xs.shape
    na, nb = oa.shape[-1], ob.shape[-1]
    wa, wb = w_out[:na], w_out[na:]
    rw_hi = router_wt.astype(BF16)
    rw_lo = (router_wt - rw_hi.astype(F32)).astype(BF16)
    tile = lambda n: pl.BlockSpec((1, ROW_TILE, n), lambda b, t: (b, t, 0))
    whole = lambda a: pl.BlockSpec(a.shape, lambda b, t: (0,) * a.ndim)
    return pl.pallas_call(
        functools.partial(_out_proj_kernel, n_lat=n_lat, ctx_row=ctx_row),
        grid=(bsz, m // ROW_TILE),
        in_specs=[tile(na), tile(nb), tile(d), whole(mod), whole(wa), whole(wb), whole(rw_hi), whole(rw_lo),
                  whole(router_b)],
        out_specs=[tile(d), tile(d), tile(LANE)],
        out_shape=[jax.ShapeDtypeStruct((bsz, m, d), F32),
                   jax.ShapeDtypeStruct((bsz, m, d), BF16),
                   jax.ShapeDtypeStruct((bsz, m, LANE), F32)],
        compiler_params=_params(("parallel", "parallel")),
        name="out_proj_router",
    )(oa, ob, xs, mod, wa, wb, rw_hi, rw_lo, router_b)


def _plan_kernel(gates_ref, tri_ref, upper_ref, pos_ref, gem_ref, cnt_ref, off_ref, own_ref):
    g = gates_ref[0]
    sel = g != 0.0
    one = jnp.where(sel, 1.0, 0.0)
    pos = jnp.dot(tri_ref[...], one.astype(BF16), preferred_element_type=F32)
    cnt = jnp.sum(one, axis=0, keepdims=True)
    cpad = jnp.floor((cnt + (SEG_ALIGN - 1)) * (1.0 / SEG_ALIGN)) * SEG_ALIGN
    off = jnp.dot(jnp.broadcast_to(cpad, (SUBLANE, LANE)), upper_ref[...], preferred_element_type=F32,
                  precision=lax.Precision.HIGHEST)[0:1]
    pos_ref[0] = jnp.where(sel, pos, -1.0).T
    gem_ref[0] = g.T
    cnt_ref[0] = jnp.broadcast_to(cnt, (SUBLANE, LANE)).astype(jnp.int32)
    off_ref[0] = jnp.broadcast_to(off, (SUBLANE, LANE)).astype(jnp.int32)
    groups = own_ref.shape[-1]
    ends = jnp.broadcast_to(off + cpad, (LANE, LANE)).T
    ends = jnp.concatenate([ends] * (groups // LANE), axis=1)
    start = lax.broadcasted_iota(jnp.int32, (LANE, groups), 1).astype(F32) * SEG_ALIGN
    real = lax.broadcasted_iota(jnp.int32, (LANE, groups), 0) < N_EXPERTS
    before = jnp.where(real, jnp.where(ends <= start, 1.0, 0.0), 0.0)
    owner = jnp.minimum(jnp.sum(before, axis=0, keepdims=True), N_EXPERTS - 1.0)
    own_ref[0] = jnp.broadcast_to(owner, (SUBLANE, groups)).astype(jnp.int32)


def _moe_kernel(cnt_s, off_s, own_s, h_ref, pos_ref, gem_ref, wg_ref, wu_ref, wd_ref,
                sg_ref, su_ref, sd_ref, x_ref, mod_ref, fn_ref, o_ref, y_ref, p_ref,
                *, bsz, rows_per_batch, n_lat, tiles_per_batch, final):
    i, s = pl.program_id(0), pl.program_id(1)
    rows, d = h_ref.shape[1], h_ref.shape[2]
    h = h_ref[0]

    def ffn(x, wg, wu, wd, gate):
        a = _silu(jnp.dot(x, wg, preferred_element_type=F32)) * jnp.dot(x, wu, preferred_element_type=F32)
        if gate is not None:
            a = a * gate
        return jnp.dot(a.astype(BF16), wd, preferred_element_type=F32)

    @pl.when((s == 0) & (i == 0))
    def _():
        y_ref[...] = jnp.zeros(y_ref.shape, BF16)

    @pl.when(s == 0)
    def _():
        o_ref[0] = ffn(h, sg_ref[...].astype(BF16), su_ref[...].astype(BF16), sd_ref[...].astype(BF16), None)

    cap = MOE_CAP
    cap_rows = lax.broadcasted_iota(jnp.int32, (cap, 1), 0)

    def onehot(pos_row, q):
        ranks = cap_rows + q * cap
        hit = pos_row == ranks.astype(F32)
        return ranks, hit, jnp.where(hit, 1.0, 0.0).astype(BF16)

    def expert_rows(x, ranks, hit, q, gate_row, w, cnt, off):
        gate = jnp.sum(jnp.where(hit, gate_row, 0.0), axis=1, keepdims=True)
        y = ffn(x, *w, gate)
        dst = pl.multiple_of(off + q * cap, SEG_ALIGN)
        y_ref[pl.ds(dst, cap), :] = jnp.where(ranks < cnt, y.astype(BF16), y_ref[pl.ds(dst, cap), :])

    experts = []
    for j in range(EXPERTS_PER_STEP):
        e = s * EXPERTS_PER_STEP + j
        experts.append((cnt_s[i, e], off_s[i, e],
                        pos_ref[0, pl.ds(e, 1), :],
                        gem_ref[0, pl.ds(e, 1), :],
                        (wg_ref[j], wu_ref[j], wd_ref[j])))

    hots = [onehot(pos_row, 0) for _, _, pos_row, _, _ in experts]
    xs = jnp.dot(jnp.concatenate([p for _, _, p in hots], axis=0), h, preferred_element_type=F32).astype(BF16)
    for j, ((cnt, off, _, gate_row, w), (ranks, hit, _)) in enumerate(zip(experts, hots)):
        expert_rows(xs[j * cap:(j + 1) * cap], ranks, hit, 0, gate_row, w, cnt, off)

    for cnt, off, pos_row, gate_row, w in experts:
        def block(q, carry):
            ranks, hit, p = onehot(pos_row, q)
            x = jnp.dot(p, h, preferred_element_type=F32).astype(BF16)
            expert_rows(x, ranks, hit, q, gate_row, w, cnt, off)
            return carry

        lax.fori_loop(1, (cnt + cap - 1) // cap, block, 0)

    @pl.when(s == N_EXPERTS // EXPERTS_PER_STEP - 1)
    def _():
        group_rows = lax.broadcasted_iota(jnp.int32, (SEG_ALIGN, 1), 0)
        groups_per_block = COMBINE_BLOCK // SEG_ALIGN

        def combine(r, carry):
            base = pl.multiple_of(r * COMBINE_BLOCK, COMBINE_BLOCK)

            def onehot_groups(c, carry):
                for u in range(GROUP_UNROLL):
                    gi = c * GROUP_UNROLL + u
                    g0 = pl.multiple_of(gi * SEG_ALIGN, SEG_ALIGN)
                    e = own_s[i, r * groups_per_block + gi]
                    rank = (group_rows + (base + g0 - off_s[i, e])).astype(F32)
                    hit = pos_ref[0, pl.ds(e, 1), :] == rank
                    p_ref[pl.ds(g0, SEG_ALIGN), :] = jnp.where(hit, 1.0, 0.0).astype(BF16)
                return carry

            lax.fori_loop(0, groups_per_block // GROUP_UNROLL, onehot_groups, 0)
            for n in range(d // ROW_TILE):
                cols = slice(n * ROW_TILE, (n + 1) * ROW_TILE)
                o_ref[0, :, cols] += lax.dot_general(p_ref[...], y_ref[pl.ds(base, COMBINE_BLOCK), cols],
                                                     (((0,), (0,)), ((), ())), preferred_element_type=F32)
            return carry

        lax.fori_loop(0, MOE_SLOTS // COMBINE_BLOCK, combine, 0)

        ctx = mod_ref[bsz:bsz + 1, 5 * d:6 * d]
        if tiles_per_batch is None:
            r = i * rows + lax.broadcasted_iota(jnp.int32, (rows, 1), 0)
            g2 = ctx
            for b in range(bsz):
                lo = b * rows_per_batch
                inside = jnp.where(r >= lo, jnp.where(r < lo + n_lat, 1, 0), 0)
                g2 = jnp.where(inside == 1, mod_ref[b:b + 1, 5 * d:6 * d], g2)
        else:
            g2 = mod_ref[pl.ds(i // tiles_per_batch, 1), 5 * d:6 * d]
        y = x_ref[0] + g2 * o_ref[0]
        if final:
            y = _rms(y) * fn_ref[...]
        o_ref[0] = y


def _moe(h, gates, xs, mod, wg, wu, wd, sg, su, sd, final_norm, layer, n_lat, final):
    bsz, m, d = xs.shape
    t = MOE_TILE
    if final:
        assert n_lat % t == 0
        tiles_per_batch = n_lat // t
        n_tiles = bsz * tiles_per_batch
        where = lambda i: (i // tiles_per_batch, i % tiles_per_batch, 0)
        out_shape = (bsz, n_lat, d)
    else:
        assert (bsz * m) % t == 0
        tiles_per_batch = None
        n_tiles = bsz * m // t
        where = lambda i: (0, i, 0)
        out_shape = (1, bsz * m, d)
        h, gates, xs = (a.reshape(1, bsz * m, a.shape[-1]) for a in (h, gates, xs))
    once = pl.Buffered(1)

    tok = jnp.arange(t)
    tri = (tok[None, :] < tok[:, None]).astype(BF16)
    lane = jnp.arange(LANE)
    upper = (lane[:, None] < lane[None, :]).astype(F32)
    groups = -(-MOE_SLOTS // SEG_ALIGN // LANE) * LANE
    em = jax.ShapeDtypeStruct((n_tiles, LANE, t), F32)
    ints = lambda n: jax.ShapeDtypeStruct((n_tiles, SUBLANE, n), jnp.int32)
    per_tile = lambda a, b: pl.BlockSpec((1, a, b), lambda i: (i, 0, 0))
    pos, gem, cnt, off, own = pl.pallas_call(
        _plan_kernel,
        grid=(n_tiles,),
        in_specs=[pl.BlockSpec((1, t, LANE), lambda i: where(i)),
                  pl.BlockSpec((t, t), lambda i: (0, 0)), pl.BlockSpec((LANE, LANE), lambda i: (0, 0))],
        out_specs=[per_tile(LANE, t), per_tile(LANE, t), per_tile(SUBLANE, LANE), per_tile(SUBLANE, LANE),
                   per_tile(SUBLANE, groups)],
        out_shape=[em, em, ints(LANE), ints(LANE), ints(groups)],
        compiler_params=_params(("parallel",)),
        name="moe_plan",
    )(gates, tri, upper)

    tile = lambda n: pl.BlockSpec((1, t, n), lambda i, s, *_: where(i), pipeline_mode=once)
    planned = lambda a, b: pl.BlockSpec((1, a, b), lambda i, s, *_: (i, 0, 0), pipeline_mode=once)
    whole = lambda a: pl.BlockSpec(a.shape, lambda i, s, *_: (0,) * a.ndim, pipeline_mode=once)
    experts = lambda a: pl.BlockSpec((None, EXPERTS_PER_STEP) + a.shape[2:], lambda i, s, *_: (layer, s, 0, 0))
    out = pl.pallas_call(
        functools.partial(_moe_kernel, bsz=bsz, rows_per_batch=m, n_lat=n_lat,
                          tiles_per_batch=tiles_per_batch, final=final),
        grid_spec=pltpu.PrefetchScalarGridSpec(
            num_scalar_prefetch=3,
            grid=(n_tiles, N_EXPERTS // EXPERTS_PER_STEP),
            in_specs=[tile(d), planned(LANE, t), planned(LANE, t),
                      experts(wg), experts(wu), experts(wd), whole(sg), whole(su), whole(sd),
                      tile(d), whole(mod), whole(final_norm)],
            out_specs=pl.BlockSpec((1, t, d), lambda i, s, *_: where(i)),
            scratch_shapes=[pltpu.VMEM((MOE_SLOTS + MOE_CAP, d), BF16), pltpu.VMEM((COMBINE_BLOCK, t), BF16)]),
        out_shape=jax.ShapeDtypeStruct(out_shape, F32),
        compiler_params=_params(("arbitrary", "arbitrary"), MOE_VMEM_LIMIT),
        name="moe_final" if final else "moe",
    )(cnt[:, 0], off[:, 0], own[:, 0], h, pos, gem, wg, wu, wd, sg, su, sd, xs, mod, final_norm)
    return out if final else out.reshape(bsz, m, d)


def _rope_tables(n_lat, m, head_dim, width):
    pos = jnp.arange(n_lat, dtype=jnp.int32)
    row = (pos // GRID_W).astype(F32)
    col = (pos % GRID_W).astype(F32)
    half = head_dim // 2
    quarter = half // 2
    lane = jnp.arange(head_dim)
    freq = (lane % quarter).astype(F32)
    inv = ROPE_THETA ** (-(2.0 * freq) / half)
    ang = jnp.where(lane[None, :] < half, row[:, None], col[:, None]) * inv[None, :]
    low = (lane % half) < quarter
    cos = jnp.cos(ang)
    sin = jnp.sin(ang)
    slo = jnp.where(low[None, :], -sin, 0.0)
    shi = jnp.where(low[None, :], 0.0, sin)
    reps = width // head_dim
    pad = lambda a, fill: jnp.concatenate(
        [jnp.tile(a, (1, reps)), jnp.full((m - n_lat, width), fill, F32)], axis=0)
    return pad(cos, 1.0), pad(slo, 0.0), pad(shi, 0.0)


def _mla_tables(tabs16):
    outs = []
    for a, fill in zip(tabs16, (1.0, 0.0, 0.0)):
        m = a.shape[0]
        slot = jnp.concatenate([jnp.full((m, MLA_NOPE), fill, F32), a,
                                jnp.full((m, QK_SLOT - MLA_QK), fill, F32)], axis=1)
        outs.append(jnp.tile(slot, (1, LANE // QK_SLOT)))
    return tuple(outs)


def _slots(w, heads, width):
    w = w.reshape(w.shape[0], heads, width)
    pad = jnp.zeros((w.shape[0], heads, QK_SLOT - width), w.dtype)
    return jnp.concatenate([w, pad], axis=-1).reshape(w.shape[0], heads * QK_SLOT)


def _block_diag(w):
    dirs = w.shape[0]
    eye = jnp.eye(LRU_BLOCKS, dtype=w.dtype)
    full = jnp.einsum('dnij,nm->dnimj', w, eye).reshape(dirs, LRU_WIDTH, LRU_WIDTH)
    nl = LRU_WIDTH // LRU_LANES
    return jnp.stack([full[:, j * LRU_LANES:(j + 1) * LRU_LANES, j * LRU_LANES:(j + 1) * LRU_LANES]
                      for j in range(nl)], axis=1)


def kernel(x, c, ctx, c_ctx, mod_w, mod_b, ab_w_in, ab_w_out, diff_lambda_q, diff_lambda_k, diff_subln, lru_conv_w, lru_conv_b, lru_w_a, lru_b_a, lru_w_x, lru_b_x, lru_lambda, cd_w_in, cd_w_out, gqa_q_norm, gqa_k_norm, mla_q_norm, mla_kv_norm, mla_w_uq, mla_w_ukv, router_w, router_bias, exp_w_gate, exp_w_up, exp_w_down, sh_w_gate, sh_w_up, sh_w_down, final_norm):
    bsz, n_lat, d = x.shape
    n_ctx = ctx.shape[1]
    m = n_lat + n_ctx
    depth = mod_w.shape[0]
    assert depth == 2 and bsz < MOD_ROWS
    assert n_lat % ROW_TILE == 0 and n_ctx % ROW_TILE == 0 and n_lat % GRID_W == 0
    ctx_row = bsz

    xs = jnp.concatenate([x, ctx], axis=1)
    c_all = jnp.concatenate([c, c_ctx[None, :], jnp.zeros((MOD_ROWS - bsz - 1, d), F32)], axis=0)
    mods = _modulation(c_all, mod_w, mod_b)

    tabs64 = _rope_tables(n_lat, m, DIFF_HEAD_DIM, LANE)
    tabs_m = _mla_tables(_rope_tables(n_lat, m, MLA_ROPE, MLA_ROPE))

    exp_w = tuple(w.astype(BF16) for w in (exp_w_gate, exp_w_up, exp_w_down))

    def moe_weights(i):
        return exp_w + (sh_w_gate[i], sh_w_up[i], sh_w_down[i])

    fn = final_norm.reshape(1, d)

    mod = mods[0]
    qt, k, vt, u, g = _proj_ab(xs, mod, ab_w_in[0].astype(BF16), tabs64, n_lat, ctx_row)
    lambda_init = 0.8 - 0.6 * math.exp(-0.3 * 0)
    dv = 2 * DIFF_HEAD_DIM + NORM_ROWS
    jobs = tuple(_Job((2 * h + j) * QK_SLOT, h, j, h * dv, dv) for h in range(DIFF_HEADS) for j in range(2))
    o_att = _attention(qt, k, vt, (diff_lambda_q[0], diff_lambda_k[0], diff_subln[0].reshape(-1, 1)),
                       jobs, n_lat, lambda_init, "diff_attention")
    o_rec = _lru(u, g, lru_conv_w[0], lru_conv_b[0].reshape(1, -1),
                 _block_diag(lru_w_a[0]).astype(BF16), lru_b_a[0].reshape(2, 1, LRU_WIDTH),
                 _block_diag(lru_w_x[0]).astype(BF16), lru_b_x[0].reshape(2, 1, LRU_WIDTH),
                 lru_lambda[0].reshape(2, 1, LRU_WIDTH), n_lat)
    xs, h2, gates = _out_proj(o_att, o_rec, xs, mod, ab_w_out[0].astype(BF16),
                              router_w[0].T, router_bias[0].reshape(-1, 1), n_lat, ctx_row)
    xs = _moe(h2, gates, xs, mod, *moe_weights(0), fn, 0, n_lat, False)

    mod = mods[1]
    w_in = cd_w_in[0]
    wq, wk = GQA_HEADS * GQA_HEAD_DIM, GQA_KV_HEADS * GQA_HEAD_DIM
    o_kr = wq + 2 * wk + MLA_Q_RANK + MLA_KV_RANK
    kr_slots = jnp.concatenate([jnp.zeros((d, MLA_NOPE), F32), w_in[:, o_kr:o_kr + MLA_ROPE],
                                jnp.zeros((d, QK_SLOT - MLA_QK), F32)], axis=1)
    w_in = jnp.concatenate([w_in[:, :o_kr], jnp.tile(kr_slots, (1, MLA_HEADS))], axis=1).astype(BF16)
    ukv = mla_w_ukv[0].reshape(MLA_KV_RANK, MLA_HEADS, MLA_NOPE + MLA_V)
    w_kn = _slots(ukv[:, :, :MLA_NOPE].reshape(MLA_KV_RANK, -1), MLA_HEADS, MLA_NOPE).astype(BF16)
    w_v = ukv[:, :, MLA_NOPE:].reshape(MLA_KV_RANK, MLA_HEADS * MLA_V).astype(BF16)
    w_uq = _slots(mla_w_uq[0], MLA_HEADS, MLA_QK).astype(BF16)
    lane = jnp.arange(wq)
    ones_bd = (lane[:, None] // GQA_HEAD_DIM == lane[None, :] // GQA_HEAD_DIM).astype(BF16)
    qgt, kg, vgt, qmt, km, vmt = _proj_cd(
        xs, mod, w_in, ones_bd, jnp.tile(gqa_q_norm[0], GQA_HEADS).reshape(1, -1),
        jnp.tile(gqa_k_norm[0], GQA_KV_HEADS).reshape(1, -1), mla_q_norm[0].reshape(1, -1),
        mla_kv_norm[0].reshape(1, -1), w_uq, w_kn, w_v, tabs64, tabs_m, n_lat, ctx_row)
    groups = GQA_HEADS // GQA_KV_HEADS
    dvn = GQA_HEAD_DIM + NORM_ROWS
    jobs = tuple(_Job(h * QK_SLOT, 0, h // groups, (h // groups) * dvn, dvn) for h in range(GQA_HEADS))
    o_gqa = _attention(qgt, kg, vgt, (), jobs, n_lat, None, "gqa_attention")
    dvn = MLA_V + NORM_ROWS
    jobs = tuple(_Job(h * QK_SLOT, h // 2, h % 2, h * dvn, dvn) for h in range(MLA_HEADS))
    o_mla = _attention(qmt, km, vmt, (), jobs, n_lat, None, "mla_attention")
    xs, h2, gates = _out_proj(o_gqa, o_mla, xs, mod, cd_w_out[0].astype(BF16),
                              router_w[1].T, router_bias[1].reshape(-1, 1), n_lat, ctx_row)
    return _moe(h2, gates, xs, mod, *moe_weights(1), fn, 1, n_lat, True)
```

```python
import functools
import math
from typing import NamedTuple

import jax
import jax.numpy as jnp
from jax import lax
from jax.experimental import pallas as pl
from jax.experimental.pallas import tpu as pltpu

F32 = jnp.float32
BF16 = jnp.bfloat16

GRID_W = 64
ROPE_THETA = 10000.0
EPS = 1e-6
DIFF_HEADS = 4
DIFF_HEAD_DIM = 64
LRU_WIDTH = 512
LRU_BLOCKS = 8
LRU_BW = LRU_WIDTH // LRU_BLOCKS
CONV_W = 4
LRU_C = 8.0
GQA_HEADS = 8
GQA_KV_HEADS = 2
GQA_HEAD_DIM = 64
MLA_HEADS = 8
MLA_Q_RANK = 256
MLA_KV_RANK = 128
MLA_NOPE = 32
MLA_ROPE = 16
MLA_V = 64
MLA_QK = MLA_NOPE + MLA_ROPE
N_EXPERTS = 64
N_GROUPS = 8
PER_GROUP = N_EXPERTS // N_GROUPS
TOPK_GROUPS = 4
TOP_K = 8
EXPERT_FF = 256
ROUTED_SCALE = 2.5

QK_SLOT = 64
NORM_ROWS = 16
LOG2E = math.log2(math.e)

ROW_TILE = 256
LRU_CHUNK = 128
LRU_LANES = 256
EXPERTS_PER_STEP = 4
LANE = 128
SUBLANE = 8
MOD_ROWS = 8
VMEM_LIMIT = 56 * 1024 * 1024
MOE_VMEM_LIMIT = 62 * 1024 * 1024
MOE_TILE = 1024
MOE_CAP = 160
SEG_ALIGN = 16
MOE_SLOTS = MOE_TILE * TOP_K + N_EXPERTS * SEG_ALIGN
COMBINE_BLOCK = 1024


def _params(sem, vmem=VMEM_LIMIT):
    return pltpu.CompilerParams(dimension_semantics=sem, vmem_limit_bytes=vmem)


def _silu(x):
    return x * jax.nn.sigmoid(x)


def _rms(x):
    return x * lax.rsqrt(jnp.mean(x * x, axis=-1, keepdims=True) + EPS)


def _mod_rows(mod_ref, ctx_row, b, k, d, row0, rows, n_lat):
    lat = mod_ref[pl.ds(b, 1), k * d:(k + 1) * d]
    ctx = mod_ref[ctx_row:ctx_row + 1, k * d:(k + 1) * d]
    if n_lat % rows == 0:
        return jnp.where(row0 >= n_lat, ctx, lat)
    r = row0 + lax.broadcasted_iota(jnp.int32, (rows, 1), 0)
    return jnp.where(r >= n_lat, ctx, lat)


def _stream_specs(x, ctx, n_lat):
    d = x.shape[-1]
    if ctx is None:
        spec = pl.BlockSpec((1, ROW_TILE, d), lambda b, t: (b, t, 0))
        return (x, x), [spec, spec]
    last = n_lat // ROW_TILE - 1
    return (x, ctx), [pl.BlockSpec((1, ROW_TILE, d), lambda b, t: (b, jnp.minimum(t, last), 0)),
                      pl.BlockSpec((1, ROW_TILE, d), lambda b, t: (b, jnp.maximum(t - last - 1, 0), 0))]


def _values_t(v, heads):
    vt = v.T
    dv = vt.shape[0] // heads
    ones = jnp.ones((NORM_ROWS, vt.shape[1]), F32)
    return jnp.concatenate([piece for h in range(heads) for piece in (vt[h * dv:(h + 1) * dv], ones)],
                           axis=0).astype(BF16)


def _rope(x, cos, sin_lo, sin_hi, half):
    n = x.shape[-1]
    return x * cos + pltpu.roll(x, n - half, 1) * sin_lo + pltpu.roll(x, half, 1) * sin_hi


def _mod_kernel(c_ref, w_ref, b_ref, o_ref):
    s = _silu(c_ref[...])
    o_ref[0] = jnp.dot(s, w_ref[0], preferred_element_type=F32, precision=lax.Precision.HIGHEST) + b_ref[0]


def _modulation(c_all, mod_w, mod_b):
    depth, d, n = mod_w.shape
    tn = n // 4
    return pl.pallas_call(
        _mod_kernel,
        grid=(depth, n // tn),
        in_specs=[pl.BlockSpec((MOD_ROWS, d), lambda i, j: (0, 0)),
                  pl.BlockSpec((1, d, tn), lambda i, j: (i, 0, j)),
                  pl.BlockSpec((1, 1, tn), lambda i, j: (i, 0, j))],
        out_specs=pl.BlockSpec((1, MOD_ROWS, tn), lambda i, j: (i, 0, j)),
        out_shape=jax.ShapeDtypeStruct((depth, MOD_ROWS, n), F32),
        compiler_params=_params(("arbitrary", "arbitrary")),
        name="modulation",
    )(c_all, mod_w, mod_b.reshape(depth, 1, n))


def _proj_ab_kernel(x_ref, xc_ref, mod_ref, w_ref, cos_ref, slo_ref, shi_ref,
                    qt_ref, k_ref, vt_ref, u_ref, g_ref, *, n_lat, ctx_row):
    b, t = pl.program_id(0), pl.program_id(1)
    d = x_ref.shape[-1]
    row0 = t * ROW_TILE
    shift = _mod_rows(mod_ref, ctx_row, b, 0, d, row0, ROW_TILE, n_lat)
    scale = _mod_rows(mod_ref, ctx_row, b, 1, d, row0, ROW_TILE, n_lat)
    x = jnp.where(row0 >= n_lat, xc_ref[0], x_ref[0])
    h = (_rms(x) * (1.0 + scale) + shift).astype(BF16)
    acc = jnp.dot(h, w_ref[...], preferred_element_type=F32)
    w = DIFF_HEADS * 2 * DIFF_HEAD_DIM
    cos, slo, shi = (jnp.tile(r[...], (1, w // LANE)) for r in (cos_ref, slo_ref, shi_ref))
    q = _rope(acc[:, 0:w], cos, slo, shi, DIFF_HEAD_DIM // 4) * (DIFF_HEAD_DIM ** -0.5 * LOG2E)
    k = _rope(acc[:, w:2 * w], cos, slo, shi, DIFF_HEAD_DIM // 4)
    qt_ref[0, 0] = q.T.astype(BF16)
    k_ref[0] = k.astype(BF16)
    vt_ref[0] = _values_t(acc[:, 2 * w:3 * w], DIFF_HEADS)
    u_ref[0] = acc[:, 3 * w:3 * w + LRU_WIDTH]
    g_ref[0] = jax.nn.gelu(acc[:, 3 * w + LRU_WIDTH:3 * w + 2 * LRU_WIDTH]).astype(BF16)


def _proj_ab(x, ctx, mod, w_in, tabs, n_lat, ctx_row):
    bsz, _, d = x.shape
    m = n_lat + ctx.shape[1]
    nt = m // ROW_TILE
    streams, stream_specs = _stream_specs(x, ctx, n_lat)
    w = DIFF_HEADS * 2 * DIFF_HEAD_DIM
    n_in = w_in.shape[1]
    tile = lambda n: pl.BlockSpec((1, ROW_TILE, n), lambda b, t: (b, t, 0))
    ttile = pl.BlockSpec((1, 1, w, ROW_TILE), lambda b, t: (b, t, 0, 0))
    wv = w + DIFF_HEADS * NORM_ROWS
    vtile = pl.BlockSpec((1, wv, ROW_TILE), lambda b, t: (b, 0, t))
    tab = pl.BlockSpec((ROW_TILE, LANE), lambda b, t: (t, 0))
    return pl.pallas_call(
        functools.partial(_proj_ab_kernel, n_lat=n_lat, ctx_row=ctx_row),
        grid=(bsz, nt),
        in_specs=stream_specs + [pl.BlockSpec(mod.shape, lambda b, t: (0, 0)),
                                 pl.BlockSpec((d, n_in), lambda b, t: (0, 0)),
                                 tab, tab, tab],
        out_specs=[ttile, tile(w), vtile, tile(LRU_WIDTH), tile(LRU_WIDTH)],
        out_shape=[jax.ShapeDtypeStruct((bsz, nt, w, ROW_TILE), BF16),
                   jax.ShapeDtypeStruct((bsz, m, w), BF16),
                   jax.ShapeDtypeStruct((bsz, wv, m), BF16),
                   jax.ShapeDtypeStruct((bsz, m, LRU_WIDTH), F32),
                   jax.ShapeDtypeStruct((bsz, m, LRU_WIDTH), BF16)],
        compiler_params=_params(("parallel", "parallel")),
        name="proj_ab",
    )(*streams, mod, w_in, *tabs)


def _group_rms(x, ones_bd):
    x2 = x * x
    hi = x2.astype(BF16)
    lo = (x2 - hi.astype(F32)).astype(BF16)
    ss = jnp.dot(hi, ones_bd, preferred_element_type=F32) + jnp.dot(lo, ones_bd, preferred_element_type=F32)
    return x * lax.rsqrt(ss * (1.0 / GQA_HEAD_DIM) + EPS)


def _proj_cd_kernel(x_ref, mod_ref, w_ref, bd_ref, qn_ref, kn_ref, cqn_ref, ckvn_ref, wuq_ref, wkn_ref, wv_ref,
                    cos_ref, slo_ref, shi_ref, cosm_ref, slom_ref, shim_ref,
                    qgt_ref, kg_ref, vgt_ref, qmt_ref, km_ref, vmt_ref, *, n_lat, ctx_row):
    b, t = pl.program_id(0), pl.program_id(1)
    d = x_ref.shape[-1]
    row0 = t * ROW_TILE
    shift = _mod_rows(mod_ref, ctx_row, b, 0, d, row0, ROW_TILE, n_lat)
    scale = _mod_rows(mod_ref, ctx_row, b, 1, d, row0, ROW_TILE, n_lat)
    h = (_rms(x_ref[0]) * (1.0 + scale) + shift).astype(BF16)
    acc = jnp.dot(h, w_ref[...], preferred_element_type=F32)
    wq = GQA_HEADS * GQA_HEAD_DIM
    wk = GQA_KV_HEADS * GQA_HEAD_DIM
    o = 0
    q = acc[:, o:o + wq]; o += wq
    k = acc[:, o:o + wk]; o += wk
    v = acc[:, o:o + wk]; o += wk
    cq = acc[:, o:o + MLA_Q_RANK]; o += MLA_Q_RANK
    ckv = acc[:, o:o + MLA_KV_RANK]; o += MLA_KV_RANK
    kr = acc[:, o:o + MLA_HEADS * QK_SLOT]
    cos, slo, shi = (jnp.tile(r[...], (1, wq // LANE)) for r in (cos_ref, slo_ref, shi_ref))
    bd = bd_ref[...]
    q = (_rope(_group_rms(q, bd) * qn_ref[...], cos, slo, shi, GQA_HEAD_DIM // 4)
         * (GQA_HEAD_DIM ** -0.5 * LOG2E))
    k = _rope(_group_rms(k, bd[:wk, :wk]) * kn_ref[...], cos[:, :wk], slo[:, :wk], shi[:, :wk], GQA_HEAD_DIM // 4)
    qgt_ref[0, 0] = q.T.astype(BF16)
    kg_ref[0] = k.astype(BF16)
    vgt_ref[0] = _values_t(v, GQA_KV_HEADS)
    cosm, slom, shim = (jnp.tile(r[...], (1, MLA_HEADS * QK_SLOT // LANE)) for r in (cosm_ref, slom_ref, shim_ref))
    cqn = (_rms(cq) * cqn_ref[...]).astype(BF16)
    mq = jnp.dot(cqn, wuq_ref[...], preferred_element_type=F32)
    qmt_ref[0, 0] = (_rope(mq, cosm, slom, shim, MLA_ROPE // 4) * (MLA_QK ** -0.5 * LOG2E)).T.astype(BF16)
    ckvn = (_rms(ckv) * ckvn_ref[...]).astype(BF16)
    km = jnp.dot(ckvn, wkn_ref[...], preferred_element_type=F32) + _rope(kr, cosm, slom, shim, MLA_ROPE // 4)
    km_ref[0] = km.astype(BF16)
    vmt_ref[0] = _values_t(jnp.dot(ckvn, wv_ref[...], preferred_element_type=F32), MLA_HEADS)


def _proj_cd(xs, mod, w_in, bd, qn, kn, cqn, ckvn, wuq, wkn, wv, tabs, tabs_m, n_lat, ctx_row):
    bsz, m, d = xs.shape
    nt = m // ROW_TILE
    wq = GQA_HEADS * GQA_HEAD_DIM
    wk = GQA_KV_HEADS * GQA_HEAD_DIM
    wm = MLA_HEADS * QK_SLOT
    wmv = MLA_HEADS * (MLA_V + NORM_ROWS)
    wgv = GQA_KV_HEADS * (GQA_HEAD_DIM + NORM_ROWS)
    tile = lambda n: pl.BlockSpec((1, ROW_TILE, n), lambda b, t: (b, t, 0))
    ttile = lambda n: pl.BlockSpec((1, 1, n, ROW_TILE), lambda b, t: (b, t, 0, 0))
    vtile = lambda n: pl.BlockSpec((1, n, ROW_TILE), lambda b, t: (b, 0, t))
    whole = lambda a: pl.BlockSpec(a.shape, lambda b, t: (0,) * a.ndim)
    tab = pl.BlockSpec((ROW_TILE, LANE), lambda b, t: (t, 0))
    tabm = tab
    return pl.pallas_call(
        functools.partial(_proj_cd_kernel, n_lat=n_lat, ctx_row=ctx_row),
        grid=(bsz, nt),
        in_specs=[tile(d), whole(mod), whole(w_in), whole(bd), whole(qn), whole(kn), whole(cqn), whole(ckvn),
                  whole(wuq), whole(wkn), whole(wv), tab, tab, tab, tabm, tabm, tabm],
        out_specs=[ttile(wq), tile(wk), vtile(wgv), ttile(wm), tile(wm), vtile(wmv)],
        out_shape=[jax.ShapeDtypeStruct((bsz, nt, wq, ROW_TILE), BF16),
                   jax.ShapeDtypeStruct((bsz, m, wk), BF16),
                   jax.ShapeDtypeStruct((bsz, wgv, m), BF16),
                   jax.ShapeDtypeStruct((bsz, nt, wm, ROW_TILE), BF16),
                   jax.ShapeDtypeStruct((bsz, m, wm), BF16),
                   jax.ShapeDtypeStruct((bsz, wmv, m), BF16)],
        compiler_params=_params(("parallel", "parallel")),
        name="proj_cd",
    )(xs, mod, w_in, bd, qn, kn, cqn, ckvn, wuq, wkn, wv, *tabs, *tabs_m)


class _Job(NamedTuple):
    q_row: int
    k_group: int
    k_half: int
    v_row: int
    dv: int


def _attend(qt_ref, qnext_ref, k_ref, vt_ref, s_ref, mp_ref, jobs, lo, hi, chained):
    assert len(jobs) % 2 == 0
    tq = qt_ref.shape[-1]
    n = hi - lo
    group = next(g for g in (11, 3, 2, 1) if n % g == 0)
    steps = n // group
    row_groups = ROW_TILE // SUBLANE

    def weights(job, ref=qt_ref):
        qh = ref[0, 0, job.q_row:job.q_row + QK_SLOT, :]
        z = jnp.zeros_like(qh)
        return jnp.concatenate([z, qh] if job.k_half else [qh, z], axis=0)

    def score(c0, buf, qw, job, mp):
        r0 = pl.multiple_of(c0 * ROW_TILE, ROW_TILE)
        kc = k_ref[0, pl.ds(r0, group * ROW_TILE), job.k_group * LANE:(job.k_group + 1) * LANE]
        s_all = jnp.dot(kc, qw, preferred_element_type=F32)
        for g in range(group):
            s = s_all[g * ROW_TILE:(g + 1) * ROW_TILE]
            s_ref[buf, c0 + g] = s
            mp = jnp.maximum(mp, jnp.max(s.reshape(row_groups, SUBLANE, tq), axis=0))
        return mp

    def value(c0, buf, m, job, acc):
        ps = [jnp.exp2(s_ref[buf, c0 + g] - m).astype(BF16) for g in range(group)]
        k0 = pl.multiple_of(c0 * ROW_TILE, ROW_TILE)
        v = vt_ref[0, job.v_row:job.v_row + job.dv, pl.ds(k0, group * ROW_TILE)]
        return acc + jnp.dot(v, jnp.concatenate(ps, axis=0), preferred_element_type=F32)

    def loop(body, init):
        return body(0, init) if steps == 1 else lax.fori_loop(0, steps, body, init)

    neg = jnp.full((SUBLANE, tq), -jnp.inf, F32)

    def prologue():
        qw = weights(jobs[0])

        def first(i, mp):
            return score(lo + i * group, 0, qw, jobs[0], mp)

        mp_ref[...] = loop(first, neg)

    if chained:
        pl.when(pl.program_id(1) == 0)(prologue)
    else:
        prologue()
    mp = mp_ref[...]
    results = []
    for j, job in enumerate(jobs):
        buf = j % 2
        m = jnp.max(mp, axis=0, keepdims=True)
        if j + 1 < len(jobs):
            nxt, qw = jobs[j + 1], weights(jobs[j + 1])
        elif chained:
            nxt, qw = jobs[0], weights(jobs[0], qnext_ref)
        else:
            nxt = None

        def body(i, carry):
            mp, acc = carry
            if nxt is not None:
                mp = score(lo + i * group, 1 - buf, qw, nxt, mp)
            return mp, value(lo + i * group, buf, m, job, acc)

        mp, acc = loop(body, (neg, jnp.zeros((job.dv, tq), F32)))
        dv = job.dv - NORM_ROWS
        results.append((acc[:dv], acc[dv:dv + 1]))
    if chained:
        mp_ref[...] = mp
    return results


def _attn_kernel(*refs, jobs, n_lat, lambda_init):
    if lambda_init is None:
        qt_ref, qnext_ref, k_ref, vt_ref, o_ref, s_ref, mp_ref = refs
    else:
        qt_ref, qnext_ref, k_ref, vt_ref, lq_ref, lk_ref, sub_ref, o_ref, s_ref, mp_ref = refs
    n_chunks = k_ref.shape[1] // ROW_TILE
    first_ctx = n_lat // ROW_TILE
    t = pl.program_id(1)

    def run(lo, hi, chained):
        res = _attend(qt_ref, qnext_ref, k_ref, vt_ref, s_ref, mp_ref, jobs, lo, hi, chained)
        if lambda_init is None:
            outs = [acc / l for acc, l in res]
        else:
            lq, lk = lq_ref[...], lk_ref[...]
            lam = (jnp.exp(jnp.sum(lq[0:1] * lk[0:1], axis=-1, keepdims=True))
                   - jnp.exp(jnp.sum(lq[1:2] * lk[1:2], axis=-1, keepdims=True)) + lambda_init)
            outs = []
            for h in range(len(res) // 2):
                (a0, l0), (a1, l1) = res[2 * h], res[2 * h + 1]
                o = a0 / l0 - lam * (a1 / l1)
                o = o * lax.rsqrt(jnp.mean(o * o, axis=0, keepdims=True) + EPS)
                outs.append(o * sub_ref[...] * (1.0 - lambda_init))
        o_ref[0] = jnp.concatenate(outs, axis=0).T.astype(BF16)

    @pl.when(t < first_ctx)
    def _():
        run(0, n_chunks, True)

    @pl.when(t >= first_ctx)
    def _():
        run(first_ctx, n_chunks, False)


def _attention(qt, k, vt, extra, jobs, n_lat, lambda_init, name):
    bsz, nt, wq, tq = qt.shape
    m = k.shape[1]
    wo = sum(j.dv - NORM_ROWS for j in jobs) // (1 if lambda_init is None else 2)
    once = pl.Buffered(1)
    whole = lambda a: pl.BlockSpec(a.shape, lambda b, t: (0,) * a.ndim)
    return pl.pallas_call(
        functools.partial(_attn_kernel, jobs=jobs, n_lat=n_lat, lambda_init=lambda_init),
        grid=(bsz, nt),
        in_specs=[pl.BlockSpec((1, 1, wq, tq), lambda b, t: (b, t, 0, 0)),
                  pl.BlockSpec((1, 1, wq, tq), lambda b, t: (b, jnp.minimum(t + 1, nt - 1), 0, 0)),
                  pl.BlockSpec((1,) + k.shape[1:], lambda b, t: (b, 0, 0), pipeline_mode=once),
                  pl.BlockSpec((1,) + vt.shape[1:], lambda b, t: (b, 0, 0), pipeline_mode=once)]
                 + [whole(a) for a in extra],
        out_specs=pl.BlockSpec((1, tq, wo), lambda b, t: (b, t, 0)),
        out_shape=jax.ShapeDtypeStruct((bsz, m, wo), BF16),
        scratch_shapes=[pltpu.VMEM((2, nt, ROW_TILE, tq), F32), pltpu.VMEM((SUBLANE, tq), F32)],
        compiler_params=_params(("arbitrary", "arbitrary")),
        name=name,
    )(qt, qt, k, vt, *extra)


def _scan_chunk(a, b, carry, reverse):
    n = a.shape[0]
    rows = lax.broadcasted_iota(jnp.int32, (n, 1), 0)
    d = 1
    while d < n:
        sh = n - d if reverse else d
        valid = rows < n - d if reverse else rows >= d
        b = jnp.where(valid, a * pltpu.roll(b, sh, 0) + b, b)
        a = jnp.where(valid, a * pltpu.roll(a, sh, 0), a)
        d *= 2
    h = a * carry + b
    return h, (h[0:1] if reverse else h[n - 1:n])


def _lru_kernel(u_ref, g_ref, cw_ref, cb_ref, wa_ref, ba_ref, wx_ref, bx_ref, lam_ref, o_ref, up_ref, hf_ref, *, n_lat):
    m = u_ref.shape[1]
    lanes = u_ref.shape[2]
    n_chunks = m // LRU_CHUNK
    n_ctx_chunks = (m - n_lat) // LRU_CHUNK
    pad = SUBLANE

    up_ref[0:pad, :] = jnp.zeros((pad, lanes), F32)
    up_ref[pad + m:pad + m + pad, :] = jnp.zeros((pad, lanes), F32)

    def copy(c, _):
        r0 = pl.multiple_of(c * LRU_CHUNK, LRU_CHUNK)
        up_ref[pl.ds(pad + r0, LRU_CHUNK), :] = u_ref[0, pl.ds(r0, LRU_CHUNK), :]
        return 0

    lax.fori_loop(0, n_chunks, copy, 0)

    cw = cw_ref[...]
    cb = cb_ref[...]
    win_rows = LRU_CHUNK + 2 * pad

    def coeffs(c, d):
        r0 = pl.multiple_of(c * LRU_CHUNK, LRU_CHUNK)
        win = up_ref[pl.ds(r0, win_rows), :]
        r = r0 + lax.broadcasted_iota(jnp.int32, (LRU_CHUNK, 1), 0)
        at = lambda k: pltpu.roll(win, (win_rows - k) % win_rows, 0)[pad:pad + LRU_CHUNK]
        y = (cw[0:1] * jnp.where((r == n_lat) | (r == n_lat + 1), 0.0, at(-2))
             + cw[1:2] * jnp.where(r == n_lat, 0.0, at(-1))
             + cw[2:3] * win[pad:pad + LRU_CHUNK]
             + cw[3:4] * jnp.where(r == n_lat - 1, 0.0, at(1))
             + cb)
        yb = y.astype(BF16)
        ra = jax.nn.sigmoid(jnp.dot(yb, wa_ref[d, 0], preferred_element_type=F32) + ba_ref[d])
        ix = jax.nn.sigmoid(jnp.dot(yb, wx_ref[d, 0], preferred_element_type=F32) + bx_ref[d])
        z = -lam_ref[d]
        softplus = jnp.maximum(z, 0.0) + jnp.log(1.0 + jnp.exp(-jnp.abs(z)))
        a = jnp.exp(-LRU_C * ra * softplus)
        return a, jnp.sqrt(1.0 - a * a) * ix * y

    def fwd(s, carry):
        c = lax.rem(s + (n_chunks - n_ctx_chunks), n_chunks)
        a, b = coeffs(c, 0)
        h, carry = _scan_chunk(a, b, carry, False)
        r0 = pl.multiple_of(c * LRU_CHUNK, LRU_CHUNK)
        hf_ref[pl.ds(r0, LRU_CHUNK), :] = h
        return carry

    lax.fori_loop(0, n_chunks, fwd, jnp.zeros((1, lanes), F32))

    def bwd(s, carry):
        c = n_chunks - 1 - s
        a, b = coeffs(c, 1)
        h, carry = _scan_chunk(a, b, carry, True)
        r0 = pl.multiple_of(c * LRU_CHUNK, LRU_CHUNK)
        tot = hf_ref[pl.ds(r0, LRU_CHUNK), :] + h
        o_ref[0, pl.ds(r0, LRU_CHUNK), :] = (tot * g_ref[0, pl.ds(r0, LRU_CHUNK), :].astype(F32)).astype(BF16)
        return carry

    lax.fori_loop(0, n_chunks, bwd, jnp.zeros((1, lanes), F32))


def _lru(u, g, conv_w, conv_b, wa_bd, ba, wx_bd, bx, lam, n_lat):
    bsz, m, c = u.shape
    nl = c // LRU_LANES
    seq = lambda: pl.BlockSpec((1, m, LRU_LANES), lambda b, j: (b, 0, j))
    vec = lambda a: pl.BlockSpec(a.shape[:-1] + (LRU_LANES,), lambda b, j: (0,) * (a.ndim - 1) + (j,))
    mat = pl.BlockSpec((2, 1, LRU_LANES, LRU_LANES), lambda b, j: (0, j, 0, 0))
    return pl.pallas_call(
        functools.partial(_lru_kernel, n_lat=n_lat),
        grid=(bsz, nl),
        in_specs=[seq(), seq(), vec(conv_w), vec(conv_b), mat, vec(ba), mat, vec(bx), vec(lam)],
        out_specs=seq(),
        out_shape=jax.ShapeDtypeStruct((bsz, m, c), BF16),
        scratch_shapes=[pltpu.VMEM((m + 2 * SUBLANE, LRU_LANES), F32), pltpu.VMEM((m, LRU_LANES), F32)],
        compiler_params=_params(("parallel", "parallel")),
        name="rglru",
    )(u, g, conv_w, conv_b, wa_bd, ba, wx_bd, bx, lam)


def _route(logits_t, bias):
    n = logits_t.shape[-1]
    scores = jax.nn.sigmoid(logits_t)
    choice = scores + bias
    sub = lax.broadcasted_iota(jnp.int32, (PER_GROUP, n), 0)
    neg = -jnp.inf
    groups, gs = [], []
    for g in range(N_GROUPS):
        cg = choice[g * PER_GROUP:(g + 1) * PER_GROUP]
        m1 = jnp.max(cg, axis=0, keepdims=True)
        i1 = jnp.min(jnp.where(cg == m1, sub, PER_GROUP), axis=0, keepdims=True)
        m2 = jnp.max(jnp.where(sub == i1, neg, cg), axis=0, keepdims=True)
        groups.append(cg)
        gs.append(m1 + m2)
    masked = []
    for g in range(N_GROUPS):
        rank = jnp.zeros((1, n), jnp.int32)
        for o in range(N_GROUPS):
            if o != g:
                ahead = (gs[o] >= gs[g]) if o < g else (gs[o] > gs[g])
                rank = rank + jnp.where(ahead, 1, 0)
        masked.append(jnp.where(rank < TOPK_GROUPS, groups[g], neg))
    masked = jnp.concatenate(masked, axis=0)
    eidx = lax.broadcasted_iota(jnp.int32, (N_EXPERTS, n), 0)
    picked = jnp.zeros((N_EXPERTS, n), F32)
    for _ in range(TOP_K):
        mx = jnp.max(masked, axis=0, keepdims=True)
        first = jnp.min(jnp.where(masked == mx, eidx, N_EXPERTS), axis=0, keepdims=True)
        hit = eidx == first
        picked = jnp.where(hit, 1.0, picked)
        masked = jnp.where(hit, neg, masked)
    w = picked * scores
    return w / jnp.sum(w, axis=0, keepdims=True) * ROUTED_SCALE


def _out_proj_kernel(oa_ref, ob_ref, x_ref, xc_ref, mod_ref, wa_ref, wb_ref, rwh_ref, rwl_ref, rb_ref,
                     xo_ref, h_ref, gates_ref, *, n_lat, ctx_row):
    b, t = pl.program_id(0), pl.program_id(1)
    d = x_ref.shape[-1]
    row0 = t * ROW_TILE
    mv = lambda k: _mod_rows(mod_ref, ctx_row, b, k, d, row0, ROW_TILE, n_lat)
    y = (jnp.dot(oa_ref[0], wa_ref[...], preferred_element_type=F32)
         + jnp.dot(ob_ref[0], wb_ref[...], preferred_element_type=F32))
    x = jnp.where(row0 >= n_lat, xc_ref[0], x_ref[0]) + mv(2) * y
    xo_ref[0] = x
    h = _rms(x) * (1.0 + mv(4)) + mv(3)
    hb = h.astype(BF16)
    h_ref[0] = hb
    h_lo = (h - hb.astype(F32)).astype(BF16)
    nt = lambda a, b: lax.dot_general(a, b, (((1,), (1,)), ((), ())), preferred_element_type=F32)
    logits_t = nt(rwh_ref[...], hb) + (nt(rwh_ref[...], h_lo) + nt(rwl_ref[...], hb))
    gates_t = _route(logits_t, rb_ref[...])
    gates_t = jnp.concatenate([gates_t, jnp.zeros((LANE - N_EXPERTS, ROW_TILE), F32)], axis=0)
    gates_ref[0] = gates_t.T


def _out_proj(oa, ob, x, ctx, mod, w_out, router_wt, router_b, n_lat, ctx_row):
    bsz, m, d = oa.shape[0], oa.shape[1], x.shape[-1]
    streams, stream_specs = _stream_specs(x, ctx, n_lat)
    na, nb = oa.shape[-1], ob.shape[-1]
    wa, wb = w_out[:na], w_out[na:]
    rw_hi = router_wt.astype(BF16)
    rw_lo = (router_wt - rw_hi.astype(F32)).astype(BF16)
    tile = lambda n: pl.BlockSpec((1, ROW_TILE, n), lambda b, t: (b, t, 0))
    whole = lambda a: pl.BlockSpec(a.shape, lambda b, t: (0,) * a.ndim)
    return pl.pallas_call(
        functools.partial(_out_proj_kernel, n_lat=n_lat, ctx_row=ctx_row),
        grid=(bsz, m // ROW_TILE),
        in_specs=[tile(na), tile(nb)] + stream_specs + [whole(mod), whole(wa), whole(wb), whole(rw_hi), whole(rw_lo),
                                                        whole(router_b)],
        out_specs=[tile(d), tile(d), tile(LANE)],
        out_shape=[jax.ShapeDtypeStruct((bsz, m, d), F32),
                   jax.ShapeDtypeStruct((bsz, m, d), BF16),
                   jax.ShapeDtypeStruct((bsz, m, LANE), F32)],
        compiler_params=_params(("parallel", "parallel")),
        name="out_proj_router",
    )(oa, ob, *streams, mod, wa, wb, rw_hi, rw_lo, router_b)


def _plan_kernel(gates_ref, tri_ref, upper_ref, pos_ref, gem_ref, cnt_ref, off_ref, own_ref):
    g = gates_ref[0]
    sel = g != 0.0
    one = jnp.where(sel, 1.0, 0.0)
    pos = jnp.dot(tri_ref[...], one.astype(BF16), preferred_element_type=F32)
    cnt = jnp.sum(one, axis=0, keepdims=True)
    cpad = jnp.floor((cnt + (SEG_ALIGN - 1)) * (1.0 / SEG_ALIGN)) * SEG_ALIGN
    off = jnp.dot(jnp.broadcast_to(cpad, (SUBLANE, LANE)), upper_ref[...], preferred_element_type=F32,
                  precision=lax.Precision.HIGHEST)[0:1]
    pos_ref[0] = jnp.where(sel, pos, -1.0).T
    gem_ref[0] = g.T
    cnt_ref[0] = jnp.broadcast_to(cnt, (SUBLANE, LANE)).astype(jnp.int32)
    off_ref[0] = jnp.broadcast_to(off, (SUBLANE, LANE)).astype(jnp.int32)
    groups = own_ref.shape[-1]
    ends = jnp.broadcast_to(off + cpad, (LANE, LANE)).T
    ends = jnp.concatenate([ends] * (groups // LANE), axis=1)
    start = lax.broadcasted_iota(jnp.int32, (LANE, groups), 1).astype(F32) * SEG_ALIGN
    real = lax.broadcasted_iota(jnp.int32, (LANE, groups), 0) < N_EXPERTS
    before = jnp.where(real, jnp.where(ends <= start, 1.0, 0.0), 0.0)
    owner = jnp.minimum(jnp.sum(before, axis=0, keepdims=True), N_EXPERTS - 1.0)
    own_ref[0] = jnp.broadcast_to(owner, (SUBLANE, groups)).astype(jnp.int32)


def _moe_kernel(cnt_s, off_s, own_s, h_ref, pos_ref, gem_ref, wg_ref, wu_ref, wd_ref,
                sg_ref, su_ref, sd_ref, x_ref, mod_ref, fn_ref, o_ref, y_ref, p_ref,
                *, bsz, rows_per_batch, n_lat, tiles_per_batch, final):
    i, s = pl.program_id(0), pl.program_id(1)
    rows, d = h_ref.shape[1], h_ref.shape[2]
    h = h_ref[0]

    def ffn(x, wg, wu, wd, gate):
        a = _silu(jnp.dot(x, wg, preferred_element_type=F32)) * jnp.dot(x, wu, preferred_element_type=F32)
        if gate is not None:
            a = a * gate
        return jnp.dot(a.astype(BF16), wd, preferred_element_type=F32)

    @pl.when((s == 0) & (i == 0))
    def _():
        y_ref[...] = jnp.zeros(y_ref.shape, BF16)

    @pl.when(s == 0)
    def _():
        o_ref[0] = ffn(h, sg_ref[...].astype(BF16), su_ref[...].astype(BF16), sd_ref[...].astype(BF16), None)

    cap = MOE_CAP
    cap_rows = lax.broadcasted_iota(jnp.int32, (cap, 1), 0)

    def onehot(pos_row, q):
        ranks = cap_rows + q * cap
        hit = pos_row == ranks.astype(F32)
        return ranks, hit, jnp.where(hit, 1.0, 0.0).astype(BF16)

    def expert_rows(x, ranks, hit, q, gate_row, w, cnt, off):
        gate = jnp.sum(jnp.where(hit, gate_row, 0.0), axis=1, keepdims=True)
        y = ffn(x, *w, gate)
        dst = pl.multiple_of(off + q * cap, SEG_ALIGN)
        y_ref[pl.ds(dst, cap), :] = jnp.where(ranks < cnt, y.astype(BF16), y_ref[pl.ds(dst, cap), :])

    experts = []
    for j in range(EXPERTS_PER_STEP):
        e = s * EXPERTS_PER_STEP + j
        experts.append((cnt_s[i, e], off_s[i, e],
                        pos_ref[0, pl.ds(e, 1), :],
                        gem_ref[0, pl.ds(e, 1), :],
                        (wg_ref[j], wu_ref[j], wd_ref[j])))

    hots = [onehot(pos_row, 0) for _, _, pos_row, _, _ in experts]
    xs = jnp.dot(jnp.concatenate([p for _, _, p in hots], axis=0), h, preferred_element_type=F32).astype(BF16)
    for j, ((cnt, off, _, gate_row, w), (ranks, hit, _)) in enumerate(zip(experts, hots)):
        expert_rows(xs[j * cap:(j + 1) * cap], ranks, hit, 0, gate_row, w, cnt, off)

    for cnt, off, pos_row, gate_row, w in experts:
        def block(q, carry):
            ranks, hit, p = onehot(pos_row, q)
            x = jnp.dot(p, h, preferred_element_type=F32).astype(BF16)
            expert_rows(x, ranks, hit, q, gate_row, w, cnt, off)
            return carry

        lax.fori_loop(1, (cnt + cap - 1) // cap, block, 0)

    @pl.when(s == N_EXPERTS // EXPERTS_PER_STEP - 1)
    def _():
        group_rows = lax.broadcasted_iota(jnp.int32, (SEG_ALIGN, 1), 0)
        groups_per_block = COMBINE_BLOCK // SEG_ALIGN

        def combine(r, carry):
            base = pl.multiple_of(r * COMBINE_BLOCK, COMBINE_BLOCK)

            for gi in range(groups_per_block):
                e = own_s[i, r * groups_per_block + gi]
                rank = (group_rows + (base + gi * SEG_ALIGN - off_s[i, e])).astype(F32)
                hit = pos_ref[0, pl.ds(e, 1), :] == rank
                p_ref[gi * SEG_ALIGN:(gi + 1) * SEG_ALIGN, :] = jnp.where(hit, 1.0, 0.0).astype(BF16)
            half = COMBINE_BLOCK // 2
            for n in range(d // ROW_TILE):
                cols = slice(n * ROW_TILE, (n + 1) * ROW_TILE)
                o_ref[0, :, cols] += sum(
                    lax.dot_general(p_ref[k0:k0 + half, :], y_ref[pl.ds(base + k0, half), cols],
                                    (((0,), (0,)), ((), ())), preferred_element_type=F32)
                    for k0 in (0, half))
            return carry

        lax.fori_loop(0, MOE_SLOTS // COMBINE_BLOCK, combine, 0)

        ctx = mod_ref[bsz:bsz + 1, 5 * d:6 * d]
        if tiles_per_batch is None:
            r = i * rows + lax.broadcasted_iota(jnp.int32, (rows, 1), 0)
            g2 = ctx
            for b in range(bsz):
                lo = b * rows_per_batch
                inside = jnp.where(r >= lo, jnp.where(r < lo + n_lat, 1, 0), 0)
                g2 = jnp.where(inside == 1, mod_ref[b:b + 1, 5 * d:6 * d], g2)
        else:
            g2 = mod_ref[pl.ds(i // tiles_per_batch, 1), 5 * d:6 * d]
        y = x_ref[0] + g2 * o_ref[0]
        if final:
            y = _rms(y) * fn_ref[...]
        o_ref[0] = y


def _moe(h, gates, xs, mod, wg, wu, wd, sg, su, sd, final_norm, layer, n_lat, final):
    bsz, m, d = xs.shape
    t = MOE_TILE
    if final:
        assert n_lat % t == 0
        tiles_per_batch = n_lat // t
        n_tiles = bsz * tiles_per_batch
        where = lambda i: (i // tiles_per_batch, i % tiles_per_batch, 0)
        out_shape = (bsz, n_lat, d)
    else:
        assert (bsz * m) % t == 0
        tiles_per_batch = None
        n_tiles = bsz * m // t
        where = lambda i: (0, i, 0)
        out_shape = (1, bsz * m, d)
        h, gates, xs = (a.reshape(1, bsz * m, a.shape[-1]) for a in (h, gates, xs))
    once = pl.Buffered(1)

    tok = jnp.arange(t)
    tri = (tok[None, :] < tok[:, None]).astype(BF16)
    lane = jnp.arange(LANE)
    upper = (lane[:, None] < lane[None, :]).astype(F32)
    groups = -(-MOE_SLOTS // SEG_ALIGN // LANE) * LANE
    em = jax.ShapeDtypeStruct((n_tiles, LANE, t), F32)
    ints = lambda n: jax.ShapeDtypeStruct((n_tiles, SUBLANE, n), jnp.int32)
    per_tile = lambda a, b: pl.BlockSpec((1, a, b), lambda i: (i, 0, 0))
    pos, gem, cnt, off, own = pl.pallas_call(
        _plan_kernel,
        grid=(n_tiles,),
        in_specs=[pl.BlockSpec((1, t, LANE), lambda i: where(i)),
                  pl.BlockSpec((t, t), lambda i: (0, 0)), pl.BlockSpec((LANE, LANE), lambda i: (0, 0))],
        out_specs=[per_tile(LANE, t), per_tile(LANE, t), per_tile(SUBLANE, LANE), per_tile(SUBLANE, LANE),
                   per_tile(SUBLANE, groups)],
        out_shape=[em, em, ints(LANE), ints(LANE), ints(groups)],
        compiler_params=_params(("parallel",)),
        name="moe_plan",
    )(gates, tri, upper)

    tile = lambda n: pl.BlockSpec((1, t, n), lambda i, s, *_: where(i), pipeline_mode=once)
    planned = lambda a, b: pl.BlockSpec((1, a, b), lambda i, s, *_: (i, 0, 0), pipeline_mode=once)
    whole = lambda a: pl.BlockSpec(a.shape, lambda i, s, *_: (0,) * a.ndim, pipeline_mode=once)
    experts = lambda a: pl.BlockSpec((None, EXPERTS_PER_STEP) + a.shape[2:], lambda i, s, *_: (layer, s, 0, 0))
    out = pl.pallas_call(
        functools.partial(_moe_kernel, bsz=bsz, rows_per_batch=m, n_lat=n_lat,
                          tiles_per_batch=tiles_per_batch, final=final),
        grid_spec=pltpu.PrefetchScalarGridSpec(
            num_scalar_prefetch=3,
            grid=(n_tiles, N_EXPERTS // EXPERTS_PER_STEP),
            in_specs=[tile(d), planned(LANE, t), planned(LANE, t),
                      experts(wg), experts(wu), experts(wd), whole(sg), whole(su), whole(sd),
                      tile(d), whole(mod), whole(final_norm)],
            out_specs=pl.BlockSpec((1, t, d), lambda i, s, *_: where(i)),
            scratch_shapes=[pltpu.VMEM((MOE_SLOTS + MOE_CAP, d), BF16), pltpu.VMEM((COMBINE_BLOCK, t), BF16)]),
        out_shape=jax.ShapeDtypeStruct(out_shape, F32),
        compiler_params=_params(("arbitrary", "arbitrary"), MOE_VMEM_LIMIT),
        name="moe_final" if final else "moe",
    )(cnt[:, 0], off[:, 0], own[:, 0], h, pos, gem, wg, wu, wd, sg, su, sd, xs, mod, final_norm)
    return out if final else out.reshape(bsz, m, d)


def _rope_tables(n_lat, m, head_dim, width):
    pos = jnp.arange(n_lat, dtype=jnp.int32)
    row = (pos // GRID_W).astype(F32)
    col = (pos % GRID_W).astype(F32)
    half = head_dim // 2
    quarter = half // 2
    lane = jnp.arange(head_dim)
    freq = (lane % quarter).astype(F32)
    inv = ROPE_THETA ** (-(2.0 * freq) / half)
    ang = jnp.where(lane[None, :] < half, row[:, None], col[:, None]) * inv[None, :]
    low = (lane % half) < quarter
    cos = jnp.cos(ang)
    sin = jnp.sin(ang)
    slo = jnp.where(low[None, :], -sin, 0.0)
    shi = jnp.where(low[None, :], 0.0, sin)
    reps = width // head_dim
    pad = lambda a, fill: jnp.concatenate(
        [jnp.tile(a, (1, reps)), jnp.full((m - n_lat, width), fill, F32)], axis=0)
    return pad(cos, 1.0), pad(slo, 0.0), pad(shi, 0.0)


def _mla_tables(tabs16):
    outs = []
    for a, fill in zip(tabs16, (1.0, 0.0, 0.0)):
        m = a.shape[0]
        slot = jnp.concatenate([jnp.full((m, MLA_NOPE), fill, F32), a,
                                jnp.full((m, QK_SLOT - MLA_QK), fill, F32)], axis=1)
        outs.append(jnp.tile(slot, (1, LANE // QK_SLOT)))
    return tuple(outs)


def _slots(w, heads, width):
    w = w.reshape(w.shape[0], heads, width)
    pad = jnp.zeros((w.shape[0], heads, QK_SLOT - width), w.dtype)
    return jnp.concatenate([w, pad], axis=-1).reshape(w.shape[0], heads * QK_SLOT)


def _block_diag(w):
    dirs = w.shape[0]
    eye = jnp.eye(LRU_BLOCKS, dtype=w.dtype)
    full = jnp.einsum('dnij,nm->dnimj', w, eye).reshape(dirs, LRU_WIDTH, LRU_WIDTH)
    nl = LRU_WIDTH // LRU_LANES
    return jnp.stack([full[:, j * LRU_LANES:(j + 1) * LRU_LANES, j * LRU_LANES:(j + 1) * LRU_LANES]
                      for j in range(nl)], axis=1)


def kernel(x, c, ctx, c_ctx, mod_w, mod_b, ab_w_in, ab_w_out, diff_lambda_q, diff_lambda_k, diff_subln, lru_conv_w, lru_conv_b, lru_w_a, lru_b_a, lru_w_x, lru_b_x, lru_lambda, cd_w_in, cd_w_out, gqa_q_norm, gqa_k_norm, mla_q_norm, mla_kv_norm, mla_w_uq, mla_w_ukv, router_w, router_bias, exp_w_gate, exp_w_up, exp_w_down, sh_w_gate, sh_w_up, sh_w_down, final_norm):
    bsz, n_lat, d = x.shape
    n_ctx = ctx.shape[1]
    m = n_lat + n_ctx
    depth = mod_w.shape[0]
    assert depth == 2 and bsz < MOD_ROWS
    assert n_lat % ROW_TILE == 0 and n_ctx % ROW_TILE == 0 and n_lat % GRID_W == 0
    ctx_row = bsz

    c_all = jnp.concatenate([c, c_ctx[None, :], jnp.zeros((MOD_ROWS - bsz - 1, d), F32)], axis=0)
    mods = _modulation(c_all, mod_w, mod_b)

    tabs64 = _rope_tables(n_lat, m, DIFF_HEAD_DIM, LANE)
    tabs_m = _mla_tables(_rope_tables(n_lat, m, MLA_ROPE, MLA_ROPE))

    exp_w = tuple(w.astype(BF16) for w in (exp_w_gate, exp_w_up, exp_w_down))

    def moe_weights(i):
        return exp_w + (sh_w_gate[i], sh_w_up[i], sh_w_down[i])

    fn = final_norm.reshape(1, d)

    mod = mods[0]
    qt, k, vt, u, g = _proj_ab(x, ctx, mod, ab_w_in[0].astype(BF16), tabs64, n_lat, ctx_row)
    lambda_init = 0.8 - 0.6 * math.exp(-0.3 * 0)
    dv = 2 * DIFF_HEAD_DIM + NORM_ROWS
    jobs = tuple(_Job((2 * h + j) * QK_SLOT, h, j, h * dv, dv) for h in range(DIFF_HEADS) for j in range(2))
    o_att = _attention(qt, k, vt, (diff_lambda_q[0], diff_lambda_k[0], diff_subln[0].reshape(-1, 1)),
                       jobs, n_lat, lambda_init, "diff_attention")
    o_rec = _lru(u, g, lru_conv_w[0], lru_conv_b[0].reshape(1, -1),
                 _block_diag(lru_w_a[0]).astype(BF16), lru_b_a[0].reshape(2, 1, LRU_WIDTH),
                 _block_diag(lru_w_x[0]).astype(BF16), lru_b_x[0].reshape(2, 1, LRU_WIDTH),
                 lru_lambda[0].reshape(2, 1, LRU_WIDTH), n_lat)
    xs, h2, gates = _out_proj(o_att, o_rec, x, ctx, mod, ab_w_out[0].astype(BF16),
                              router_w[0].T, router_bias[0].reshape(-1, 1), n_lat, ctx_row)
    xs = _moe(h2, gates, xs, mod, *moe_weights(0), fn, 0, n_lat, False)

    mod = mods[1]
    w_in = cd_w_in[0]
    wq, wk = GQA_HEADS * GQA_HEAD_DIM, GQA_KV_HEADS * GQA_HEAD_DIM
    o_kr = wq + 2 * wk + MLA_Q_RANK + MLA_KV_RANK
    kr_slots = jnp.concatenate([jnp.zeros((d, MLA_NOPE), F32), w_in[:, o_kr:o_kr + MLA_ROPE],
                                jnp.zeros((d, QK_SLOT - MLA_QK), F32)], axis=1)
    w_in = jnp.concatenate([w_in[:, :o_kr], jnp.tile(kr_slots, (1, MLA_HEADS))], axis=1).astype(BF16)
    ukv = mla_w_ukv[0].reshape(MLA_KV_RANK, MLA_HEADS, MLA_NOPE + MLA_V)
    w_kn = _slots(ukv[:, :, :MLA_NOPE].reshape(MLA_KV_RANK, -1), MLA_HEADS, MLA_NOPE).astype(BF16)
    w_v = ukv[:, :, MLA_NOPE:].reshape(MLA_KV_RANK, MLA_HEADS * MLA_V).astype(BF16)
    w_uq = _slots(mla_w_uq[0], MLA_HEADS, MLA_QK).astype(BF16)
    lane = jnp.arange(wq)
    ones_bd = (lane[:, None] // GQA_HEAD_DIM == lane[None, :] // GQA_HEAD_DIM).astype(BF16)
    qgt, kg, vgt, qmt, km, vmt = _proj_cd(
        xs, mod, w_in, ones_bd, jnp.tile(gqa_q_norm[0], GQA_HEADS).reshape(1, -1),
        jnp.tile(gqa_k_norm[0], GQA_KV_HEADS).reshape(1, -1), mla_q_norm[0].reshape(1, -1),
        mla_kv_norm[0].reshape(1, -1), w_uq, w_kn, w_v, tabs64, tabs_m, n_lat, ctx_row)
    groups = GQA_HEADS // GQA_KV_HEADS
    dvn = GQA_HEAD_DIM + NORM_ROWS
    jobs = tuple(_Job(h * QK_SLOT, 0, h // groups, (h // groups) * dvn, dvn) for h in range(GQA_HEADS))
    o_gqa = _attention(qgt, kg, vgt, (), jobs, n_lat, None, "gqa_attention")
    dvn = MLA_V + NORM_ROWS
    jobs = tuple(_Job(h * QK_SLOT, h // 2, h % 2, h * dvn, dvn) for h in range(MLA_HEADS))
    o_mla = _attention(qmt, km, vmt, (), jobs, n_lat, None, "mla_attention")
    xs, h2, gates = _out_proj(o_gqa, o_mla, xs, None, mod, cd_w_out[0].astype(BF16),
                              router_w[1].T, router_bias[1].reshape(-1, 1), n_lat, ctx_row)
    return _moe(h2, gates, xs, mod, *moe_weights(1), fn, 1, n_lat, True)
```

```python
import functools
import math
from typing import NamedTuple

import jax
import jax.numpy as jnp
from jax import lax
from jax.experimental import pallas as pl
from jax.experimental.pallas import tpu as pltpu

F32 = jnp.float32
BF16 = jnp.bfloat16

GRID_W = 64
ROPE_THETA = 10000.0
EPS = 1e-6
DIFF_HEADS = 4
DIFF_HEAD_DIM = 64
LRU_WIDTH = 512
LRU_BLOCKS = 8
LRU_BW = LRU_WIDTH // LRU_BLOCKS
CONV_W = 4
LRU_C = 8.0
GQA_HEADS = 8
GQA_KV_HEADS = 2
GQA_HEAD_DIM = 64
MLA_HEADS = 8
MLA_Q_RANK = 256
MLA_KV_RANK = 128
MLA_NOPE = 32
MLA_ROPE = 16
MLA_V = 64
MLA_QK = MLA_NOPE + MLA_ROPE
N_EXPERTS = 64
N_GROUPS = 8
PER_GROUP = N_EXPERTS // N_GROUPS
TOPK_GROUPS = 4
TOP_K = 8
EXPERT_FF = 256
ROUTED_SCALE = 2.5

QK_SLOT = 64
NORM_ROWS = 16
LOG2E = math.log2(math.e)

ROW_TILE = 256
ROUTE_ROWS = 128
HALF_ROWS = 128
LRU_CHUNK = 128
LRU_LANES = 256
EXPERTS_PER_STEP = 4
LANE = 128
SUBLANE = 8
MOD_ROWS = 8
VMEM_LIMIT = 56 * 1024 * 1024
MOE_VMEM_LIMIT = 62 * 1024 * 1024
MOE_TILE = 1024
MOE_CAP = 160
SEG_ALIGN = 16
MOE_SLOTS = MOE_TILE * TOP_K + N_EXPERTS * SEG_ALIGN
COMBINE_BLOCK = 1024


def _params(sem, vmem=VMEM_LIMIT):
    return pltpu.CompilerParams(dimension_semantics=sem, vmem_limit_bytes=vmem)


def _silu(x):
    return x * jax.nn.sigmoid(x)


def _rms(x):
    return x * lax.rsqrt(jnp.mean(x * x, axis=-1, keepdims=True) + EPS)


def _mod_rows(mod_ref, ctx_row, b, k, d, row0, rows, n_lat):
    lat = mod_ref[pl.ds(b, 1), k * d:(k + 1) * d]
    ctx = mod_ref[ctx_row:ctx_row + 1, k * d:(k + 1) * d]
    if n_lat % rows == 0:
        return jnp.where(row0 >= n_lat, ctx, lat)
    r = row0 + lax.broadcasted_iota(jnp.int32, (rows, 1), 0)
    return jnp.where(r >= n_lat, ctx, lat)


def _stream_specs(x, ctx, n_lat):
    d = x.shape[-1]
    if ctx is None:
        spec = pl.BlockSpec((1, ROW_TILE, d), lambda b, t: (b, t, 0))
        return (x, x), [spec, spec]
    last = n_lat // ROW_TILE - 1
    return (x, ctx), [pl.BlockSpec((1, ROW_TILE, d), lambda b, t: (b, jnp.minimum(t, last), 0)),
                      pl.BlockSpec((1, ROW_TILE, d), lambda b, t: (b, jnp.maximum(t - last - 1, 0), 0))]


def _values_t(v, heads):
    vt = v.T
    dv = vt.shape[0] // heads
    ones = jnp.ones((NORM_ROWS, vt.shape[1]), F32)
    return jnp.concatenate([piece for h in range(heads) for piece in (vt[h * dv:(h + 1) * dv], ones)],
                           axis=0).astype(BF16)


def _rope(x, cos, sin_lo, sin_hi, half):
    n = x.shape[-1]
    return x * cos + pltpu.roll(x, n - half, 1) * sin_lo + pltpu.roll(x, half, 1) * sin_hi


def _mod_kernel(c_ref, w_ref, b_ref, o_ref):
    s = _silu(c_ref[...])
    o_ref[0] = jnp.dot(s, w_ref[0], preferred_element_type=F32, precision=lax.Precision.HIGHEST) + b_ref[0]


def _modulation(c_all, mod_w, mod_b):
    depth, d, n = mod_w.shape
    tn = n // 4
    return pl.pallas_call(
        _mod_kernel,
        grid=(depth, n // tn),
        in_specs=[pl.BlockSpec((MOD_ROWS, d), lambda i, j: (0, 0)),
                  pl.BlockSpec((1, d, tn), lambda i, j: (i, 0, j)),
                  pl.BlockSpec((1, 1, tn), lambda i, j: (i, 0, j))],
        out_specs=pl.BlockSpec((1, MOD_ROWS, tn), lambda i, j: (i, 0, j)),
        out_shape=jax.ShapeDtypeStruct((depth, MOD_ROWS, n), F32),
        compiler_params=_params(("arbitrary", "arbitrary")),
        name="modulation",
    )(c_all, mod_w, mod_b.reshape(depth, 1, n))


def _proj_ab_kernel(x_ref, xc_ref, mod_ref, w_ref, cos_ref, slo_ref, shi_ref,
                    qt_ref, k_ref, vt_ref, u_ref, g_ref, *, n_lat, ctx_row):
    b, t = pl.program_id(0), pl.program_id(1)
    d = x_ref.shape[-1]
    row0 = t * ROW_TILE
    shift = _mod_rows(mod_ref, ctx_row, b, 0, d, row0, ROW_TILE, n_lat)
    scale = _mod_rows(mod_ref, ctx_row, b, 1, d, row0, ROW_TILE, n_lat)
    w = DIFF_HEADS * 2 * DIFF_HEAD_DIM
    for r0 in range(0, ROW_TILE, HALF_ROWS):
        rows = slice(r0, r0 + HALF_ROWS)
        x = jnp.where(row0 >= n_lat, xc_ref[0, rows], x_ref[0, rows])
        h = (_rms(x) * (1.0 + scale) + shift).astype(BF16)
        acc = jnp.dot(h, w_ref[...], preferred_element_type=F32)
        cos, slo, shi = (jnp.tile(r[rows], (1, w // LANE)) for r in (cos_ref, slo_ref, shi_ref))
        q = _rope(acc[:, 0:w], cos, slo, shi, DIFF_HEAD_DIM // 4) * (DIFF_HEAD_DIM ** -0.5 * LOG2E)
        k = _rope(acc[:, w:2 * w], cos, slo, shi, DIFF_HEAD_DIM // 4)
        qt_ref[0, 0, :, rows] = q.T.astype(BF16)
        k_ref[0, rows] = k.astype(BF16)
        vt_ref[0, :, rows] = _values_t(acc[:, 2 * w:3 * w], DIFF_HEADS)
        u_ref[0, rows] = acc[:, 3 * w:3 * w + LRU_WIDTH]
        g_ref[0, rows] = jax.nn.gelu(acc[:, 3 * w + LRU_WIDTH:3 * w + 2 * LRU_WIDTH]).astype(BF16)


def _proj_ab(x, ctx, mod, w_in, tabs, n_lat, ctx_row):
    bsz, _, d = x.shape
    m = n_lat + ctx.shape[1]
    nt = m // ROW_TILE
    streams, stream_specs = _stream_specs(x, ctx, n_lat)
    w = DIFF_HEADS * 2 * DIFF_HEAD_DIM
    n_in = w_in.shape[1]
    tile = lambda n: pl.BlockSpec((1, ROW_TILE, n), lambda b, t: (b, t, 0))
    ttile = pl.BlockSpec((1, 1, w, ROW_TILE), lambda b, t: (b, t, 0, 0))
    wv = w + DIFF_HEADS * NORM_ROWS
    vtile = pl.BlockSpec((1, wv, ROW_TILE), lambda b, t: (b, 0, t))
    tab = pl.BlockSpec((ROW_TILE, LANE), lambda b, t: (t, 0))
    return pl.pallas_call(
        functools.partial(_proj_ab_kernel, n_lat=n_lat, ctx_row=ctx_row),
        grid=(bsz, nt),
        in_specs=stream_specs + [pl.BlockSpec(mod.shape, lambda b, t: (0, 0)),
                                 pl.BlockSpec((d, n_in), lambda b, t: (0, 0)),
                                 tab, tab, tab],
        out_specs=[ttile, tile(w), vtile, tile(LRU_WIDTH), tile(LRU_WIDTH)],
        out_shape=[jax.ShapeDtypeStruct((bsz, nt, w, ROW_TILE), BF16),
                   jax.ShapeDtypeStruct((bsz, m, w), BF16),
                   jax.ShapeDtypeStruct((bsz, wv, m), BF16),
                   jax.ShapeDtypeStruct((bsz, m, LRU_WIDTH), F32),
                   jax.ShapeDtypeStruct((bsz, m, LRU_WIDTH), BF16)],
        compiler_params=_params(("parallel", "parallel")),
        name="proj_ab",
    )(*streams, mod, w_in, *tabs)


def _group_rms(x, ones_bd):
    x2 = x * x
    hi = x2.astype(BF16)
    lo = (x2 - hi.astype(F32)).astype(BF16)
    ss = jnp.dot(hi, ones_bd, preferred_element_type=F32) + jnp.dot(lo, ones_bd, preferred_element_type=F32)
    return x * lax.rsqrt(ss * (1.0 / GQA_HEAD_DIM) + EPS)


def _proj_cd_kernel(x_ref, mod_ref, w_ref, bd_ref, qn_ref, kn_ref, cqn_ref, ckvn_ref, wuq_ref, wkn_ref, wv_ref,
                    cos_ref, slo_ref, shi_ref, cosm_ref, slom_ref, shim_ref,
                    qgt_ref, kg_ref, vgt_ref, qmt_ref, km_ref, vmt_ref, *, n_lat, ctx_row):
    b, t = pl.program_id(0), pl.program_id(1)
    d = x_ref.shape[-1]
    row0 = t * ROW_TILE
    shift = _mod_rows(mod_ref, ctx_row, b, 0, d, row0, ROW_TILE, n_lat)
    scale = _mod_rows(mod_ref, ctx_row, b, 1, d, row0, ROW_TILE, n_lat)
    wq = GQA_HEADS * GQA_HEAD_DIM
    wk = GQA_KV_HEADS * GQA_HEAD_DIM
    bd = bd_ref[...]
    for r0 in range(0, ROW_TILE, HALF_ROWS):
        rows = slice(r0, r0 + HALF_ROWS)
        h = (_rms(x_ref[0, rows]) * (1.0 + scale) + shift).astype(BF16)
        acc = jnp.dot(h, w_ref[...], preferred_element_type=F32)
        o = 0
        q = acc[:, o:o + wq]; o += wq
        k = acc[:, o:o + wk]; o += wk
        v = acc[:, o:o + wk]; o += wk
        cq = acc[:, o:o + MLA_Q_RANK]; o += MLA_Q_RANK
        ckv = acc[:, o:o + MLA_KV_RANK]; o += MLA_KV_RANK
        kr = acc[:, o:o + MLA_HEADS * QK_SLOT]
        cos, slo, shi = (jnp.tile(r[rows], (1, wq // LANE)) for r in (cos_ref, slo_ref, shi_ref))
        q = (_rope(_group_rms(q, bd) * qn_ref[...], cos, slo, shi, GQA_HEAD_DIM // 4)
             * (GQA_HEAD_DIM ** -0.5 * LOG2E))
        k = _rope(_group_rms(k, bd[:wk, :wk]) * kn_ref[...], cos[:, :wk], slo[:, :wk], shi[:, :wk],
                  GQA_HEAD_DIM // 4)
        qgt_ref[0, 0, :, rows] = q.T.astype(BF16)
        kg_ref[0, rows] = k.astype(BF16)
        vgt_ref[0, :, rows] = _values_t(v, GQA_KV_HEADS)
        cosm, slom, shim = (jnp.tile(r[rows], (1, MLA_HEADS * QK_SLOT // LANE))
                            for r in (cosm_ref, slom_ref, shim_ref))
        cqn = (_rms(cq) * cqn_ref[...]).astype(BF16)
        mq = jnp.dot(cqn, wuq_ref[...], preferred_element_type=F32)
        qmt_ref[0, 0, :, rows] = (_rope(mq, cosm, slom, shim, MLA_ROPE // 4)
                                  * (MLA_QK ** -0.5 * LOG2E)).T.astype(BF16)
        ckvn = (_rms(ckv) * ckvn_ref[...]).astype(BF16)
        km = jnp.dot(ckvn, wkn_ref[...], preferred_element_type=F32) + _rope(kr, cosm, slom, shim, MLA_ROPE // 4)
        km_ref[0, rows] = km.astype(BF16)
        vmt_ref[0, :, rows] = _values_t(jnp.dot(ckvn, wv_ref[...], preferred_element_type=F32), MLA_HEADS)


def _proj_cd(xs, mod, w_in, bd, qn, kn, cqn, ckvn, wuq, wkn, wv, tabs, tabs_m, n_lat, ctx_row):
    bsz, m, d = xs.shape
    nt = m // ROW_TILE
    wq = GQA_HEADS * GQA_HEAD_DIM
    wk = GQA_KV_HEADS * GQA_HEAD_DIM
    wm = MLA_HEADS * QK_SLOT
    wmv = MLA_HEADS * (MLA_V + NORM_ROWS)
    wgv = GQA_KV_HEADS * (GQA_HEAD_DIM + NORM_ROWS)
    tile = lambda n: pl.BlockSpec((1, ROW_TILE, n), lambda b, t: (b, t, 0))
    ttile = lambda n: pl.BlockSpec((1, 1, n, ROW_TILE), lambda b, t: (b, t, 0, 0))
    vtile = lambda n: pl.BlockSpec((1, n, ROW_TILE), lambda b, t: (b, 0, t))
    whole = lambda a: pl.BlockSpec(a.shape, lambda b, t: (0,) * a.ndim)
    tab = pl.BlockSpec((ROW_TILE, LANE), lambda b, t: (t, 0))
    tabm = tab
    return pl.pallas_call(
        functools.partial(_proj_cd_kernel, n_lat=n_lat, ctx_row=ctx_row),
        grid=(bsz, nt),
        in_specs=[tile(d), whole(mod), whole(w_in), whole(bd), whole(qn), whole(kn), whole(cqn), whole(ckvn),
                  whole(wuq), whole(wkn), whole(wv), tab, tab, tab, tabm, tabm, tabm],
        out_specs=[ttile(wq), tile(wk), vtile(wgv), ttile(wm), tile(wm), vtile(wmv)],
        out_shape=[jax.ShapeDtypeStruct((bsz, nt, wq, ROW_TILE), BF16),
                   jax.ShapeDtypeStruct((bsz, m, wk), BF16),
                   jax.ShapeDtypeStruct((bsz, wgv, m), BF16),
                   jax.ShapeDtypeStruct((bsz, nt, wm, ROW_TILE), BF16),
                   jax.ShapeDtypeStruct((bsz, m, wm), BF16),
                   jax.ShapeDtypeStruct((bsz, wmv, m), BF16)],
        compiler_params=_params(("parallel", "parallel")),
        name="proj_cd",
    )(xs, mod, w_in, bd, qn, kn, cqn, ckvn, wuq, wkn, wv, *tabs, *tabs_m)


class _Job(NamedTuple):
    q_row: int
    k_group: int
    k_half: int
    v_row: int
    dv: int


def _attend(qt_ref, qnext_ref, k_ref, vt_ref, s_ref, mp_ref, jobs, lo, hi, chained):
    assert len(jobs) % 2 == 0
    tq = qt_ref.shape[-1]
    n = hi - lo
    group = next(g for g in (11, 3, 2, 1) if n % g == 0)
    steps = n // group
    row_groups = ROW_TILE // SUBLANE

    def weights(job, ref=qt_ref):
        qh = ref[0, 0, job.q_row:job.q_row + QK_SLOT, :]
        z = jnp.zeros_like(qh)
        return jnp.concatenate([z, qh] if job.k_half else [qh, z], axis=0)

    def score(c0, buf, qw, job, mp):
        r0 = pl.multiple_of(c0 * ROW_TILE, ROW_TILE)
        kc = k_ref[0, pl.ds(r0, group * ROW_TILE), job.k_group * LANE:(job.k_group + 1) * LANE]
        s_all = jnp.dot(kc, qw, preferred_element_type=F32)
        for g in range(group):
            s = s_all[g * ROW_TILE:(g + 1) * ROW_TILE]
            s_ref[buf, c0 + g] = s
            mp = jnp.maximum(mp, jnp.max(s.reshape(row_groups, SUBLANE, tq), axis=0))
        return mp

    def value(c0, buf, m, job, acc):
        ps = [jnp.exp2(s_ref[buf, c0 + g] - m).astype(BF16) for g in range(group)]
        k0 = pl.multiple_of(c0 * ROW_TILE, ROW_TILE)
        v = vt_ref[0, job.v_row:job.v_row + job.dv, pl.ds(k0, group * ROW_TILE)]
        return acc + jnp.dot(v, jnp.concatenate(ps, axis=0), preferred_element_type=F32)

    def loop(body, init):
        return body(0, init) if steps == 1 else lax.fori_loop(0, steps, body, init)

    neg = jnp.full((SUBLANE, tq), -jnp.inf, F32)

    def prologue():
        qw = weights(jobs[0])

        def first(i, mp):
            return score(lo + i * group, 0, qw, jobs[0], mp)

        mp_ref[...] = loop(first, neg)

    if chained:
        pl.when(pl.program_id(1) == 0)(prologue)
    else:
        prologue()
    mp = mp_ref[...]
    results = []
    for j, job in enumerate(jobs):
        buf = j % 2
        m = jnp.max(mp, axis=0, keepdims=True)
        if j + 1 < len(jobs):
            nxt, qw = jobs[j + 1], weights(jobs[j + 1])
        elif chained:
            nxt, qw = jobs[0], weights(jobs[0], qnext_ref)
        else:
            nxt = None

        def body(i, carry):
            mp, acc = carry
            if nxt is not None:
                mp = score(lo + i * group, 1 - buf, qw, nxt, mp)
            return mp, value(lo + i * group, buf, m, job, acc)

        mp, acc = loop(body, (neg, jnp.zeros((job.dv, tq), F32)))
        dv = job.dv - NORM_ROWS
        results.append((acc[:dv], acc[dv:dv + 1]))
    if chained:
        mp_ref[...] = mp
    return results


def _attn_kernel(*refs, jobs, n_lat, lambda_init):
    if lambda_init is None:
        qt_ref, qnext_ref, k_ref, vt_ref, o_ref, s_ref, mp_ref = refs
    else:
        qt_ref, qnext_ref, k_ref, vt_ref, lq_ref, lk_ref, sub_ref, o_ref, s_ref, mp_ref = refs
    n_chunks = k_ref.shape[1] // ROW_TILE
    first_ctx = n_lat // ROW_TILE
    t = pl.program_id(1)

    def run(lo, hi, chained):
        res = _attend(qt_ref, qnext_ref, k_ref, vt_ref, s_ref, mp_ref, jobs, lo, hi, chained)
        if lambda_init is None:
            outs = [acc / l for acc, l in res]
        else:
            lq, lk = lq_ref[...], lk_ref[...]
            lam = (jnp.exp(jnp.sum(lq[0:1] * lk[0:1], axis=-1, keepdims=True))
                   - jnp.exp(jnp.sum(lq[1:2] * lk[1:2], axis=-1, keepdims=True)) + lambda_init)
            outs = []
            for h in range(len(res) // 2):
                (a0, l0), (a1, l1) = res[2 * h], res[2 * h + 1]
                o = a0 / l0 - lam * (a1 / l1)
                o = o * lax.rsqrt(jnp.mean(o * o, axis=0, keepdims=True) + EPS)
                outs.append(o * sub_ref[...] * (1.0 - lambda_init))
        o_ref[0] = jnp.concatenate(outs, axis=0).T.astype(BF16)

    @pl.when(t < first_ctx)
    def _():
        run(0, n_chunks, True)

    @pl.when(t >= first_ctx)
    def _():
        run(first_ctx, n_chunks, False)


def _attention(qt, k, vt, extra, jobs, n_lat, lambda_init, name):
    bsz, nt, wq, tq = qt.shape
    m = k.shape[1]
    wo = sum(j.dv - NORM_ROWS for j in jobs) // (1 if lambda_init is None else 2)
    once = pl.Buffered(1)
    whole = lambda a: pl.BlockSpec(a.shape, lambda b, t: (0,) * a.ndim)
    return pl.pallas_call(
        functools.partial(_attn_kernel, jobs=jobs, n_lat=n_lat, lambda_init=lambda_init),
        grid=(bsz, nt),
        in_specs=[pl.BlockSpec((1, 1, wq, tq), lambda b, t: (b, t, 0, 0)),
                  pl.BlockSpec((1, 1, wq, tq), lambda b, t: (b, jnp.minimum(t + 1, nt - 1), 0, 0)),
                  pl.BlockSpec((1,) + k.shape[1:], lambda b, t: (b, 0, 0), pipeline_mode=once),
                  pl.BlockSpec((1,) + vt.shape[1:], lambda b, t: (b, 0, 0), pipeline_mode=once)]
                 + [whole(a) for a in extra],
        out_specs=pl.BlockSpec((1, tq, wo), lambda b, t: (b, t, 0)),
        out_shape=jax.ShapeDtypeStruct((bsz, m, wo), BF16),
        scratch_shapes=[pltpu.VMEM((2, nt, ROW_TILE, tq), F32), pltpu.VMEM((SUBLANE, tq), F32)],
        compiler_params=_params(("arbitrary", "arbitrary")),
        name=name,
    )(qt, qt, k, vt, *extra)


def _scan_chunk(a, b, carry, reverse):
    n = a.shape[0]
    rows = lax.broadcasted_iota(jnp.int32, (n, 1), 0)
    d = 1
    while d < n:
        sh = n - d if reverse else d
        valid = rows < n - d if reverse else rows >= d
        b = jnp.where(valid, a * pltpu.roll(b, sh, 0) + b, b)
        a = jnp.where(valid, a * pltpu.roll(a, sh, 0), a)
        d *= 2
    h = a * carry + b
    return h, (h[0:1] if reverse else h[n - 1:n])


def _lru_kernel(u_ref, g_ref, cw_ref, cb_ref, wa_ref, ba_ref, wx_ref, bx_ref, lam_ref, o_ref, up_ref, hf_ref, *, n_lat):
    m = u_ref.shape[1]
    lanes = u_ref.shape[2]
    n_chunks = m // LRU_CHUNK
    n_ctx_chunks = (m - n_lat) // LRU_CHUNK
    pad = SUBLANE

    up_ref[0:pad, :] = jnp.zeros((pad, lanes), F32)
    up_ref[pad + m:pad + m + pad, :] = jnp.zeros((pad, lanes), F32)

    def copy(c, _):
        r0 = pl.multiple_of(c * LRU_CHUNK, LRU_CHUNK)
        up_ref[pl.ds(pad + r0, LRU_CHUNK), :] = u_ref[0, pl.ds(r0, LRU_CHUNK), :]
        return 0

    lax.fori_loop(0, n_chunks, copy, 0)

    cw = cw_ref[...]
    cb = cb_ref[...]
    win_rows = LRU_CHUNK + 2 * pad

    def coeffs(c, d):
        r0 = pl.multiple_of(c * LRU_CHUNK, LRU_CHUNK)
        win = up_ref[pl.ds(r0, win_rows), :]
        r = r0 + lax.broadcasted_iota(jnp.int32, (LRU_CHUNK, 1), 0)
        at = lambda k: pltpu.roll(win, (win_rows - k) % win_rows, 0)[pad:pad + LRU_CHUNK]
        y = (cw[0:1] * jnp.where((r == n_lat) | (r == n_lat + 1), 0.0, at(-2))
             + cw[1:2] * jnp.where(r == n_lat, 0.0, at(-1))
             + cw[2:3] * win[pad:pad + LRU_CHUNK]
             + cw[3:4] * jnp.where(r == n_lat - 1, 0.0, at(1))
             + cb)
        yb = y.astype(BF16)
        ra = jax.nn.sigmoid(jnp.dot(yb, wa_ref[d, 0], preferred_element_type=F32) + ba_ref[d])
        ix = jax.nn.sigmoid(jnp.dot(yb, wx_ref[d, 0], preferred_element_type=F32) + bx_ref[d])
        z = -lam_ref[d]
        softplus = jnp.maximum(z, 0.0) + jnp.log(1.0 + jnp.exp(-jnp.abs(z)))
        a = jnp.exp(-LRU_C * ra * softplus)
        return a, jnp.sqrt(1.0 - a * a) * ix * y

    def fwd(s, carry):
        c = lax.rem(s + (n_chunks - n_ctx_chunks), n_chunks)
        a, b = coeffs(c, 0)
        h, carry = _scan_chunk(a, b, carry, False)
        r0 = pl.multiple_of(c * LRU_CHUNK, LRU_CHUNK)
        hf_ref[pl.ds(r0, LRU_CHUNK), :] = h
        return carry

    lax.fori_loop(0, n_chunks, fwd, jnp.zeros((1, lanes), F32))

    def bwd(s, carry):
        c = n_chunks - 1 - s
        a, b = coeffs(c, 1)
        h, carry = _scan_chunk(a, b, carry, True)
        r0 = pl.multiple_of(c * LRU_CHUNK, LRU_CHUNK)
        tot = hf_ref[pl.ds(r0, LRU_CHUNK), :] + h
        o_ref[0, pl.ds(r0, LRU_CHUNK), :] = (tot * g_ref[0, pl.ds(r0, LRU_CHUNK), :].astype(F32)).astype(BF16)
        return carry

    lax.fori_loop(0, n_chunks, bwd, jnp.zeros((1, lanes), F32))


def _lru(u, g, conv_w, conv_b, wa_bd, ba, wx_bd, bx, lam, n_lat):
    bsz, m, c = u.shape
    nl = c // LRU_LANES
    seq = lambda: pl.BlockSpec((1, m, LRU_LANES), lambda b, j: (b, 0, j))
    vec = lambda a: pl.BlockSpec(a.shape[:-1] + (LRU_LANES,), lambda b, j: (0,) * (a.ndim - 1) + (j,))
    mat = pl.BlockSpec((2, 1, LRU_LANES, LRU_LANES), lambda b, j: (0, j, 0, 0))
    return pl.pallas_call(
        functools.partial(_lru_kernel, n_lat=n_lat),
        grid=(bsz, nl),
        in_specs=[seq(), seq(), vec(conv_w), vec(conv_b), mat, vec(ba), mat, vec(bx), vec(lam)],
        out_specs=seq(),
        out_shape=jax.ShapeDtypeStruct((bsz, m, c), BF16),
        scratch_shapes=[pltpu.VMEM((m + 2 * SUBLANE, LRU_LANES), F32), pltpu.VMEM((m, LRU_LANES), F32)],
        compiler_params=_params(("parallel", "parallel")),
        name="rglru",
    )(u, g, conv_w, conv_b, wa_bd, ba, wx_bd, bx, lam)


def _route(logits_t, bias):
    n = logits_t.shape[-1]
    scores = jax.nn.sigmoid(logits_t)
    choice = scores + bias
    sub = lax.broadcasted_iota(jnp.int32, (PER_GROUP, n), 0)
    neg = -jnp.inf
    groups, gs = [], []
    for g in range(N_GROUPS):
        cg = choice[g * PER_GROUP:(g + 1) * PER_GROUP]
        m1 = jnp.max(cg, axis=0, keepdims=True)
        i1 = jnp.min(jnp.where(cg == m1, sub, PER_GROUP), axis=0, keepdims=True)
        m2 = jnp.max(jnp.where(sub == i1, neg, cg), axis=0, keepdims=True)
        groups.append(cg)
        gs.append(m1 + m2)
    masked = []
    for g in range(N_GROUPS):
        rank = jnp.zeros((1, n), jnp.int32)
        for o in range(N_GROUPS):
            if o != g:
                ahead = (gs[o] >= gs[g]) if o < g else (gs[o] > gs[g])
                rank = rank + jnp.where(ahead, 1, 0)
        masked.append(jnp.where(rank < TOPK_GROUPS, groups[g], neg))
    masked = jnp.concatenate(masked, axis=0)
    eidx = lax.broadcasted_iota(jnp.int32, (N_EXPERTS, n), 0)
    picked = jnp.zeros((N_EXPERTS, n), F32)
    for _ in range(TOP_K):
        mx = jnp.max(masked, axis=0, keepdims=True)
        first = jnp.min(jnp.where(masked == mx, eidx, N_EXPERTS), axis=0, keepdims=True)
        hit = eidx == first
        picked = jnp.where(hit, 1.0, picked)
        masked = jnp.where(hit, neg, masked)
    w = picked * scores
    return w / jnp.sum(w, axis=0, keepdims=True) * ROUTED_SCALE


def _out_proj_kernel(oa_ref, ob_ref, x_ref, xc_ref, mod_ref, wa_ref, wb_ref, rwh_ref, rwl_ref, rb_ref,
                     xo_ref, h_ref, gates_ref, *, n_lat, ctx_row):
    b, t = pl.program_id(0), pl.program_id(1)
    d = x_ref.shape[-1]
    row0 = t * ROW_TILE
    mv = lambda k: _mod_rows(mod_ref, ctx_row, b, k, d, row0, ROW_TILE, n_lat)
    nt = lambda a, b: lax.dot_general(a, b, (((1,), (1,)), ((), ())), preferred_element_type=F32)
    for r0 in range(0, ROW_TILE, ROUTE_ROWS):
        rows = slice(r0, r0 + ROUTE_ROWS)
        y = (jnp.dot(oa_ref[0, rows], wa_ref[...], preferred_element_type=F32)
             + jnp.dot(ob_ref[0, rows], wb_ref[...], preferred_element_type=F32))
        x = jnp.where(row0 >= n_lat, xc_ref[0, rows], x_ref[0, rows]) + mv(2) * y
        xo_ref[0, rows] = x
        h = _rms(x) * (1.0 + mv(4)) + mv(3)
        hb = h.astype(BF16)
        h_ref[0, rows] = hb
        h_lo = (h - hb.astype(F32)).astype(BF16)
        logits_t = nt(rwh_ref[...], hb) + (nt(rwh_ref[...], h_lo) + nt(rwl_ref[...], hb))
        gates_t = _route(logits_t, rb_ref[...])
        gates_t = jnp.concatenate([gates_t, jnp.zeros((LANE - N_EXPERTS, ROUTE_ROWS), F32)], axis=0)
        gates_ref[0, rows] = gates_t.T


def _out_proj(oa, ob, x, ctx, mod, w_out, router_wt, router_b, n_lat, ctx_row):
    bsz, m, d = oa.shape[0], oa.shape[1], x.shape[-1]
    streams, stream_specs = _stream_specs(x, ctx, n_lat)
    na, nb = oa.shape[-1], ob.shape[-1]
    wa, wb = w_out[:na], w_out[na:]
    rw_hi = router_wt.astype(BF16)
    rw_lo = (router_wt - rw_hi.astype(F32)).astype(BF16)
    tile = lambda n: pl.BlockSpec((1, ROW_TILE, n), lambda b, t: (b, t, 0))
    whole = lambda a: pl.BlockSpec(a.shape, lambda b, t: (0,) * a.ndim)
    return pl.pallas_call(
        functools.partial(_out_proj_kernel, n_lat=n_lat, ctx_row=ctx_row),
        grid=(bsz, m // ROW_TILE),
        in_specs=[tile(na), tile(nb)] + stream_specs + [whole(mod), whole(wa), whole(wb), whole(rw_hi), whole(rw_lo),
                                                        whole(router_b)],
        out_specs=[tile(d), tile(d), tile(LANE)],
        out_shape=[jax.ShapeDtypeStruct((bsz, m, d), F32),
                   jax.ShapeDtypeStruct((bsz, m, d), BF16),
                   jax.ShapeDtypeStruct((bsz, m, LANE), F32)],
        compiler_params=_params(("parallel", "parallel")),
        name="out_proj_router",
    )(oa, ob, *streams, mod, wa, wb, rw_hi, rw_lo, router_b)


def _plan_kernel(gates_ref, tri_ref, upper_ref, pos_ref, gem_ref, cnt_ref, off_ref, own_ref):
    g = gates_ref[0]
    sel = g != 0.0
    one = jnp.where(sel, 1.0, 0.0)
    pos = jnp.dot(tri_ref[...], one.astype(BF16), preferred_element_type=F32)
    cnt = jnp.sum(one, axis=0, keepdims=True)
    cpad = jnp.floor((cnt + (SEG_ALIGN - 1)) * (1.0 / SEG_ALIGN)) * SEG_ALIGN
    off = jnp.dot(jnp.broadcast_to(cpad, (SUBLANE, LANE)), upper_ref[...], preferred_element_type=F32,
                  precision=lax.Precision.HIGHEST)[0:1]
    pos_ref[0] = jnp.where(sel, pos, -1.0).T
    gem_ref[0] = g.T
    cnt_ref[0] = jnp.broadcast_to(cnt, (SUBLANE, LANE)).astype(jnp.int32)
    off_ref[0] = jnp.broadcast_to(off, (SUBLANE, LANE)).astype(jnp.int32)
    groups = own_ref.shape[-1]
    ends = jnp.broadcast_to(off + cpad, (LANE, LANE)).T
    ends = jnp.concatenate([ends] * (groups // LANE), axis=1)
    start = lax.broadcasted_iota(jnp.int32, (LANE, groups), 1).astype(F32) * SEG_ALIGN
    real = lax.broadcasted_iota(jnp.int32, (LANE, groups), 0) < N_EXPERTS
    before = jnp.where(real, jnp.where(ends <= start, 1.0, 0.0), 0.0)
    owner = jnp.minimum(jnp.sum(before, axis=0, keepdims=True), N_EXPERTS - 1.0)
    own_ref[0] = jnp.broadcast_to(owner, (SUBLANE, groups)).astype(jnp.int32)


def _moe_kernel(cnt_s, off_s, own_s, h_ref, pos_ref, gem_ref, wg_ref, wu_ref, wd_ref,
                sg_ref, su_ref, sd_ref, x_ref, mod_ref, fn_ref, o_ref, y_ref, p_ref,
                *, bsz, rows_per_batch, n_lat, tiles_per_batch, final):
    i, s = pl.program_id(0), pl.program_id(1)
    rows, d = h_ref.shape[1], h_ref.shape[2]
    h = h_ref[0]

    def ffn(x, wg, wu, wd, gate):
        a = _silu(jnp.dot(x, wg, preferred_element_type=F32)) * jnp.dot(x, wu, preferred_element_type=F32)
        if gate is not None:
            a = a * gate
        return jnp.dot(a.astype(BF16), wd, preferred_element_type=F32)

    @pl.when((s == 0) & (i == 0))
    def _():
        y_ref[...] = jnp.zeros(y_ref.shape, BF16)

    @pl.when(s == 0)
    def _():
        o_ref[0] = ffn(h, sg_ref[...].astype(BF16), su_ref[...].astype(BF16), sd_ref[...].astype(BF16), None)

    cap = MOE_CAP
    cap_rows = lax.broadcasted_iota(jnp.int32, (cap, 1), 0)

    def onehot(pos_row, q):
        ranks = cap_rows + q * cap
        hit = pos_row == ranks.astype(F32)
        return ranks, hit, jnp.where(hit, 1.0, 0.0).astype(BF16)

    def expert_rows(x, ranks, hit, q, gate_row, w, cnt, off):
        gate = jnp.sum(jnp.where(hit, gate_row, 0.0), axis=1, keepdims=True)
        y = ffn(x, *w, gate)
        dst = pl.multiple_of(off + q * cap, SEG_ALIGN)
        y_ref[pl.ds(dst, cap), :] = jnp.where(ranks < cnt, y.astype(BF16), y_ref[pl.ds(dst, cap), :])

    experts = []
    for j in range(EXPERTS_PER_STEP):
        e = s * EXPERTS_PER_STEP + j
        experts.append((cnt_s[i, e], off_s[i, e],
                        pos_ref[0, pl.ds(e, 1), :],
                        gem_ref[0, pl.ds(e, 1), :],
                        (wg_ref[j], wu_ref[j], wd_ref[j])))

    hots = [onehot(pos_row, 0) for _, _, pos_row, _, _ in experts]
    xs = jnp.dot(jnp.concatenate([p for _, _, p in hots], axis=0), h, preferred_element_type=F32).astype(BF16)
    for j, ((cnt, off, _, gate_row, w), (ranks, hit, _)) in enumerate(zip(experts, hots)):
        expert_rows(xs[j * cap:(j + 1) * cap], ranks, hit, 0, gate_row, w, cnt, off)

    for cnt, off, pos_row, gate_row, w in experts:
        def block(q, carry):
            ranks, hit, p = onehot(pos_row, q)
            x = jnp.dot(p, h, preferred_element_type=F32).astype(BF16)
            expert_rows(x, ranks, hit, q, gate_row, w, cnt, off)
            return carry

        lax.fori_loop(1, (cnt + cap - 1) // cap, block, 0)

    @pl.when(s == N_EXPERTS // EXPERTS_PER_STEP - 1)
    def _():
        group_rows = lax.broadcasted_iota(jnp.int32, (SEG_ALIGN, 1), 0)
        groups_per_block = COMBINE_BLOCK // SEG_ALIGN

        def combine(r, carry):
            base = pl.multiple_of(r * COMBINE_BLOCK, COMBINE_BLOCK)

            for gi in range(groups_per_block):
                e = own_s[i, r * groups_per_block + gi]
                rank = (group_rows + (base + gi * SEG_ALIGN - off_s[i, e])).astype(F32)
                hit = pos_ref[0, pl.ds(e, 1), :] == rank
                p_ref[gi * SEG_ALIGN:(gi + 1) * SEG_ALIGN, :] = jnp.where(hit, 1.0, 0.0).astype(BF16)
            half = COMBINE_BLOCK // 2
            for n in range(d // ROW_TILE):
                cols = slice(n * ROW_TILE, (n + 1) * ROW_TILE)
                o_ref[0, :, cols] += sum(
                    lax.dot_general(p_ref[k0:k0 + half, :], y_ref[pl.ds(base + k0, half), cols],
                                    (((0,), (0,)), ((), ())), preferred_element_type=F32)
                    for k0 in (0, half))
            return carry

        lax.fori_loop(0, MOE_SLOTS // COMBINE_BLOCK, combine, 0)

        ctx = mod_ref[bsz:bsz + 1, 5 * d:6 * d]
        if tiles_per_batch is None:
            r = i * rows + lax.broadcasted_iota(jnp.int32, (rows, 1), 0)
            g2 = ctx
            for b in range(bsz):
                lo = b * rows_per_batch
                inside = jnp.where(r >= lo, jnp.where(r < lo + n_lat, 1, 0), 0)
                g2 = jnp.where(inside == 1, mod_ref[b:b + 1, 5 * d:6 * d], g2)
        else:
            g2 = mod_ref[pl.ds(i // tiles_per_batch, 1), 5 * d:6 * d]
        y = x_ref[0] + g2 * o_ref[0]
        if final:
            y = _rms(y) * fn_ref[...]
        o_ref[0] = y


def _moe(h, gates, xs, mod, wg, wu, wd, sg, su, sd, final_norm, layer, n_lat, final):
    bsz, m, d = xs.shape
    t = MOE_TILE
    if final:
        assert n_lat % t == 0
        tiles_per_batch = n_lat // t
        n_tiles = bsz * tiles_per_batch
        where = lambda i: (i // tiles_per_batch, i % tiles_per_batch, 0)
        out_shape = (bsz, n_lat, d)
    else:
        assert (bsz * m) % t == 0
        tiles_per_batch = None
        n_tiles = bsz * m // t
        where = lambda i: (0, i, 0)
        out_shape = (1, bsz * m, d)
        h, gates, xs = (a.reshape(1, bsz * m, a.shape[-1]) for a in (h, gates, xs))
    once = pl.Buffered(1)

    tok = jnp.arange(t)
    tri = (tok[None, :] < tok[:, None]).astype(BF16)
    lane = jnp.arange(LANE)
    upper = (lane[:, None] < lane[None, :]).astype(F32)
    groups = -(-MOE_SLOTS // SEG_ALIGN // LANE) * LANE
    em = jax.ShapeDtypeStruct((n_tiles, LANE, t), F32)
    ints = lambda n: jax.ShapeDtypeStruct((n_tiles, SUBLANE, n), jnp.int32)
    per_tile = lambda a, b: pl.BlockSpec((1, a, b), lambda i: (i, 0, 0))
    pos, gem, cnt, off, own = pl.pallas_call(
        _plan_kernel,
        grid=(n_tiles,),
        in_specs=[pl.BlockSpec((1, t, LANE), lambda i: where(i)),
                  pl.BlockSpec((t, t), lambda i: (0, 0)), pl.BlockSpec((LANE, LANE), lambda i: (0, 0))],
        out_specs=[per_tile(LANE, t), per_tile(LANE, t), per_tile(SUBLANE, LANE), per_tile(SUBLANE, LANE),
                   per_tile(SUBLANE, groups)],
        out_shape=[em, em, ints(LANE), ints(LANE), ints(groups)],
        compiler_params=_params(("parallel",)),
        name="moe_plan",
    )(gates, tri, upper)

    tile = lambda n: pl.BlockSpec((1, t, n), lambda i, s, *_: where(i), pipeline_mode=once)
    planned = lambda a, b: pl.BlockSpec((1, a, b), lambda i, s, *_: (i, 0, 0), pipeline_mode=once)
    whole = lambda a: pl.BlockSpec(a.shape, lambda i, s, *_: (0,) * a.ndim, pipeline_mode=once)
    experts = lambda a: pl.BlockSpec((None, EXPERTS_PER_STEP) + a.shape[2:], lambda i, s, *_: (layer, s, 0, 0))
    out = pl.pallas_call(
        functools.partial(_moe_kernel, bsz=bsz, rows_per_batch=m, n_lat=n_lat,
                          tiles_per_batch=tiles_per_batch, final=final),
        grid_spec=pltpu.PrefetchScalarGridSpec(
            num_scalar_prefetch=3,
            grid=(n_tiles, N_EXPERTS // EXPERTS_PER_STEP),
            in_specs=[tile(d), planned(LANE, t), planned(LANE, t),
                      experts(wg), experts(wu), experts(wd), whole(sg), whole(su), whole(sd),
                      tile(d), whole(mod), whole(final_norm)],
            out_specs=pl.BlockSpec((1, t, d), lambda i, s, *_: where(i)),
            scratch_shapes=[pltpu.VMEM((MOE_SLOTS + MOE_CAP, d), BF16), pltpu.VMEM((COMBINE_BLOCK, t), BF16)]),
        out_shape=jax.ShapeDtypeStruct(out_shape, F32),
        compiler_params=_params(("arbitrary", "arbitrary"), MOE_VMEM_LIMIT),
        name="moe_final" if final else "moe",
    )(cnt[:, 0], off[:, 0], own[:, 0], h, pos, gem, wg, wu, wd, sg, su, sd, xs, mod, final_norm)
    return out if final else out.reshape(bsz, m, d)


def _rope_tables(n_lat, m, head_dim, width):
    pos = jnp.arange(n_lat, dtype=jnp.int32)
    row = (pos // GRID_W).astype(F32)
    col = (pos % GRID_W).astype(F32)
    half = head_dim // 2
    quarter = half // 2
    lane = jnp.arange(head_dim)
    freq = (lane % quarter).astype(F32)
    inv = ROPE_THETA ** (-(2.0 * freq) / half)
    ang = jnp.where(lane[None, :] < half, row[:, None], col[:, None]) * inv[None, :]
    low = (lane % half) < quarter
    cos = jnp.cos(ang)
    sin = jnp.sin(ang)
    slo = jnp.where(low[None, :], -sin, 0.0)
    shi = jnp.where(low[None, :], 0.0, sin)
    reps = width // head_dim
    pad = lambda a, fill: jnp.concatenate(
        [jnp.tile(a, (1, reps)), jnp.full((m - n_lat, width), fill, F32)], axis=0)
    return pad(cos, 1.0), pad(slo, 0.0), pad(shi, 0.0)


def _mla_tables(tabs16):
    outs = []
    for a, fill in zip(tabs16, (1.0, 0.0, 0.0)):
        m = a.shape[0]
        slot = jnp.concatenate([jnp.full((m, MLA_NOPE), fill, F32), a,
                                jnp.full((m, QK_SLOT - MLA_QK), fill, F32)], axis=1)
        outs.append(jnp.tile(slot, (1, LANE // QK_SLOT)))
    return tuple(outs)


def _slots(w, heads, width):
    w = w.reshape(w.shape[0], heads, width)
    pad = jnp.zeros((w.shape[0], heads, QK_SLOT - width), w.dtype)
    return jnp.concatenate([w, pad], axis=-1).reshape(w.shape[0], heads * QK_SLOT)


def _block_diag(w):
    dirs = w.shape[0]
    eye = jnp.eye(LRU_BLOCKS, dtype=w.dtype)
    full = jnp.einsum('dnij,nm->dnimj', w, eye).reshape(dirs, LRU_WIDTH, LRU_WIDTH)
    nl = LRU_WIDTH // LRU_LANES
    return jnp.stack([full[:, j * LRU_LANES:(j + 1) * LRU_LANES, j * LRU_LANES:(j + 1) * LRU_LANES]
                      for j in range(nl)], axis=1)


def kernel(x, c, ctx, c_ctx, mod_w, mod_b, ab_w_in, ab_w_out, diff_lambda_q, diff_lambda_k, diff_subln, lru_conv_w, lru_conv_b, lru_w_a, lru_b_a, lru_w_x, lru_b_x, lru_lambda, cd_w_in, cd_w_out, gqa_q_norm, gqa_k_norm, mla_q_norm, mla_kv_norm, mla_w_uq, mla_w_ukv, router_w, router_bias, exp_w_gate, exp_w_up, exp_w_down, sh_w_gate, sh_w_up, sh_w_down, final_norm):
    bsz, n_lat, d = x.shape
    n_ctx = ctx.shape[1]
    m = n_lat + n_ctx
    depth = mod_w.shape[0]
    assert depth == 2 and bsz < MOD_ROWS
    assert n_lat % ROW_TILE == 0 and n_ctx % ROW_TILE == 0 and n_lat % GRID_W == 0
    ctx_row = bsz

    c_all = jnp.concatenate([c, c_ctx[None, :], jnp.zeros((MOD_ROWS - bsz - 1, d), F32)], axis=0)
    mods = _modulation(c_all, mod_w, mod_b)

    tabs64 = _rope_tables(n_lat, m, DIFF_HEAD_DIM, LANE)
    tabs_m = _mla_tables(_rope_tables(n_lat, m, MLA_ROPE, MLA_ROPE))

    exp_w = tuple(w.astype(BF16) for w in (exp_w_gate, exp_w_up, exp_w_down))

    def moe_weights(i):
        return exp_w + (sh_w_gate[i], sh_w_up[i], sh_w_down[i])

    fn = final_norm.reshape(1, d)

    mod = mods[0]
    qt, k, vt, u, g = _proj_ab(x, ctx, mod, ab_w_in[0].astype(BF16), tabs64, n_lat, ctx_row)
    lambda_init = 0.8 - 0.6 * math.exp(-0.3 * 0)
    dv = 2 * DIFF_HEAD_DIM + NORM_ROWS
    jobs = tuple(_Job((2 * h + j) * QK_SLOT, h, j, h * dv, dv) for h in range(DIFF_HEADS) for j in range(2))
    o_att = _attention(qt, k, vt, (diff_lambda_q[0], diff_lambda_k[0], diff_subln[0].reshape(-1, 1)),
                       jobs, n_lat, lambda_init, "diff_attention")
    o_rec = _lru(u, g, lru_conv_w[0], lru_conv_b[0].reshape(1, -1),
                 _block_diag(lru_w_a[0]).astype(BF16), lru_b_a[0].reshape(2, 1, LRU_WIDTH),
                 _block_diag(lru_w_x[0]).astype(BF16), lru_b_x[0].reshape(2, 1, LRU_WIDTH),
                 lru_lambda[0].reshape(2, 1, LRU_WIDTH), n_lat)
    xs, h2, gates = _out_proj(o_att, o_rec, x, ctx, mod, ab_w_out[0].astype(BF16),
                              router_w[0].T, router_bias[0].reshape(-1, 1), n_lat, ctx_row)
    xs = _moe(h2, gates, xs, mod, *moe_weights(0), fn, 0, n_lat, False)

    mod = mods[1]
    w_in = cd_w_in[0]
    wq, wk = GQA_HEADS * GQA_HEAD_DIM, GQA_KV_HEADS * GQA_HEAD_DIM
    o_kr = wq + 2 * wk + MLA_Q_RANK + MLA_KV_RANK
    kr_slots = jnp.concatenate([jnp.zeros((d, MLA_NOPE), F32), w_in[:, o_kr:o_kr + MLA_ROPE],
                                jnp.zeros((d, QK_SLOT - MLA_QK), F32)], axis=1)
    w_in = jnp.concatenate([w_in[:, :o_kr], jnp.tile(kr_slots, (1, MLA_HEADS))], axis=1).astype(BF16)
    ukv = mla_w_ukv[0].reshape(MLA_KV_RANK, MLA_HEADS, MLA_NOPE + MLA_V)
    w_kn = _slots(ukv[:, :, :MLA_NOPE].reshape(MLA_KV_RANK, -1), MLA_HEADS, MLA_NOPE).astype(BF16)
    w_v = ukv[:, :, MLA_NOPE:].reshape(MLA_KV_RANK, MLA_HEADS * MLA_V).astype(BF16)
    w_uq = _slots(mla_w_uq[0], MLA_HEADS, MLA_QK).astype(BF16)
    lane = jnp.arange(wq)
    ones_bd = (lane[:, None] // GQA_HEAD_DIM == lane[None, :] // GQA_HEAD_DIM).astype(BF16)
    qgt, kg, vgt, qmt, km, vmt = _proj_cd(
        xs, mod, w_in, ones_bd, jnp.tile(gqa_q_norm[0], GQA_HEADS).reshape(1, -1),
        jnp.tile(gqa_k_norm[0], GQA_KV_HEADS).reshape(1, -1), mla_q_norm[0].reshape(1, -1),
        mla_kv_norm[0].reshape(1, -1), w_uq, w_kn, w_v, tabs64, tabs_m, n_lat, ctx_row)
    groups = GQA_HEADS // GQA_KV_HEADS
    dvn = GQA_HEAD_DIM + NORM_ROWS
    jobs = tuple(_Job(h * QK_SLOT, 0, h // groups, (h // groups) * dvn, dvn) for h in range(GQA_HEADS))
    o_gqa = _attention(qgt, kg, vgt, (), jobs, n_lat, None, "gqa_attention")
    dvn = MLA_V + NORM_ROWS
    jobs = tuple(_Job(h * QK_SLOT, h // 2, h % 2, h * dvn, dvn) for h in range(MLA_HEADS))
    o_mla = _attention(qmt, km, vmt, (), jobs, n_lat, None, "mla_attention")
    xs, h2, gates = _out_proj(o_gqa, o_mla, xs, None, mod, cd_w_out[0].astype(BF16),
                              router_w[1].T, router_bias[1].reshape(-1, 1), n_lat, ctx_row)
    return _moe(h2, gates, xs, mod, *moe_weights(1), fn, 1, n_lat, True)
```

```python
import functools
import math
from typing import NamedTuple

import jax
import jax.numpy as jnp
from jax import lax
from jax.experimental import pallas as pl
from jax.experimental.pallas import tpu as pltpu

F32 = jnp.float32
BF16 = jnp.bfloat16

GRID_W = 64
ROPE_THETA = 10000.0
EPS = 1e-6
DIFF_HEADS = 4
DIFF_HEAD_DIM = 64
LRU_WIDTH = 512
LRU_BLOCKS = 8
LRU_C = 8.0
GQA_HEADS = 8
GQA_KV_HEADS = 2
GQA_HEAD_DIM = 64
MLA_HEADS = 8
MLA_Q_RANK = 256
MLA_KV_RANK = 128
MLA_NOPE = 32
MLA_ROPE = 16
MLA_V = 64
MLA_QK = MLA_NOPE + MLA_ROPE
N_EXPERTS = 64
N_GROUPS = 8
PER_GROUP = N_EXPERTS // N_GROUPS
TOPK_GROUPS = 4
TOP_K = 8
ROUTED_SCALE = 2.5

QK_SLOT = 64
NORM_ROWS = 16
LOG2E = math.log2(math.e)

ROW_TILE = 256
ROUTE_ROWS = 128
HALF_ROWS = 128
LRU_CHUNK = 128
LRU_LANES = 256
EXPERTS_PER_STEP = 4
LANE = 128
SUBLANE = 8
MOD_ROWS = 8
VMEM_LIMIT = 56 * 1024 * 1024
MOE_VMEM_LIMIT = 62 * 1024 * 1024
MOE_TILE = 1024
MOE_CAP = 160
SEG_ALIGN = 16
MOE_SLOTS = MOE_TILE * TOP_K + N_EXPERTS * SEG_ALIGN
COMBINE_BLOCK = 1024


def _params(sem, vmem=VMEM_LIMIT):
    return pltpu.CompilerParams(dimension_semantics=sem, vmem_limit_bytes=vmem)


def _silu(x):
    return x * jax.nn.sigmoid(x)


def _rms(x):
    return x * lax.rsqrt(jnp.mean(x * x, axis=-1, keepdims=True) + EPS)


def _mod_rows(mod_ref, ctx_row, b, k, d, row0, rows, n_lat):
    lat = mod_ref[pl.ds(b, 1), k * d:(k + 1) * d]
    ctx = mod_ref[ctx_row:ctx_row + 1, k * d:(k + 1) * d]
    if n_lat % rows == 0:
        return jnp.where(row0 >= n_lat, ctx, lat)
    r = row0 + lax.broadcasted_iota(jnp.int32, (rows, 1), 0)
    return jnp.where(r >= n_lat, ctx, lat)


def _stream_specs(x, ctx, n_lat):
    d = x.shape[-1]
    if ctx is None:
        spec = pl.BlockSpec((1, ROW_TILE, d), lambda b, t: (b, t, 0))
        return (x, x), [spec, spec]
    last = n_lat // ROW_TILE - 1
    return (x, ctx), [pl.BlockSpec((1, ROW_TILE, d), lambda b, t: (b, jnp.minimum(t, last), 0)),
                      pl.BlockSpec((1, ROW_TILE, d), lambda b, t: (b, jnp.maximum(t - last - 1, 0), 0))]


def _values_t(v, heads):
    vt = v.T
    dv = vt.shape[0] // heads
    ones = jnp.ones((NORM_ROWS, vt.shape[1]), F32)
    return jnp.concatenate([piece for h in range(heads) for piece in (vt[h * dv:(h + 1) * dv], ones)],
                           axis=0).astype(BF16)


def _rope(x, cos, sin_lo, sin_hi, half):
    n = x.shape[-1]
    return x * cos + pltpu.roll(x, n - half, 1) * sin_lo + pltpu.roll(x, half, 1) * sin_hi


def _mod_kernel(c_ref, w_ref, b_ref, o_ref):
    s = _silu(c_ref[...])
    o_ref[0] = jnp.dot(s, w_ref[0], preferred_element_type=F32, precision=lax.Precision.HIGHEST) + b_ref[0]


def _modulation(c_all, mod_w, mod_b):
    depth, d, n = mod_w.shape
    tn = n // 4
    return pl.pallas_call(
        _mod_kernel,
        grid=(depth, n // tn),
        in_specs=[pl.BlockSpec((MOD_ROWS, d), lambda i, j: (0, 0)),
                  pl.BlockSpec((1, d, tn), lambda i, j: (i, 0, j)),
                  pl.BlockSpec((1, 1, tn), lambda i, j: (i, 0, j))],
        out_specs=pl.BlockSpec((1, MOD_ROWS, tn), lambda i, j: (i, 0, j)),
        out_shape=jax.ShapeDtypeStruct((depth, MOD_ROWS, n), F32),
        compiler_params=_params(("arbitrary", "arbitrary")),
        name="modulation",
    )(c_all, mod_w, mod_b.reshape(depth, 1, n))


def _proj_ab_kernel(x_ref, xc_ref, mod_ref, w_ref, cos_ref, slo_ref, shi_ref,
                    qt_ref, k_ref, vt_ref, u_ref, g_ref, *, n_lat, ctx_row):
    b, t = pl.program_id(0), pl.program_id(1)
    d = x_ref.shape[-1]
    row0 = t * ROW_TILE
    shift = _mod_rows(mod_ref, ctx_row, b, 0, d, row0, ROW_TILE, n_lat)
    scale = _mod_rows(mod_ref, ctx_row, b, 1, d, row0, ROW_TILE, n_lat)
    w = DIFF_HEADS * 2 * DIFF_HEAD_DIM
    for r0 in range(0, ROW_TILE, HALF_ROWS):
        rows = slice(r0, r0 + HALF_ROWS)
        x = jnp.where(row0 >= n_lat, xc_ref[0, rows], x_ref[0, rows])
        h = (_rms(x) * (1.0 + scale) + shift).astype(BF16)
        acc = jnp.dot(h, w_ref[...], preferred_element_type=F32)
        cos, slo, shi = (jnp.tile(r[rows], (1, w // LANE)) for r in (cos_ref, slo_ref, shi_ref))
        q = _rope(acc[:, 0:w], cos, slo, shi, DIFF_HEAD_DIM // 4) * (DIFF_HEAD_DIM ** -0.5 * LOG2E)
        k = _rope(acc[:, w:2 * w], cos, slo, shi, DIFF_HEAD_DIM // 4)
        qt_ref[0, 0, :, rows] = q.T.astype(BF16)
        k_ref[0, rows] = k.astype(BF16)
        vt_ref[0, :, rows] = _values_t(acc[:, 2 * w:3 * w], DIFF_HEADS)
        u_ref[0, rows] = acc[:, 3 * w:3 * w + LRU_WIDTH]
        g_ref[0, rows] = jax.nn.gelu(acc[:, 3 * w + LRU_WIDTH:3 * w + 2 * LRU_WIDTH]).astype(BF16)


def _proj_ab(x, ctx, mod, w_in, tabs, n_lat, ctx_row):
    bsz, _, d = x.shape
    m = n_lat + ctx.shape[1]
    nt = m // ROW_TILE
    streams, stream_specs = _stream_specs(x, ctx, n_lat)
    w = DIFF_HEADS * 2 * DIFF_HEAD_DIM
    n_in = w_in.shape[1]
    tile = lambda n: pl.BlockSpec((1, ROW_TILE, n), lambda b, t: (b, t, 0))
    ttile = pl.BlockSpec((1, 1, w, ROW_TILE), lambda b, t: (b, t, 0, 0))
    wv = w + DIFF_HEADS * NORM_ROWS
    vtile = pl.BlockSpec((1, wv, ROW_TILE), lambda b, t: (b, 0, t))
    tab = pl.BlockSpec((ROW_TILE, LANE), lambda b, t: (t, 0))
    return pl.pallas_call(
        functools.partial(_proj_ab_kernel, n_lat=n_lat, ctx_row=ctx_row),
        grid=(bsz, nt),
        in_specs=stream_specs + [pl.BlockSpec(mod.shape, lambda b, t: (0, 0)),
                                 pl.BlockSpec((d, n_in), lambda b, t: (0, 0)),
                                 tab, tab, tab],
        out_specs=[ttile, tile(w), vtile, tile(LRU_WIDTH), tile(LRU_WIDTH)],
        out_shape=[jax.ShapeDtypeStruct((bsz, nt, w, ROW_TILE), BF16),
                   jax.ShapeDtypeStruct((bsz, m, w), BF16),
                   jax.ShapeDtypeStruct((bsz, wv, m), BF16),
                   jax.ShapeDtypeStruct((bsz, m, LRU_WIDTH), F32),
                   jax.ShapeDtypeStruct((bsz, m, LRU_WIDTH), BF16)],
        compiler_params=_params(("parallel", "parallel")),
        name="proj_ab",
    )(*streams, mod, w_in, *tabs)


def _group_rms(x, ones_bd):
    x2 = x * x
    hi = x2.astype(BF16)
    lo = (x2 - hi.astype(F32)).astype(BF16)
    ss = jnp.dot(hi, ones_bd, preferred_element_type=F32) + jnp.dot(lo, ones_bd, preferred_element_type=F32)
    return x * lax.rsqrt(ss * (1.0 / GQA_HEAD_DIM) + EPS)


def _proj_cd_kernel(x_ref, mod_ref, w_ref, bd_ref, qn_ref, kn_ref, cqn_ref, ckvn_ref, wuq_ref, wkn_ref, wv_ref,
                    cos_ref, slo_ref, shi_ref, cosm_ref, slom_ref, shim_ref,
                    qgt_ref, kg_ref, vgt_ref, qmt_ref, km_ref, vmt_ref, *, n_lat, ctx_row):
    b, t = pl.program_id(0), pl.program_id(1)
    d = x_ref.shape[-1]
    row0 = t * ROW_TILE
    shift = _mod_rows(mod_ref, ctx_row, b, 0, d, row0, ROW_TILE, n_lat)
    scale = _mod_rows(mod_ref, ctx_row, b, 1, d, row0, ROW_TILE, n_lat)
    wq = GQA_HEADS * GQA_HEAD_DIM
    wk = GQA_KV_HEADS * GQA_HEAD_DIM
    bd = bd_ref[...]
    for r0 in range(0, ROW_TILE, HALF_ROWS):
        rows = slice(r0, r0 + HALF_ROWS)
        h = (_rms(x_ref[0, rows]) * (1.0 + scale) + shift).astype(BF16)
        acc = jnp.dot(h, w_ref[...], preferred_element_type=F32)
        o = 0
        q = acc[:, o:o + wq]; o += wq
        k = acc[:, o:o + wk]; o += wk
        v = acc[:, o:o + wk]; o += wk
        cq = acc[:, o:o + MLA_Q_RANK]; o += MLA_Q_RANK
        ckv = acc[:, o:o + MLA_KV_RANK]; o += MLA_KV_RANK
        kr = acc[:, o:o + MLA_HEADS * QK_SLOT]
        cos, slo, shi = (jnp.tile(r[rows], (1, wq // LANE)) for r in (cos_ref, slo_ref, shi_ref))
        q = (_rope(_group_rms(q, bd) * qn_ref[...], cos, slo, shi, GQA_HEAD_DIM // 4)
             * (GQA_HEAD_DIM ** -0.5 * LOG2E))
        k = _rope(_group_rms(k, bd[:wk, :wk]) * kn_ref[...], cos[:, :wk], slo[:, :wk], shi[:, :wk],
                  GQA_HEAD_DIM // 4)
        qgt_ref[0, 0, :, rows] = q.T.astype(BF16)
        kg_ref[0, rows] = k.astype(BF16)
        vgt_ref[0, :, rows] = _values_t(v, GQA_KV_HEADS)
        cosm, slom, shim = (jnp.tile(r[rows], (1, MLA_HEADS * QK_SLOT // LANE))
                            for r in (cosm_ref, slom_ref, shim_ref))
        cqn = (_rms(cq) * cqn_ref[...]).astype(BF16)
        mq = jnp.dot(cqn, wuq_ref[...], preferred_element_type=F32)
        qmt_ref[0, 0, :, rows] = (_rope(mq, cosm, slom, shim, MLA_ROPE // 4)
                                  * (MLA_QK ** -0.5 * LOG2E)).T.astype(BF16)
        ckvn = (_rms(ckv) * ckvn_ref[...]).astype(BF16)
        km = jnp.dot(ckvn, wkn_ref[...], preferred_element_type=F32) + _rope(kr, cosm, slom, shim, MLA_ROPE // 4)
        km_ref[0, rows] = km.astype(BF16)
        vmt_ref[0, :, rows] = _values_t(jnp.dot(ckvn, wv_ref[...], preferred_element_type=F32), MLA_HEADS)


def _proj_cd(xs, mod, w_in, bd, qn, kn, cqn, ckvn, wuq, wkn, wv, tabs, tabs_m, n_lat, ctx_row):
    bsz, m, d = xs.shape
    nt = m // ROW_TILE
    wq = GQA_HEADS * GQA_HEAD_DIM
    wk = GQA_KV_HEADS * GQA_HEAD_DIM
    wm = MLA_HEADS * QK_SLOT
    wmv = MLA_HEADS * (MLA_V + NORM_ROWS)
    wgv = GQA_KV_HEADS * (GQA_HEAD_DIM + NORM_ROWS)
    tile = lambda n: pl.BlockSpec((1, ROW_TILE, n), lambda b, t: (b, t, 0))
    ttile = lambda n: pl.BlockSpec((1, 1, n, ROW_TILE), lambda b, t: (b, t, 0, 0))
    vtile = lambda n: pl.BlockSpec((1, n, ROW_TILE), lambda b, t: (b, 0, t))
    whole = lambda a: pl.BlockSpec(a.shape, lambda b, t: (0,) * a.ndim)
    tab = pl.BlockSpec((ROW_TILE, LANE), lambda b, t: (t, 0))
    tabm = tab
    return pl.pallas_call(
        functools.partial(_proj_cd_kernel, n_lat=n_lat, ctx_row=ctx_row),
        grid=(bsz, nt),
        in_specs=[tile(d), whole(mod), whole(w_in), whole(bd), whole(qn), whole(kn), whole(cqn), whole(ckvn),
                  whole(wuq), whole(wkn), whole(wv), tab, tab, tab, tabm, tabm, tabm],
        out_specs=[ttile(wq), tile(wk), vtile(wgv), ttile(wm), tile(wm), vtile(wmv)],
        out_shape=[jax.ShapeDtypeStruct((bsz, nt, wq, ROW_TILE), BF16),
                   jax.ShapeDtypeStruct((bsz, m, wk), BF16),
                   jax.ShapeDtypeStruct((bsz, wgv, m), BF16),
                   jax.ShapeDtypeStruct((bsz, nt, wm, ROW_TILE), BF16),
                   jax.ShapeDtypeStruct((bsz, m, wm), BF16),
                   jax.ShapeDtypeStruct((bsz, wmv, m), BF16)],
        compiler_params=_params(("parallel", "parallel")),
        name="proj_cd",
    )(xs, mod, w_in, bd, qn, kn, cqn, ckvn, wuq, wkn, wv, *tabs, *tabs_m)


class _Job(NamedTuple):
    q_row: int
    k_group: int
    k_half: int
    v_row: int
    dv: int


def _attend(qt_ref, qnext_ref, k_ref, vt_ref, s_ref, mp_ref, jobs, lo, hi, chained):
    assert len(jobs) % 2 == 0
    tq = qt_ref.shape[-1]
    n = hi - lo
    group = next(g for g in (11, 3, 2, 1) if n % g == 0)
    steps = n // group
    row_groups = ROW_TILE // SUBLANE

    def weights(job, ref=qt_ref):
        qh = ref[0, 0, job.q_row:job.q_row + QK_SLOT, :]
        z = jnp.zeros_like(qh)
        return jnp.concatenate([z, qh] if job.k_half else [qh, z], axis=0)

    def score(c0, buf, qw, job, mp):
        r0 = pl.multiple_of(c0 * ROW_TILE, ROW_TILE)
        kc = k_ref[0, pl.ds(r0, group * ROW_TILE), job.k_group * LANE:(job.k_group + 1) * LANE]
        s_all = jnp.dot(kc, qw, preferred_element_type=F32)
        for g in range(group):
            s = s_all[g * ROW_TILE:(g + 1) * ROW_TILE]
            s_ref[buf, c0 + g] = s
            mp = jnp.maximum(mp, jnp.max(s.reshape(row_groups, SUBLANE, tq), axis=0))
        return mp

    def value(c0, buf, m, job, acc):
        ps = [jnp.exp2(s_ref[buf, c0 + g] - m).astype(BF16) for g in range(group)]
        k0 = pl.multiple_of(c0 * ROW_TILE, ROW_TILE)
        v = vt_ref[0, job.v_row:job.v_row + job.dv, pl.ds(k0, group * ROW_TILE)]
        return acc + jnp.dot(v, jnp.concatenate(ps, axis=0), preferred_element_type=F32)

    def loop(body, init):
        return body(0, init) if steps == 1 else lax.fori_loop(0, steps, body, init)

    neg = jnp.full((SUBLANE, tq), -jnp.inf, F32)

    def prologue():
        qw = weights(jobs[0])

        def first(i, mp):
            return score(lo + i * group, 0, qw, jobs[0], mp)

        mp_ref[...] = loop(first, neg)

    if chained:
        pl.when(pl.program_id(1) == 0)(prologue)
    else:
        prologue()
    mp = mp_ref[...]
    results = []
    for j, job in enumerate(jobs):
        buf = j % 2
        m = jnp.max(mp, axis=0, keepdims=True)
        if j + 1 < len(jobs):
            nxt, qw = jobs[j + 1], weights(jobs[j + 1])
        elif chained:
            nxt, qw = jobs[0], weights(jobs[0], qnext_ref)
        else:
            nxt = None

        def body(i, carry):
            mp, acc = carry
            if nxt is not None:
                mp = score(lo + i * group, 1 - buf, qw, nxt, mp)
            return mp, value(lo + i * group, buf, m, job, acc)

        mp, acc = loop(body, (neg, jnp.zeros((job.dv, tq), F32)))
        dv = job.dv - NORM_ROWS
        results.append((acc[:dv], acc[dv:dv + 1]))
    if chained:
        mp_ref[...] = mp
    return results


def _attn_kernel(*refs, jobs, n_lat, lambda_init):
    if lambda_init is None:
        qt_ref, qnext_ref, k_ref, vt_ref, o_ref, s_ref, mp_ref = refs
    else:
        qt_ref, qnext_ref, k_ref, vt_ref, lq_ref, lk_ref, sub_ref, o_ref, s_ref, mp_ref = refs
    n_chunks = k_ref.shape[1] // ROW_TILE
    first_ctx = n_lat // ROW_TILE
    t = pl.program_id(1)

    def run(lo, hi, chained):
        res = _attend(qt_ref, qnext_ref, k_ref, vt_ref, s_ref, mp_ref, jobs, lo, hi, chained)
        if lambda_init is None:
            outs = [acc / l for acc, l in res]
        else:
            lq, lk = lq_ref[...], lk_ref[...]
            lam = (jnp.exp(jnp.sum(lq[0:1] * lk[0:1], axis=-1, keepdims=True))
                   - jnp.exp(jnp.sum(lq[1:2] * lk[1:2], axis=-1, keepdims=True)) + lambda_init)
            outs = []
            for h in range(len(res) // 2):
                (a0, l0), (a1, l1) = res[2 * h], res[2 * h + 1]
                o = a0 / l0 - lam * (a1 / l1)
                o = o * lax.rsqrt(jnp.mean(o * o, axis=0, keepdims=True) + EPS)
                outs.append(o * sub_ref[...] * (1.0 - lambda_init))
        o_ref[0] = jnp.concatenate(outs, axis=0).T.astype(BF16)

    @pl.when(t < first_ctx)
    def _():
        run(0, n_chunks, True)

    @pl.when(t >= first_ctx)
    def _():
        run(first_ctx, n_chunks, False)


def _attention(qt, k, vt, extra, jobs, n_lat, lambda_init, name):
    bsz, nt, wq, tq = qt.shape
    m = k.shape[1]
    wo = sum(j.dv - NORM_ROWS for j in jobs) // (1 if lambda_init is None else 2)
    once = pl.Buffered(1)
    whole = lambda a: pl.BlockSpec(a.shape, lambda b, t: (0,) * a.ndim)
    return pl.pallas_call(
        functools.partial(_attn_kernel, jobs=jobs, n_lat=n_lat, lambda_init=lambda_init),
        grid=(bsz, nt),
        in_specs=[pl.BlockSpec((1, 1, wq, tq), lambda b, t: (b, t, 0, 0)),
                  pl.BlockSpec((1, 1, wq, tq), lambda b, t: (b, jnp.minimum(t + 1, nt - 1), 0, 0)),
                  pl.BlockSpec((1,) + k.shape[1:], lambda b, t: (b, 0, 0), pipeline_mode=once),
                  pl.BlockSpec((1,) + vt.shape[1:], lambda b, t: (b, 0, 0), pipeline_mode=once)]
                 + [whole(a) for a in extra],
        out_specs=pl.BlockSpec((1, tq, wo), lambda b, t: (b, t, 0)),
        out_shape=jax.ShapeDtypeStruct((bsz, m, wo), BF16),
        scratch_shapes=[pltpu.VMEM((2, nt, ROW_TILE, tq), F32), pltpu.VMEM((SUBLANE, tq), F32)],
        compiler_params=_params(("arbitrary", "arbitrary")),
        name=name,
    )(qt, qt, k, vt, *extra)


def _scan_chunk(a, b, carry, reverse):
    n, c = a.shape
    groups = n // SUBLANE
    a = a.reshape(groups, SUBLANE, c)
    b = b.reshape(groups, SUBLANE, c)
    sub = lax.broadcasted_iota(jnp.int32, (1, SUBLANE, 1), 1)
    d = 1
    while d < SUBLANE:
        sh = SUBLANE - d if reverse else d
        valid = sub < SUBLANE - d if reverse else sub >= d
        b = jnp.where(valid, a * pltpu.roll(b, sh, 1) + b, b)
        a = jnp.where(valid, a * pltpu.roll(a, sh, 1), a)
        d *= 2
    hs = [None] * groups
    for j in (range(groups - 1, -1, -1) if reverse else range(groups)):
        hs[j] = a[j] * carry + b[j]
        carry = hs[j][0:1] if reverse else hs[j][SUBLANE - 1:SUBLANE]
    return jnp.concatenate(hs, axis=0), carry


def _lru_kernel(u_ref, g_ref, cw_ref, cb_ref, wa_ref, ba_ref, wx_ref, bx_ref, lam_ref, o_ref, up_ref, hf_ref, *, n_lat):
    m = u_ref.shape[1]
    lanes = u_ref.shape[2]
    n_chunks = m // LRU_CHUNK
    n_ctx_chunks = (m - n_lat) // LRU_CHUNK
    pad = SUBLANE

    up_ref[0:pad, :] = jnp.zeros((pad, lanes), F32)
    up_ref[pad + m:pad + m + pad, :] = jnp.zeros((pad, lanes), F32)

    def copy(c, _):
        r0 = pl.multiple_of(c * LRU_CHUNK, LRU_CHUNK)
        up_ref[pl.ds(pad + r0, LRU_CHUNK), :] = u_ref[0, pl.ds(r0, LRU_CHUNK), :]
        return 0

    lax.fori_loop(0, n_chunks, copy, 0)

    cw = cw_ref[...]
    cb = cb_ref[...]
    win_rows = LRU_CHUNK + 2 * pad

    def coeffs(c, d):
        r0 = pl.multiple_of(c * LRU_CHUNK, LRU_CHUNK)
        win = up_ref[pl.ds(r0, win_rows), :]
        r = r0 + lax.broadcasted_iota(jnp.int32, (LRU_CHUNK, 1), 0)
        at = lambda k: pltpu.roll(win, (win_rows - k) % win_rows, 0)[pad:pad + LRU_CHUNK]
        y = (cw[0:1] * jnp.where((r == n_lat) | (r == n_lat + 1), 0.0, at(-2))
             + cw[1:2] * jnp.where(r == n_lat, 0.0, at(-1))
             + cw[2:3] * win[pad:pad + LRU_CHUNK]
             + cw[3:4] * jnp.where(r == n_lat - 1, 0.0, at(1))
             + cb)
        yb = y.astype(BF16)
        ra = jax.nn.sigmoid(jnp.dot(yb, wa_ref[d, 0], preferred_element_type=F32) + ba_ref[d])
        ix = jax.nn.sigmoid(jnp.dot(yb, wx_ref[d, 0], preferred_element_type=F32) + bx_ref[d])
        z = -lam_ref[d]
        softplus = jnp.maximum(z, 0.0) + jnp.log(1.0 + jnp.exp(-jnp.abs(z)))
        a = jnp.exp(-LRU_C * ra * softplus)
        return a, jnp.sqrt(1.0 - a * a) * ix * y

    def fwd(s, carry):
        c = lax.rem(s + (n_chunks - n_ctx_chunks), n_chunks)
        a, b = coeffs(c, 0)
        h, carry = _scan_chunk(a, b, carry, False)
        r0 = pl.multiple_of(c * LRU_CHUNK, LRU_CHUNK)
        hf_ref[pl.ds(r0, LRU_CHUNK), :] = h
        return carry

    lax.fori_loop(0, n_chunks, fwd, jnp.zeros((1, lanes), F32))

    def bwd(s, carry):
        c = n_chunks - 1 - s
        a, b = coeffs(c, 1)
        h, carry = _scan_chunk(a, b, carry, True)
        r0 = pl.multiple_of(c * LRU_CHUNK, LRU_CHUNK)
        tot = hf_ref[pl.ds(r0, LRU_CHUNK), :] + h
        o_ref[0, pl.ds(r0, LRU_CHUNK), :] = (tot * g_ref[0, pl.ds(r0, LRU_CHUNK), :].astype(F32)).astype(BF16)
        return carry

    lax.fori_loop(0, n_chunks, bwd, jnp.zeros((1, lanes), F32))


def _lru(u, g, conv_w, conv_b, wa_bd, ba, wx_bd, bx, lam, n_lat):
    bsz, m, c = u.shape
    nl = c // LRU_LANES
    seq = lambda: pl.BlockSpec((1, m, LRU_LANES), lambda b, j: (b, 0, j))
    vec = lambda a: pl.BlockSpec(a.shape[:-1] + (LRU_LANES,), lambda b, j: (0,) * (a.ndim - 1) + (j,))
    mat = pl.BlockSpec((2, 1, LRU_LANES, LRU_LANES), lambda b, j: (0, j, 0, 0))
    return pl.pallas_call(
        functools.partial(_lru_kernel, n_lat=n_lat),
        grid=(bsz, nl),
        in_specs=[seq(), seq(), vec(conv_w), vec(conv_b), mat, vec(ba), mat, vec(bx), vec(lam)],
        out_specs=seq(),
        out_shape=jax.ShapeDtypeStruct((bsz, m, c), BF16),
        scratch_shapes=[pltpu.VMEM((m + 2 * SUBLANE, LRU_LANES), F32), pltpu.VMEM((m, LRU_LANES), F32)],
        compiler_params=_params(("parallel", "parallel")),
        name="rglru",
    )(u, g, conv_w, conv_b, wa_bd, ba, wx_bd, bx, lam)


def _route(logits_t, bias):
    n = logits_t.shape[-1]
    scores = jax.nn.sigmoid(logits_t)
    choice = scores + bias
    sub = lax.broadcasted_iota(jnp.int32, (PER_GROUP, n), 0)
    neg = -jnp.inf
    groups, gs = [], []
    for g in range(N_GROUPS):
        cg = choice[g * PER_GROUP:(g + 1) * PER_GROUP]
        m1 = jnp.max(cg, axis=0, keepdims=True)
        i1 = jnp.min(jnp.where(cg == m1, sub, PER_GROUP), axis=0, keepdims=True)
        m2 = jnp.max(jnp.where(sub == i1, neg, cg), axis=0, keepdims=True)
        groups.append(cg)
        gs.append(m1 + m2)
    masked = []
    for g in range(N_GROUPS):
        rank = jnp.zeros((1, n), jnp.int32)
        for o in range(N_GROUPS):
            if o != g:
                ahead = (gs[o] >= gs[g]) if o < g else (gs[o] > gs[g])
                rank = rank + jnp.where(ahead, 1, 0)
        masked.append(jnp.where(rank < TOPK_GROUPS, groups[g], neg))
    masked = jnp.concatenate(masked, axis=0)
    eidx = lax.broadcasted_iota(jnp.int32, (N_EXPERTS, n), 0)
    picked = jnp.zeros((N_EXPERTS, n), F32)
    for _ in range(TOP_K):
        mx = jnp.max(masked, axis=0, keepdims=True)
        first = jnp.min(jnp.where(masked == mx, eidx, N_EXPERTS), axis=0, keepdims=True)
        hit = eidx == first
        picked = jnp.where(hit, 1.0, picked)
        masked = jnp.where(hit, neg, masked)
    w = picked * scores
    return w / jnp.sum(w, axis=0, keepdims=True) * ROUTED_SCALE


def _out_proj_kernel(oa_ref, ob_ref, x_ref, xc_ref, mod_ref, wa_ref, wb_ref, rwh_ref, rwl_ref, rb_ref,
                     xo_ref, h_ref, gates_ref, *, n_lat, ctx_row):
    b, t = pl.program_id(0), pl.program_id(1)
    d = x_ref.shape[-1]
    row0 = t * ROW_TILE
    mv = lambda k: _mod_rows(mod_ref, ctx_row, b, k, d, row0, ROW_TILE, n_lat)
    nt = lambda a, b: lax.dot_general(a, b, (((1,), (1,)), ((), ())), preferred_element_type=F32)
    for r0 in range(0, ROW_TILE, ROUTE_ROWS):
        rows = slice(r0, r0 + ROUTE_ROWS)
        y = (jnp.dot(oa_ref[0, rows], wa_ref[...], preferred_element_type=F32)
             + jnp.dot(ob_ref[0, rows], wb_ref[...], preferred_element_type=F32))
        x = jnp.where(row0 >= n_lat, xc_ref[0, rows], x_ref[0, rows]) + mv(2) * y
        xo_ref[0, rows] = x
        h = _rms(x) * (1.0 + mv(4)) + mv(3)
        hb = h.astype(BF16)
        h_ref[0, rows] = hb
        h_lo = (h - hb.astype(F32)).astype(BF16)
        logits_t = nt(rwh_ref[...], hb) + (nt(rwh_ref[...], h_lo) + nt(rwl_ref[...], hb))
        gates_t = _route(logits_t, rb_ref[...])
        gates_t = jnp.concatenate([gates_t, jnp.zeros((LANE - N_EXPERTS, ROUTE_ROWS), F32)], axis=0)
        gates_ref[0, rows] = gates_t.T


def _out_proj(oa, ob, x, ctx, mod, w_out, router_wt, router_b, n_lat, ctx_row):
    bsz, m, d = oa.shape[0], oa.shape[1], x.shape[-1]
    streams, stream_specs = _stream_specs(x, ctx, n_lat)
    na, nb = oa.shape[-1], ob.shape[-1]
    wa, wb = w_out[:na], w_out[na:]
    rw_hi = router_wt.astype(BF16)
    rw_lo = (router_wt - rw_hi.astype(F32)).astype(BF16)
    tile = lambda n: pl.BlockSpec((1, ROW_TILE, n), lambda b, t: (b, t, 0))
    whole = lambda a: pl.BlockSpec(a.shape, lambda b, t: (0,) * a.ndim)
    return pl.pallas_call(
        functools.partial(_out_proj_kernel, n_lat=n_lat, ctx_row=ctx_row),
        grid=(bsz, m // ROW_TILE),
        in_specs=[tile(na), tile(nb)] + stream_specs + [whole(mod), whole(wa), whole(wb), whole(rw_hi), whole(rw_lo),
                                                        whole(router_b)],
        out_specs=[tile(d), tile(d), tile(LANE)],
        out_shape=[jax.ShapeDtypeStruct((bsz, m, d), F32),
                   jax.ShapeDtypeStruct((bsz, m, d), BF16),
                   jax.ShapeDtypeStruct((bsz, m, LANE), F32)],
        compiler_params=_params(("parallel", "parallel")),
        name="out_proj_router",
    )(oa, ob, *streams, mod, wa, wb, rw_hi, rw_lo, router_b)


def _plan_kernel(gates_ref, tri_ref, upper_ref, pos_ref, gem_ref, cnt_ref, off_ref, own_ref):
    g = gates_ref[0]
    sel = g != 0.0
    one = jnp.where(sel, 1.0, 0.0)
    pos = jnp.dot(tri_ref[...], one.astype(BF16), preferred_element_type=F32)
    cnt = jnp.sum(one, axis=0, keepdims=True)
    cpad = jnp.floor((cnt + (SEG_ALIGN - 1)) * (1.0 / SEG_ALIGN)) * SEG_ALIGN
    off = jnp.dot(jnp.broadcast_to(cpad, (SUBLANE, LANE)), upper_ref[...], preferred_element_type=F32,
                  precision=lax.Precision.HIGHEST)[0:1]
    pos_ref[0] = jnp.where(sel, pos, -1.0).T
    gem_ref[0] = g.T
    cnt_ref[0] = jnp.broadcast_to(cnt, (SUBLANE, LANE)).astype(jnp.int32)
    off_ref[0] = jnp.broadcast_to(off, (SUBLANE, LANE)).astype(jnp.int32)
    groups = own_ref.shape[-1]
    ends = jnp.broadcast_to(off + cpad, (LANE, LANE)).T
    ends = jnp.concatenate([ends] * (groups // LANE), axis=1)
    start = lax.broadcasted_iota(jnp.int32, (LANE, groups), 1).astype(F32) * SEG_ALIGN
    real = lax.broadcasted_iota(jnp.int32, (LANE, groups), 0) < N_EXPERTS
    before = jnp.where(real, jnp.where(ends <= start, 1.0, 0.0), 0.0)
    owner = jnp.minimum(jnp.sum(before, axis=0, keepdims=True), N_EXPERTS - 1.0)
    own_ref[0] = jnp.broadcast_to(owner, (SUBLANE, groups)).astype(jnp.int32)


def _moe_kernel(cnt_s, off_s, own_s, h_ref, pos_ref, gem_ref, wg_ref, wu_ref, wd_ref,
                sg_ref, su_ref, sd_ref, x_ref, mod_ref, fn_ref, o_ref, y_ref, p_ref,
                *, bsz, rows_per_batch, n_lat, tiles_per_batch, final):
    i, s = pl.program_id(0), pl.program_id(1)
    rows, d = h_ref.shape[1], h_ref.shape[2]
    h = h_ref[0]

    def ffn(x, wg, wu, wd, gate):
        a = _silu(jnp.dot(x, wg, preferred_element_type=F32)) * jnp.dot(x, wu, preferred_element_type=F32)
        if gate is not None:
            a = a * gate
        return jnp.dot(a.astype(BF16), wd, preferred_element_type=F32)

    @pl.when((s == 0) & (i == 0))
    def _():
        y_ref[...] = jnp.zeros(y_ref.shape, BF16)

    @pl.when(s == 0)
    def _():
        o_ref[0] = ffn(h, sg_ref[...].astype(BF16), su_ref[...].astype(BF16), sd_ref[...].astype(BF16), None)

    cap = MOE_CAP
    cap_rows = lax.broadcasted_iota(jnp.int32, (cap, 1), 0)

    def onehot(pos_row, q):
        ranks = cap_rows + q * cap
        hit = pos_row == ranks.astype(F32)
        return ranks, hit, jnp.where(hit, 1.0, 0.0).astype(BF16)

    def expert_rows(x, ranks, hit, q, gate_row, w, cnt, off):
        gate = jnp.sum(jnp.where(hit, gate_row, 0.0), axis=1, keepdims=True)
        y = ffn(x, *w, gate)
        dst = pl.multiple_of(off + q * cap, SEG_ALIGN)
        y_ref[pl.ds(dst, cap), :] = jnp.where(ranks < cnt, y.astype(BF16), y_ref[pl.ds(dst, cap), :])

    experts = []
    for j in range(EXPERTS_PER_STEP):
        e = s * EXPERTS_PER_STEP + j
        experts.append((cnt_s[i, e], off_s[i, e],
                        pos_ref[0, pl.ds(e, 1), :],
                        gem_ref[0, pl.ds(e, 1), :],
                        (wg_ref[j], wu_ref[j], wd_ref[j])))

    hots = [onehot(pos_row, 0) for _, _, pos_row, _, _ in experts]
    xs = jnp.dot(jnp.concatenate([p for _, _, p in hots], axis=0), h, preferred_element_type=F32).astype(BF16)
    for j, ((cnt, off, _, gate_row, w), (ranks, hit, _)) in enumerate(zip(experts, hots)):
        expert_rows(xs[j * cap:(j + 1) * cap], ranks, hit, 0, gate_row, w, cnt, off)

    for cnt, off, pos_row, gate_row, w in experts:
        def block(q, carry):
            ranks, hit, p = onehot(pos_row, q)
            x = jnp.dot(p, h, preferred_element_type=F32).astype(BF16)
            expert_rows(x, ranks, hit, q, gate_row, w, cnt, off)
            return carry

        lax.fori_loop(1, (cnt + cap - 1) // cap, block, 0)

    @pl.when(s == N_EXPERTS // EXPERTS_PER_STEP - 1)
    def _():
        group_rows = lax.broadcasted_iota(jnp.int32, (SEG_ALIGN, 1), 0)
        groups_per_block = COMBINE_BLOCK // SEG_ALIGN

        def combine(r, carry):
            base = pl.multiple_of(r * COMBINE_BLOCK, COMBINE_BLOCK)

            for gi in range(groups_per_block):
                e = own_s[i, r * groups_per_block + gi]
                rank = (group_rows + (base + gi * SEG_ALIGN - off_s[i, e])).astype(F32)
                hit = pos_ref[0, pl.ds(e, 1), :] == rank
                p_ref[gi * SEG_ALIGN:(gi + 1) * SEG_ALIGN, :] = jnp.where(hit, 1.0, 0.0).astype(BF16)
            half = COMBINE_BLOCK // 2
            for n in range(d // ROW_TILE):
                cols = slice(n * ROW_TILE, (n + 1) * ROW_TILE)
                o_ref[0, :, cols] += sum(
                    lax.dot_general(p_ref[k0:k0 + half, :], y_ref[pl.ds(base + k0, half), cols],
                                    (((0,), (0,)), ((), ())), preferred_element_type=F32)
                    for k0 in (0, half))
            return carry

        lax.fori_loop(0, MOE_SLOTS // COMBINE_BLOCK, combine, 0)

        ctx = mod_ref[bsz:bsz + 1, 5 * d:6 * d]
        if tiles_per_batch is None:
            r = i * rows + lax.broadcasted_iota(jnp.int32, (rows, 1), 0)
            g2 = ctx
            for b in range(bsz):
                lo = b * rows_per_batch
                inside = jnp.where(r >= lo, jnp.where(r < lo + n_lat, 1, 0), 0)
                g2 = jnp.where(inside == 1, mod_ref[b:b + 1, 5 * d:6 * d], g2)
        else:
            g2 = mod_ref[pl.ds(i // tiles_per_batch, 1), 5 * d:6 * d]
        y = x_ref[0] + g2 * o_ref[0]
        if final:
            y = _rms(y) * fn_ref[...]
        o_ref[0] = y


def _moe(h, gates, xs, mod, wg, wu, wd, sg, su, sd, final_norm, layer, n_lat, final):
    bsz, m, d = xs.shape
    t = MOE_TILE
    if final:
        assert n_lat % t == 0
        tiles_per_batch = n_lat // t
        n_tiles = bsz * tiles_per_batch
        where = lambda i: (i // tiles_per_batch, i % tiles_per_batch, 0)
        out_shape = (bsz, n_lat, d)
    else:
        assert (bsz * m) % t == 0
        tiles_per_batch = None
        n_tiles = bsz * m // t
        where = lambda i: (0, i, 0)
        out_shape = (1, bsz * m, d)
        h, gates, xs = (a.reshape(1, bsz * m, a.shape[-1]) for a in (h, gates, xs))
    once = pl.Buffered(1)

    tok = jnp.arange(t)
    tri = (tok[None, :] < tok[:, None]).astype(BF16)
    lane = jnp.arange(LANE)
    upper = (lane[:, None] < lane[None, :]).astype(F32)
    groups = -(-MOE_SLOTS // SEG_ALIGN // LANE) * LANE
    em = jax.ShapeDtypeStruct((n_tiles, LANE, t), F32)
    ints = lambda n: jax.ShapeDtypeStruct((n_tiles, SUBLANE, n), jnp.int32)
    per_tile = lambda a, b: pl.BlockSpec((1, a, b), lambda i: (i, 0, 0))
    pos, gem, cnt, off, own = pl.pallas_call(
        _plan_kernel,
        grid=(n_tiles,),
        in_specs=[pl.BlockSpec((1, t, LANE), lambda i: where(i)),
                  pl.BlockSpec((t, t), lambda i: (0, 0)), pl.BlockSpec((LANE, LANE), lambda i: (0, 0))],
        out_specs=[per_tile(LANE, t), per_tile(LANE, t), per_tile(SUBLANE, LANE), per_tile(SUBLANE, LANE),
                   per_tile(SUBLANE, groups)],
        out_shape=[em, em, ints(LANE), ints(LANE), ints(groups)],
        compiler_params=_params(("parallel",)),
        name="moe_plan",
    )(gates, tri, upper)

    tile = lambda n: pl.BlockSpec((1, t, n), lambda i, s, *_: where(i), pipeline_mode=once)
    planned = lambda a, b: pl.BlockSpec((1, a, b), lambda i, s, *_: (i, 0, 0), pipeline_mode=once)
    whole = lambda a: pl.BlockSpec(a.shape, lambda i, s, *_: (0,) * a.ndim, pipeline_mode=once)
    experts = lambda a: pl.BlockSpec((None, EXPERTS_PER_STEP) + a.shape[2:], lambda i, s, *_: (layer, s, 0, 0))
    out = pl.pallas_call(
        functools.partial(_moe_kernel, bsz=bsz, rows_per_batch=m, n_lat=n_lat,
                          tiles_per_batch=tiles_per_batch, final=final),
        grid_spec=pltpu.PrefetchScalarGridSpec(
            num_scalar_prefetch=3,
            grid=(n_tiles, N_EXPERTS // EXPERTS_PER_STEP),
            in_specs=[tile(d), planned(LANE, t), planned(LANE, t),
                      experts(wg), experts(wu), experts(wd), whole(sg), whole(su), whole(sd),
                      tile(d), whole(mod), whole(final_norm)],
            out_specs=pl.BlockSpec((1, t, d), lambda i, s, *_: where(i)),
            scratch_shapes=[pltpu.VMEM((MOE_SLOTS + MOE_CAP, d), BF16), pltpu.VMEM((COMBINE_BLOCK, t), BF16)]),
        out_shape=jax.ShapeDtypeStruct(out_shape, F32),
        compiler_params=_params(("arbitrary", "arbitrary"), MOE_VMEM_LIMIT),
        name="moe_final" if final else "moe",
    )(cnt[:, 0], off[:, 0], own[:, 0], h, pos, gem, wg, wu, wd, sg, su, sd, xs, mod, final_norm)
    return out if final else out.reshape(bsz, m, d)


def _rope_tables(n_lat, m, head_dim, width):
    pos = jnp.arange(n_lat, dtype=jnp.int32)
    row = (pos // GRID_W).astype(F32)
    col = (pos % GRID_W).astype(F32)
    half = head_dim // 2
    quarter = half // 2
    lane = jnp.arange(head_dim)
    freq = (lane % quarter).astype(F32)
    inv = ROPE_THETA ** (-(2.0 * freq) / half)
    ang = jnp.where(lane[None, :] < half, row[:, None], col[:, None]) * inv[None, :]
    low = (lane % half) < quarter
    cos = jnp.cos(ang)
    sin = jnp.sin(ang)
    slo = jnp.where(low[None, :], -sin, 0.0)
    shi = jnp.where(low[None, :], 0.0, sin)
    reps = width // head_dim
    pad = lambda a, fill: jnp.concatenate(
        [jnp.tile(a, (1, reps)), jnp.full((m - n_lat, width), fill, F32)], axis=0)
    return pad(cos, 1.0), pad(slo, 0.0), pad(shi, 0.0)


def _mla_tables(tabs16):
    outs = []
    for a, fill in zip(tabs16, (1.0, 0.0, 0.0)):
        m = a.shape[0]
        slot = jnp.concatenate([jnp.full((m, MLA_NOPE), fill, F32), a,
                                jnp.full((m, QK_SLOT - MLA_QK), fill, F32)], axis=1)
        outs.append(jnp.tile(slot, (1, LANE // QK_SLOT)))
    return tuple(outs)


def _slots(w, heads, width):
    w = w.reshape(w.shape[0], heads, width)
    pad = jnp.zeros((w.shape[0], heads, QK_SLOT - width), w.dtype)
    return jnp.concatenate([w, pad], axis=-1).reshape(w.shape[0], heads * QK_SLOT)


def _block_diag(w):
    dirs = w.shape[0]
    eye = jnp.eye(LRU_BLOCKS, dtype=w.dtype)
    full = jnp.einsum('dnij,nm->dnimj', w, eye).reshape(dirs, LRU_WIDTH, LRU_WIDTH)
    nl = LRU_WIDTH // LRU_LANES
    return jnp.stack([full[:, j * LRU_LANES:(j + 1) * LRU_LANES, j * LRU_LANES:(j + 1) * LRU_LANES]
                      for j in range(nl)], axis=1)


def kernel(x, c, ctx, c_ctx, mod_w, mod_b, ab_w_in, ab_w_out, diff_lambda_q, diff_lambda_k, diff_subln, lru_conv_w, lru_conv_b, lru_w_a, lru_b_a, lru_w_x, lru_b_x, lru_lambda, cd_w_in, cd_w_out, gqa_q_norm, gqa_k_norm, mla_q_norm, mla_kv_norm, mla_w_uq, mla_w_ukv, router_w, router_bias, exp_w_gate, exp_w_up, exp_w_down, sh_w_gate, sh_w_up, sh_w_down, final_norm):
    bsz, n_lat, d = x.shape
    n_ctx = ctx.shape[1]
    m = n_lat + n_ctx
    depth = mod_w.shape[0]
    assert depth == 2 and bsz < MOD_ROWS
    assert n_lat % ROW_TILE == 0 and n_ctx % ROW_TILE == 0 and n_lat % GRID_W == 0
    ctx_row = bsz

    c_all = jnp.concatenate([c, c_ctx[None, :], jnp.zeros((MOD_ROWS - bsz - 1, d), F32)], axis=0)
    mods = _modulation(c_all, mod_w, mod_b)

    tabs64 = _rope_tables(n_lat, m, DIFF_HEAD_DIM, LANE)
    tabs_m = _mla_tables(_rope_tables(n_lat, m, MLA_ROPE, MLA_ROPE))

    exp_w = tuple(w.astype(BF16) for w in (exp_w_gate, exp_w_up, exp_w_down))

    def moe_weights(i):
        return exp_w + (sh_w_gate[i], sh_w_up[i], sh_w_down[i])

    fn = final_norm.reshape(1, d)

    mod = mods[0]
    qt, k, vt, u, g = _proj_ab(x, ctx, mod, ab_w_in[0].astype(BF16), tabs64, n_lat, ctx_row)
    lambda_init = 0.8 - 0.6 * math.exp(-0.3 * 0)
    dv = 2 * DIFF_HEAD_DIM + NORM_ROWS
    jobs = tuple(_Job((2 * h + j) * QK_SLOT, h, j, h * dv, dv) for h in range(DIFF_HEADS) for j in range(2))
    o_att = _attention(qt, k, vt, (diff_lambda_q[0], diff_lambda_k[0], diff_subln[0].reshape(-1, 1)),
                       jobs, n_lat, lambda_init, "diff_attention")
    o_rec = _lru(u, g, lru_conv_w[0], lru_conv_b[0].reshape(1, -1),
                 _block_diag(lru_w_a[0]).astype(BF16), lru_b_a[0].reshape(2, 1, LRU_WIDTH),
                 _block_diag(lru_w_x[0]).astype(BF16), lru_b_x[0].reshape(2, 1, LRU_WIDTH),
                 lru_lambda[0].reshape(2, 1, LRU_WIDTH), n_lat)
    xs, h2, gates = _out_proj(o_att, o_rec, x, ctx, mod, ab_w_out[0].astype(BF16),
                              router_w[0].T, router_bias[0].reshape(-1, 1), n_lat, ctx_row)
    xs = _moe(h2, gates, xs, mod, *moe_weights(0), fn, 0, n_lat, False)

    mod = mods[1]
    w_in = cd_w_in[0]
    wq, wk = GQA_HEADS * GQA_HEAD_DIM, GQA_KV_HEADS * GQA_HEAD_DIM
    o_kr = wq + 2 * wk + MLA_Q_RANK + MLA_KV_RANK
    kr_slots = jnp.concatenate([jnp.zeros((d, MLA_NOPE), F32), w_in[:, o_kr:o_kr + MLA_ROPE],
                                jnp.zeros((d, QK_SLOT - MLA_QK), F32)], axis=1)
    w_in = jnp.concatenate([w_in[:, :o_kr], jnp.tile(kr_slots, (1, MLA_HEADS))], axis=1).astype(BF16)
    ukv = mla_w_ukv[0].reshape(MLA_KV_RANK, MLA_HEADS, MLA_NOPE + MLA_V)
    w_kn = _slots(ukv[:, :, :MLA_NOPE].reshape(MLA_KV_RANK, -1), MLA_HEADS, MLA_NOPE).astype(BF16)
    w_v = ukv[:, :, MLA_NOPE:].reshape(MLA_KV_RANK, MLA_HEADS * MLA_V).astype(BF16)
    w_uq = _slots(mla_w_uq[0], MLA_HEADS, MLA_QK).astype(BF16)
    lane = jnp.arange(wq)
    ones_bd = (lane[:, None] // GQA_HEAD_DIM == lane[None, :] // GQA_HEAD_DIM).astype(BF16)
    qgt, kg, vgt, qmt, km, vmt = _proj_cd(
        xs, mod, w_in, ones_bd, jnp.tile(gqa_q_norm[0], GQA_HEADS).reshape(1, -1),
        jnp.tile(gqa_k_norm[0], GQA_KV_HEADS).reshape(1, -1), mla_q_norm[0].reshape(1, -1),
        mla_kv_norm[0].reshape(1, -1), w_uq, w_kn, w_v, tabs64, tabs_m, n_lat, ctx_row)
    groups = GQA_HEADS // GQA_KV_HEADS
    dvn = GQA_HEAD_DIM + NORM_ROWS
    jobs = tuple(_Job(h * QK_SLOT, 0, h // groups, (h // groups) * dvn, dvn) for h in range(GQA_HEADS))
    o_gqa = _attention(qgt, kg, vgt, (), jobs, n_lat, None, "gqa_attention")
    dvn = MLA_V + NORM_ROWS
    jobs = tuple(_Job(h * QK_SLOT, h // 2, h % 2, h * dvn, dvn) for h in range(MLA_HEADS))
    o_mla = _attention(qmt, km, vmt, (), jobs, n_lat, None, "mla_attention")
    xs, h2, gates = _out_proj(o_gqa, o_mla, xs, None, mod, cd_w_out[0].astype(BF16),
                              router_w[1].T, router_bias[1].reshape(-1, 1), n_lat, ctx_row)
    return _moe(h2, gates, xs, mod, *moe_weights(1), fn, 1, n_lat, True)
```
